```python
import math
import jax, jax.numpy as jnp
from jax import lax
import numpy as np

D_MODEL = 2048
BATCH = 4
SEQ = 4096
DEPTH = 1

D_MIX = D_MODEL
RET_HEADS = 4
RET_DK = 256
RET_DV = 256
RET_CHUNK = 128
ROPE_BASE = 10000.0
NSA_HEADS = 8
NSA_KV_GROUPS = 2
NSA_HPG = NSA_HEADS // NSA_KV_GROUPS
NSA_DH = 128
CMP_LEN = 32
CMP_STRIDE = 16
SEL_LEN = 64
SEL_TOPK = 16
SEL_Q_BLOCK = 64
WIN = 512
WIN_Q_BLOCK = 128
N_BRANCH = 3
REL_BUCKETS = 32
REL_MAX_DIST = 128
D_FF = 5632
EPS = 1e-6
NEG = -1e30
FORCE = 1e4

RET_W = RET_HEADS * RET_DV
NSA_W = NSA_HEADS * NSA_DH
KV_W = NSA_KV_GROUPS * NSA_DH
IN_SPLITS = [RET_HEADS * RET_DK, RET_HEADS * RET_DK, RET_W, RET_W, NSA_W,
             KV_W, KV_W, KV_W, KV_W, KV_W, KV_W, NSA_HEADS * N_BRANCH]
D_IN = sum(IN_SPLITS)

kernel_name = "hymba_retnet_nsa_macaron_block"


def rmsnorm(x, g):
    xf = x.astype(jnp.float32)
    y = xf * lax.rsqrt(jnp.mean(xf * xf, axis=-1, keepdims=True) + EPS)
    return (y * g.astype(jnp.float32)).astype(x.dtype)


def swiglu(x, w1, w3, w2):
    return (jax.nn.silu(x @ w1) * (x @ w3)) @ w2


def t5_bucket(rel):
    n = jnp.maximum(rel, 0)
    max_exact = REL_BUCKETS // 2
    nf = jnp.maximum(n, 1).astype(jnp.float32)
    large = max_exact + (jnp.log(nf / max_exact) / math.log(REL_MAX_DIST / max_exact)
                         * (REL_BUCKETS - max_exact)).astype(jnp.int32)
    large = jnp.minimum(large, REL_BUCKETS - 1)
    return jnp.where(n < max_exact, n, large)


def masked_softmax(logits, mask):
    p = jax.nn.softmax(jnp.where(mask, logits, NEG), axis=-1)
    return jnp.where(mask, p, 0.0)


def rotary(x, pos):
    half = x.shape[-1] // 2
    inv = ROPE_BASE ** (-jnp.arange(half, dtype=jnp.float32) / half)
    ang = pos.astype(jnp.float32)[:, None] * inv[None, :]
    cos = jnp.cos(ang)[None, :, None, :]
    sin = jnp.sin(ang)[None, :, None, :]
    xf = x.astype(jnp.float32)
    x1, x2 = xf[..., :half], xf[..., half:]
    return jnp.concatenate([x1 * cos - x2 * sin, x1 * sin + x2 * cos], axis=-1)


def retention(q, k, v):
    B, S, H, DK = q.shape
    DV = v.shape[-1]
    C = RET_CHUNK
    N = S // C
    log_g = jnp.log(1.0 - 2.0 ** (-5.0 - jnp.arange(H, dtype=jnp.float32)))
    idx = jnp.arange(C, dtype=jnp.float32)
    diff = idx[:, None] - idx[None, :]
    inner_decay = jnp.where(diff >= 0.0,
                            jnp.exp(jnp.maximum(diff, 0.0)[None] * log_g[:, None, None]), 0.0)
    xi = jnp.exp((idx + 1.0)[None] * log_g[:, None])
    zeta = jnp.exp((C - 1.0 - idx)[None] * log_g[:, None])
    chunk_decay = jnp.exp(C * log_g)
    qc = q.reshape(B, N, C, H, DK)
    kc = k.reshape(B, N, C, H, DK)
    vc = v.reshape(B, N, C, H, DV)
    s = jnp.einsum('bnchd,bnmhd->bnhcm', qc, kc) * inner_decay
    y_in = jnp.einsum('bnhcm,bnmhe->bnche', s, vc)
    u = jnp.einsum('bnmhd,bnmhe,hm->nbhde', kc, vc, zeta)

    def step(state, u_n):
        return chunk_decay[None, :, None, None] * state + u_n, state

    _, r_prev = lax.scan(step, jnp.zeros((B, H, DK, DV), jnp.float32), u)
    y_cross = jnp.einsum('bnchd,nbhde,hc->bnche', qc, r_prev, xi)
    return (y_in + y_cross).reshape(B, S, H, DV)


def compress(kv, pe, w1, w2, cmp_idx):
    blk = kv[:, :, cmp_idx] + pe.astype(kv.dtype)
    hdn = jax.nn.silu(jnp.einsum('bgnld,ldf->bgnf', blk, w1))
    return hdn @ w2


def nsa(q, kc_, vc_, ks_, vs_, kw_, vw_, gates, pe_k, w1_k, w2_k, pe_v, w1_v, w2_v, rel_bias):
    B, S, H, dh = q.shape
    G, hpg = NSA_KV_GROUPS, NSA_HPG
    scale = dh ** -0.5
    pos = jnp.arange(S, dtype=jnp.int32)
    qg = q.reshape(B, S, G, hpg, dh).transpose(0, 2, 3, 1, 4)
    to_g = lambda t: t.reshape(B, S, G, dh).transpose(0, 2, 1, 3)
    kc_, vc_, ks_, vs_, kw_, vw_ = map(to_g, (kc_, vc_, ks_, vs_, kw_, vw_))

    n_cmp = (S - CMP_LEN) // CMP_STRIDE + 1
    cmp_idx = np.arange(n_cmp)[:, None] * CMP_STRIDE + np.arange(CMP_LEN)[None, :]
    k_cmp = compress(kc_, pe_k, w1_k, w2_k, cmp_idx)
    v_cmp = compress(vc_, pe_v, w1_v, w2_v, cmp_idx)
    cmp_end = jnp.asarray(np.arange(n_cmp) * CMP_STRIDE + CMP_LEN - 1, jnp.int32)
    rel_c = pos[:, None] - cmp_end[None, :]
    bias_c = rel_bias.astype(jnp.float32)[:, t5_bucket(rel_c)].reshape(G, hpg, S, n_cmp)
    logit_c = jnp.einsum('bghsd,bgnd->bghsn', qg, k_cmp).astype(jnp.float32) * scale + bias_c
    p_cmp = masked_softmax(logit_c, rel_c >= 0)
    o_cmp = jnp.einsum('bghsn,bgnd->bghsd', p_cmp.astype(v_cmp.dtype), v_cmp)

    n_sel = S // SEL_LEN
    n_top = min(SEL_TOPK, n_sel)
    sel_of = cmp_idx // SEL_LEN
    overlap = jnp.asarray((sel_of[:, :, None] == np.arange(n_sel)[None, None, :]).sum(1)
                          .astype(np.float32) / CMP_LEN)
    imp = jnp.einsum('bghsn,nj->bgsj', p_cmp, overlap)
    blk = jnp.arange(n_sel, dtype=jnp.int32)
    cur = pos // SEL_LEN
    causal = (blk[None, :] * SEL_LEN) <= pos[:, None]
    forced = (blk[None, :] == 0) | (blk[None, :] == cur[:, None]) | (blk[None, :] == cur[:, None] - 1)
    score = jnp.where(forced, FORCE, jnp.where(causal, imp, NEG))
    top_val, top_idx = lax.top_k(score, n_top)
    top_ok = top_val > (NEG * 0.5)

    k_blocks = ks_.reshape(B, G, n_sel, SEL_LEN, dh)
    v_blocks = vs_.reshape(B, G, n_sel, SEL_LEN, dh)
    bi = jnp.arange(B)[:, None, None, None]
    gi = jnp.arange(G)[None, :, None, None]
    tbl_g = rel_bias.astype(jnp.float32).reshape(G, hpg, REL_BUCKETS).transpose(0, 2, 1)
    Qb = SEL_Q_BLOCK
    nq = S // Qb

    def sel_block(args):
        qb, idxb, okb, posb = args
        kg = k_blocks[bi, gi, idxb].reshape(B, G, Qb, n_top * SEL_LEN, dh)
        vg = v_blocks[bi, gi, idxb].reshape(B, G, Qb, n_top * SEL_LEN, dh)
        kpos = (idxb[..., None] * SEL_LEN + jnp.arange(SEL_LEN, dtype=jnp.int32)).reshape(B, G, Qb, -1)
        rel = posb[None, None, :, None] - kpos
        mask = jnp.repeat(okb, SEL_LEN, axis=-1) & (rel >= 0)
        bias = jnp.moveaxis(tbl_g[gi, t5_bucket(rel)], -1, 2)
        logits = jnp.einsum('bghqd,bgqkd->bghqk', qb, kg).astype(jnp.float32) * scale + bias
        p = masked_softmax(logits, mask[:, :, None])
        return jnp.einsum('bghqk,bgqkd->bghqd', p.astype(vg.dtype), vg)

    q_blocks = jnp.moveaxis(qg.reshape(B, G, hpg, nq, Qb, dh), 3, 0)
    idx_blocks = jnp.moveaxis(top_idx.reshape(B, G, nq, Qb, n_top), 2, 0)
    ok_blocks = jnp.moveaxis(top_ok.reshape(B, G, nq, Qb, n_top), 2, 0)
    pos_blocks = pos.reshape(nq, Qb)
    o_sel = lax.map(sel_block, (q_blocks, idx_blocks, ok_blocks, pos_blocks))
    o_sel = jnp.moveaxis(o_sel, 0, 3).reshape(B, G, hpg, S, dh)

    Wb = WIN_Q_BLOCK
    nb = S // Wb
    nw = WIN // Wb
    Kw = (nw + 1) * Wb

    def band(kv):
        padded = jnp.pad(kv, ((0, 0), (0, 0), (WIN, 0), (0, 0))).reshape(B, G, nb + nw, Wb, dh)
        return jnp.concatenate([padded[:, :, i:i + nb] for i in range(nw + 1)], axis=3)

    kwb, vwb = band(kw_), band(vw_)
    qi = np.arange(Wb)[:, None]
    kj = np.arange(Kw)[None, :]
    rel_w = qi + WIN - kj
    in_band = (rel_w >= 0) & (rel_w < WIN)
    real = (np.arange(nb)[:, None] * Wb + np.arange(Kw)[None, :] - WIN) >= 0
    mask_w = jnp.asarray(in_band[None] & real[:, None, :])
    bias_w = rel_bias.astype(jnp.float32)[:, t5_bucket(jnp.asarray(rel_w, jnp.int32))]
    bias_w = bias_w.reshape(G, hpg, 1, Wb, Kw)
    qw = qg.reshape(B, G, hpg, nb, Wb, dh)
    logit_w = jnp.einsum('bghnqd,bgnkd->bghnqk', qw, kwb).astype(jnp.float32) * scale + bias_w
    p_w = masked_softmax(logit_w, mask_w)
    o_win = jnp.einsum('bghnqk,bgnkd->bghnqd', p_w.astype(vwb.dtype), vwb).reshape(B, G, hpg, S, dh)

    g = jax.nn.sigmoid(gates.astype(jnp.float32)).reshape(B, S, G, hpg, N_BRANCH).transpose(0, 2, 3, 1, 4)
    o = (g[..., 0:1] * o_cmp.astype(jnp.float32) + g[..., 1:2] * o_sel.astype(jnp.float32)
         + g[..., 2:3] * o_win.astype(jnp.float32))
    return o.transpose(0, 3, 1, 2, 4).reshape(B, S, H * dh).astype(q.dtype)


def token_mix(h, w_in, ret_gn_gain, pe_k, w1_k, w2_k, pe_v, w1_v, w2_v, w_out, rel_bias):
    B, S, _ = h.shape
    proj = h @ w_in
    cols = []
    off = 0
    for w in IN_SPLITS:
        cols.append(proj[..., off:off + w])
        off += w
    rq, rk, rv, rg, nq_, kc_, vc_, ks_, vs_, kw_, vw_, ngate = cols
    pos = jnp.arange(S, dtype=jnp.int32)
    q_r = rotary(rq.reshape(B, S, RET_HEADS, RET_DK), pos)
    k_r = rotary(rk.reshape(B, S, RET_HEADS, RET_DK), pos) * (RET_DK ** -0.5)
    v_r = rv.reshape(B, S, RET_HEADS, RET_DV).astype(jnp.float32)
    y = retention(q_r, k_r, v_r)
    mu = jnp.mean(y, axis=-1, keepdims=True)
    var = jnp.mean((y - mu) ** 2, axis=-1, keepdims=True)
    y = ((y - mu) * lax.rsqrt(var + EPS)).reshape(B, S, RET_W) * ret_gn_gain.astype(jnp.float32)
    y_ret = (jax.nn.silu(rg.astype(jnp.float32)) * y).astype(h.dtype)
    y_nsa = nsa(nq_.reshape(B, S, NSA_HEADS, NSA_DH), kc_, vc_, ks_, vs_, kw_, vw_, ngate,
                pe_k, w1_k, w2_k, pe_v, w1_v, w2_v, rel_bias)
    return jnp.concatenate([y_ret, y_nsa], axis=-1) @ w_out


def setup_inputs(seed: int = 0) -> dict:
    key = jax.random.key(seed)
    ks = jax.random.split(key, 24)
    f32 = jnp.float32

    def nrm(k, shape, scale):
        return jax.random.normal(k, shape, f32) * scale

    def gain(k, shape):
        return 1.0 + 0.05 * jax.random.normal(k, shape, f32)

    L, dh = CMP_LEN, NSA_DH
    return {
        "x": nrm(ks[0], (BATCH, SEQ, D_MODEL), 1.0),
        "ffn1_norm": gain(ks[1], (DEPTH, D_MODEL)),
        "ffn1_w1": nrm(ks[2], (DEPTH, D_MODEL, D_FF), D_MODEL ** -0.5),
        "ffn1_w3": nrm(ks[3], (DEPTH, D_MODEL, D_FF), D_MODEL ** -0.5),
        "ffn1_w2": nrm(ks[4], (DEPTH, D_FF, D_MODEL), D_FF ** -0.5),
        "mix_norm": gain(ks[5], (DEPTH, D_MODEL)),
        "w_in": nrm(ks[6], (DEPTH, D_MODEL, D_IN), D_MODEL ** -0.5),
        "ret_gn_gain": gain(ks[7], (DEPTH, RET_W)),
        "cmp_pe_k": nrm(ks[8], (DEPTH, L, dh), 0.1),
        "cmp_w1_k": nrm(ks[9], (DEPTH, L, dh, dh), (L * dh) ** -0.5),
        "cmp_w2_k": nrm(ks[10], (DEPTH, dh, dh), dh ** -0.5),
        "cmp_pe_v": nrm(ks[11], (DEPTH, L, dh), 0.1),
        "cmp_w1_v": nrm(ks[12], (DEPTH, L, dh, dh), (L * dh) ** -0.5),
        "cmp_w2_v": nrm(ks[13], (DEPTH, dh, dh), dh ** -0.5),
        "w_out": nrm(ks[14], (DEPTH, D_MIX, D_MODEL), D_MIX ** -0.5),
        "ffn2_norm": gain(ks[15], (DEPTH, D_MODEL)),
        "ffn2_w1": nrm(ks[16], (DEPTH, D_MODEL, D_FF), D_MODEL ** -0.5),
        "ffn2_w3": nrm(ks[17], (DEPTH, D_MODEL, D_FF), D_MODEL ** -0.5),
        "ffn2_w2": nrm(ks[18], (DEPTH, D_FF, D_MODEL), D_FF ** -0.5),
        "rel_bias": nrm(ks[19], (NSA_HEADS, REL_BUCKETS), 0.5),
        "final_norm": gain(ks[20], (D_MODEL,)),
    }


def reference(x, ffn1_norm, ffn1_w1, ffn1_w3, ffn1_w2, mix_norm, w_in, ret_gn_gain,
              cmp_pe_k, cmp_w1_k, cmp_w2_k, cmp_pe_v, cmp_w1_v, cmp_w2_v, w_out,
              ffn2_norm, ffn2_w1, ffn2_w3, ffn2_w2, rel_bias, final_norm):
    for l in range(DEPTH):
        x = x + 0.5 * swiglu(rmsnorm(x, ffn1_norm[l]), ffn1_w1[l], ffn1_w3[l], ffn1_w2[l])
        x = x + token_mix(rmsnorm(x, mix_norm[l]), w_in[l], ret_gn_gain[l],
                          cmp_pe_k[l], cmp_w1_k[l], cmp_w2_k[l],
                          cmp_pe_v[l], cmp_w1_v[l], cmp_w2_v[l], w_out[l], rel_bias)
        x = x + 0.5 * swiglu(rmsnorm(x, ffn2_norm[l]), ffn2_w1[l], ffn2_w3[l], ffn2_w2[l])
    return rmsnorm(x, final_norm)
```

```python
import functools
import math

import jax
import jax.numpy as jnp
import numpy as np
from jax import lax
from jax.experimental import pallas as pl
from jax.experimental.pallas import tpu as pltpu

F32 = jnp.float32
BF16 = jnp.bfloat16

RET_HEADS = 4
RET_DK = 256
RET_DV = 256
ROPE_BASE = 10000.0
NSA_HEADS = 8
NSA_KV_GROUPS = 2
NSA_HPG = NSA_HEADS // NSA_KV_GROUPS
NSA_DH = 128
CMP_LEN = 32
CMP_STRIDE = 16
SEL_LEN = 64
SEL_TOPK = 16
WIN = 512
N_BRANCH = 3
REL_BUCKETS = 32
REL_MAX_DIST = 128
EPS = 1e-6
NEG = -1e30
FORCE = 1e4

RET_W = RET_HEADS * RET_DV
NSA_W = NSA_HEADS * NSA_DH
KV_W = NSA_KV_GROUPS * NSA_DH
IN_SPLITS = [RET_HEADS * RET_DK, RET_HEADS * RET_DK, RET_W, RET_W, NSA_W,
             KV_W, KV_W, KV_W, KV_W, KV_W, KV_W, NSA_HEADS * N_BRANCH]
D_IN = sum(IN_SPLITS)
IN_OFF = [sum(IN_SPLITS[:i]) for i in range(len(IN_SPLITS))]

LANE = 128
IN_TN = 768
D_IN_PAD = 9 * IN_TN
FFN_TM = 512
FFN_TF = 512
IN_TM = 1024
RET_C = 128
ATT_T = 256
OUT_TM = 256
N_CMP_PAD = 256
SEL_PAD = 128
VMEM_LIMIT = 52 * 1024 * 1024


def _dot(a, b):
    return jnp.dot(a, b, preferred_element_type=F32)


def _dot_nt(a, b):
    return lax.dot_general(a, b, (((1,), (1,)), ((), ())), preferred_element_type=F32)


def _dot_tn(a, b):
    return lax.dot_general(a, b, (((0,), (0,)), ((), ())), preferred_element_type=F32)


def _sigmoid(x):
    return 1.0 / (1.0 + jnp.exp(-x))


def _rms(x, g):
    ms = jnp.mean(x * x, axis=-1, keepdims=True)
    return x * lax.rsqrt(ms + EPS) * g


def _ffn_kernel(x_ref, g_ref, w1_ref, w3_ref, w2_ref, fg_ref, o_ref, n_ref, acc_ref, *, final_norm):
    j = pl.program_id(1)

    @pl.when(j == 0)
    def _():
        n_ref[...] = _rms(x_ref[...], g_ref[...]).astype(BF16)
        acc_ref[...] = jnp.zeros_like(acc_ref)

    n = n_ref[...]
    a = _dot(n, w1_ref[...])
    b = _dot(n, w3_ref[...])
    h = (a * _sigmoid(a) * b).astype(BF16)
    acc_ref[...] += _dot(h, w2_ref[...])

    @pl.when(j == pl.num_programs(1) - 1)
    def _():
        y = x_ref[...] + 0.5 * acc_ref[...]
        if final_norm:
            y = _rms(y, fg_ref[...])
        o_ref[...] = y


def _ffn(x2d, g, w1, w3, w2, fg, final_norm):
    tok, d = x2d.shape
    dff = w1.shape[1]
    tm, tf = FFN_TM, FFN_TF
    return pl.pallas_call(
        functools.partial(_ffn_kernel, final_norm=final_norm),
        grid=(tok // tm, dff // tf),
        in_specs=[
            pl.BlockSpec((tm, d), lambda i, j: (i, 0)),
            pl.BlockSpec((1, d), lambda i, j: (0, 0)),
            pl.BlockSpec((d, tf), lambda i, j: (0, j)),
            pl.BlockSpec((d, tf), lambda i, j: (0, j)),
            pl.BlockSpec((tf, d), lambda i, j: (j, 0)),
            pl.BlockSpec((1, d), lambda i, j: (0, 0)),
        ],
        out_specs=pl.BlockSpec((tm, d), lambda i, j: (i, 0)),
        out_shape=jax.ShapeDtypeStruct((tok, d), F32),
        scratch_shapes=[pltpu.VMEM((tm, d), BF16), pltpu.VMEM((tm, d), F32)],
        compiler_params=pltpu.CompilerParams(
            dimension_semantics=("parallel", "arbitrary"), vmem_limit_bytes=VMEM_LIMIT),
        name="ffn",
    )(x2d, g.reshape(1, d), w1, w3, w2, fg.reshape(1, d))


def _inproj_kernel(x_ref, g_ref, w_ref, o_ref, n_ref):
    @pl.when(pl.program_id(1) == 0)
    def _():
        n_ref[...] = _rms(x_ref[...], g_ref[...]).astype(BF16)

    o_ref[...] = _dot(n_ref[...], w_ref[...])


def _inproj(x2d, g, w_pad):
    tok, d = x2d.shape
    n_out = w_pad.shape[1]
    tm, tn = IN_TM, IN_TN
    return pl.pallas_call(
        _inproj_kernel,
        grid=(tok // tm, n_out // tn),
        in_specs=[
            pl.BlockSpec((tm, d), lambda i, j: (i, 0)),
            pl.BlockSpec((1, d), lambda i, j: (0, 0)),
            pl.BlockSpec((d, tn), lambda i, j: (0, j)),
        ],
        out_specs=pl.BlockSpec((tm, tn), lambda i, j: (i, j)),
        out_shape=jax.ShapeDtypeStruct((tok, n_out), F32),
        scratch_shapes=[pltpu.VMEM((tm, d), BF16)],
        compiler_params=pltpu.CompilerParams(
            dimension_semantics=("parallel", "arbitrary"), vmem_limit_bytes=VMEM_LIMIT),
        name="in_proj",
    )(x2d, g.reshape(1, d), w_pad)


def _ret_kernel(cd_ref, q_ref, k_ref, v_ref, g_ref, cos_ref, sin_ref, dec_ref, xi_ref, zeta_ref,
                gain_ref, o_ref, state_ref):
    @pl.when(pl.program_id(1) == 0)
    def _():
        state_ref[...] = jnp.zeros_like(state_ref)

    cos = cos_ref[...]
    sin = sin_ref[...]
    half = RET_DK // 2

    def rot(t):
        t1, t2 = t[:, :half], t[:, half:]
        return jnp.concatenate([t1 * cos - t2 * sin, t1 * sin + t2 * cos], axis=-1)

    for h in range(RET_HEADS):
        cs = slice(h * RET_DK, (h + 1) * RET_DK)
        q = rot(q_ref[:, cs])
        k = rot(k_ref[:, cs]) * (RET_DK ** -0.5)
        v = v_ref[:, cs].astype(BF16)
        qb = q.astype(BF16)
        s = _dot_nt(qb, k.astype(BF16)) * dec_ref[h]
        state = state_ref[h]
        y = _dot(s.astype(BF16), v) + _dot((q * xi_ref[h]).astype(BF16), state.astype(BF16))
        state_ref[h] = cd_ref[h] * state + _dot_tn((k * zeta_ref[h]).astype(BF16), v)
        mu = jnp.mean(y, axis=-1, keepdims=True)
        yc = y - mu
        var = jnp.mean(yc * yc, axis=-1, keepdims=True)
        yn = yc * lax.rsqrt(var + EPS) * gain_ref[:, cs]
        gate = g_ref[:, cs]
        o_ref[:, cs] = gate * _sigmoid(gate) * yn


def _retention(proj, gain, batch, seq):
    tok = proj.shape[0]
    c = RET_C
    n_chunks = seq // c
    h, dk = RET_HEADS, RET_DK
    half = dk // 2
    inv = ROPE_BASE ** (-jnp.arange(half, dtype=F32) / half)
    ang = jnp.arange(seq, dtype=F32)[:, None] * inv[None, :]
    cos, sin = jnp.cos(ang), jnp.sin(ang)
    log_g = jnp.log(1.0 - 2.0 ** (-5.0 - jnp.arange(h, dtype=F32)))
    idx = jnp.arange(c, dtype=F32)
    diff = idx[:, None] - idx[None, :]
    dec = jnp.where(diff >= 0.0, jnp.exp(jnp.maximum(diff, 0.0)[None] * log_g[:, None, None]), 0.0)
    xi = jnp.broadcast_to(jnp.exp((idx + 1.0)[None] * log_g[:, None])[:, :, None], (h, c, dk))
    zeta = jnp.broadcast_to(jnp.exp((c - 1.0 - idx)[None] * log_g[:, None])[:, :, None], (h, c, dk))
    cd = jnp.exp(c * log_g)

    w = h * dk
    row = lambda b, n: (b * n_chunks + n)
    return pl.pallas_call(
        _ret_kernel,
        grid=(batch, n_chunks),
        in_specs=[
            pl.BlockSpec(memory_space=pltpu.SMEM),
            pl.BlockSpec((c, w), lambda b, n: (row(b, n), 0)),
            pl.BlockSpec((c, w), lambda b, n: (row(b, n), 1)),
            pl.BlockSpec((c, w), lambda b, n: (row(b, n), 2)),
            pl.BlockSpec((c, w), lambda b, n: (row(b, n), 3)),
            pl.BlockSpec((c, half), lambda b, n: (n, 0)),
            pl.BlockSpec((c, half), lambda b, n: (n, 0)),
            pl.BlockSpec((h, c, c), lambda b, n: (0, 0, 0)),
            pl.BlockSpec((h, c, dk), lambda b, n: (0, 0, 0)),
            pl.BlockSpec((h, c, dk), lambda b, n: (0, 0, 0)),
            pl.BlockSpec((1, w), lambda b, n: (0, 0)),
        ],
        out_specs=pl.BlockSpec((c, w), lambda b, n: (row(b, n), 0)),
        out_shape=jax.ShapeDtypeStruct((tok, w), F32),
        scratch_shapes=[pltpu.VMEM((h, dk, RET_DV), F32)],
        compiler_params=pltpu.CompilerParams(
            dimension_semantics=("parallel", "arbitrary"), vmem_limit_bytes=VMEM_LIMIT),
        name="retention",
    )(cd, proj, proj, proj, proj, cos, sin, dec, xi, zeta, gain.reshape(1, w))


def _compress_kernel(k_ref, v_ref, pek_ref, w1k_ref, w1kf_ref, w2k_ref, pev_ref, w1v_ref, w1vf_ref,
                     w2v_ref, ok_ref, ov_ref):
    dh = NSA_DH
    nblk = k_ref.shape[0] // CMP_STRIDE

    def one(x_ref, pe_ref, w1_ref, w1f_ref, w2_ref, o_ref):
        ab = jnp.zeros((nblk, 2 * dh), F32)
        for l in range(CMP_STRIDE):
            rows = x_ref[pl.ds(l, nblk, stride=CMP_STRIDE), :].astype(BF16)
            ab = ab + _dot(rows, w1_ref[l])
        a, b = ab[:, :dh], ab[:, dh:]
        b_next = pltpu.roll(b, nblk - 1, axis=0)
        ridx = lax.broadcasted_iota(jnp.int32, (nblk, dh), 0)
        b_next = jnp.where(ridx < nblk - 1, b_next, 0.0)
        pe_term = _dot(pe_ref[...], w1f_ref[...])[0:1, :]
        hdn = a + b_next + pe_term
        o_ref[0, 0] = _dot((hdn * _sigmoid(hdn)).astype(BF16), w2_ref[...])

    one(k_ref, pek_ref, w1k_ref, w1kf_ref, w2k_ref, ok_ref)
    one(v_ref, pev_ref, w1v_ref, w1vf_ref, w2v_ref, ov_ref)


def _compress(proj, pe_k, w1_k, w2_k, pe_v, w1_v, w2_v, batch, seq):
    dh, g = NSA_DH, NSA_KV_GROUPS
    nblk = seq // CMP_STRIDE
    half = CMP_LEN // 2

    def prep(pe, w1, w2):
        w1b = w1.astype(BF16)
        w1_pair = jnp.concatenate([w1b[:half], w1b[half:]], axis=-1)
        pe_flat = jnp.broadcast_to(pe.reshape(1, CMP_LEN * dh), (8, CMP_LEN * dh)).astype(BF16)
        return pe_flat, w1_pair, w1b.reshape(CMP_LEN * dh, dh), w2.astype(BF16)

    args_k = prep(pe_k, w1_k, w2_k)
    args_v = prep(pe_v, w1_v, w2_v)
    kcol = IN_OFF[5] // dh
    vcol = IN_OFF[6] // dh
    const = lambda shape: pl.BlockSpec(shape, lambda b, gg: (0,) * len(shape))
    wspecs = [const((8, CMP_LEN * dh)), const((half, dh, 2 * dh)), const((CMP_LEN * dh, dh)),
              const((dh, dh))]
    out_spec = pl.BlockSpec((1, 1, nblk, dh), lambda b, gg: (b, gg, 0, 0))
    return pl.pallas_call(
        _compress_kernel,
        grid=(batch, g),
        in_specs=[pl.BlockSpec((seq, dh), lambda b, gg: (b, kcol + gg)),
                  pl.BlockSpec((seq, dh), lambda b, gg: (b, vcol + gg))] + wspecs + wspecs,
        out_specs=[out_spec, out_spec],
        out_shape=[jax.ShapeDtypeStruct((batch, g, nblk, dh), F32)] * 2,
        compiler_params=pltpu.CompilerParams(
            dimension_semantics=("parallel", "parallel"), vmem_limit_bytes=VMEM_LIMIT),
        name="compress",
    )(proj, proj, *args_k, *args_v)


def _cmpattn_kernel(q_ref, kc_ref, vc_ref, bias_ref, ovt_ref, o_ref, sel_ref):
    tq = q_ref.shape[0]
    ncp = kc_ref.shape[2]
    dh = NSA_DH
    s0 = pl.program_id(2) * tq
    pos_r = s0 + lax.broadcasted_iota(jnp.int32, (tq, ncp), 0)
    n_c = lax.broadcasted_iota(jnp.int32, (tq, ncp), 1)
    mask = pos_r >= n_c * CMP_STRIDE + (CMP_LEN - 1)
    kc = kc_ref[0, 0].astype(BF16)
    vc = vc_ref[0, 0].astype(BF16)
    scale = dh ** -0.5
    psum = jnp.zeros((tq, ncp), F32)
    for hh in range(NSA_HPG):
        cs = slice(hh * dh, (hh + 1) * dh)
        q = q_ref[:, cs].astype(BF16)
        logits = _dot_nt(q, kc) * scale + bias_ref[hh]
        lm = jnp.where(mask, logits, NEG)
        m = jnp.max(lm, axis=-1, keepdims=True)
        e = jnp.where(mask, jnp.exp(lm - m), 0.0)
        den = jnp.sum(e, axis=-1, keepdims=True)
        p = e / jnp.where(den > 0.0, den, 1.0)
        o_ref[:, cs] = _dot(p.astype(BF16), vc)
        psum = psum + p
    nsel = ovt_ref.shape[0]
    imp_t = _dot_nt(ovt_ref[...], psum.astype(BF16))
    jb = lax.broadcasted_iota(jnp.int32, (nsel, tq), 0)
    pos = s0 + lax.broadcasted_iota(jnp.int32, (nsel, tq), 1)
    cur = jnp.right_shift(pos, SEL_LEN.bit_length() - 1)
    causal = jb * SEL_LEN <= pos
    forced = (jb == 0) | (jb == cur) | (jb == cur - 1)
    score = jnp.where(forced, FORCE, jnp.where(causal, imp_t, NEG))
    rank = jnp.zeros((nsel, tq), jnp.int32)
    for kk in range(nsel):
        row = score[kk:kk + 1, :]
        beats = (row > score) | ((row == score) & (jb > kk))
        rank = rank + jnp.where(beats, 1, 0)
    sel = jnp.where((rank < SEL_TOPK) & causal, 1.0, 0.0)
    sel = jnp.concatenate([sel, jnp.zeros((SEL_PAD - nsel, tq), F32)], axis=0)
    sel_ref[0, 0] = sel.T.astype(BF16)


def _cmp_attention(proj, k_cmp, v_cmp, bias_c, batch, seq):
    dh, g, hpg = NSA_DH, NSA_KV_GROUPS, NSA_HPG
    tq = ATT_T
    nq = seq // tq
    ncp = k_cmp.shape[2]
    nsel = seq // SEL_LEN
    n_cmp = (seq - CMP_LEN) // CMP_STRIDE + 1
    cmp_idx = np.arange(n_cmp)[:, None] * CMP_STRIDE + np.arange(CMP_LEN)[None, :]
    overlap = ((cmp_idx // SEL_LEN)[:, :, None] == np.arange(nsel)[None, None, :]).sum(1) / CMP_LEN
    ovt = np.zeros((nsel, ncp), np.float32)
    ovt[:, :n_cmp] = overlap.T
    qcol = IN_OFF[4] // (hpg * dh)
    return pl.pallas_call(
        _cmpattn_kernel,
        grid=(batch, g, nq),
        in_specs=[
            pl.BlockSpec((tq, hpg * dh), lambda b, gg, t: (b * nq + t, qcol + gg)),
            pl.BlockSpec((1, 1, ncp, dh), lambda b, gg, t: (b, gg, 0, 0)),
            pl.BlockSpec((1, 1, ncp, dh), lambda b, gg, t: (b, gg, 0, 0)),
            pl.BlockSpec((hpg, tq, ncp), lambda b, gg, t: (gg, t, 0)),
            pl.BlockSpec((nsel, ncp), lambda b, gg, t: (0, 0)),
        ],
        out_specs=[
            pl.BlockSpec((tq, hpg * dh), lambda b, gg, t: (b * nq + t, gg)),
            pl.BlockSpec((1, 1, tq, SEL_PAD), lambda b, gg, t: (b, gg, t, 0)),
        ],
        out_shape=[jax.ShapeDtypeStruct((batch * seq, g * hpg * dh), F32),
                   jax.ShapeDtypeStruct((batch, g, seq, SEL_PAD), BF16)],
        compiler_params=pltpu.CompilerParams(
            dimension_semantics=("parallel", "parallel", "parallel"), vmem_limit_bytes=VMEM_LIMIT),
        name="cmp_attn",
    )(proj, k_cmp, v_cmp, bias_c, jnp.asarray(ovt, BF16))


def _band_kernel(qi_ref, ki_ref, off_ref, first_ref, last_ref, q_ref, k_ref, v_ref, bias_ref, *rest,
                 use_sel):
    if use_sel:
        sel_ref, e_ref, o_ref, m_ref, l_ref, acc_ref = rest
    else:
        o_ref, m_ref, l_ref, acc_ref = rest
    t = pl.program_id(2)
    dh = NSA_DH

    @pl.when(first_ref[t] == 1)
    def _():
        m_ref[...] = jnp.full_like(m_ref, NEG)
        l_ref[...] = jnp.zeros_like(l_ref)
        acc_ref[...] = jnp.zeros_like(acc_ref)

    k = k_ref[...].astype(BF16)
    v = v_ref[...].astype(BF16)
    scale = dh ** -0.5
    if use_sel:
        keep = _dot(sel_ref[0, 0], e_ref[...]) > 0.5
    for hh in range(NSA_HPG):
        cs = slice(hh * dh, (hh + 1) * dh)
        q = q_ref[:, cs].astype(BF16)
        s = _dot_nt(q, k) * scale + bias_ref[hh, 0]
        if use_sel:
            s = jnp.where(keep, s, NEG)
        m_prev = m_ref[hh]
        m_new = jnp.maximum(m_prev, jnp.max(s, axis=-1, keepdims=True))
        alpha = jnp.exp(m_prev - m_new)
        p = jnp.exp(s - m_new)
        l_ref[hh] = alpha * l_ref[hh] + jnp.sum(p, axis=-1, keepdims=True)
        acc_ref[hh] = alpha * acc_ref[hh] + _dot(p.astype(BF16), v)
        m_ref[hh] = m_new

    @pl.when(last_ref[t] == 1)
    def _():
        for hh in range(NSA_HPG):
            o_ref[:, hh * dh:(hh + 1) * dh] = acc_ref[hh] / l_ref[hh]


def _band_attention(proj, bias_tiles, steps, kcol, vcol, batch, seq, sel=None, e_mat=None):
    dh, g, hpg = NSA_DH, NSA_KV_GROUPS, NSA_HPG
    tq = ATT_T
    nq = seq // tq
    qi = np.array([s[0] for s in steps], np.int32)
    ki = np.array([s[1] for s in steps], np.int32)
    off = np.array([s[2] for s in steps], np.int32)
    first = np.concatenate([[1], (qi[1:] != qi[:-1]).astype(np.int32)]).astype(np.int32)
    last = np.concatenate([(qi[1:] != qi[:-1]).astype(np.int32), [1]]).astype(np.int32)
    qcol = IN_OFF[4] // (hpg * dh)
    use_sel = sel is not None
    in_specs = [
        pl.BlockSpec((tq, hpg * dh), lambda b, gg, t, qi_r, ki_r, off_r, f_r, l_r: (b * nq + qi_r[t], qcol + gg)),
        pl.BlockSpec((tq, dh), lambda b, gg, t, qi_r, ki_r, off_r, f_r, l_r: (b * nq + ki_r[t], kcol + gg)),
        pl.BlockSpec((tq, dh), lambda b, gg, t, qi_r, ki_r, off_r, f_r, l_r: (b * nq + ki_r[t], vcol + gg)),
        pl.BlockSpec((hpg, 1, tq, tq), lambda b, gg, t, qi_r, ki_r, off_r, f_r, l_r: (gg, off_r[t], 0, 0)),
    ]
    args = [proj, proj, proj, bias_tiles]
    if use_sel:
        in_specs += [
            pl.BlockSpec((1, 1, tq, SEL_PAD), lambda b, gg, t, qi_r, ki_r, off_r, f_r, l_r: (b, gg, qi_r[t], 0)),
            pl.BlockSpec((SEL_PAD, tq), lambda b, gg, t, qi_r, ki_r, off_r, f_r, l_r: (0, ki_r[t])),
        ]
        args += [sel, e_mat]
    grid_spec = pltpu.PrefetchScalarGridSpec(
        num_scalar_prefetch=5,
        grid=(batch, g, len(steps)),
        in_specs=in_specs,
        out_specs=pl.BlockSpec((tq, hpg * dh),
                               lambda b, gg, t, qi_r, ki_r, off_r, f_r, l_r: (b * nq + qi_r[t], gg)),
        scratch_shapes=[pltpu.VMEM((hpg, tq, 1), F32), pltpu.VMEM((hpg, tq, 1), F32),
                        pltpu.VMEM((hpg, tq, dh), F32)],
    )
    return pl.pallas_call(
        functools.partial(_band_kernel, use_sel=use_sel),
        grid_spec=grid_spec,
        out_shape=jax.ShapeDtypeStruct((batch * seq, g * hpg * dh), F32),
        compiler_params=pltpu.CompilerParams(
            dimension_semantics=("parallel", "parallel", "arbitrary"), vmem_limit_bytes=VMEM_LIMIT),
        name="sel_attn" if use_sel else "win_attn",
    )(jnp.asarray(qi), jnp.asarray(ki), jnp.asarray(off), jnp.asarray(first), jnp.asarray(last), *args)


def _out_kernel(x_ref, yr_ref, oc_ref, os_ref, ow_ref, gate_ref, w_ref, o_ref):
    dh = NSA_DH
    sig = _sigmoid(gate_ref[...])
    parts = [yr_ref[...].astype(BF16)]
    for h in range(NSA_HEADS):
        cs = slice(h * dh, (h + 1) * dh)
        c = h * N_BRANCH
        yh = (sig[:, c:c + 1] * oc_ref[:, cs] + sig[:, c + 1:c + 2] * os_ref[:, cs]
              + sig[:, c + 2:c + 3] * ow_ref[:, cs])
        parts.append(yh.astype(BF16))
    y = jnp.concatenate(parts, axis=-1)
    o_ref[...] = x_ref[...] + _dot(y, w_ref[...])


def _out_proj(x2d, y_ret, o_cmp, o_sel, o_win, proj, w_out):
    tok, d = x2d.shape
    tm = OUT_TM
    wmix = y_ret.shape[1]
    gcol = IN_OFF[11] // LANE
    row = lambda shape: pl.BlockSpec(shape, lambda i: (i, 0))
    return pl.pallas_call(
        _out_kernel,
        grid=(tok // tm,),
        in_specs=[row((tm, d)), row((tm, wmix)), row((tm, wmix)), row((tm, wmix)), row((tm, wmix)),
                  pl.BlockSpec((tm, LANE), lambda i: (i, gcol)),
                  pl.BlockSpec(w_out.shape, lambda i: (0, 0))],
        out_specs=row((tm, d)),
        out_shape=jax.ShapeDtypeStruct((tok, d), F32),
        compiler_params=pltpu.CompilerParams(
            dimension_semantics=("parallel",), vmem_limit_bytes=VMEM_LIMIT),
        name="out_proj",
    )(x2d, y_ret, o_cmp, o_sel, o_win, proj, w_out)


def _t5_bucket_of(rel):
    n = jnp.maximum(rel, 0)
    max_exact = REL_BUCKETS // 2
    nf = jnp.maximum(n, 1).astype(F32)
    large = max_exact + (jnp.log(nf / max_exact) / math.log(REL_MAX_DIST / max_exact)
                         * (REL_BUCKETS - max_exact)).astype(jnp.int32)
    large = jnp.minimum(large, REL_BUCKETS - 1)
    return jnp.where(n < max_exact, n, large)


def _bias_tiles(rel_bias, t, lo, hi):
    i = np.arange(t)[:, None]
    j = np.arange(t)[None, :]
    d = jnp.asarray(np.stack([o * t + i - j for o in range(3)]), jnp.int32)
    tab = rel_bias.astype(F32)[:, _t5_bucket_of(d)]
    return jnp.where((d >= lo) & (d < hi), tab, NEG)


def _token_mix(x2d, mix_norm, w_in, ret_gn_gain, pe_k, w1_k, w2_k, pe_v, w1_v, w2_v, w_out, rel_bias,
               batch, seq):
    d = x2d.shape[1]
    dh = NSA_DH
    w_pad = jnp.concatenate([w_in.astype(BF16), jnp.zeros((d, D_IN_PAD - D_IN), BF16)], axis=1)
    proj = _inproj(x2d, mix_norm, w_pad)
    y_ret = _retention(proj, ret_gn_gain, batch, seq)

    k_cmp, v_cmp = _compress(proj, pe_k, w1_k, w2_k, pe_v, w1_v, w2_v, batch, seq)
    ncp = k_cmp.shape[2]
    pos = jnp.arange(seq, dtype=jnp.int32)
    cmp_end = jnp.arange(ncp, dtype=jnp.int32) * CMP_STRIDE + CMP_LEN - 1
    bias_c = rel_bias.astype(F32)[:, _t5_bucket_of(pos[:, None] - cmp_end[None, :])]
    o_cmp, sel = _cmp_attention(proj, k_cmp, v_cmp, bias_c, batch, seq)

    t = ATT_T
    nq = seq // t
    e_np = np.zeros((SEL_PAD, seq), np.float32)
    e_np[np.arange(seq) // SEL_LEN, np.arange(seq)] = 1.0
    sel_steps = [(qi, ki, min(qi - ki, 2)) for qi in range(nq) for ki in range(qi + 1)]
    o_sel = _band_attention(proj, _bias_tiles(rel_bias, t, 0, 1 << 30), sel_steps,
                            IN_OFF[7] // dh, IN_OFF[8] // dh, batch, seq,
                            sel=sel, e_mat=jnp.asarray(e_np, BF16))
    nwt = WIN // t
    win_steps = [(qi, qi - o, o) for qi in range(nq) for o in range(nwt, -1, -1) if qi - o >= 0]
    o_win = _band_attention(proj, _bias_tiles(rel_bias, t, 0, WIN), win_steps,
                            IN_OFF[9] // dh, IN_OFF[10] // dh, batch, seq)
    return _out_proj(x2d, y_ret, o_cmp, o_sel, o_win, proj, w_out.astype(BF16))


def kernel(x, ffn1_norm, ffn1_w1, ffn1_w3, ffn1_w2, mix_norm, w_in, ret_gn_gain, cmp_pe_k, cmp_w1_k,
           cmp_w2_k, cmp_pe_v, cmp_w1_v, cmp_w2_v, w_out, ffn2_norm, ffn2_w1, ffn2_w3, ffn2_w2,
           rel_bias, final_norm):
    batch, seq, d = x.shape
    depth = ffn1_norm.shape[0]
    h = x.reshape(batch * seq, d)
    for l in range(depth):
        last = l == depth - 1
        h = _ffn(h, ffn1_norm[l], ffn1_w1[l].astype(BF16), ffn1_w3[l].astype(BF16),
                 ffn1_w2[l].astype(BF16), final_norm, False)
        h = _token_mix(h, mix_norm[l], w_in[l], ret_gn_gain[l], cmp_pe_k[l], cmp_w1_k[l], cmp_w2_k[l],
                       cmp_pe_v[l], cmp_w1_v[l], cmp_w2_v[l], w_out[l], rel_bias, batch, seq)
        h = _ffn(h, ffn2_norm[l], ffn2_w1[l].astype(BF16), ffn2_w3[l].astype(BF16),
                 ffn2_w2[l].astype(BF16), final_norm, last)
    if depth == 0:
        raise ValueError("depth must be positive")
    return h.reshape(batch, seq, d)
```

```python
import functools
import math

import jax
import jax.numpy as jnp
import numpy as np
from jax import lax
from jax.experimental import pallas as pl
from jax.experimental.pallas import tpu as pltpu

F32 = jnp.float32
BF16 = jnp.bfloat16

RET_HEADS = 4
RET_DK = 256
RET_DV = 256
ROPE_BASE = 10000.0
NSA_HEADS = 8
NSA_KV_GROUPS = 2
NSA_HPG = NSA_HEADS // NSA_KV_GROUPS
NSA_DH = 128
CMP_LEN = 32
CMP_STRIDE = 16
SEL_LEN = 64
SEL_TOPK = 16
WIN = 512
N_BRANCH = 3
REL_BUCKETS = 32
REL_MAX_DIST = 128
EPS = 1e-6
NEG = -1e30
FORCE = 1e4

RET_W = RET_HEADS * RET_DV
NSA_W = NSA_HEADS * NSA_DH
KV_W = NSA_KV_GROUPS * NSA_DH
IN_SPLITS = [RET_HEADS * RET_DK, RET_HEADS * RET_DK, RET_W, RET_W, NSA_W,
             KV_W, KV_W, KV_W, KV_W, KV_W, KV_W, NSA_HEADS * N_BRANCH]
D_IN = sum(IN_SPLITS)
IN_OFF = [sum(IN_SPLITS[:i]) for i in range(len(IN_SPLITS))]

LANE = 128
IN_TN = 768
D_IN_PAD = 9 * IN_TN
FFN_TM = 512
FFN_TF = 512
IN_TM = 1024
RET_C = 128
ATT_T = 256
ATT_TK = 512
ATT_NVAR = 4
OUT_TM = 256
N_CMP_PAD = 256
SEL_PAD = 128
VMEM_LIMIT = 52 * 1024 * 1024


def _dot(a, b):
    return jnp.dot(a, b, preferred_element_type=F32)


def _dot_nt(a, b):
    return lax.dot_general(a, b, (((1,), (1,)), ((), ())), preferred_element_type=F32)


def _dot_tn(a, b):
    return lax.dot_general(a, b, (((0,), (0,)), ((), ())), preferred_element_type=F32)


def _sigmoid(x):
    return 1.0 / (1.0 + jnp.exp(-x))


def _rms(x, g):
    ms = jnp.mean(x * x, axis=-1, keepdims=True)
    return x * lax.rsqrt(ms + EPS) * g


def _ffn_kernel(x_ref, g_ref, w1_ref, w3_ref, w2_ref, fg_ref, o_ref, n_ref, acc_ref, *, final_norm):
    j = pl.program_id(1)

    @pl.when(j == 0)
    def _():
        n_ref[...] = _rms(x_ref[...], g_ref[...]).astype(BF16)
        acc_ref[...] = jnp.zeros_like(acc_ref)

    n = n_ref[...]
    a = _dot(n, w1_ref[...])
    b = _dot(n, w3_ref[...])
    h = (a * _sigmoid(a) * b).astype(BF16)
    acc_ref[...] += _dot(h, w2_ref[...])

    @pl.when(j == pl.num_programs(1) - 1)
    def _():
        y = x_ref[...] + 0.5 * acc_ref[...]
        if final_norm:
            y = _rms(y, fg_ref[...])
        o_ref[...] = y


def _ffn(x2d, g, w1, w3, w2, fg, final_norm):
    tok, d = x2d.shape
    dff = w1.shape[1]
    tm, tf = FFN_TM, FFN_TF
    return pl.pallas_call(
        functools.partial(_ffn_kernel, final_norm=final_norm),
        grid=(tok // tm, dff // tf),
        in_specs=[
            pl.BlockSpec((tm, d), lambda i, j: (i, 0)),
            pl.BlockSpec((1, d), lambda i, j: (0, 0)),
            pl.BlockSpec((d, tf), lambda i, j: (0, j)),
            pl.BlockSpec((d, tf), lambda i, j: (0, j)),
            pl.BlockSpec((tf, d), lambda i, j: (j, 0)),
            pl.BlockSpec((1, d), lambda i, j: (0, 0)),
        ],
        out_specs=pl.BlockSpec((tm, d), lambda i, j: (i, 0)),
        out_shape=jax.ShapeDtypeStruct((tok, d), F32),
        scratch_shapes=[pltpu.VMEM((tm, d), BF16), pltpu.VMEM((tm, d), F32)],
        compiler_params=pltpu.CompilerParams(
            dimension_semantics=("parallel", "arbitrary"), vmem_limit_bytes=VMEM_LIMIT),
        name="ffn",
    )(x2d, g.reshape(1, d), w1, w3, w2, fg.reshape(1, d))


def _inproj_kernel(x_ref, g_ref, w_ref, o_ref, n_ref):
    @pl.when(pl.program_id(1) == 0)
    def _():
        n_ref[...] = _rms(x_ref[...], g_ref[...]).astype(BF16)

    o_ref[...] = _dot(n_ref[...], w_ref[...])


def _inproj(x2d, g, w_pad):
    tok, d = x2d.shape
    n_out = w_pad.shape[1]
    tm, tn = IN_TM, IN_TN
    return pl.pallas_call(
        _inproj_kernel,
        grid=(tok // tm, n_out // tn),
        in_specs=[
            pl.BlockSpec((tm, d), lambda i, j: (i, 0)),
            pl.BlockSpec((1, d), lambda i, j: (0, 0)),
            pl.BlockSpec((d, tn), lambda i, j: (0, j)),
        ],
        out_specs=pl.BlockSpec((tm, tn), lambda i, j: (i, j)),
        out_shape=jax.ShapeDtypeStruct((tok, n_out), F32),
        scratch_shapes=[pltpu.VMEM((tm, d), BF16)],
        compiler_params=pltpu.CompilerParams(
            dimension_semantics=("parallel", "arbitrary"), vmem_limit_bytes=VMEM_LIMIT),
        name="in_proj",
    )(x2d, g.reshape(1, d), w_pad)


def _ret_kernel(cd_ref, q_ref, k_ref, v_ref, g_ref, cos_ref, sin_ref, dec_ref, xi_ref, zeta_ref,
                gain_ref, o_ref, state_ref):
    @pl.when(pl.program_id(1) == 0)
    def _():
        state_ref[...] = jnp.zeros_like(state_ref)

    cos = cos_ref[...]
    sin = sin_ref[...]
    half = RET_DK // 2

    def rot(t):
        t1, t2 = t[:, :half], t[:, half:]
        return jnp.concatenate([t1 * cos - t2 * sin, t1 * sin + t2 * cos], axis=-1)

    for h in range(RET_HEADS):
        cs = slice(h * RET_DK, (h + 1) * RET_DK)
        q = rot(q_ref[:, cs])
        k = rot(k_ref[:, cs]) * (RET_DK ** -0.5)
        v = v_ref[:, cs].astype(BF16)
        qb = q.astype(BF16)
        s = _dot_nt(qb, k.astype(BF16)) * dec_ref[h]
        state = state_ref[h]
        y = _dot(s.astype(BF16), v) + _dot((q * xi_ref[h]).astype(BF16), state.astype(BF16))
        state_ref[h] = cd_ref[h] * state + _dot_tn((k * zeta_ref[h]).astype(BF16), v)
        mu = jnp.mean(y, axis=-1, keepdims=True)
        yc = y - mu
        var = jnp.mean(yc * yc, axis=-1, keepdims=True)
        yn = yc * lax.rsqrt(var + EPS) * gain_ref[:, cs]
        gate = g_ref[:, cs]
        o_ref[:, cs] = gate * _sigmoid(gate) * yn


def _retention(proj, gain, batch, seq):
    tok = proj.shape[0]
    c = RET_C
    n_chunks = seq // c
    h, dk = RET_HEADS, RET_DK
    half = dk // 2
    inv = ROPE_BASE ** (-jnp.arange(half, dtype=F32) / half)
    ang = jnp.arange(seq, dtype=F32)[:, None] * inv[None, :]
    cos, sin = jnp.cos(ang), jnp.sin(ang)
    log_g = jnp.log(1.0 - 2.0 ** (-5.0 - jnp.arange(h, dtype=F32)))
    idx = jnp.arange(c, dtype=F32)
    diff = idx[:, None] - idx[None, :]
    dec = jnp.where(diff >= 0.0, jnp.exp(jnp.maximum(diff, 0.0)[None] * log_g[:, None, None]), 0.0)
    xi = jnp.broadcast_to(jnp.exp((idx + 1.0)[None] * log_g[:, None])[:, :, None], (h, c, dk))
    zeta = jnp.broadcast_to(jnp.exp((c - 1.0 - idx)[None] * log_g[:, None])[:, :, None], (h, c, dk))
    cd = jnp.exp(c * log_g)

    w = h * dk
    row = lambda b, n: (b * n_chunks + n)
    return pl.pallas_call(
        _ret_kernel,
        grid=(batch, n_chunks),
        in_specs=[
            pl.BlockSpec(memory_space=pltpu.SMEM),
            pl.BlockSpec((c, w), lambda b, n: (row(b, n), 0)),
            pl.BlockSpec((c, w), lambda b, n: (row(b, n), 1)),
            pl.BlockSpec((c, w), lambda b, n: (row(b, n), 2)),
            pl.BlockSpec((c, w), lambda b, n: (row(b, n), 3)),
            pl.BlockSpec((c, half), lambda b, n: (n, 0)),
            pl.BlockSpec((c, half), lambda b, n: (n, 0)),
            pl.BlockSpec((h, c, c), lambda b, n: (0, 0, 0)),
            pl.BlockSpec((h, c, dk), lambda b, n: (0, 0, 0)),
            pl.BlockSpec((h, c, dk), lambda b, n: (0, 0, 0)),
            pl.BlockSpec((1, w), lambda b, n: (0, 0)),
        ],
        out_specs=pl.BlockSpec((c, w), lambda b, n: (row(b, n), 0)),
        out_shape=jax.ShapeDtypeStruct((tok, w), F32),
        scratch_shapes=[pltpu.VMEM((h, dk, RET_DV), F32)],
        compiler_params=pltpu.CompilerParams(
            dimension_semantics=("parallel", "arbitrary"), vmem_limit_bytes=VMEM_LIMIT),
        name="retention",
    )(cd, proj, proj, proj, proj, cos, sin, dec, xi, zeta, gain.reshape(1, w))


def _compress_kernel(k_ref, v_ref, pek_ref, w1k_ref, w1kf_ref, w2k_ref, pev_ref, w1v_ref, w1vf_ref,
                     w2v_ref, ok_ref, ov_ref):
    dh = NSA_DH
    nblk = k_ref.shape[0] // CMP_STRIDE

    def one(x_ref, pe_ref, w1_ref, w1f_ref, w2_ref, o_ref):
        ab = jnp.zeros((nblk, 2 * dh), F32)
        for l in range(CMP_STRIDE):
            rows = x_ref[pl.ds(l, nblk, stride=CMP_STRIDE), :].astype(BF16)
            ab = ab + _dot(rows, w1_ref[l])
        a, b = ab[:, :dh], ab[:, dh:]
        b_next = pltpu.roll(b, nblk - 1, axis=0)
        ridx = lax.broadcasted_iota(jnp.int32, (nblk, dh), 0)
        b_next = jnp.where(ridx < nblk - 1, b_next, 0.0)
        pe_term = _dot(pe_ref[...], w1f_ref[...])[0:1, :]
        hdn = a + b_next + pe_term
        o_ref[0, 0] = _dot((hdn * _sigmoid(hdn)).astype(BF16), w2_ref[...])

    one(k_ref, pek_ref, w1k_ref, w1kf_ref, w2k_ref, ok_ref)
    one(v_ref, pev_ref, w1v_ref, w1vf_ref, w2v_ref, ov_ref)


def _compress(proj, pe_k, w1_k, w2_k, pe_v, w1_v, w2_v, batch, seq):
    dh, g = NSA_DH, NSA_KV_GROUPS
    nblk = seq // CMP_STRIDE
    half = CMP_LEN // 2

    def prep(pe, w1, w2):
        w1b = w1.astype(BF16)
        w1_pair = jnp.concatenate([w1b[:half], w1b[half:]], axis=-1)
        pe_flat = jnp.broadcast_to(pe.reshape(1, CMP_LEN * dh), (8, CMP_LEN * dh)).astype(BF16)
        return pe_flat, w1_pair, w1b.reshape(CMP_LEN * dh, dh), w2.astype(BF16)

    args_k = prep(pe_k, w1_k, w2_k)
    args_v = prep(pe_v, w1_v, w2_v)
    kcol = IN_OFF[5] // dh
    vcol = IN_OFF[6] // dh
    const = lambda shape: pl.BlockSpec(shape, lambda b, gg: (0,) * len(shape))
    wspecs = [const((8, CMP_LEN * dh)), const((half, dh, 2 * dh)), const((CMP_LEN * dh, dh)),
              const((dh, dh))]
    out_spec = pl.BlockSpec((1, 1, nblk, dh), lambda b, gg: (b, gg, 0, 0))
    return pl.pallas_call(
        _compress_kernel,
        grid=(batch, g),
        in_specs=[pl.BlockSpec((seq, dh), lambda b, gg: (b, kcol + gg)),
                  pl.BlockSpec((seq, dh), lambda b, gg: (b, vcol + gg))] + wspecs + wspecs,
        out_specs=[out_spec, out_spec],
        out_shape=[jax.ShapeDtypeStruct((batch, g, nblk, dh), F32)] * 2,
        compiler_params=pltpu.CompilerParams(
            dimension_semantics=("parallel", "parallel"), vmem_limit_bytes=VMEM_LIMIT),
        name="compress",
    )(proj, proj, *args_k, *args_v)


def _cmpattn_kernel(q_ref, kc_ref, vc_ref, bias_ref, ovt_ref, o_ref, sel_ref):
    tq = q_ref.shape[0]
    ncp = kc_ref.shape[2]
    dh = NSA_DH
    s0 = pl.program_id(2) * tq
    pos_r = s0 + lax.broadcasted_iota(jnp.int32, (tq, ncp), 0)
    n_c = lax.broadcasted_iota(jnp.int32, (tq, ncp), 1)
    mask = pos_r >= n_c * CMP_STRIDE + (CMP_LEN - 1)
    kc = kc_ref[0, 0].astype(BF16)
    vc = vc_ref[0, 0].astype(BF16)
    scale = dh ** -0.5
    psum = jnp.zeros((tq, ncp), F32)
    for hh in range(NSA_HPG):
        cs = slice(hh * dh, (hh + 1) * dh)
        q = q_ref[:, cs].astype(BF16)
        logits = _dot_nt(q, kc) * scale + bias_ref[hh]
        lm = jnp.where(mask, logits, NEG)
        m = jnp.max(lm, axis=-1, keepdims=True)
        e = jnp.where(mask, jnp.exp(lm - m), 0.0)
        den = jnp.sum(e, axis=-1, keepdims=True)
        p = e / jnp.where(den > 0.0, den, 1.0)
        o_ref[:, cs] = _dot(p.astype(BF16), vc)
        psum = psum + p
    nsel = ovt_ref.shape[0]
    imp_t = _dot_nt(ovt_ref[...], psum.astype(BF16))
    jb = lax.broadcasted_iota(jnp.int32, (nsel, tq), 0)
    pos = s0 + lax.broadcasted_iota(jnp.int32, (nsel, tq), 1)
    cur = jnp.right_shift(pos, SEL_LEN.bit_length() - 1)
    causal = jb * SEL_LEN <= pos
    forced = (jb == 0) | (jb == cur) | (jb == cur - 1)
    score = jnp.where(forced, FORCE, jnp.where(causal, imp_t, NEG))
    rank = jnp.zeros((nsel, tq), jnp.int32)
    for kk in range(nsel):
        row = score[kk:kk + 1, :]
        beats = (row > score) | ((row == score) & (jb > kk))
        rank = rank + jnp.where(beats, 1, 0)
    selb = jnp.where((rank < SEL_TOPK) & causal, 0.0, NEG)
    selb = jnp.concatenate([selb, jnp.zeros((SEL_PAD - nsel, tq), F32)], axis=0)
    sel_ref[0, 0] = selb.astype(BF16)


def _cmp_attention(proj, k_cmp, v_cmp, bias_c, batch, seq):
    dh, g, hpg = NSA_DH, NSA_KV_GROUPS, NSA_HPG
    tq = ATT_T
    nq = seq // tq
    ncp = k_cmp.shape[2]
    nsel = seq // SEL_LEN
    n_cmp = (seq - CMP_LEN) // CMP_STRIDE + 1
    cmp_idx = np.arange(n_cmp)[:, None] * CMP_STRIDE + np.arange(CMP_LEN)[None, :]
    overlap = ((cmp_idx // SEL_LEN)[:, :, None] == np.arange(nsel)[None, None, :]).sum(1) / CMP_LEN
    ovt = np.zeros((nsel, ncp), np.float32)
    ovt[:, :n_cmp] = overlap.T
    qcol = IN_OFF[4] // (hpg * dh)
    return pl.pallas_call(
        _cmpattn_kernel,
        grid=(batch, g, nq),
        in_specs=[
            pl.BlockSpec((tq, hpg * dh), lambda b, gg, t: (b * nq + t, qcol + gg)),
            pl.BlockSpec((1, 1, ncp, dh), lambda b, gg, t: (b, gg, 0, 0)),
            pl.BlockSpec((1, 1, ncp, dh), lambda b, gg, t: (b, gg, 0, 0)),
            pl.BlockSpec((hpg, tq, ncp), lambda b, gg, t: (gg, t, 0)),
            pl.BlockSpec((nsel, ncp), lambda b, gg, t: (0, 0)),
        ],
        out_specs=[
            pl.BlockSpec((tq, hpg * dh), lambda b, gg, t: (b * nq + t, gg)),
            pl.BlockSpec((1, 1, SEL_PAD, tq), lambda b, gg, t: (b, gg, 0, t)),
        ],
        out_shape=[jax.ShapeDtypeStruct((batch * seq, g * hpg * dh), F32),
                   jax.ShapeDtypeStruct((batch, g, SEL_PAD, seq), BF16)],
        compiler_params=pltpu.CompilerParams(
            dimension_semantics=("parallel", "parallel", "parallel"), vmem_limit_bytes=VMEM_LIMIT),
        name="cmp_attn",
    )(proj, k_cmp, v_cmp, bias_c, jnp.asarray(ovt, BF16))


def _band_kernel(qi_ref, ki_ref, var_ref, first_ref, last_ref, q_ref, k_ref, v_ref, bias_ref, *rest,
                 use_sel):
    if use_sel:
        selb_ref, e_ref, o_ref, qt_ref, m_ref, l_ref, acc_ref = rest
    else:
        o_ref, qt_ref, m_ref, l_ref, acc_ref = rest
    t = pl.program_id(2)
    dh = NSA_DH
    tq = q_ref.shape[0]

    @pl.when(first_ref[t] == 1)
    def _():
        m_ref[...] = jnp.full_like(m_ref, NEG)
        l_ref[...] = jnp.zeros_like(l_ref)
        acc_ref[...] = jnp.zeros_like(acc_ref)
        scale = dh ** -0.5
        for hh in range(NSA_HPG):
            qt_ref[0:dh, hh * tq:(hh + 1) * tq] = (q_ref[:, hh * dh:(hh + 1) * dh] * scale).T.astype(BF16)
            if use_sel:
                qt_ref[dh:, hh * tq:(hh + 1) * tq] = selb_ref[0, 0]

    k = k_ref[...].astype(BF16)
    if use_sel:
        k = jnp.concatenate([k, e_ref[...]], axis=1)
    s = _dot(k, qt_ref[...]) + bias_ref[0, 0]
    m_prev = m_ref[...]
    m_new = jnp.maximum(m_prev, jnp.max(s, axis=0, keepdims=True))
    alpha = jnp.exp(m_prev - m_new)
    p = jnp.exp(s - m_new)
    l_ref[...] = alpha * l_ref[...] + jnp.sum(p, axis=0, keepdims=True)
    vt = v_ref[...].T.astype(BF16)
    acc_ref[...] = alpha * acc_ref[...] + _dot(vt, p.astype(BF16))
    m_ref[...] = m_new

    @pl.when(last_ref[t] == 1)
    def _():
        o = acc_ref[...] / l_ref[...]
        for hh in range(NSA_HPG):
            o_ref[:, hh * dh:(hh + 1) * dh] = o[:, hh * tq:(hh + 1) * tq].T


def _band_steps(seq, window):
    tq, tk = ATT_T, ATT_TK
    steps = []
    for qi in range(seq // tq):
        k_hi = qi * tq // tk
        k_lo = max(0, (qi * tq - window + 1) // tk) if window else 0
        steps += [(qi, ki, min((qi * tq - ki * tk) // tq, ATT_NVAR - 1)) for ki in range(k_lo, k_hi + 1)]
    return steps


def _band_attention(proj, bias_tiles, steps, kcol, vcol, batch, seq, selb=None, e_mat=None):
    dh, g, hpg = NSA_DH, NSA_KV_GROUPS, NSA_HPG
    tq, tk = ATT_T, ATT_TK
    nq, nk = seq // tq, seq // tk
    qi = np.array([s[0] for s in steps], np.int32)
    ki = np.array([s[1] for s in steps], np.int32)
    var = np.array([s[2] for s in steps], np.int32)
    first = np.concatenate([[1], (qi[1:] != qi[:-1]).astype(np.int32)]).astype(np.int32)
    last = np.concatenate([(qi[1:] != qi[:-1]).astype(np.int32), [1]]).astype(np.int32)
    qcol = IN_OFF[4] // (hpg * dh)
    use_sel = selb is not None
    in_specs = [
        pl.BlockSpec((tq, hpg * dh), lambda b, gg, t, qi_r, ki_r, v_r, f_r, l_r: (b * nq + qi_r[t], qcol + gg)),
        pl.BlockSpec((tk, dh), lambda b, gg, t, qi_r, ki_r, v_r, f_r, l_r: (b * nk + ki_r[t], kcol + gg)),
        pl.BlockSpec((tk, dh), lambda b, gg, t, qi_r, ki_r, v_r, f_r, l_r: (b * nk + ki_r[t], vcol + gg)),
        pl.BlockSpec((1, 1, tk, hpg * tq), lambda b, gg, t, qi_r, ki_r, v_r, f_r, l_r: (gg, v_r[t], 0, 0)),
    ]
    args = [proj, proj, proj, bias_tiles]
    if use_sel:
        in_specs += [
            pl.BlockSpec((1, 1, SEL_PAD, tq), lambda b, gg, t, qi_r, ki_r, v_r, f_r, l_r: (b, gg, 0, qi_r[t])),
            pl.BlockSpec((tk, SEL_PAD), lambda b, gg, t, qi_r, ki_r, v_r, f_r, l_r: (ki_r[t], 0)),
        ]
        args += [selb, e_mat]
    kdim = dh + SEL_PAD if use_sel else dh
    grid_spec = pltpu.PrefetchScalarGridSpec(
        num_scalar_prefetch=5,
        grid=(batch, g, len(steps)),
        in_specs=in_specs,
        out_specs=pl.BlockSpec((tq, hpg * dh),
                               lambda b, gg, t, qi_r, ki_r, v_r, f_r, l_r: (b * nq + qi_r[t], gg)),
        scratch_shapes=[pltpu.VMEM((kdim, hpg * tq), BF16), pltpu.VMEM((1, hpg * tq), F32),
                        pltpu.VMEM((1, hpg * tq), F32), pltpu.VMEM((dh, hpg * tq), F32)],
    )
    return pl.pallas_call(
        functools.partial(_band_kernel, use_sel=use_sel),
        grid_spec=grid_spec,
        out_shape=jax.ShapeDtypeStruct((batch * seq, g * hpg * dh), F32),
        compiler_params=pltpu.CompilerParams(
            dimension_semantics=("parallel", "parallel", "arbitrary"), vmem_limit_bytes=VMEM_LIMIT),
        name="sel_attn" if use_sel else "win_attn",
    )(jnp.asarray(qi), jnp.asarray(ki), jnp.asarray(var), jnp.asarray(first), jnp.asarray(last), *args)


def _out_kernel(x_ref, yr_ref, oc_ref, os_ref, ow_ref, gate_ref, w_ref, o_ref):
    dh = NSA_DH
    sig = _sigmoid(gate_ref[...])
    parts = [yr_ref[...].astype(BF16)]
    for h in range(NSA_HEADS):
        cs = slice(h * dh, (h + 1) * dh)
        c = h * N_BRANCH
        yh = (sig[:, c:c + 1] * oc_ref[:, cs] + sig[:, c + 1:c + 2] * os_ref[:, cs]
              + sig[:, c + 2:c + 3] * ow_ref[:, cs])
        parts.append(yh.astype(BF16))
    y = jnp.concatenate(parts, axis=-1)
    o_ref[...] = x_ref[...] + _dot(y, w_ref[...])


def _out_proj(x2d, y_ret, o_cmp, o_sel, o_win, proj, w_out):
    tok, d = x2d.shape
    tm = OUT_TM
    wmix = y_ret.shape[1]
    gcol = IN_OFF[11] // LANE
    row = lambda shape: pl.BlockSpec(shape, lambda i: (i, 0))
    return pl.pallas_call(
        _out_kernel,
        grid=(tok // tm,),
        in_specs=[row((tm, d)), row((tm, wmix)), row((tm, wmix)), row((tm, wmix)), row((tm, wmix)),
                  pl.BlockSpec((tm, LANE), lambda i: (i, gcol)),
                  pl.BlockSpec(w_out.shape, lambda i: (0, 0))],
        out_specs=row((tm, d)),
        out_shape=jax.ShapeDtypeStruct((tok, d), F32),
        compiler_params=pltpu.CompilerParams(
            dimension_semantics=("parallel",), vmem_limit_bytes=VMEM_LIMIT),
        name="out_proj",
    )(x2d, y_ret, o_cmp, o_sel, o_win, proj, w_out)


def _t5_bucket_of(rel):
    n = jnp.maximum(rel, 0)
    max_exact = REL_BUCKETS // 2
    nf = jnp.maximum(n, 1).astype(F32)
    large = max_exact + (jnp.log(nf / max_exact) / math.log(REL_MAX_DIST / max_exact)
                         * (REL_BUCKETS - max_exact)).astype(jnp.int32)
    large = jnp.minimum(large, REL_BUCKETS - 1)
    return jnp.where(n < max_exact, n, large)


def _bias_lookup(rel_bias, rel):
    bucket = _t5_bucket_of(rel)[None]
    tab = rel_bias.astype(F32).reshape((rel_bias.shape[0], REL_BUCKETS) + (1,) * rel.ndim)
    out = jnp.zeros((rel_bias.shape[0],) + rel.shape, F32)
    for b in range(REL_BUCKETS):
        out = jnp.where(bucket == b, tab[:, b], out)
    return out


def _bias_tiles(rel_bias, lo, hi):
    tq, tk, g, hpg = ATT_T, ATT_TK, NSA_KV_GROUPS, NSA_HPG
    v = np.arange(ATT_NVAR)[:, None, None]
    j = np.arange(tk)[None, :, None]
    i = np.arange(tq)[None, None, :]
    d = jnp.asarray(v * tq + i - j, jnp.int32)
    tab = jnp.where((d >= lo) & (d < hi), _bias_lookup(rel_bias, d), NEG)
    tab = tab.reshape(g, hpg, ATT_NVAR, tk, tq).transpose(0, 2, 3, 1, 4)
    return tab.reshape(g, ATT_NVAR, tk, hpg * tq)


def _token_mix(x2d, mix_norm, w_in, ret_gn_gain, pe_k, w1_k, w2_k, pe_v, w1_v, w2_v, w_out, rel_bias,
               batch, seq):
    d = x2d.shape[1]
    dh = NSA_DH
    w_pad = jnp.concatenate([w_in.astype(BF16), jnp.zeros((d, D_IN_PAD - D_IN), BF16)], axis=1)
    proj = _inproj(x2d, mix_norm, w_pad)
    y_ret = _retention(proj, ret_gn_gain, batch, seq)

    k_cmp, v_cmp = _compress(proj, pe_k, w1_k, w2_k, pe_v, w1_v, w2_v, batch, seq)
    ncp = k_cmp.shape[2]
    pos = jnp.arange(seq, dtype=jnp.int32)
    cmp_end = jnp.arange(ncp, dtype=jnp.int32) * CMP_STRIDE + CMP_LEN - 1
    bias_c = _bias_lookup(rel_bias, pos[:, None] - cmp_end[None, :])
    o_cmp, selb = _cmp_attention(proj, k_cmp, v_cmp, bias_c, batch, seq)

    e_np = np.zeros((seq, SEL_PAD), np.float32)
    e_np[np.arange(seq), np.arange(seq) // SEL_LEN] = 1.0
    o_sel = _band_attention(proj, _bias_tiles(rel_bias, 0, 1 << 30), _band_steps(seq, 0),
                            IN_OFF[7] // dh, IN_OFF[8] // dh, batch, seq,
                            selb=selb, e_mat=jnp.asarray(e_np, BF16))
    o_win = _band_attention(proj, _bias_tiles(rel_bias, 0, WIN), _band_steps(seq, WIN),
                            IN_OFF[9] // dh, IN_OFF[10] // dh, batch, seq)
    return _out_proj(x2d, y_ret, o_cmp, o_sel, o_win, proj, w_out.astype(BF16))


def kernel(x, ffn1_norm, ffn1_w1, ffn1_w3, ffn1_w2, mix_norm, w_in, ret_gn_gain, cmp_pe_k, cmp_w1_k,
           cmp_w2_k, cmp_pe_v, cmp_w1_v, cmp_w2_v, w_out, ffn2_norm, ffn2_w1, ffn2_w3, ffn2_w2,
           rel_bias, final_norm):
    batch, seq, d = x.shape
    depth = ffn1_norm.shape[0]
    h = x.reshape(batch * seq, d)
    for l in range(depth):
        last = l == depth - 1
        h = _ffn(h, ffn1_norm[l], ffn1_w1[l].astype(BF16), ffn1_w3[l].astype(BF16),
                 ffn1_w2[l].astype(BF16), final_norm, False)
        h = _token_mix(h, mix_norm[l], w_in[l], ret_gn_gain[l], cmp_pe_k[l], cmp_w1_k[l], cmp_w2_k[l],
                       cmp_pe_v[l], cmp_w1_v[l], cmp_w2_v[l], w_out[l], rel_bias, batch, seq)
        h = _ffn(h, ffn2_norm[l], ffn2_w1[l].astype(BF16), ffn2_w3[l].astype(BF16),
                 ffn2_w2[l].astype(BF16), final_norm, last)
    if depth == 0:
        raise ValueError("depth must be positive")
    return h.reshape(batch, seq, d)
```

```python
import functools
import math

import jax
import jax.numpy as jnp
import numpy as np
from jax import lax
from jax.experimental import pallas as pl
from jax.experimental.pallas import tpu as pltpu

F32 = jnp.float32
BF16 = jnp.bfloat16

RET_HEADS = 4
RET_DK = 256
RET_DV = 256
ROPE_BASE = 10000.0
NSA_HEADS = 8
NSA_KV_GROUPS = 2
NSA_HPG = NSA_HEADS // NSA_KV_GROUPS
NSA_DH = 128
CMP_LEN = 32
CMP_STRIDE = 16
SEL_LEN = 64
SEL_TOPK = 16
WIN = 512
N_BRANCH = 3
REL_BUCKETS = 32
REL_MAX_DIST = 128
EPS = 1e-6
NEG = -1e30
FORCE = 1e4

RET_W = RET_HEADS * RET_DV
NSA_W = NSA_HEADS * NSA_DH
KV_W = NSA_KV_GROUPS * NSA_DH
IN_SPLITS = [RET_HEADS * RET_DK, RET_HEADS * RET_DK, RET_W, RET_W, NSA_W,
             KV_W, KV_W, KV_W, KV_W, KV_W, KV_W, NSA_HEADS * N_BRANCH]
D_IN = sum(IN_SPLITS)
IN_OFF = [sum(IN_SPLITS[:i]) for i in range(len(IN_SPLITS))]

LANE = 128
IN_TN = 768
D_IN_PAD = 9 * IN_TN
FFN_TM = 1024
FFN_TF = 512
IN_TM = 1024
RET_C = 128
CMP_TQ = 256
ATT_T = 512
ATT_TK = 512
ATT_NVAR = 3
OUT_TM = 256
SEL_PAD = 128
CMP_BAND = 64
VMEM_LIMIT = 56 * 1024 * 1024
LOG2E = math.log2(math.e)


def _dot(a, b):
    return jnp.dot(a, b, preferred_element_type=F32)


def _dot_nt(a, b):
    return lax.dot_general(a, b, (((1,), (1,)), ((), ())), preferred_element_type=F32)


def _dot_tn(a, b):
    return lax.dot_general(a, b, (((0,), (0,)), ((), ())), preferred_element_type=F32)


def _sigmoid(x):
    return 1.0 / (1.0 + jnp.exp(-x))


def _rms(x, g):
    ms = jnp.mean(x * x, axis=-1, keepdims=True)
    return x * lax.rsqrt(ms + EPS) * g


def _ffn_kernel(x_ref, g_ref, w1_ref, w3_ref, w2_ref, fg_ref, o_ref, n_ref, *, final_norm):
    j = pl.program_id(1)

    @pl.when(j == 0)
    def _():
        x = x_ref[...]
        n_ref[...] = _rms(x, g_ref[...]).astype(BF16)
        o_ref[...] = x

    n = n_ref[...]
    a = _dot(n, w1_ref[...])
    b = _dot(n, w3_ref[...])
    h = (0.5 * a * _sigmoid(a) * b).astype(BF16)
    o_ref[...] += _dot(h, w2_ref[...])

    if final_norm:
        @pl.when(j == pl.num_programs(1) - 1)
        def _():
            o_ref[...] = _rms(o_ref[...], fg_ref[...])


def _ffn(x2d, g, w1, w3, w2, fg, final_norm):
    tok, d = x2d.shape
    dff = w1.shape[1]
    tm, tf = FFN_TM, FFN_TF
    return pl.pallas_call(
        functools.partial(_ffn_kernel, final_norm=final_norm),
        grid=(tok // tm, dff // tf),
        in_specs=[
            pl.BlockSpec((tm, d), lambda i, j: (i, 0)),
            pl.BlockSpec((1, d), lambda i, j: (0, 0)),
            pl.BlockSpec((d, tf), lambda i, j: (0, j)),
            pl.BlockSpec((d, tf), lambda i, j: (0, j)),
            pl.BlockSpec((tf, d), lambda i, j: (j, 0)),
            pl.BlockSpec((1, d), lambda i, j: (0, 0)),
        ],
        out_specs=pl.BlockSpec((tm, d), lambda i, j: (i, 0)),
        out_shape=jax.ShapeDtypeStruct((tok, d), F32),
        scratch_shapes=[pltpu.VMEM((tm, d), BF16)],
        compiler_params=pltpu.CompilerParams(
            dimension_semantics=("parallel", "arbitrary"), vmem_limit_bytes=VMEM_LIMIT),
        name="ffn",
    )(x2d, g.reshape(1, d), w1, w3, w2, fg.reshape(1, d))


def _inproj_kernel(x_ref, g_ref, w_ref, o_ref, n_ref):
    @pl.when(pl.program_id(1) == 0)
    def _():
        n_ref[...] = _rms(x_ref[...], g_ref[...]).astype(BF16)

    o_ref[...] = _dot(n_ref[...], w_ref[...])


def _inproj(x2d, g, w_pad):
    tok, d = x2d.shape
    n_out = w_pad.shape[1]
    tm, tn = IN_TM, IN_TN
    return pl.pallas_call(
        _inproj_kernel,
        grid=(tok // tm, n_out // tn),
        in_specs=[
            pl.BlockSpec((tm, d), lambda i, j: (i, 0)),
            pl.BlockSpec((1, d), lambda i, j: (0, 0)),
            pl.BlockSpec((d, tn), lambda i, j: (0, j)),
        ],
        out_specs=pl.BlockSpec((tm, tn), lambda i, j: (i, j)),
        out_shape=jax.ShapeDtypeStruct((tok, n_out), F32),
        scratch_shapes=[pltpu.VMEM((tm, d), BF16)],
        compiler_params=pltpu.CompilerParams(
            dimension_semantics=("parallel", "arbitrary"), vmem_limit_bytes=VMEM_LIMIT),
        name="in_proj",
    )(x2d, g.reshape(1, d), w_pad)


def _ret_kernel(cd_ref, q_ref, k_ref, v_ref, g_ref, cos_ref, sin_ref, dec_ref, xi_ref, zeta_ref,
                gain_ref, o_ref, state_ref):
    @pl.when(pl.program_id(1) == 0)
    def _():
        state_ref[...] = jnp.zeros_like(state_ref)

    cos = cos_ref[...]
    sin = sin_ref[...]
    half = RET_DK // 2

    def rot(t):
        t1, t2 = t[:, :half], t[:, half:]
        return jnp.concatenate([t1 * cos - t2 * sin, t1 * sin + t2 * cos], axis=-1)

    for h in range(RET_HEADS):
        cs = slice(h * RET_DK, (h + 1) * RET_DK)
        q = rot(q_ref[:, cs])
        k = rot(k_ref[:, cs]) * (RET_DK ** -0.5)
        v = v_ref[:, cs].astype(BF16)
        qb = q.astype(BF16)
        s = _dot_nt(qb, k.astype(BF16)) * dec_ref[h]
        state = state_ref[h]
        y = _dot(s.astype(BF16), v) + _dot((q * xi_ref[h]).astype(BF16), state.astype(BF16))
        state_ref[h] = cd_ref[h] * state + _dot_tn((k * zeta_ref[h]).astype(BF16), v)
        mu = jnp.mean(y, axis=-1, keepdims=True)
        yc = y - mu
        var = jnp.mean(yc * yc, axis=-1, keepdims=True)
        yn = yc * lax.rsqrt(var + EPS) * gain_ref[:, cs]
        gate = g_ref[:, cs]
        o_ref[:, cs] = gate * _sigmoid(gate) * yn


def _retention(proj, gain, batch, seq):
    tok = proj.shape[0]
    c = RET_C
    n_chunks = seq // c
    h, dk = RET_HEADS, RET_DK
    half = dk // 2
    inv = ROPE_BASE ** (-jnp.arange(half, dtype=F32) / half)
    ang = jnp.arange(seq, dtype=F32)[:, None] * inv[None, :]
    cos, sin = jnp.cos(ang), jnp.sin(ang)
    log_g = jnp.log(1.0 - 2.0 ** (-5.0 - jnp.arange(h, dtype=F32)))
    idx = jnp.arange(c, dtype=F32)
    diff = idx[:, None] - idx[None, :]
    dec = jnp.where(diff >= 0.0, jnp.exp(jnp.maximum(diff, 0.0)[None] * log_g[:, None, None]), 0.0)
    xi = jnp.broadcast_to(jnp.exp((idx + 1.0)[None] * log_g[:, None])[:, :, None], (h, c, dk))
    zeta = jnp.broadcast_to(jnp.exp((c - 1.0 - idx)[None] * log_g[:, None])[:, :, None], (h, c, dk))
    cd = jnp.exp(c * log_g)

    w = h * dk
    row = lambda b, n: (b * n_chunks + n)
    return pl.pallas_call(
        _ret_kernel,
        grid=(batch, n_chunks),
        in_specs=[
            pl.BlockSpec(memory_space=pltpu.SMEM),
            pl.BlockSpec((c, w), lambda b, n: (row(b, n), 0)),
            pl.BlockSpec((c, w), lambda b, n: (row(b, n), 1)),
            pl.BlockSpec((c, w), lambda b, n: (row(b, n), 2)),
            pl.BlockSpec((c, w), lambda b, n: (row(b, n), 3)),
            pl.BlockSpec((c, half), lambda b, n: (n, 0)),
            pl.BlockSpec((c, half), lambda b, n: (n, 0)),
            pl.BlockSpec((h, c, c), lambda b, n: (0, 0, 0)),
            pl.BlockSpec((h, c, dk), lambda b, n: (0, 0, 0)),
            pl.BlockSpec((h, c, dk), lambda b, n: (0, 0, 0)),
            pl.BlockSpec((1, w), lambda b, n: (0, 0)),
        ],
        out_specs=pl.BlockSpec((c, w), lambda b, n: (row(b, n), 0)),
        out_shape=jax.ShapeDtypeStruct((tok, w), F32),
        scratch_shapes=[pltpu.VMEM((h, dk, RET_DV), F32)],
        compiler_params=pltpu.CompilerParams(
            dimension_semantics=("parallel", "arbitrary"), vmem_limit_bytes=VMEM_LIMIT),
        name="retention",
    )(cd, proj, proj, proj, proj, cos, sin, dec, xi, zeta, gain.reshape(1, w))


def _compress_kernel(k_ref, v_ref, pek_ref, w1k_ref, w1kf_ref, w2k_ref, pev_ref, w1v_ref, w1vf_ref,
                     w2v_ref, ok_ref, ov_ref):
    dh = NSA_DH
    nblk = k_ref.shape[0] // CMP_STRIDE

    def one(x_ref, pe_ref, w1_ref, w1f_ref, w2_ref, o_ref):
        ab = jnp.zeros((nblk, 2 * dh), F32)
        for l in range(CMP_STRIDE):
            rows = x_ref[pl.ds(l, nblk, stride=CMP_STRIDE), :].astype(BF16)
            ab = ab + _dot(rows, w1_ref[l])
        a, b = ab[:, :dh], ab[:, dh:]
        b_next = pltpu.roll(b, nblk - 1, axis=0)
        ridx = lax.broadcasted_iota(jnp.int32, (nblk, dh), 0)
        b_next = jnp.where(ridx < nblk - 1, b_next, 0.0)
        pe_term = _dot(pe_ref[...], w1f_ref[...])[0:1, :]
        hdn = a + b_next + pe_term
        o_ref[0, 0] = _dot((hdn * _sigmoid(hdn)).astype(BF16), w2_ref[...])

    one(k_ref, pek_ref, w1k_ref, w1kf_ref, w2k_ref, ok_ref)
    one(v_ref, pev_ref, w1v_ref, w1vf_ref, w2v_ref, ov_ref)


def _compress(proj, pe_k, w1_k, w2_k, pe_v, w1_v, w2_v, batch, seq):
    dh, g = NSA_DH, NSA_KV_GROUPS
    nblk = seq // CMP_STRIDE
    half = CMP_LEN // 2

    def prep(pe, w1, w2):
        w1b = w1.astype(BF16)
        w1_pair = jnp.concatenate([w1b[:half], w1b[half:]], axis=-1)
        pe_flat = jnp.broadcast_to(pe.reshape(1, CMP_LEN * dh), (8, CMP_LEN * dh)).astype(BF16)
        return pe_flat, w1_pair, w1b.reshape(CMP_LEN * dh, dh), w2.astype(BF16)

    args_k = prep(pe_k, w1_k, w2_k)
    args_v = prep(pe_v, w1_v, w2_v)
    kcol = IN_OFF[5] // dh
    vcol = IN_OFF[6] // dh
    const = lambda shape: pl.BlockSpec(shape, lambda b, gg: (0,) * len(shape))
    wspecs = [const((8, CMP_LEN * dh)), const((half, dh, 2 * dh)), const((CMP_LEN * dh, dh)),
              const((dh, dh))]
    out_spec = pl.BlockSpec((1, 1, nblk, dh), lambda b, gg: (b, gg, 0, 0))
    return pl.pallas_call(
        _compress_kernel,
        grid=(batch, g),
        in_specs=[pl.BlockSpec((seq, dh), lambda b, gg: (b, kcol + gg)),
                  pl.BlockSpec((seq, dh), lambda b, gg: (b, vcol + gg))] + wspecs + wspecs,
        out_specs=[out_spec, out_spec],
        out_shape=[jax.ShapeDtypeStruct((batch, g, nblk, dh), F32)] * 2,
        compiler_params=pltpu.CompilerParams(
            dimension_semantics=("parallel", "parallel"), vmem_limit_bytes=VMEM_LIMIT),
        name="compress",
    )(proj, proj, *args_k, *args_v)


def _cmpattn_kernel(q_ref, kc_ref, vc_ref, tb_ref, pp_ref, ovt_ref, o_ref, sel_ref):
    tq = q_ref.shape[0]
    ncp = kc_ref.shape[2]
    dh = NSA_DH
    s0 = pl.program_id(2) * tq
    hpg = NSA_HPG
    pos_r = s0 + (lax.broadcasted_iota(jnp.int32, (hpg * tq, ncp), 0) & (tq - 1))
    n_c = lax.broadcasted_iota(jnp.int32, (hpg * tq, ncp), 1)
    mask = pos_r >= n_c * CMP_STRIDE + (CMP_LEN - 1)
    kc = kc_ref[0, 0].astype(BF16)
    vc = vc_ref[0, 0].astype(BF16)
    scale = dh ** -0.5
    q = jnp.concatenate([q_ref[:, hh * dh:(hh + 1) * dh] for hh in range(hpg)], axis=0).astype(BF16)
    bias = _dot(tb_ref[...].reshape(hpg * tq, tb_ref.shape[2]), pp_ref[0])
    lm = jnp.where(mask, _dot_nt(q, kc) * scale + bias, NEG)
    m = jnp.max(lm, axis=-1, keepdims=True)
    e = jnp.where(mask, jnp.exp(lm - m), 0.0)
    den = jnp.sum(e, axis=-1, keepdims=True)
    p = e / jnp.where(den > 0.0, den, 1.0)
    o = _dot(p.astype(BF16), vc)
    psum = p[0:tq]
    for hh in range(hpg):
        o_ref[:, hh * dh:(hh + 1) * dh] = o[hh * tq:(hh + 1) * tq]
        if hh:
            psum = psum + p[hh * tq:(hh + 1) * tq]
    nsel = ovt_ref.shape[0]
    imp_t = _dot_nt(ovt_ref[...], psum.astype(BF16))
    jb = lax.broadcasted_iota(jnp.int32, (nsel, tq), 0)
    pos = s0 + lax.broadcasted_iota(jnp.int32, (nsel, tq), 1)
    cur = jnp.right_shift(pos, SEL_LEN.bit_length() - 1)
    causal = jb * SEL_LEN <= pos
    forced = (jb == 0) | (jb == cur) | (jb == cur - 1)
    score = jnp.where(forced, FORCE, jnp.where(causal, imp_t, NEG))
    rank = jnp.zeros((nsel, tq), jnp.int32)
    for kk in range(nsel):
        row = score[kk:kk + 1, :]
        beats = (row > score) | ((row == score) & (jb > kk))
        rank = rank + jnp.where(beats, 1, 0)
    selb = jnp.where((rank < SEL_TOPK) & causal, 0.0, NEG)
    selb = jnp.concatenate([selb, jnp.zeros((SEL_PAD - nsel, tq), F32)], axis=0)
    sel_ref[0, 0] = selb.astype(BF16)


def _cmp_attention(proj, k_cmp, v_cmp, rel_bias, batch, seq):
    dh, g, hpg = NSA_DH, NSA_KV_GROUPS, NSA_HPG
    tq = CMP_TQ
    nq = seq // tq
    ncp = k_cmp.shape[2]
    nsel = seq // SEL_LEN
    n_cmp = (seq - CMP_LEN) // CMP_STRIDE + 1
    cmp_idx = np.arange(n_cmp)[:, None] * CMP_STRIDE + np.arange(CMP_LEN)[None, :]
    overlap = ((cmp_idx // SEL_LEN)[:, :, None] == np.arange(nsel)[None, None, :]).sum(1) / CMP_LEN
    ovt = np.zeros((nsel, ncp), np.float32)
    ovt[:, :n_cmp] = overlap.T
    nb = tq // CMP_STRIDE
    assert 2 * nb + 1 <= CMP_BAND and CMP_STRIDE * (nb + 1) - (CMP_LEN - 1) >= REL_MAX_DIST
    i = np.arange(tq)[:, None]
    r = np.arange(CMP_BAND)[None, :]
    rel = np.where(r < 2 * nb, i - CMP_STRIDE * (r - nb) - (CMP_LEN - 1), REL_MAX_DIST)
    tb = jnp.where(jnp.asarray(r <= 2 * nb), _bias_lookup(rel_bias, jnp.asarray(rel, jnp.int32)), 0.0)
    tb_hi = tb.astype(BF16)
    tb_lo = (tb - tb_hi.astype(F32)).astype(BF16)
    tb = jnp.concatenate([tb_hi, tb_lo], axis=-1)
    n = np.arange(ncp)[None, None, :]
    first = (np.arange(nq) * nb - nb)[:, None, None]
    rr = np.arange(CMP_BAND)[None, :, None]
    pp = np.where(rr < 2 * nb, n == first + rr, (rr == 2 * nb) & (n < first)).astype(np.float32)
    pp = np.concatenate([pp, pp], axis=1)
    qcol = IN_OFF[4] // (hpg * dh)
    return pl.pallas_call(
        _cmpattn_kernel,
        grid=(batch, g, nq),
        in_specs=[
            pl.BlockSpec((tq, hpg * dh), lambda b, gg, t: (b * nq + t, qcol + gg)),
            pl.BlockSpec((1, 1, ncp, dh), lambda b, gg, t: (b, gg, 0, 0)),
            pl.BlockSpec((1, 1, ncp, dh), lambda b, gg, t: (b, gg, 0, 0)),
            pl.BlockSpec((hpg, tq, 2 * CMP_BAND), lambda b, gg, t: (gg, 0, 0)),
            pl.BlockSpec((1, 2 * CMP_BAND, ncp), lambda b, gg, t: (t, 0, 0)),
            pl.BlockSpec((nsel, ncp), lambda b, gg, t: (0, 0)),
        ],
        out_specs=[
            pl.BlockSpec((tq, hpg * dh), lambda b, gg, t: (b * nq + t, gg)),
            pl.BlockSpec((1, 1, SEL_PAD, tq), lambda b, gg, t: (b, gg, 0, t)),
        ],
        out_shape=[jax.ShapeDtypeStruct((batch * seq, g * hpg * dh), F32),
                   jax.ShapeDtypeStruct((batch, g, SEL_PAD, seq), BF16)],
        compiler_params=pltpu.CompilerParams(
            dimension_semantics=("parallel", "parallel", "parallel"), vmem_limit_bytes=VMEM_LIMIT),
        name="cmp_attn",
    )(proj, k_cmp, v_cmp, tb, jnp.asarray(pp, BF16), jnp.asarray(ovt, BF16))


def _band_kernel(qi_ref, ki_ref, var_ref, first_ref, last_ref, q_ref, k_ref, v_ref, bias_ref, *rest,
                 use_sel):
    if use_sel:
        selb_ref, e_ref, o_ref, qt_ref, m_ref, l_ref, acc_ref = rest
    else:
        o_ref, qt_ref, m_ref, l_ref, acc_ref = rest
    t = pl.program_id(2)
    dh = NSA_DH
    tq = q_ref.shape[0]

    @pl.when(first_ref[t] == 1)
    def _():
        m_ref[...] = jnp.full_like(m_ref, NEG)
        l_ref[...] = jnp.zeros_like(l_ref)
        acc_ref[...] = jnp.zeros_like(acc_ref)
        scale = dh ** -0.5 * LOG2E
        for hh in range(NSA_HPG):
            qt_ref[0:dh, hh * tq:(hh + 1) * tq] = (q_ref[:, hh * dh:(hh + 1) * dh] * scale).T.astype(BF16)
            if use_sel:
                qt_ref[dh:, hh * tq:(hh + 1) * tq] = selb_ref[0, 0]

    k = k_ref[...].astype(BF16)
    if use_sel:
        k = jnp.concatenate([k, e_ref[...]], axis=1)
    s = _dot(k, qt_ref[...]) + bias_ref[0, 0]
    m_prev = m_ref[...]
    m_new = jnp.maximum(m_prev, jnp.max(s, axis=0, keepdims=True))
    alpha = jnp.exp2(m_prev - m_new)
    p = jnp.exp2(s - m_new)
    l_ref[...] = alpha * l_ref[...] + jnp.sum(p, axis=0, keepdims=True)
    vt = v_ref[...].T.astype(BF16)
    acc_ref[...] = alpha * acc_ref[...] + _dot(vt, p.astype(BF16))
    m_ref[...] = m_new

    @pl.when(last_ref[t] == 1)
    def _():
        o = acc_ref[...] / l_ref[...]
        for hh in range(NSA_HPG):
            o_ref[:, hh * dh:(hh + 1) * dh] = o[:, hh * tq:(hh + 1) * tq].T


def _band_steps(seq, window):
    tq, tk = ATT_T, ATT_TK
    steps = []
    for qi in range(seq // tq):
        k_hi = qi * tq // tk
        k_lo = max(0, (qi * tq - window + 1) // tk) if window else 0
        steps += [(qi, ki, min((qi * tq - ki * tk) // tq, ATT_NVAR - 1)) for ki in range(k_lo, k_hi + 1)]
    return steps


def _band_attention(proj, bias_tiles, steps, kcol, vcol, batch, seq, selb=None, e_mat=None):
    dh, g, hpg = NSA_DH, NSA_KV_GROUPS, NSA_HPG
    tq, tk = ATT_T, ATT_TK
    nq, nk = seq // tq, seq // tk
    qi = np.array([s[0] for s in steps], np.int32)
    ki = np.array([s[1] for s in steps], np.int32)
    var = np.array([s[2] for s in steps], np.int32)
    first = np.concatenate([[1], (qi[1:] != qi[:-1]).astype(np.int32)]).astype(np.int32)
    last = np.concatenate([(qi[1:] != qi[:-1]).astype(np.int32), [1]]).astype(np.int32)
    qcol = IN_OFF[4] // (hpg * dh)
    use_sel = selb is not None
    in_specs = [
        pl.BlockSpec((tq, hpg * dh), lambda b, gg, t, qi_r, ki_r, v_r, f_r, l_r: (b * nq + qi_r[t], qcol + gg)),
        pl.BlockSpec((tk, dh), lambda b, gg, t, qi_r, ki_r, v_r, f_r, l_r: (b * nk + ki_r[t], kcol + gg)),
        pl.BlockSpec((tk, dh), lambda b, gg, t, qi_r, ki_r, v_r, f_r, l_r: (b * nk + ki_r[t], vcol + gg)),
        pl.BlockSpec((1, 1, tk, hpg * tq), lambda b, gg, t, qi_r, ki_r, v_r, f_r, l_r: (gg, v_r[t], 0, 0)),
    ]
    args = [proj, proj, proj, bias_tiles]
    if use_sel:
        in_specs += [
            pl.BlockSpec((1, 1, SEL_PAD, tq), lambda b, gg, t, qi_r, ki_r, v_r, f_r, l_r: (b, gg, 0, qi_r[t])),
            pl.BlockSpec((tk, SEL_PAD), lambda b, gg, t, qi_r, ki_r, v_r, f_r, l_r: (ki_r[t], 0)),
        ]
        args += [selb, e_mat]
    kdim = dh + SEL_PAD if use_sel else dh
    grid_spec = pltpu.PrefetchScalarGridSpec(
        num_scalar_prefetch=5,
        grid=(batch, g, len(steps)),
        in_specs=in_specs,
        out_specs=pl.BlockSpec((tq, hpg * dh),
                               lambda b, gg, t, qi_r, ki_r, v_r, f_r, l_r: (b * nq + qi_r[t], gg)),
        scratch_shapes=[pltpu.VMEM((kdim, hpg * tq), BF16), pltpu.VMEM((1, hpg * tq), F32),
                        pltpu.VMEM((1, hpg * tq), F32), pltpu.VMEM((dh, hpg * tq), F32)],
    )
    return pl.pallas_call(
        functools.partial(_band_kernel, use_sel=use_sel),
        grid_spec=grid_spec,
        out_shape=jax.ShapeDtypeStruct((batch * seq, g * hpg * dh), F32),
        compiler_params=pltpu.CompilerParams(
            dimension_semantics=("parallel", "parallel", "arbitrary"), vmem_limit_bytes=VMEM_LIMIT),
        name="sel_attn" if use_sel else "win_attn",
    )(jnp.asarray(qi), jnp.asarray(ki), jnp.asarray(var), jnp.asarray(first), jnp.asarray(last), *args)


def _out_kernel(x_ref, yr_ref, oc_ref, os_ref, ow_ref, gate_ref, w_ref, o_ref):
    dh = NSA_DH
    sig = _sigmoid(gate_ref[...])
    parts = [yr_ref[...].astype(BF16)]
    for h in range(NSA_HEADS):
        cs = slice(h * dh, (h + 1) * dh)
        c = h * N_BRANCH
        yh = (sig[:, c:c + 1] * oc_ref[:, cs] + sig[:, c + 1:c + 2] * os_ref[:, cs]
              + sig[:, c + 2:c + 3] * ow_ref[:, cs])
        parts.append(yh.astype(BF16))
    y = jnp.concatenate(parts, axis=-1)
    o_ref[...] = x_ref[...] + _dot(y, w_ref[...])


def _out_proj(x2d, y_ret, o_cmp, o_sel, o_win, proj, w_out):
    tok, d = x2d.shape
    tm = OUT_TM
    wmix = y_ret.shape[1]
    gcol = IN_OFF[11] // LANE
    row = lambda shape: pl.BlockSpec(shape, lambda i: (i, 0))
    return pl.pallas_call(
        _out_kernel,
        grid=(tok // tm,),
        in_specs=[row((tm, d)), row((tm, wmix)), row((tm, wmix)), row((tm, wmix)), row((tm, wmix)),
                  pl.BlockSpec((tm, LANE), lambda i: (i, gcol)),
                  pl.BlockSpec(w_out.shape, lambda i: (0, 0))],
        out_specs=row((tm, d)),
        out_shape=jax.ShapeDtypeStruct((tok, d), F32),
        compiler_params=pltpu.CompilerParams(
            dimension_semantics=("parallel",), vmem_limit_bytes=VMEM_LIMIT),
        name="out_proj",
    )(x2d, y_ret, o_cmp, o_sel, o_win, proj, w_out)


def _t5_bucket_of(rel):
    n = jnp.maximum(rel, 0)
    max_exact = REL_BUCKETS // 2
    nf = jnp.maximum(n, 1).astype(F32)
    large = max_exact + (jnp.log(nf / max_exact) / math.log(REL_MAX_DIST / max_exact)
                         * (REL_BUCKETS - max_exact)).astype(jnp.int32)
    large = jnp.minimum(large, REL_BUCKETS - 1)
    return jnp.where(n < max_exact, n, large)


def _bias_lookup(rel_bias, rel):
    bucket = _t5_bucket_of(rel)[None]
    tab = rel_bias.astype(F32).reshape((rel_bias.shape[0], REL_BUCKETS) + (1,) * rel.ndim)
    out = jnp.zeros((rel_bias.shape[0],) + rel.shape, F32)
    for b in range(REL_BUCKETS):
        out = jnp.where(bucket == b, tab[:, b], out)
    return out


def _bias_tiles(rel_bias, window):
    t, g, hpg = ATT_T, NSA_KV_GROUPS, NSA_HPG
    assert ATT_TK == t and t >= REL_MAX_DIST and (not window or window == t)
    j = np.arange(t)[:, None]
    i = np.arange(t)[None, :]
    cyc = _bias_lookup(rel_bias, jnp.asarray((i - j) % t, jnp.int32)) * LOG2E
    far = _bias_lookup(rel_bias, jnp.full((1, 1), REL_MAX_DIST, jnp.int32)) * LOG2E
    upper = jnp.asarray(i >= j)
    tile0 = jnp.where(upper, cyc, NEG)
    if window:
        tiles = [tile0, jnp.where(upper, NEG, cyc)]
    else:
        tiles = [tile0, jnp.where(upper, far, cyc), jnp.broadcast_to(far, cyc.shape)]
    tab = jnp.stack(tiles, axis=1)
    nv = tab.shape[1]
    tab = tab.reshape(g, hpg, nv, t, t).transpose(0, 2, 3, 1, 4)
    return tab.reshape(g, nv, t, hpg * t)


def _token_mix(x2d, mix_norm, w_in, ret_gn_gain, pe_k, w1_k, w2_k, pe_v, w1_v, w2_v, w_out, rel_bias,
               batch, seq):
    d = x2d.shape[1]
    dh = NSA_DH
    w_pad = jnp.concatenate([w_in.astype(BF16), jnp.zeros((d, D_IN_PAD - D_IN), BF16)], axis=1)
    proj = _inproj(x2d, mix_norm, w_pad)
    y_ret = _retention(proj, ret_gn_gain, batch, seq)

    k_cmp, v_cmp = _compress(proj, pe_k, w1_k, w2_k, pe_v, w1_v, w2_v, batch, seq)
    o_cmp, selb = _cmp_attention(proj, k_cmp, v_cmp, rel_bias, batch, seq)

    e_np = np.zeros((seq, SEL_PAD), np.float32)
    e_np[np.arange(seq), np.arange(seq) // SEL_LEN] = 1.0
    o_sel = _band_attention(proj, _bias_tiles(rel_bias, 0), _band_steps(seq, 0),
                            IN_OFF[7] // dh, IN_OFF[8] // dh, batch, seq,
                            selb=selb, e_mat=jnp.asarray(e_np, BF16))
    o_win = _band_attention(proj, _bias_tiles(rel_bias, WIN), _band_steps(seq, WIN),
                            IN_OFF[9] // dh, IN_OFF[10] // dh, batch, seq)
    return _out_proj(x2d, y_ret, o_cmp, o_sel, o_win, proj, w_out.astype(BF16))


def kernel(x, ffn1_norm, ffn1_w1, ffn1_w3, ffn1_w2, mix_norm, w_in, ret_gn_gain, cmp_pe_k, cmp_w1_k,
           cmp_w2_k, cmp_pe_v, cmp_w1_v, cmp_w2_v, w_out, ffn2_norm, ffn2_w1, ffn2_w3, ffn2_w2,
           rel_bias, final_norm):
    batch, seq, d = x.shape
    depth = ffn1_norm.shape[0]
    h = x.reshape(batch * seq, d)
    for l in range(depth):
        last = l == depth - 1
        h = _ffn(h, ffn1_norm[l], ffn1_w1[l].astype(BF16), ffn1_w3[l].astype(BF16),
                 ffn1_w2[l].astype(BF16), final_norm, False)
        h = _token_mix(h, mix_norm[l], w_in[l], ret_gn_gain[l], cmp_pe_k[l], cmp_w1_k[l], cmp_w2_k[l],
                       cmp_pe_v[l], cmp_w1_v[l], cmp_w2_v[l], w_out[l], rel_bias, batch, seq)
        h = _ffn(h, ffn2_norm[l], ffn2_w1[l].astype(BF16), ffn2_w3[l].astype(BF16),
                 ffn2_w2[l].astype(BF16), final_norm, last)
    if depth == 0:
        raise ValueError("depth must be positive")
    return h.reshape(batch, seq, d)
```

```python
import functools
import math

import jax
import jax.numpy as jnp
import numpy as np
from jax import lax
from jax.experimental import pallas as pl
from jax.experimental.pallas import tpu as pltpu

F32 = jnp.float32
BF16 = jnp.bfloat16

RET_HEADS = 4
RET_DK = 256
RET_DV = 256
ROPE_BASE = 10000.0
NSA_HEADS = 8
NSA_KV_GROUPS = 2
NSA_HPG = NSA_HEADS // NSA_KV_GROUPS
NSA_DH = 128
CMP_LEN = 32
CMP_STRIDE = 16
SEL_LEN = 64
SEL_TOPK = 16
WIN = 512
N_BRANCH = 3
REL_BUCKETS = 32
REL_MAX_DIST = 128
EPS = 1e-6
NEG = -1e30
FORCE = 1e4

RET_W = RET_HEADS * RET_DV
NSA_W = NSA_HEADS * NSA_DH
KV_W = NSA_KV_GROUPS * NSA_DH
IN_SPLITS = [RET_HEADS * RET_DK, RET_HEADS * RET_DK, RET_W, RET_W, NSA_W,
             KV_W, KV_W, KV_W, KV_W, KV_W, KV_W, NSA_HEADS * N_BRANCH]
D_IN = sum(IN_SPLITS)
IN_OFF = [sum(IN_SPLITS[:i]) for i in range(len(IN_SPLITS))]

LANE = 128
IN_TN = 1152
D_IN_PAD = 6 * IN_TN
GATE_ROWS = 8
FFN_TM = 1024
FFN_TF = 512
IN_TM = 1024
RET_C = 128
CMP_TQ = 256
ATT_T = 512
ATT_TK = 512
ATT_NVAR = 3
OUT_TM = 256
SEL_PAD = 128
ACC_PAD = 8
CMP_BAND = 64
VMEM_LIMIT = 56 * 1024 * 1024
LOG2E = math.log2(math.e)


def _dot(a, b):
    return jnp.dot(a, b, preferred_element_type=F32)


def _dot_nt(a, b):
    return lax.dot_general(a, b, (((1,), (1,)), ((), ())), preferred_element_type=F32)


def _dot_tn(a, b):
    return lax.dot_general(a, b, (((0,), (0,)), ((), ())), preferred_element_type=F32)


def _sigmoid(x):
    return 1.0 / (1.0 + jnp.exp(-x))


def _rms(x, g):
    ms = jnp.mean(x * x, axis=-1, keepdims=True)
    return x * lax.rsqrt(ms + EPS) * g


def _ffn_kernel(x_ref, g_ref, w1_ref, w3_ref, w2_ref, fg_ref, o_ref, n_ref, *, final_norm):
    j = pl.program_id(1)

    @pl.when(j == 0)
    def _():
        x = x_ref[...]
        n_ref[...] = _rms(x, g_ref[...]).astype(BF16)
        o_ref[...] = x

    n = n_ref[...]
    a = _dot(n, w1_ref[...])
    b = _dot(n, w3_ref[...])
    h = (0.5 * a * _sigmoid(a) * b).astype(BF16)
    o_ref[...] += _dot(h, w2_ref[...])

    if final_norm:
        @pl.when(j == pl.num_programs(1) - 1)
        def _():
            o_ref[...] = _rms(o_ref[...], fg_ref[...])


def _ffn(x2d, g, w1, w3, w2, fg, final_norm):
    tok, d = x2d.shape
    dff = w1.shape[1]
    tm, tf = FFN_TM, FFN_TF
    return pl.pallas_call(
        functools.partial(_ffn_kernel, final_norm=final_norm),
        grid=(tok // tm, dff // tf),
        in_specs=[
            pl.BlockSpec((tm, d), lambda i, j: (i, 0)),
            pl.BlockSpec((1, d), lambda i, j: (0, 0)),
            pl.BlockSpec((d, tf), lambda i, j: (0, j)),
            pl.BlockSpec((d, tf), lambda i, j: (0, j)),
            pl.BlockSpec((tf, d), lambda i, j: (j, 0)),
            pl.BlockSpec((1, d), lambda i, j: (0, 0)),
        ],
        out_specs=pl.BlockSpec((tm, d), lambda i, j: (i, 0)),
        out_shape=jax.ShapeDtypeStruct((tok, d), F32),
        scratch_shapes=[pltpu.VMEM((tm, d), BF16)],
        compiler_params=pltpu.CompilerParams(
            dimension_semantics=("parallel", "arbitrary"), vmem_limit_bytes=VMEM_LIMIT),
        name="ffn",
    )(x2d, g.reshape(1, d), w1, w3, w2, fg.reshape(1, d))


def _inproj_kernel(x_ref, g_ref, w_ref, o_ref, n_ref):
    @pl.when(pl.program_id(1) == 0)
    def _():
        n_ref[...] = _rms(x_ref[...], g_ref[...]).astype(BF16)

    o_ref[...] = _dot(n_ref[...], w_ref[...])


def _inproj(x2d, g, w_pad):
    tok, d = x2d.shape
    n_out = w_pad.shape[1]
    tm, tn = IN_TM, IN_TN
    return pl.pallas_call(
        _inproj_kernel,
        grid=(tok // tm, n_out // tn),
        in_specs=[
            pl.BlockSpec((tm, d), lambda i, j: (i, 0)),
            pl.BlockSpec((1, d), lambda i, j: (0, 0)),
            pl.BlockSpec((d, tn), lambda i, j: (0, j)),
        ],
        out_specs=pl.BlockSpec((tm, tn), lambda i, j: (i, j)),
        out_shape=jax.ShapeDtypeStruct((tok, n_out), F32),
        scratch_shapes=[pltpu.VMEM((tm, d), BF16)],
        compiler_params=pltpu.CompilerParams(
            dimension_semantics=("parallel", "arbitrary"), vmem_limit_bytes=VMEM_LIMIT),
        name="in_proj",
    )(x2d, g.reshape(1, d), w_pad)


def _ret_kernel(cd_ref, q_ref, k_ref, v_ref, g_ref, cos_ref, sin_ref, dec_ref, xi_ref, zeta_ref,
                gain_ref, o_ref, state_ref):
    @pl.when(pl.program_id(1) == 0)
    def _():
        state_ref[...] = jnp.zeros_like(state_ref)

    cos = cos_ref[...]
    sin = sin_ref[...]
    half = RET_DK // 2

    def rot(t):
        t1, t2 = t[:, :half], t[:, half:]
        return jnp.concatenate([t1 * cos - t2 * sin, t1 * sin + t2 * cos], axis=-1)

    for h in range(RET_HEADS):
        cs = slice(h * RET_DK, (h + 1) * RET_DK)
        q = rot(q_ref[:, cs])
        k = rot(k_ref[:, cs]) * (RET_DK ** -0.5)
        v = v_ref[:, cs].astype(BF16)
        qb = q.astype(BF16)
        s = _dot_nt(qb, k.astype(BF16)) * dec_ref[h]
        state = state_ref[h]
        y = _dot(s.astype(BF16), v) + _dot((q * xi_ref[h]).astype(BF16), state.astype(BF16))
        state_ref[h] = cd_ref[h] * state + _dot_tn((k * zeta_ref[h]).astype(BF16), v)
        mu = jnp.mean(y, axis=-1, keepdims=True)
        yc = y - mu
        var = jnp.mean(yc * yc, axis=-1, keepdims=True)
        yn = yc * lax.rsqrt(var + EPS) * gain_ref[:, cs]
        gate = g_ref[:, cs]
        o_ref[:, cs] = gate * _sigmoid(gate) * yn


def _retention(proj, gain, batch, seq):
    tok = proj.shape[0]
    c = RET_C
    n_chunks = seq // c
    h, dk = RET_HEADS, RET_DK
    half = dk // 2
    inv = ROPE_BASE ** (-jnp.arange(half, dtype=F32) / half)
    ang = jnp.arange(seq, dtype=F32)[:, None] * inv[None, :]
    cos, sin = jnp.cos(ang), jnp.sin(ang)
    log_g = jnp.log(1.0 - 2.0 ** (-5.0 - jnp.arange(h, dtype=F32)))
    idx = jnp.arange(c, dtype=F32)
    diff = idx[:, None] - idx[None, :]
    dec = jnp.where(diff >= 0.0, jnp.exp(jnp.maximum(diff, 0.0)[None] * log_g[:, None, None]), 0.0)
    xi = jnp.broadcast_to(jnp.exp((idx + 1.0)[None] * log_g[:, None])[:, :, None], (h, c, dk))
    zeta = jnp.broadcast_to(jnp.exp((c - 1.0 - idx)[None] * log_g[:, None])[:, :, None], (h, c, dk))
    cd = jnp.exp(c * log_g)

    w = h * dk
    row = lambda b, n: (b * n_chunks + n)
    return pl.pallas_call(
        _ret_kernel,
        grid=(batch, n_chunks),
        in_specs=[
            pl.BlockSpec(memory_space=pltpu.SMEM),
            pl.BlockSpec((c, w), lambda b, n: (row(b, n), 0)),
            pl.BlockSpec((c, w), lambda b, n: (row(b, n), 1)),
            pl.BlockSpec((c, w), lambda b, n: (row(b, n), 2)),
            pl.BlockSpec((c, w), lambda b, n: (row(b, n), 3)),
            pl.BlockSpec((c, half), lambda b, n: (n, 0)),
            pl.BlockSpec((c, half), lambda b, n: (n, 0)),
            pl.BlockSpec((h, c, c), lambda b, n: (0, 0, 0)),
            pl.BlockSpec((h, c, dk), lambda b, n: (0, 0, 0)),
            pl.BlockSpec((h, c, dk), lambda b, n: (0, 0, 0)),
            pl.BlockSpec((1, w), lambda b, n: (0, 0)),
        ],
        out_specs=pl.BlockSpec((c, w), lambda b, n: (row(b, n), 0)),
        out_shape=jax.ShapeDtypeStruct((tok, w), F32),
        scratch_shapes=[pltpu.VMEM((h, dk, RET_DV), F32)],
        compiler_params=pltpu.CompilerParams(
            dimension_semantics=("parallel", "arbitrary"), vmem_limit_bytes=VMEM_LIMIT),
        name="retention",
    )(cd, proj, proj, proj, proj, cos, sin, dec, xi, zeta, gain.reshape(1, w))


def _compress_kernel(k_ref, v_ref, pek_ref, w1k_ref, w1kf_ref, w2k_ref, pev_ref, w1v_ref, w1vf_ref,
                     w2v_ref, ok_ref, ov_ref):
    dh = NSA_DH
    nblk = k_ref.shape[0] // CMP_STRIDE

    def one(x_ref, pe_ref, w1_ref, w1f_ref, w2_ref, o_ref):
        ab = jnp.zeros((nblk, 2 * dh), F32)
        for l in range(CMP_STRIDE):
            rows = x_ref[pl.ds(l, nblk, stride=CMP_STRIDE), :].astype(BF16)
            ab = ab + _dot(rows, w1_ref[l])
        a, b = ab[:, :dh], ab[:, dh:]
        b_next = pltpu.roll(b, nblk - 1, axis=0)
        ridx = lax.broadcasted_iota(jnp.int32, (nblk, dh), 0)
        b_next = jnp.where(ridx < nblk - 1, b_next, 0.0)
        pe_term = _dot(pe_ref[...], w1f_ref[...])[0:1, :]
        hdn = a + b_next + pe_term
        o_ref[0, 0] = _dot((hdn * _sigmoid(hdn)).astype(BF16), w2_ref[...])

    one(k_ref, pek_ref, w1k_ref, w1kf_ref, w2k_ref, ok_ref)
    one(v_ref, pev_ref, w1v_ref, w1vf_ref, w2v_ref, ov_ref)


def _compress(proj, pe_k, w1_k, w2_k, pe_v, w1_v, w2_v, batch, seq):
    dh, g = NSA_DH, NSA_KV_GROUPS
    nblk = seq // CMP_STRIDE
    half = CMP_LEN // 2

    def prep(pe, w1, w2):
        w1b = w1.astype(BF16)
        w1_pair = jnp.concatenate([w1b[:half], w1b[half:]], axis=-1)
        pe_flat = jnp.broadcast_to(pe.reshape(1, CMP_LEN * dh), (8, CMP_LEN * dh)).astype(BF16)
        return pe_flat, w1_pair, w1b.reshape(CMP_LEN * dh, dh), w2.astype(BF16)

    args_k = prep(pe_k, w1_k, w2_k)
    args_v = prep(pe_v, w1_v, w2_v)
    kcol = IN_OFF[5] // dh
    vcol = IN_OFF[6] // dh
    const = lambda shape: pl.BlockSpec(shape, lambda b, gg: (0,) * len(shape))
    wspecs = [const((8, CMP_LEN * dh)), const((half, dh, 2 * dh)), const((CMP_LEN * dh, dh)),
              const((dh, dh))]
    out_spec = pl.BlockSpec((1, 1, nblk, dh), lambda b, gg: (b, gg, 0, 0))
    return pl.pallas_call(
        _compress_kernel,
        grid=(batch, g),
        in_specs=[pl.BlockSpec((seq, dh), lambda b, gg: (b, kcol + gg)),
                  pl.BlockSpec((seq, dh), lambda b, gg: (b, vcol + gg))] + wspecs + wspecs,
        out_specs=[out_spec, out_spec],
        out_shape=[jax.ShapeDtypeStruct((batch, g, nblk, dh), F32)] * 2,
        compiler_params=pltpu.CompilerParams(
            dimension_semantics=("parallel", "parallel"), vmem_limit_bytes=VMEM_LIMIT),
        name="compress",
    )(proj, proj, *args_k, *args_v)


def _cmpattn_kernel(q_ref, kc_ref, vc_ref, tb_ref, pp_ref, ovt_ref, gate_ref, o_ref, sel_ref):
    tq = q_ref.shape[0]
    ncp = kc_ref.shape[2]
    dh = NSA_DH
    s0 = pl.program_id(2) * tq
    hpg = NSA_HPG
    pos_r = s0 + (lax.broadcasted_iota(jnp.int32, (hpg * tq, ncp), 0) & (tq - 1))
    n_c = lax.broadcasted_iota(jnp.int32, (hpg * tq, ncp), 1)
    mask = pos_r >= n_c * CMP_STRIDE + (CMP_LEN - 1)
    kc = kc_ref[0, 0].astype(BF16)
    vc = vc_ref[0, 0].astype(BF16)
    scale = dh ** -0.5
    q = jnp.concatenate([q_ref[:, hh * dh:(hh + 1) * dh] for hh in range(hpg)], axis=0).astype(BF16)
    bias = _dot(tb_ref[...].reshape(hpg * tq, tb_ref.shape[2]), pp_ref[0])
    lm = jnp.where(mask, _dot_nt(q, kc) * scale + bias, NEG)
    m = jnp.max(lm, axis=-1, keepdims=True)
    e = jnp.where(mask, jnp.exp(lm - m), 0.0)
    den = jnp.sum(e, axis=-1, keepdims=True)
    p = e / jnp.where(den > 0.0, den, 1.0)
    o = _dot(p.astype(BF16), vc)
    gate = _sigmoid(gate_ref[0, 0])
    psum = p[0:tq]
    for hh in range(hpg):
        o_ref[:, hh * dh:(hh + 1) * dh] = o[hh * tq:(hh + 1) * tq] * gate[:, hh:hh + 1]
        if hh:
            psum = psum + p[hh * tq:(hh + 1) * tq]
    nsel = ovt_ref.shape[0]
    imp_t = _dot_nt(ovt_ref[...], psum.astype(BF16))
    jb = lax.broadcasted_iota(jnp.int32, (nsel, tq), 0)
    pos = s0 + lax.broadcasted_iota(jnp.int32, (nsel, tq), 1)
    cur = jnp.right_shift(pos, SEL_LEN.bit_length() - 1)
    causal = jb * SEL_LEN <= pos
    forced = (jb == 0) | (jb == cur) | (jb == cur - 1)
    score = jnp.where(forced, FORCE, jnp.where(causal, imp_t, NEG))
    rank = jnp.zeros((nsel, tq), jnp.int32)
    for kk in range(nsel):
        row = score[kk:kk + 1, :]
        beats = (row > score) | ((row == score) & (jb > kk))
        rank = rank + jnp.where(beats, 1, 0)
    selb = jnp.where((rank < SEL_TOPK) & causal, 0.0, NEG)
    selb = jnp.concatenate([selb, jnp.zeros((SEL_PAD - nsel, tq), F32)], axis=0)
    sel_ref[0, 0] = selb.astype(BF16)


def _cmp_attention(proj, k_cmp, v_cmp, rel_bias, gates, batch, seq):
    dh, g, hpg = NSA_DH, NSA_KV_GROUPS, NSA_HPG
    tq = CMP_TQ
    nq = seq // tq
    ncp = k_cmp.shape[2]
    nsel = seq // SEL_LEN
    n_cmp = (seq - CMP_LEN) // CMP_STRIDE + 1
    cmp_idx = np.arange(n_cmp)[:, None] * CMP_STRIDE + np.arange(CMP_LEN)[None, :]
    overlap = ((cmp_idx // SEL_LEN)[:, :, None] == np.arange(nsel)[None, None, :]).sum(1) / CMP_LEN
    ovt = np.zeros((nsel, ncp), np.float32)
    ovt[:, :n_cmp] = overlap.T
    nb = tq // CMP_STRIDE
    assert 2 * nb + 1 <= CMP_BAND and CMP_STRIDE * (nb + 1) - (CMP_LEN - 1) >= REL_MAX_DIST
    i = np.arange(tq)[:, None]
    r = np.arange(CMP_BAND)[None, :]
    rel = np.where(r < 2 * nb, i - CMP_STRIDE * (r - nb) - (CMP_LEN - 1), REL_MAX_DIST)
    tb = jnp.where(jnp.asarray(r <= 2 * nb), _bias_lookup(rel_bias, jnp.asarray(rel, jnp.int32)), 0.0)
    tb_hi = tb.astype(BF16)
    tb_lo = (tb - tb_hi.astype(F32)).astype(BF16)
    tb = jnp.concatenate([tb_hi, tb_lo], axis=-1)
    n = np.arange(ncp)[None, None, :]
    first = (np.arange(nq) * nb - nb)[:, None, None]
    rr = np.arange(CMP_BAND)[None, :, None]
    pp = np.where(rr < 2 * nb, n == first + rr, (rr == 2 * nb) & (n < first)).astype(np.float32)
    pp = np.concatenate([pp, pp], axis=1)
    qcol = IN_OFF[4] // (hpg * dh)
    return pl.pallas_call(
        _cmpattn_kernel,
        grid=(batch, g, nq),
        in_specs=[
            pl.BlockSpec((tq, hpg * dh), lambda b, gg, t: (b * nq + t, qcol + gg)),
            pl.BlockSpec((1, 1, ncp, dh), lambda b, gg, t: (b, gg, 0, 0)),
            pl.BlockSpec((1, 1, ncp, dh), lambda b, gg, t: (b, gg, 0, 0)),
            pl.BlockSpec((hpg, tq, 2 * CMP_BAND), lambda b, gg, t: (gg, 0, 0)),
            pl.BlockSpec((1, 2 * CMP_BAND, ncp), lambda b, gg, t: (t, 0, 0)),
            pl.BlockSpec((nsel, ncp), lambda b, gg, t: (0, 0)),
            pl.BlockSpec((1, 1, tq, GATE_ROWS), lambda b, gg, t: (b, gg, t, 0)),
        ],
        out_specs=[
            pl.BlockSpec((tq, hpg * dh), lambda b, gg, t: (b * nq + t, gg)),
            pl.BlockSpec((1, 1, SEL_PAD, tq), lambda b, gg, t: (b, gg, 0, t)),
        ],
        out_shape=[jax.ShapeDtypeStruct((batch * seq, g * hpg * dh), F32),
                   jax.ShapeDtypeStruct((batch, g, SEL_PAD, seq), BF16)],
        compiler_params=pltpu.CompilerParams(
            dimension_semantics=("parallel", "parallel", "parallel"), vmem_limit_bytes=VMEM_LIMIT),
        name="cmp_attn",
    )(proj, k_cmp, v_cmp, tb, jnp.asarray(pp, BF16), jnp.asarray(ovt, BF16),
      gates[:, :, 0].transpose(0, 1, 3, 2))


def _band_kernel(qi_ref, ki_ref, var_ref, first_ref, last_ref, q_ref, k_ref, v_ref, bias_ref, gate_ref,
                 *rest, use_sel):
    if use_sel:
        selb_ref, e_ref, o_ref, qt_ref, m_ref, acc_ref = rest
    else:
        o_ref, qt_ref, m_ref, acc_ref = rest
    t = pl.program_id(2)
    dh = NSA_DH
    tq = q_ref.shape[0]

    @pl.when(first_ref[t] == 1)
    def _():
        m_ref[...] = jnp.full_like(m_ref, NEG)
        acc_ref[...] = jnp.zeros_like(acc_ref)
        scale = dh ** -0.5 * LOG2E
        for hh in range(NSA_HPG):
            qt_ref[0:dh, hh * tq:(hh + 1) * tq] = (q_ref[:, hh * dh:(hh + 1) * dh] * scale).T.astype(BF16)
            if use_sel:
                qt_ref[dh:, hh * tq:(hh + 1) * tq] = selb_ref[0, 0]

    k = k_ref[...].astype(BF16)
    if use_sel:
        k = jnp.concatenate([k, e_ref[...]], axis=1)
    vt = jnp.concatenate([v_ref[...].T, jnp.ones((ACC_PAD, k.shape[0]), F32)], axis=0).astype(BF16)
    s = _dot(k, qt_ref[...]) + bias_ref[0, 0]
    m_prev = m_ref[...]
    m_new = jnp.maximum(m_prev, jnp.max(s, axis=0, keepdims=True))
    alpha = jnp.exp2(m_prev - m_new)
    p = jnp.exp2(s - m_new).astype(BF16)
    acc_ref[...] = alpha * acc_ref[...] + _dot(vt, p)
    m_ref[...] = m_new

    @pl.when(last_ref[t] == 1)
    def _():
        gate = _sigmoid(gate_ref[0, 0, 0])
        for hh in range(NSA_HPG):
            cols = slice(hh * tq, (hh + 1) * tq)
            o = acc_ref[0:dh, cols] * (gate[hh:hh + 1, :] / acc_ref[dh:dh + 1, cols])
            o_ref[:, hh * dh:(hh + 1) * dh] = o.T


def _band_steps(seq, window):
    tq, tk = ATT_T, ATT_TK
    steps = []
    for qi in range(seq // tq):
        k_hi = qi * tq // tk
        k_lo = max(0, (qi * tq - window + 1) // tk) if window else 0
        steps += [(qi, ki, min((qi * tq - ki * tk) // tq, ATT_NVAR - 1)) for ki in range(k_lo, k_hi + 1)]
    return steps


def _band_attention(proj, bias_tiles, steps, kcol, vcol, gates, branch, batch, seq, selb=None, e_mat=None):
    dh, g, hpg = NSA_DH, NSA_KV_GROUPS, NSA_HPG
    tq, tk = ATT_T, ATT_TK
    nq, nk = seq // tq, seq // tk
    qi = np.array([s[0] for s in steps], np.int32)
    ki = np.array([s[1] for s in steps], np.int32)
    var = np.array([s[2] for s in steps], np.int32)
    first = np.concatenate([[1], (qi[1:] != qi[:-1]).astype(np.int32)]).astype(np.int32)
    last = np.concatenate([(qi[1:] != qi[:-1]).astype(np.int32), [1]]).astype(np.int32)
    qcol = IN_OFF[4] // (hpg * dh)
    use_sel = selb is not None
    in_specs = [
        pl.BlockSpec((tq, hpg * dh), lambda b, gg, t, qi_r, ki_r, v_r, f_r, l_r: (b * nq + qi_r[t], qcol + gg)),
        pl.BlockSpec((tk, dh), lambda b, gg, t, qi_r, ki_r, v_r, f_r, l_r: (b * nk + ki_r[t], kcol + gg)),
        pl.BlockSpec((tk, dh), lambda b, gg, t, qi_r, ki_r, v_r, f_r, l_r: (b * nk + ki_r[t], vcol + gg)),
        pl.BlockSpec((1, 1, tk, hpg * tq), lambda b, gg, t, qi_r, ki_r, v_r, f_r, l_r: (gg, v_r[t], 0, 0)),
        pl.BlockSpec((1, 1, 1, GATE_ROWS, tq),
                     lambda b, gg, t, qi_r, ki_r, v_r, f_r, l_r: (b, gg, branch, 0, qi_r[t])),
    ]
    args = [proj, proj, proj, bias_tiles, gates]
    if use_sel:
        in_specs += [
            pl.BlockSpec((1, 1, SEL_PAD, tq), lambda b, gg, t, qi_r, ki_r, v_r, f_r, l_r: (b, gg, 0, qi_r[t])),
            pl.BlockSpec((tk, SEL_PAD), lambda b, gg, t, qi_r, ki_r, v_r, f_r, l_r: (ki_r[t], 0)),
        ]
        args += [selb, e_mat]
    kdim = dh + SEL_PAD if use_sel else dh
    grid_spec = pltpu.PrefetchScalarGridSpec(
        num_scalar_prefetch=5,
        grid=(batch, g, len(steps)),
        in_specs=in_specs,
        out_specs=pl.BlockSpec((tq, hpg * dh),
                               lambda b, gg, t, qi_r, ki_r, v_r, f_r, l_r: (b * nq + qi_r[t], gg)),
        scratch_shapes=[pltpu.VMEM((kdim, hpg * tq), BF16), pltpu.VMEM((1, hpg * tq), F32),
                        pltpu.VMEM((dh + ACC_PAD, hpg * tq), F32)],
    )
    return pl.pallas_call(
        functools.partial(_band_kernel, use_sel=use_sel),
        grid_spec=grid_spec,
        out_shape=jax.ShapeDtypeStruct((batch * seq, g * hpg * dh), F32),
        compiler_params=pltpu.CompilerParams(
            dimension_semantics=("parallel", "parallel", "arbitrary"), vmem_limit_bytes=VMEM_LIMIT),
        name="sel_attn" if use_sel else "win_attn",
    )(jnp.asarray(qi), jnp.asarray(ki), jnp.asarray(var), jnp.asarray(first), jnp.asarray(last), *args)


def _out_kernel(x_ref, yr_ref, oc_ref, os_ref, ow_ref, w_ref, o_ref):
    y_nsa = oc_ref[...] + os_ref[...] + ow_ref[...]
    y = jnp.concatenate([yr_ref[...].astype(BF16), y_nsa.astype(BF16)], axis=-1)
    o_ref[...] = x_ref[...] + _dot(y, w_ref[...])


def _out_proj(x2d, y_ret, o_cmp, o_sel, o_win, w_out):
    tok, d = x2d.shape
    tm = OUT_TM
    wmix = y_ret.shape[1]
    row = lambda shape: pl.BlockSpec(shape, lambda i: (i, 0))
    return pl.pallas_call(
        _out_kernel,
        grid=(tok // tm,),
        in_specs=[row((tm, d)), row((tm, wmix)), row((tm, wmix)), row((tm, wmix)), row((tm, wmix)),
                  pl.BlockSpec(w_out.shape, lambda i: (0, 0))],
        out_specs=row((tm, d)),
        out_shape=jax.ShapeDtypeStruct((tok, d), F32),
        compiler_params=pltpu.CompilerParams(
            dimension_semantics=("parallel",), vmem_limit_bytes=VMEM_LIMIT),
        name="out_proj",
    )(x2d, y_ret, o_cmp, o_sel, o_win, w_out)


def _t5_bucket_of(rel):
    n = jnp.maximum(rel, 0)
    max_exact = REL_BUCKETS // 2
    nf = jnp.maximum(n, 1).astype(F32)
    large = max_exact + (jnp.log(nf / max_exact) / math.log(REL_MAX_DIST / max_exact)
                         * (REL_BUCKETS - max_exact)).astype(jnp.int32)
    large = jnp.minimum(large, REL_BUCKETS - 1)
    return jnp.where(n < max_exact, n, large)


def _bias_lookup(rel_bias, rel):
    bucket = _t5_bucket_of(rel)[None]
    tab = rel_bias.astype(F32).reshape((rel_bias.shape[0], REL_BUCKETS) + (1,) * rel.ndim)
    out = jnp.zeros((rel_bias.shape[0],) + rel.shape, F32)
    for b in range(REL_BUCKETS):
        out = jnp.where(bucket == b, tab[:, b], out)
    return out


def _bias_tiles(rel_bias, window):
    t, g, hpg = ATT_T, NSA_KV_GROUPS, NSA_HPG
    assert ATT_TK == t and t >= REL_MAX_DIST and (not window or window == t)
    j = np.arange(t)[:, None]
    i = np.tile(np.arange(t), hpg)[None, :]
    bucket = _t5_bucket_of(jnp.asarray((i - j) % t, jnp.int32))[None]
    val = jnp.repeat(rel_bias.astype(F32).reshape(g, 1, hpg, REL_BUCKETS) * LOG2E, t, axis=2)
    cyc = jnp.zeros((g, t, hpg * t), F32)
    for b in range(REL_BUCKETS):
        cyc = jnp.where(bucket == b, val[..., b], cyc)
    far = val[..., REL_BUCKETS - 1]
    upper = jnp.asarray(i >= j)[None]
    tile0 = jnp.where(upper, cyc, NEG)
    if window:
        tiles = [tile0, jnp.where(upper, NEG, cyc)]
    else:
        tiles = [tile0, jnp.where(upper, far, cyc), jnp.broadcast_to(far, cyc.shape)]
    return jnp.stack(tiles, axis=1)


def _token_mix(x2d, mix_norm, w_in, ret_gn_gain, pe_k, w1_k, w2_k, pe_v, w1_v, w2_v, w_out, rel_bias,
               batch, seq):
    d = x2d.shape[1]
    dh = NSA_DH
    w_pad = jnp.concatenate([w_in.astype(BF16), jnp.zeros((d, D_IN_PAD - D_IN), BF16)], axis=1)
    proj = _inproj(x2d, mix_norm, w_pad)
    y_ret = _retention(proj, ret_gn_gain, batch, seq)

    gates = proj[:, IN_OFF[11]:IN_OFF[11] + IN_SPLITS[11]]
    gates = gates.reshape(batch, seq, NSA_KV_GROUPS, NSA_HPG, N_BRANCH).transpose(0, 2, 4, 3, 1)
    gates = jnp.pad(gates, ((0, 0), (0, 0), (0, 0), (0, GATE_ROWS - NSA_HPG), (0, 0)))

    k_cmp, v_cmp = _compress(proj, pe_k, w1_k, w2_k, pe_v, w1_v, w2_v, batch, seq)
    o_cmp, selb = _cmp_attention(proj, k_cmp, v_cmp, rel_bias, gates, batch, seq)

    e_np = np.zeros((seq, SEL_PAD), np.float32)
    e_np[np.arange(seq), np.arange(seq) // SEL_LEN] = 1.0
    o_sel = _band_attention(proj, _bias_tiles(rel_bias, 0), _band_steps(seq, 0),
                            IN_OFF[7] // dh, IN_OFF[8] // dh, gates, 1, batch, seq,
                            selb=selb, e_mat=jnp.asarray(e_np, BF16))
    o_win = _band_attention(proj, _bias_tiles(rel_bias, WIN), _band_steps(seq, WIN),
                            IN_OFF[9] // dh, IN_OFF[10] // dh, gates, 2, batch, seq)
    return _out_proj(x2d, y_ret, o_cmp, o_sel, o_win, w_out.astype(BF16))


def kernel(x, ffn1_norm, ffn1_w1, ffn1_w3, ffn1_w2, mix_norm, w_in, ret_gn_gain, cmp_pe_k, cmp_w1_k,
           cmp_w2_k, cmp_pe_v, cmp_w1_v, cmp_w2_v, w_out, ffn2_norm, ffn2_w1, ffn2_w3, ffn2_w2,
           rel_bias, final_norm):
    batch, seq, d = x.shape
    depth = ffn1_norm.shape[0]
    h = x.reshape(batch * seq, d)
    for l in range(depth):
        last = l == depth - 1
        h = _ffn(h, ffn1_norm[l], ffn1_w1[l].astype(BF16), ffn1_w3[l].astype(BF16),
                 ffn1_w2[l].astype(BF16), final_norm, False)
        h = _token_mix(h, mix_norm[l], w_in[l], ret_gn_gain[l], cmp_pe_k[l], cmp_w1_k[l], cmp_w2_k[l],
                       cmp_pe_v[l], cmp_w1_v[l], cmp_w2_v[l], w_out[l], rel_bias, batch, seq)
        h = _ffn(h, ffn2_norm[l], ffn2_w1[l].astype(BF16), ffn2_w3[l].astype(BF16),
                 ffn2_w2[l].astype(BF16), final_norm, last)
    if depth == 0:
        raise ValueError("depth must be positive")
    return h.reshape(batch, seq, d)
```

```python
import functools
import math

import jax
import jax.numpy as jnp
import numpy as np
from jax import lax
from jax.experimental import pallas as pl
from jax.experimental.pallas import tpu as pltpu

F32 = jnp.float32
BF16 = jnp.bfloat16

RET_HEADS = 4
RET_DK = 256
RET_DV = 256
ROPE_BASE = 10000.0
NSA_HEADS = 8
NSA_KV_GROUPS = 2
NSA_HPG = NSA_HEADS // NSA_KV_GROUPS
NSA_DH = 128
CMP_LEN = 32
CMP_STRIDE = 16
SEL_LEN = 64
SEL_TOPK = 16
WIN = 512
N_BRANCH = 3
REL_BUCKETS = 32
REL_MAX_DIST = 128
EPS = 1e-6
NEG = -1e30
FORCE = 1e4

RET_W = RET_HEADS * RET_DV
NSA_W = NSA_HEADS * NSA_DH
KV_W = NSA_KV_GROUPS * NSA_DH
IN_SPLITS = [RET_HEADS * RET_DK, RET_HEADS * RET_DK, RET_W, RET_W, NSA_W,
             KV_W, KV_W, KV_W, KV_W, KV_W, KV_W, NSA_HEADS * N_BRANCH]
D_IN = sum(IN_SPLITS)
IN_OFF = [sum(IN_SPLITS[:i]) for i in range(len(IN_SPLITS))]

LANE = 128
IN_TN = 1152
D_IN_PAD = 6 * IN_TN
CMP_KV_TILE = IN_OFF[5] // IN_TN
CMP_KV_LOCAL = IN_OFF[5] % IN_TN
assert CMP_KV_LOCAL + 2 * KV_W <= IN_TN and IN_OFF[6] == IN_OFF[5] + KV_W
GATE_ROWS = 8
FFN_TM = 1024
FFN_TF = 512
IN_TM = 1024
RET_C = 128
CMP_TQ = 256
ATT_T = 512
ATT_TK = 512
ATT_NVAR = 3
OUT_TM = 256
SEL_PAD = 128
ACC_PAD = 8
CMP_BAND = 64
VMEM_LIMIT = 56 * 1024 * 1024
LOG2E = math.log2(math.e)


def _dot(a, b):
    return jnp.dot(a, b, preferred_element_type=F32)


def _dot_nt(a, b):
    return lax.dot_general(a, b, (((1,), (1,)), ((), ())), preferred_element_type=F32)


def _dot_tn(a, b):
    return lax.dot_general(a, b, (((0,), (0,)), ((), ())), preferred_element_type=F32)


def _sigmoid(x):
    return 1.0 / (1.0 + jnp.exp(-x))


def _rms(x, g):
    ms = jnp.mean(x * x, axis=-1, keepdims=True)
    return x * lax.rsqrt(ms + EPS) * g


def _ffn_kernel(x_ref, g_ref, w1_ref, w3_ref, w2_ref, fg_ref, o_ref, n_ref, *, final_norm):
    j = pl.program_id(1)

    @pl.when(j == 0)
    def _():
        x = x_ref[...]
        n_ref[...] = _rms(x, g_ref[...]).astype(BF16)
        o_ref[...] = x

    n = n_ref[...]
    a = _dot(n, w1_ref[...])
    b = _dot(n, w3_ref[...])
    h = (0.5 * a * _sigmoid(a) * b).astype(BF16)
    o_ref[...] += _dot(h, w2_ref[...])

    if final_norm:
        @pl.when(j == pl.num_programs(1) - 1)
        def _():
            o_ref[...] = _rms(o_ref[...], fg_ref[...])


def _ffn(x2d, g, w1, w3, w2, fg, final_norm):
    tok, d = x2d.shape
    dff = w1.shape[1]
    tm, tf = FFN_TM, FFN_TF
    return pl.pallas_call(
        functools.partial(_ffn_kernel, final_norm=final_norm),
        grid=(tok // tm, dff // tf),
        in_specs=[
            pl.BlockSpec((tm, d), lambda i, j: (i, 0)),
            pl.BlockSpec((1, d), lambda i, j: (0, 0)),
            pl.BlockSpec((d, tf), lambda i, j: (0, j)),
            pl.BlockSpec((d, tf), lambda i, j: (0, j)),
            pl.BlockSpec((tf, d), lambda i, j: (j, 0)),
            pl.BlockSpec((1, d), lambda i, j: (0, 0)),
        ],
        out_specs=pl.BlockSpec((tm, d), lambda i, j: (i, 0)),
        out_shape=jax.ShapeDtypeStruct((tok, d), F32),
        scratch_shapes=[pltpu.VMEM((tm, d), BF16)],
        compiler_params=pltpu.CompilerParams(
            dimension_semantics=("parallel", "arbitrary"), vmem_limit_bytes=VMEM_LIMIT),
        name="ffn",
    )(x2d, g.reshape(1, d), w1, w3, w2, fg.reshape(1, d))


def _inproj_kernel(x_ref, g_ref, w_ref, o_ref, kv_ref, n_ref):
    j = pl.program_id(1)

    @pl.when(j == 0)
    def _():
        n_ref[...] = _rms(x_ref[...], g_ref[...]).astype(BF16)

    res = _dot(n_ref[...], w_ref[...])
    o_ref[...] = res.astype(o_ref.dtype)

    @pl.when(j == CMP_KV_TILE)
    def _():
        kv_ref[...] = res[:, CMP_KV_LOCAL:CMP_KV_LOCAL + 2 * KV_W]


def _inproj(x2d, g, w_pad):
    tok, d = x2d.shape
    n_out = w_pad.shape[1]
    tm, tn = IN_TM, IN_TN
    return pl.pallas_call(
        _inproj_kernel,
        grid=(tok // tm, n_out // tn),
        in_specs=[
            pl.BlockSpec((tm, d), lambda i, j: (i, 0)),
            pl.BlockSpec((1, d), lambda i, j: (0, 0)),
            pl.BlockSpec((d, tn), lambda i, j: (0, j)),
        ],
        out_specs=[pl.BlockSpec((tm, tn), lambda i, j: (i, j)),
                   pl.BlockSpec((tm, 2 * KV_W), lambda i, j: (i, 0))],
        out_shape=[jax.ShapeDtypeStruct((tok, n_out), BF16),
                   jax.ShapeDtypeStruct((tok, 2 * KV_W), F32)],
        scratch_shapes=[pltpu.VMEM((tm, d), BF16)],
        compiler_params=pltpu.CompilerParams(
            dimension_semantics=("parallel", "arbitrary"), vmem_limit_bytes=VMEM_LIMIT),
        name="in_proj",
    )(x2d, g.reshape(1, d), w_pad)


def _ret_kernel(cd_ref, q_ref, k_ref, v_ref, g_ref, cos_ref, sin_ref, dec_ref, xi_ref, zeta_ref,
                gain_ref, o_ref, state_ref):
    @pl.when(pl.program_id(1) == 0)
    def _():
        state_ref[...] = jnp.zeros_like(state_ref)

    cos = cos_ref[...]
    sin = sin_ref[...]
    half = RET_DK // 2

    def rot(t):
        t1, t2 = t[:, :half], t[:, half:]
        return jnp.concatenate([t1 * cos - t2 * sin, t1 * sin + t2 * cos], axis=-1)

    for h in range(RET_HEADS):
        cs = slice(h * RET_DK, (h + 1) * RET_DK)
        q = rot(q_ref[:, cs].astype(F32))
        k = rot(k_ref[:, cs].astype(F32)) * (RET_DK ** -0.5)
        v = v_ref[:, cs].astype(BF16)
        qb = q.astype(BF16)
        s = _dot_nt(qb, k.astype(BF16)) * dec_ref[h]
        state = state_ref[h]
        y = _dot(s.astype(BF16), v) + _dot((q * xi_ref[h]).astype(BF16), state.astype(BF16))
        state_ref[h] = cd_ref[h] * state + _dot_tn((k * zeta_ref[h]).astype(BF16), v)
        mu = jnp.mean(y, axis=-1, keepdims=True)
        yc = y - mu
        var = jnp.mean(yc * yc, axis=-1, keepdims=True)
        yn = yc * lax.rsqrt(var + EPS) * gain_ref[:, cs]
        gate = g_ref[:, cs].astype(F32)
        o_ref[:, cs] = (gate * _sigmoid(gate) * yn).astype(o_ref.dtype)


def _retention(proj, gain, batch, seq):
    tok = proj.shape[0]
    c = RET_C
    n_chunks = seq // c
    h, dk = RET_HEADS, RET_DK
    half = dk // 2
    inv = ROPE_BASE ** (-jnp.arange(half, dtype=F32) / half)
    ang = jnp.arange(seq, dtype=F32)[:, None] * inv[None, :]
    cos, sin = jnp.cos(ang), jnp.sin(ang)
    log_g = jnp.log(1.0 - 2.0 ** (-5.0 - jnp.arange(h, dtype=F32)))
    idx = jnp.arange(c, dtype=F32)
    diff = idx[:, None] - idx[None, :]
    dec = jnp.where(diff >= 0.0, jnp.exp(jnp.maximum(diff, 0.0)[None] * log_g[:, None, None]), 0.0)
    xi = jnp.broadcast_to(jnp.exp((idx + 1.0)[None] * log_g[:, None])[:, :, None], (h, c, dk))
    zeta = jnp.broadcast_to(jnp.exp((c - 1.0 - idx)[None] * log_g[:, None])[:, :, None], (h, c, dk))
    cd = jnp.exp(c * log_g)

    w = h * dk
    row = lambda b, n: (b * n_chunks + n)
    return pl.pallas_call(
        _ret_kernel,
        grid=(batch, n_chunks),
        in_specs=[
            pl.BlockSpec(memory_space=pltpu.SMEM),
            pl.BlockSpec((c, w), lambda b, n: (row(b, n), 0)),
            pl.BlockSpec((c, w), lambda b, n: (row(b, n), 1)),
            pl.BlockSpec((c, w), lambda b, n: (row(b, n), 2)),
            pl.BlockSpec((c, w), lambda b, n: (row(b, n), 3)),
            pl.BlockSpec((c, half), lambda b, n: (n, 0)),
            pl.BlockSpec((c, half), lambda b, n: (n, 0)),
            pl.BlockSpec((h, c, c), lambda b, n: (0, 0, 0)),
            pl.BlockSpec((h, c, dk), lambda b, n: (0, 0, 0)),
            pl.BlockSpec((h, c, dk), lambda b, n: (0, 0, 0)),
            pl.BlockSpec((1, w), lambda b, n: (0, 0)),
        ],
        out_specs=pl.BlockSpec((c, w), lambda b, n: (row(b, n), 0)),
        out_shape=jax.ShapeDtypeStruct((tok, w), BF16),
        scratch_shapes=[pltpu.VMEM((h, dk, RET_DV), F32)],
        compiler_params=pltpu.CompilerParams(
            dimension_semantics=("parallel", "arbitrary"), vmem_limit_bytes=VMEM_LIMIT),
        name="retention",
    )(cd, proj, proj, proj, proj, cos, sin, dec, xi, zeta, gain.reshape(1, w))


def _compress_kernel(k_ref, v_ref, pek_ref, w1k_ref, w1kf_ref, w2k_ref, pev_ref, w1v_ref, w1vf_ref,
                     w2v_ref, ok_ref, ov_ref):
    dh = NSA_DH
    nblk = k_ref.shape[0] // CMP_STRIDE

    def one(x_ref, pe_ref, w1_ref, w1f_ref, w2_ref, o_ref):
        ab = jnp.zeros((nblk, 2 * dh), F32)
        for l in range(CMP_STRIDE):
            rows = x_ref[pl.ds(l, nblk, stride=CMP_STRIDE), :].astype(BF16)
            ab = ab + _dot(rows, w1_ref[l])
        a, b = ab[:, :dh], ab[:, dh:]
        b_next = pltpu.roll(b, nblk - 1, axis=0)
        ridx = lax.broadcasted_iota(jnp.int32, (nblk, dh), 0)
        b_next = jnp.where(ridx < nblk - 1, b_next, 0.0)
        pe_term = _dot(pe_ref[...], w1f_ref[...])[0:1, :]
        hdn = a + b_next + pe_term
        o_ref[0, 0] = _dot((hdn * _sigmoid(hdn)).astype(BF16), w2_ref[...])

    one(k_ref, pek_ref, w1k_ref, w1kf_ref, w2k_ref, ok_ref)
    one(v_ref, pev_ref, w1v_ref, w1vf_ref, w2v_ref, ov_ref)


def _compress(kv, pe_k, w1_k, w2_k, pe_v, w1_v, w2_v, batch, seq):
    dh, g = NSA_DH, NSA_KV_GROUPS
    nblk = seq // CMP_STRIDE
    half = CMP_LEN // 2

    def prep(pe, w1, w2):
        w1b = w1.astype(BF16)
        w1_pair = jnp.concatenate([w1b[:half], w1b[half:]], axis=-1)
        pe_flat = jnp.broadcast_to(pe.reshape(1, CMP_LEN * dh), (8, CMP_LEN * dh)).astype(BF16)
        return pe_flat, w1_pair, w1b.reshape(CMP_LEN * dh, dh), w2.astype(BF16)

    args_k = prep(pe_k, w1_k, w2_k)
    args_v = prep(pe_v, w1_v, w2_v)
    kcol = 0
    vcol = KV_W // dh
    const = lambda shape: pl.BlockSpec(shape, lambda b, gg: (0,) * len(shape))
    wspecs = [const((8, CMP_LEN * dh)), const((half, dh, 2 * dh)), const((CMP_LEN * dh, dh)),
              const((dh, dh))]
    out_spec = pl.BlockSpec((1, 1, nblk, dh), lambda b, gg: (b, gg, 0, 0))
    return pl.pallas_call(
        _compress_kernel,
        grid=(batch, g),
        in_specs=[pl.BlockSpec((seq, dh), lambda b, gg: (b, kcol + gg)),
                  pl.BlockSpec((seq, dh), lambda b, gg: (b, vcol + gg))] + wspecs + wspecs,
        out_specs=[out_spec, out_spec],
        out_shape=[jax.ShapeDtypeStruct((batch, g, nblk, dh), F32)] * 2,
        compiler_params=pltpu.CompilerParams(
            dimension_semantics=("parallel", "parallel"), vmem_limit_bytes=VMEM_LIMIT),
        name="compress",
    )(kv, kv, *args_k, *args_v)


def _cmpattn_kernel(q_ref, kc_ref, vc_ref, tb_ref, pp_ref, ovt_ref, gate_ref, o_ref, sel_ref):
    tq = q_ref.shape[0]
    ncp = kc_ref.shape[2]
    dh = NSA_DH
    s0 = pl.program_id(2) * tq
    hpg = NSA_HPG
    pos_r = s0 + (lax.broadcasted_iota(jnp.int32, (hpg * tq, ncp), 0) & (tq - 1))
    n_c = lax.broadcasted_iota(jnp.int32, (hpg * tq, ncp), 1)
    mask = pos_r >= n_c * CMP_STRIDE + (CMP_LEN - 1)
    kc = kc_ref[0, 0].astype(BF16)
    vc = vc_ref[0, 0].astype(BF16)
    scale = dh ** -0.5
    q = jnp.concatenate([q_ref[:, hh * dh:(hh + 1) * dh] for hh in range(hpg)], axis=0).astype(BF16)
    bias = _dot(tb_ref[...].reshape(hpg * tq, tb_ref.shape[2]), pp_ref[0])
    lm = jnp.where(mask, _dot_nt(q, kc) * scale + bias, NEG)
    m = jnp.max(lm, axis=-1, keepdims=True)
    e = jnp.where(mask, jnp.exp(lm - m), 0.0)
    den = jnp.sum(e, axis=-1, keepdims=True)
    p = e / jnp.where(den > 0.0, den, 1.0)
    o = _dot(p.astype(BF16), vc)
    gate = _sigmoid(gate_ref[0, 0])
    psum = p[0:tq]
    for hh in range(hpg):
        o_ref[:, hh * dh:(hh + 1) * dh] = (o[hh * tq:(hh + 1) * tq] * gate[:, hh:hh + 1]).astype(o_ref.dtype)
        if hh:
            psum = psum + p[hh * tq:(hh + 1) * tq]
    nsel = ovt_ref.shape[0]
    imp_t = _dot_nt(ovt_ref[...], psum.astype(BF16))
    jb = lax.broadcasted_iota(jnp.int32, (nsel, tq), 0)
    pos = s0 + lax.broadcasted_iota(jnp.int32, (nsel, tq), 1)
    cur = jnp.right_shift(pos, SEL_LEN.bit_length() - 1)
    causal = jb * SEL_LEN <= pos
    forced = (jb == 0) | (jb == cur) | (jb == cur - 1)
    score = jnp.where(forced, FORCE, jnp.where(causal, imp_t, NEG))
    rank = jnp.zeros((nsel, tq), jnp.int32)
    for kk in range(nsel):
        row = score[kk:kk + 1, :]
        beats = (row > score) | ((row == score) & (jb > kk))
        rank = rank + jnp.where(beats, 1, 0)
    selb = jnp.where((rank < SEL_TOPK) & causal, 0.0, NEG)
    selb = jnp.concatenate([selb, jnp.zeros((SEL_PAD - nsel, tq), F32)], axis=0)
    sel_ref[0, 0] = selb.astype(BF16)


def _cmp_attention(proj, k_cmp, v_cmp, rel_bias, gates, batch, seq):
    dh, g, hpg = NSA_DH, NSA_KV_GROUPS, NSA_HPG
    tq = CMP_TQ
    nq = seq // tq
    ncp = k_cmp.shape[2]
    nsel = seq // SEL_LEN
    n_cmp = (seq - CMP_LEN) // CMP_STRIDE + 1
    cmp_idx = np.arange(n_cmp)[:, None] * CMP_STRIDE + np.arange(CMP_LEN)[None, :]
    overlap = ((cmp_idx // SEL_LEN)[:, :, None] == np.arange(nsel)[None, None, :]).sum(1) / CMP_LEN
    ovt = np.zeros((nsel, ncp), np.float32)
    ovt[:, :n_cmp] = overlap.T
    nb = tq // CMP_STRIDE
    assert 2 * nb + 1 <= CMP_BAND and CMP_STRIDE * (nb + 1) - (CMP_LEN - 1) >= REL_MAX_DIST
    i = np.arange(tq)[:, None]
    r = np.arange(CMP_BAND)[None, :]
    rel = np.where(r < 2 * nb, i - CMP_STRIDE * (r - nb) - (CMP_LEN - 1), REL_MAX_DIST)
    tb = jnp.where(jnp.asarray(r <= 2 * nb), _bias_lookup(rel_bias, jnp.asarray(rel, jnp.int32)), 0.0)
    tb_hi = tb.astype(BF16)
    tb_lo = (tb - tb_hi.astype(F32)).astype(BF16)
    tb = jnp.concatenate([tb_hi, tb_lo], axis=-1)
    n = np.arange(ncp)[None, None, :]
    first = (np.arange(nq) * nb - nb)[:, None, None]
    rr = np.arange(CMP_BAND)[None, :, None]
    pp = np.where(rr < 2 * nb, n == first + rr, (rr == 2 * nb) & (n < first)).astype(np.float32)
    pp = np.concatenate([pp, pp], axis=1)
    qcol = IN_OFF[4] // (hpg * dh)
    return pl.pallas_call(
        _cmpattn_kernel,
        grid=(batch, g, nq),
        in_specs=[
            pl.BlockSpec((tq, hpg * dh), lambda b, gg, t: (b * nq + t, qcol + gg)),
            pl.BlockSpec((1, 1, ncp, dh), lambda b, gg, t: (b, gg, 0, 0)),
            pl.BlockSpec((1, 1, ncp, dh), lambda b, gg, t: (b, gg, 0, 0)),
            pl.BlockSpec((hpg, tq, 2 * CMP_BAND), lambda b, gg, t: (gg, 0, 0)),
            pl.BlockSpec((1, 2 * CMP_BAND, ncp), lambda b, gg, t: (t, 0, 0)),
            pl.BlockSpec((nsel, ncp), lambda b, gg, t: (0, 0)),
            pl.BlockSpec((1, 1, tq, GATE_ROWS), lambda b, gg, t: (b, gg, t, 0)),
        ],
        out_specs=[
            pl.BlockSpec((tq, hpg * dh), lambda b, gg, t: (b * nq + t, gg)),
            pl.BlockSpec((1, 1, SEL_PAD, tq), lambda b, gg, t: (b, gg, 0, t)),
        ],
        out_shape=[jax.ShapeDtypeStruct((batch * seq, g * hpg * dh), BF16),
                   jax.ShapeDtypeStruct((batch, g, SEL_PAD, seq), BF16)],
        compiler_params=pltpu.CompilerParams(
            dimension_semantics=("parallel", "parallel", "parallel"), vmem_limit_bytes=VMEM_LIMIT),
        name="cmp_attn",
    )(proj, k_cmp, v_cmp, tb, jnp.asarray(pp, BF16), jnp.asarray(ovt, BF16),
      gates[:, :, 0].transpose(0, 1, 3, 2))


def _band_kernel(qi_ref, ki_ref, var_ref, first_ref, last_ref, q_ref, k_ref, v_ref, bias_ref, gate_ref,
                 *rest, use_sel):
    if use_sel:
        selb_ref, e_ref, o_ref, qt_ref, m_ref, acc_ref = rest
    else:
        o_ref, qt_ref, m_ref, acc_ref = rest
    t = pl.program_id(2)
    dh = NSA_DH
    tq = q_ref.shape[0]

    @pl.when(first_ref[t] == 1)
    def _():
        m_ref[...] = jnp.full_like(m_ref, NEG)
        acc_ref[...] = jnp.zeros_like(acc_ref)
        scale = dh ** -0.5 * LOG2E
        for hh in range(NSA_HPG):
            q = q_ref[:, hh * dh:(hh + 1) * dh].astype(F32)
            qt_ref[0:dh, hh * tq:(hh + 1) * tq] = (q * scale).T.astype(BF16)
            if use_sel:
                qt_ref[dh:, hh * tq:(hh + 1) * tq] = selb_ref[0, 0]

    k = k_ref[...].astype(BF16)
    if use_sel:
        k = jnp.concatenate([k, e_ref[...]], axis=1)
    vt = jnp.concatenate([v_ref[...].astype(F32).T, jnp.ones((ACC_PAD, k.shape[0]), F32)],
                         axis=0).astype(BF16)
    s = _dot(k, qt_ref[...]) + bias_ref[0, var_ref[t]]
    m_prev = m_ref[...]
    m_new = jnp.maximum(m_prev, jnp.max(s, axis=0, keepdims=True))
    alpha = jnp.exp2(m_prev - m_new)
    p = jnp.exp2(s - m_new).astype(BF16)
    acc_ref[...] = alpha * acc_ref[...] + _dot(vt, p)
    m_ref[...] = m_new

    @pl.when(last_ref[t] == 1)
    def _():
        gate = _sigmoid(gate_ref[0, 0, 0])
        for hh in range(NSA_HPG):
            cols = slice(hh * tq, (hh + 1) * tq)
            o = acc_ref[0:dh, cols] * (gate[hh:hh + 1, :] / acc_ref[dh:dh + 1, cols])
            o_ref[:, hh * dh:(hh + 1) * dh] = o.T.astype(o_ref.dtype)


def _band_steps(seq, window):
    tq, tk = ATT_T, ATT_TK
    steps = []
    for qi in range(seq // tq):
        k_hi = qi * tq // tk
        k_lo = max(0, (qi * tq - window + 1) // tk) if window else 0
        steps += [(qi, ki, min((qi * tq - ki * tk) // tq, ATT_NVAR - 1)) for ki in range(k_lo, k_hi + 1)]
    return steps


def _band_attention(proj, bias_tiles, steps, kcol, vcol, gates, branch, batch, seq, selb=None, e_mat=None):
    dh, g, hpg = NSA_DH, NSA_KV_GROUPS, NSA_HPG
    tq, tk = ATT_T, ATT_TK
    nq, nk = seq // tq, seq // tk
    qi = np.array([s[0] for s in steps], np.int32)
    ki = np.array([s[1] for s in steps], np.int32)
    var = np.array([s[2] for s in steps], np.int32)
    first = np.concatenate([[1], (qi[1:] != qi[:-1]).astype(np.int32)]).astype(np.int32)
    last = np.concatenate([(qi[1:] != qi[:-1]).astype(np.int32), [1]]).astype(np.int32)
    qcol = IN_OFF[4] // (hpg * dh)
    use_sel = selb is not None
    in_specs = [
        pl.BlockSpec((tq, hpg * dh), lambda b, gg, t, qi_r, ki_r, v_r, f_r, l_r: (b * nq + qi_r[t], qcol + gg)),
        pl.BlockSpec((tk, dh), lambda b, gg, t, qi_r, ki_r, v_r, f_r, l_r: (b * nk + ki_r[t], kcol + gg)),
        pl.BlockSpec((tk, dh), lambda b, gg, t, qi_r, ki_r, v_r, f_r, l_r: (b * nk + ki_r[t], vcol + gg)),
        pl.BlockSpec((1, bias_tiles.shape[1], tk, hpg * tq),
                     lambda b, gg, t, qi_r, ki_r, v_r, f_r, l_r: (gg, 0, 0, 0)),
        pl.BlockSpec((1, 1, 1, GATE_ROWS, tq),
                     lambda b, gg, t, qi_r, ki_r, v_r, f_r, l_r: (b, gg, branch, 0, qi_r[t])),
    ]
    args = [proj, proj, proj, bias_tiles, gates]
    if use_sel:
        in_specs += [
            pl.BlockSpec((1, 1, SEL_PAD, tq), lambda b, gg, t, qi_r, ki_r, v_r, f_r, l_r: (b, gg, 0, qi_r[t])),
            pl.BlockSpec((tk, SEL_PAD), lambda b, gg, t, qi_r, ki_r, v_r, f_r, l_r: (ki_r[t], 0)),
        ]
        args += [selb, e_mat]
    kdim = dh + SEL_PAD if use_sel else dh
    grid_spec = pltpu.PrefetchScalarGridSpec(
        num_scalar_prefetch=5,
        grid=(batch, g, len(steps)),
        in_specs=in_specs,
        out_specs=pl.BlockSpec((tq, hpg * dh),
                               lambda b, gg, t, qi_r, ki_r, v_r, f_r, l_r: (b * nq + qi_r[t], gg)),
        scratch_shapes=[pltpu.VMEM((kdim, hpg * tq), BF16), pltpu.VMEM((1, hpg * tq), F32),
                        pltpu.VMEM((dh + ACC_PAD, hpg * tq), F32)],
    )
    return pl.pallas_call(
        functools.partial(_band_kernel, use_sel=use_sel),
        grid_spec=grid_spec,
        out_shape=jax.ShapeDtypeStruct((batch * seq, g * hpg * dh), BF16),
        compiler_params=pltpu.CompilerParams(
            dimension_semantics=("parallel", "parallel", "arbitrary"), vmem_limit_bytes=VMEM_LIMIT),
        name="sel_attn" if use_sel else "win_attn",
    )(jnp.asarray(qi), jnp.asarray(ki), jnp.asarray(var), jnp.asarray(first), jnp.asarray(last), *args)


def _out_kernel(x_ref, yr_ref, oc_ref, os_ref, ow_ref, w_ref, o_ref):
    y_nsa = oc_ref[...].astype(F32) + os_ref[...].astype(F32) + ow_ref[...].astype(F32)
    y = jnp.concatenate([yr_ref[...], y_nsa.astype(BF16)], axis=-1)
    o_ref[...] = x_ref[...] + _dot(y, w_ref[...])


def _out_proj(x2d, y_ret, o_cmp, o_sel, o_win, w_out):
    tok, d = x2d.shape
    tm = OUT_TM
    wmix = y_ret.shape[1]
    row = lambda shape: pl.BlockSpec(shape, lambda i: (i, 0))
    return pl.pallas_call(
        _out_kernel,
        grid=(tok // tm,),
        in_specs=[row((tm, d)), row((tm, wmix)), row((tm, wmix)), row((tm, wmix)), row((tm, wmix)),
                  pl.BlockSpec(w_out.shape, lambda i: (0, 0))],
        out_specs=row((tm, d)),
        out_shape=jax.ShapeDtypeStruct((tok, d), F32),
        compiler_params=pltpu.CompilerParams(
            dimension_semantics=("parallel",), vmem_limit_bytes=VMEM_LIMIT),
        name="out_proj",
    )(x2d, y_ret, o_cmp, o_sel, o_win, w_out)


def _t5_bucket_of(rel):
    n = jnp.maximum(rel, 0)
    max_exact = REL_BUCKETS // 2
    nf = jnp.maximum(n, 1).astype(F32)
    large = max_exact + (jnp.log(nf / max_exact) / math.log(REL_MAX_DIST / max_exact)
                         * (REL_BUCKETS - max_exact)).astype(jnp.int32)
    large = jnp.minimum(large, REL_BUCKETS - 1)
    return jnp.where(n < max_exact, n, large)


def _bias_lookup(rel_bias, rel):
    bucket = _t5_bucket_of(rel)[None]
    tab = rel_bias.astype(F32).reshape((rel_bias.shape[0], REL_BUCKETS) + (1,) * rel.ndim)
    out = jnp.zeros((rel_bias.shape[0],) + rel.shape, F32)
    for b in range(REL_BUCKETS):
        out = jnp.where(bucket == b, tab[:, b], out)
    return out


def _bias_tiles(rel_bias, window):
    t, g, hpg = ATT_T, NSA_KV_GROUPS, NSA_HPG
    assert ATT_TK == t and t >= REL_MAX_DIST and (not window or window == t)
    j = np.arange(t)[:, None]
    i = np.tile(np.arange(t), hpg)[None, :]
    bucket = _t5_bucket_of(jnp.asarray((i - j) % t, jnp.int32))[None]
    val = jnp.repeat(rel_bias.astype(F32).reshape(g, 1, hpg, REL_BUCKETS) * LOG2E, t, axis=2)
    cyc = jnp.zeros((g, t, hpg * t), F32)
    for b in range(REL_BUCKETS):
        cyc = jnp.where(bucket == b, val[..., b], cyc)
    far = val[..., REL_BUCKETS - 1]
    upper = jnp.asarray(i >= j)[None]
    tile0 = jnp.where(upper, cyc, NEG)
    if window:
        tiles = [tile0, jnp.where(upper, NEG, cyc)]
    else:
        tiles = [tile0, jnp.where(upper, far, cyc), jnp.broadcast_to(far, cyc.shape)]
    return jnp.stack(tiles, axis=1)


def _token_mix(x2d, mix_norm, w_in, ret_gn_gain, pe_k, w1_k, w2_k, pe_v, w1_v, w2_v, w_out, rel_bias,
               batch, seq):
    d = x2d.shape[1]
    dh = NSA_DH
    w_pad = jnp.concatenate([w_in.astype(BF16), jnp.zeros((d, D_IN_PAD - D_IN), BF16)], axis=1)
    proj, cmp_kv = _inproj(x2d, mix_norm, w_pad)
    y_ret = _retention(proj, ret_gn_gain, batch, seq)

    gates = proj[:, IN_OFF[11]:IN_OFF[11] + IN_SPLITS[11]].astype(F32)
    gates = gates.reshape(batch, seq, NSA_KV_GROUPS, NSA_HPG, N_BRANCH).transpose(0, 2, 4, 3, 1)
    gates = jnp.pad(gates, ((0, 0), (0, 0), (0, 0), (0, GATE_ROWS - NSA_HPG), (0, 0)))

    k_cmp, v_cmp = _compress(cmp_kv, pe_k, w1_k, w2_k, pe_v, w1_v, w2_v, batch, seq)
    o_cmp, selb = _cmp_attention(proj, k_cmp, v_cmp, rel_bias, gates, batch, seq)

    e_np = np.zeros((seq, SEL_PAD), np.float32)
    e_np[np.arange(seq), np.arange(seq) // SEL_LEN] = 1.0
    o_sel = _band_attention(proj, _bias_tiles(rel_bias, 0), _band_steps(seq, 0),
                            IN_OFF[7] // dh, IN_OFF[8] // dh, gates, 1, batch, seq,
                            selb=selb, e_mat=jnp.asarray(e_np, BF16))
    o_win = _band_attention(proj, _bias_tiles(rel_bias, WIN), _band_steps(seq, WIN),
                            IN_OFF[9] // dh, IN_OFF[10] // dh, gates, 2, batch, seq)
    return _out_proj(x2d, y_ret, o_cmp, o_sel, o_win, w_out.astype(BF16))


def kernel(x, ffn1_norm, ffn1_w1, ffn1_w3, ffn1_w2, mix_norm, w_in, ret_gn_gain, cmp_pe_k, cmp_w1_k,
           cmp_w2_k, cmp_pe_v, cmp_w1_v, cmp_w2_v, w_out, ffn2_norm, ffn2_w1, ffn2_w3, ffn2_w2,
           rel_bias, final_norm):
    batch, seq, d = x.shape
    depth = ffn1_norm.shape[0]
    h = x.reshape(batch * seq, d)
    for l in range(depth):
        last = l == depth - 1
        h = _ffn(h, ffn1_norm[l], ffn1_w1[l].astype(BF16), ffn1_w3[l].astype(BF16),
                 ffn1_w2[l].astype(BF16), final_norm, False)
        h = _token_mix(h, mix_norm[l], w_in[l], ret_gn_gain[l], cmp_pe_k[l], cmp_w1_k[l], cmp_w2_k[l],
                       cmp_pe_v[l], cmp_w1_v[l], cmp_w2_v[l], w_out[l], rel_bias, batch, seq)
        h = _ffn(h, ffn2_norm[l], ffn2_w1[l].astype(BF16), ffn2_w3[l].astype(BF16),
                 ffn2_w2[l].astype(BF16), final_norm, last)
    if depth == 0:
        raise ValueError("depth must be positive")
    return h.reshape(batch, seq, d)
```

```python
import functools
import math

import jax
import jax.numpy as jnp
import numpy as np
from jax import lax
from jax.experimental import pallas as pl
from jax.experimental.pallas import tpu as pltpu

F32 = jnp.float32
BF16 = jnp.bfloat16

RET_HEADS = 4
RET_DK = 256
RET_DV = 256
ROPE_BASE = 10000.0
NSA_HEADS = 8
NSA_KV_GROUPS = 2
NSA_HPG = NSA_HEADS // NSA_KV_GROUPS
NSA_DH = 128
CMP_LEN = 32
CMP_STRIDE = 16
SEL_LEN = 64
SEL_TOPK = 16
WIN = 512
N_BRANCH = 3
REL_BUCKETS = 32
REL_MAX_DIST = 128
EPS = 1e-6
NEG = -1e30
FORCE = 1e4

RET_W = RET_HEADS * RET_DV
NSA_W = NSA_HEADS * NSA_DH
KV_W = NSA_KV_GROUPS * NSA_DH
IN_SPLITS = [RET_HEADS * RET_DK, RET_HEADS * RET_DK, RET_W, RET_W, NSA_W,
             KV_W, KV_W, KV_W, KV_W, KV_W, KV_W, NSA_HEADS * N_BRANCH]
D_IN = sum(IN_SPLITS)
IN_OFF = [sum(IN_SPLITS[:i]) for i in range(len(IN_SPLITS))]

LANE = 128
IN_TN = 2304
D_IN_PAD = 3 * IN_TN
CMP_KV_TILE = IN_OFF[5] // IN_TN
CMP_KV_LOCAL = IN_OFF[5] % IN_TN
assert CMP_KV_LOCAL + 2 * KV_W <= IN_TN and IN_OFF[6] == IN_OFF[5] + KV_W
GATE_ROWS = 8
FFN_TM = 1024
FFN_TF = 512
IN_TM = 512
RET_C = 256
CMP_TQ = 256
ATT_T = 512
ATT_TK = 512
ATT_NVAR = 3
OUT_TM = 256
SEL_PAD = 128
ACC_PAD = 8
CMP_BAND = 64
VMEM_LIMIT = 56 * 1024 * 1024
CAST_BLOCK_BYTES = 6 * 1024 * 1024
LOG2E = math.log2(math.e)


def _dot(a, b):
    return jnp.dot(a, b, preferred_element_type=F32)


def _dot_nt(a, b):
    return lax.dot_general(a, b, (((1,), (1,)), ((), ())), preferred_element_type=F32)


def _dot_tn(a, b):
    return lax.dot_general(a, b, (((0,), (0,)), ((), ())), preferred_element_type=F32)


def _sigmoid(x):
    return 1.0 / (1.0 + jnp.exp(-x))


def _rms(x, g):
    ms = jnp.mean(x * x, axis=-1, keepdims=True)
    return x * lax.rsqrt(ms + EPS) * g


def _ffn_kernel(x_ref, g_ref, w1_ref, w3_ref, w2_ref, fg_ref, o_ref, n_ref, *, final_norm):
    j = pl.program_id(1)

    @pl.when(j == 0)
    def _():
        x = x_ref[...]
        n_ref[...] = _rms(x, g_ref[...]).astype(BF16)
        o_ref[...] = x

    n = n_ref[...]
    a = _dot(n, w1_ref[...])
    b = _dot(n, w3_ref[...])
    h = (0.5 * a * _sigmoid(a) * b).astype(BF16)
    o_ref[...] += _dot(h, w2_ref[...])

    if final_norm:
        @pl.when(j == pl.num_programs(1) - 1)
        def _():
            o_ref[...] = _rms(o_ref[...], fg_ref[...])


def _ffn(x2d, g, w1, w3, w2, fg, final_norm):
    tok, d = x2d.shape
    dff = w1.shape[1]
    tm, tf = FFN_TM, FFN_TF
    return pl.pallas_call(
        functools.partial(_ffn_kernel, final_norm=final_norm),
        grid=(tok // tm, dff // tf),
        in_specs=[
            pl.BlockSpec((tm, d), lambda i, j: (i, 0)),
            pl.BlockSpec((1, d), lambda i, j: (0, 0)),
            pl.BlockSpec((d, tf), lambda i, j: (0, j)),
            pl.BlockSpec((d, tf), lambda i, j: (0, j)),
            pl.BlockSpec((tf, d), lambda i, j: (j, 0)),
            pl.BlockSpec((1, d), lambda i, j: (0, 0)),
        ],
        out_specs=pl.BlockSpec((tm, d), lambda i, j: (i, 0)),
        out_shape=jax.ShapeDtypeStruct((tok, d), F32),
        scratch_shapes=[pltpu.VMEM((tm, d), BF16)],
        compiler_params=pltpu.CompilerParams(
            dimension_semantics=("parallel", "arbitrary"), vmem_limit_bytes=VMEM_LIMIT),
        name="ffn",
    )(x2d, g.reshape(1, d), w1, w3, w2, fg.reshape(1, d))


def _inproj_kernel(x_ref, g_ref, w_ref, o_ref, kv_ref, n_ref):
    j = pl.program_id(1)

    @pl.when(j == 0)
    def _():
        n_ref[...] = _rms(x_ref[...], g_ref[...]).astype(BF16)

    res = _dot(n_ref[...], w_ref[...])
    o_ref[...] = res.astype(o_ref.dtype)

    @pl.when(j == CMP_KV_TILE)
    def _():
        kv_ref[...] = res[:, CMP_KV_LOCAL:CMP_KV_LOCAL + 2 * KV_W]


def _inproj(x2d, g, w_pad):
    tok, d = x2d.shape
    n_out = w_pad.shape[1]
    tm, tn = IN_TM, IN_TN
    return pl.pallas_call(
        _inproj_kernel,
        grid=(tok // tm, n_out // tn),
        in_specs=[
            pl.BlockSpec((tm, d), lambda i, j: (i, 0)),
            pl.BlockSpec((1, d), lambda i, j: (0, 0)),
            pl.BlockSpec((d, tn), lambda i, j: (0, j)),
        ],
        out_specs=[pl.BlockSpec((tm, tn), lambda i, j: (i, j)),
                   pl.BlockSpec((tm, 2 * KV_W), lambda i, j: (i, 0))],
        out_shape=[jax.ShapeDtypeStruct((tok, n_out), BF16),
                   jax.ShapeDtypeStruct((tok, 2 * KV_W), F32)],
        scratch_shapes=[pltpu.VMEM((tm, d), BF16)],
        compiler_params=pltpu.CompilerParams(
            dimension_semantics=("parallel", "arbitrary"), vmem_limit_bytes=VMEM_LIMIT),
        name="in_proj",
    )(x2d, g.reshape(1, d), w_pad)


def _ret_kernel(cd_ref, q_ref, k_ref, v_ref, g_ref, cos_ref, sin_ref, dec_ref, xi_ref, zeta_ref,
                gain_ref, o_ref, state_ref):
    @pl.when(pl.program_id(1) == 0)
    def _():
        state_ref[...] = jnp.zeros_like(state_ref)

    cos = cos_ref[...]
    sin = sin_ref[...]
    half = RET_DK // 2

    def rot(t):
        t1, t2 = t[:, :half], t[:, half:]
        return jnp.concatenate([t1 * cos - t2 * sin, t1 * sin + t2 * cos], axis=-1)

    for h in range(RET_HEADS):
        cs = slice(h * RET_DK, (h + 1) * RET_DK)
        q = rot(q_ref[:, cs].astype(F32))
        k = rot(k_ref[:, cs].astype(F32)) * (RET_DK ** -0.5)
        v = v_ref[:, cs].astype(BF16)
        qb = q.astype(BF16)
        s = _dot_nt(qb, k.astype(BF16)) * dec_ref[h]
        state = state_ref[h]
        y = _dot(s.astype(BF16), v) + _dot((q * xi_ref[h]).astype(BF16), state.astype(BF16))
        state_ref[h] = cd_ref[h] * state + _dot_tn((k * zeta_ref[h]).astype(BF16), v)
        mu = jnp.mean(y, axis=-1, keepdims=True)
        yc = y - mu
        var = jnp.mean(yc * yc, axis=-1, keepdims=True)
        yn = yc * lax.rsqrt(var + EPS) * gain_ref[:, cs]
        gate = g_ref[:, cs].astype(F32)
        o_ref[:, cs] = (gate * _sigmoid(gate) * yn).astype(o_ref.dtype)


def _retention(proj, gain, batch, seq):
    tok = proj.shape[0]
    c = RET_C
    n_chunks = seq // c
    h, dk = RET_HEADS, RET_DK
    half = dk // 2
    inv = ROPE_BASE ** (-jnp.arange(half, dtype=F32) / half)
    ang = jnp.arange(seq, dtype=F32)[:, None] * inv[None, :]
    cos, sin = jnp.cos(ang), jnp.sin(ang)
    log_g = jnp.log(1.0 - 2.0 ** (-5.0 - jnp.arange(h, dtype=F32)))
    idx = jnp.arange(c, dtype=F32)
    diff = idx[:, None] - idx[None, :]
    dec = jnp.where(diff >= 0.0, jnp.exp(jnp.maximum(diff, 0.0)[None] * log_g[:, None, None]), 0.0)
    xi = jnp.broadcast_to(jnp.exp((idx + 1.0)[None] * log_g[:, None])[:, :, None], (h, c, dk))
    zeta = jnp.broadcast_to(jnp.exp((c - 1.0 - idx)[None] * log_g[:, None])[:, :, None], (h, c, dk))
    cd = jnp.exp(c * log_g)

    w = h * dk
    row = lambda b, n: (b * n_chunks + n)
    return pl.pallas_call(
        _ret_kernel,
        grid=(batch, n_chunks),
        in_specs=[
            pl.BlockSpec(memory_space=pltpu.SMEM),
            pl.BlockSpec((c, w), lambda b, n: (row(b, n), 0)),
            pl.BlockSpec((c, w), lambda b, n: (row(b, n), 1)),
            pl.BlockSpec((c, w), lambda b, n: (row(b, n), 2)),
            pl.BlockSpec((c, w), lambda b, n: (row(b, n), 3)),
            pl.BlockSpec((c, half), lambda b, n: (n, 0)),
            pl.BlockSpec((c, half), lambda b, n: (n, 0)),
            pl.BlockSpec((h, c, c), lambda b, n: (0, 0, 0)),
            pl.BlockSpec((h, c, dk), lambda b, n: (0, 0, 0)),
            pl.BlockSpec((h, c, dk), lambda b, n: (0, 0, 0)),
            pl.BlockSpec((1, w), lambda b, n: (0, 0)),
        ],
        out_specs=pl.BlockSpec((c, w), lambda b, n: (row(b, n), 0)),
        out_shape=jax.ShapeDtypeStruct((tok, w), BF16),
        scratch_shapes=[pltpu.VMEM((h, dk, RET_DV), F32)],
        compiler_params=pltpu.CompilerParams(
            dimension_semantics=("parallel", "arbitrary"), vmem_limit_bytes=VMEM_LIMIT),
        name="retention",
    )(cd, proj, proj, proj, proj, cos, sin, dec, xi, zeta, gain.reshape(1, w))


def _compress_kernel(k_ref, v_ref, pek_ref, w1k_ref, w1kf_ref, w2k_ref, pev_ref, w1v_ref, w1vf_ref,
                     w2v_ref, ok_ref, ov_ref):
    dh = NSA_DH
    nblk = k_ref.shape[0] // CMP_STRIDE

    def one(x_ref, pe_ref, w1_ref, w1f_ref, w2_ref, o_ref):
        ab = jnp.zeros((nblk, 2 * dh), F32)
        for l in range(CMP_STRIDE):
            rows = x_ref[pl.ds(l, nblk, stride=CMP_STRIDE), :].astype(BF16)
            ab = ab + _dot(rows, w1_ref[l])
        a, b = ab[:, :dh], ab[:, dh:]
        b_next = pltpu.roll(b, nblk - 1, axis=0)
        ridx = lax.broadcasted_iota(jnp.int32, (nblk, dh), 0)
        b_next = jnp.where(ridx < nblk - 1, b_next, 0.0)
        pe_term = _dot(pe_ref[...], w1f_ref[...])[0:1, :]
        hdn = a + b_next + pe_term
        o_ref[0, 0] = _dot((hdn * _sigmoid(hdn)).astype(BF16), w2_ref[...])

    one(k_ref, pek_ref, w1k_ref, w1kf_ref, w2k_ref, ok_ref)
    one(v_ref, pev_ref, w1v_ref, w1vf_ref, w2v_ref, ov_ref)


def _compress(kv, pe_k, w1_k, w2_k, pe_v, w1_v, w2_v, batch, seq):
    dh, g = NSA_DH, NSA_KV_GROUPS
    nblk = seq // CMP_STRIDE
    half = CMP_LEN // 2

    def prep(pe, w1, w2):
        w1b = w1.astype(BF16)
        w1_pair = jnp.concatenate([w1b[:half], w1b[half:]], axis=-1)
        pe_flat = jnp.broadcast_to(pe.reshape(1, CMP_LEN * dh), (8, CMP_LEN * dh)).astype(BF16)
        return pe_flat, w1_pair, w1b.reshape(CMP_LEN * dh, dh), w2.astype(BF16)

    args_k = prep(pe_k, w1_k, w2_k)
    args_v = prep(pe_v, w1_v, w2_v)
    kcol = 0
    vcol = KV_W // dh
    const = lambda shape: pl.BlockSpec(shape, lambda b, gg: (0,) * len(shape))
    wspecs = [const((8, CMP_LEN * dh)), const((half, dh, 2 * dh)), const((CMP_LEN * dh, dh)),
              const((dh, dh))]
    out_spec = pl.BlockSpec((1, 1, nblk, dh), lambda b, gg: (b, gg, 0, 0))
    return pl.pallas_call(
        _compress_kernel,
        grid=(batch, g),
        in_specs=[pl.BlockSpec((seq, dh), lambda b, gg: (b, kcol + gg)),
                  pl.BlockSpec((seq, dh), lambda b, gg: (b, vcol + gg))] + wspecs + wspecs,
        out_specs=[out_spec, out_spec],
        out_shape=[jax.ShapeDtypeStruct((batch, g, nblk, dh), F32)] * 2,
        compiler_params=pltpu.CompilerParams(
            dimension_semantics=("parallel", "parallel"), vmem_limit_bytes=VMEM_LIMIT),
        name="compress",
    )(kv, kv, *args_k, *args_v)


def _cmpattn_kernel(q_ref, kc_ref, vc_ref, tb_ref, pp_ref, ovt_ref, gate_ref, o_ref, sel_ref, rank_ref):
    tq = q_ref.shape[0]
    ncp = kc_ref.shape[2]
    dh = NSA_DH
    s0 = pl.program_id(2) * tq
    hpg = NSA_HPG
    pos_r = s0 + (lax.broadcasted_iota(jnp.int32, (hpg * tq, ncp), 0) & (tq - 1))
    n_c = lax.broadcasted_iota(jnp.int32, (hpg * tq, ncp), 1)
    mask = pos_r >= n_c * CMP_STRIDE + (CMP_LEN - 1)
    kc = kc_ref[0, 0].astype(BF16)
    vc = vc_ref[0, 0].astype(BF16)
    scale = dh ** -0.5
    q = jnp.concatenate([q_ref[:, hh * dh:(hh + 1) * dh] for hh in range(hpg)], axis=0).astype(BF16)
    bias = _dot(tb_ref[...].reshape(hpg * tq, tb_ref.shape[2]), pp_ref[0])
    lm = jnp.where(mask, _dot_nt(q, kc) * scale + bias, NEG)
    m = jnp.max(lm, axis=-1, keepdims=True)
    e = jnp.where(mask, jnp.exp(lm - m), 0.0)
    den = jnp.sum(e, axis=-1, keepdims=True)
    p = e / jnp.where(den > 0.0, den, 1.0)
    o = _dot(p.astype(BF16), vc)
    gate = _sigmoid(gate_ref[0, 0])
    psum = p[0:tq]
    for hh in range(hpg):
        o_ref[:, hh * dh:(hh + 1) * dh] = (o[hh * tq:(hh + 1) * tq] * gate[:, hh:hh + 1]).astype(o_ref.dtype)
        if hh:
            psum = psum + p[hh * tq:(hh + 1) * tq]
    nsel = ovt_ref.shape[0]
    imp_t = _dot_nt(ovt_ref[...], psum.astype(BF16))
    jb = lax.broadcasted_iota(jnp.int32, (nsel, tq), 0)
    pos = s0 + lax.broadcasted_iota(jnp.int32, (nsel, tq), 1)
    cur = jnp.right_shift(pos, SEL_LEN.bit_length() - 1)
    causal = jb * SEL_LEN <= pos
    forced = (jb == 0) | (jb == cur) | (jb == cur - 1)
    score = jnp.where(forced, FORCE, jnp.where(causal, imp_t, NEG))
    per_tile = tq // SEL_LEN
    rank_ref[...] = jnp.zeros_like(rank_ref)
    for c in range(nsel // per_tile):
        @pl.when(c <= pl.program_id(2))
        def _():
            rank = rank_ref[...]
            for kk in range(c * per_tile, (c + 1) * per_tile):
                row = score[kk:kk + 1, :]
                beats = (row > score) | ((row == score) & (jb > kk))
                rank = rank + jnp.where(beats, 1, 0)
            rank_ref[...] = rank
    rank = rank_ref[...]
    selb = jnp.where((rank < SEL_TOPK) & causal, 0.0, NEG)
    selb = jnp.concatenate([selb, jnp.zeros((SEL_PAD - nsel, tq), F32)], axis=0)
    sel_ref[0, 0] = selb.astype(BF16)


def _cmp_attention(proj, k_cmp, v_cmp, rel_bias, gates, batch, seq):
    dh, g, hpg = NSA_DH, NSA_KV_GROUPS, NSA_HPG
    tq = CMP_TQ
    nq = seq // tq
    ncp = k_cmp.shape[2]
    nsel = seq // SEL_LEN
    n_cmp = (seq - CMP_LEN) // CMP_STRIDE + 1
    cmp_idx = np.arange(n_cmp)[:, None] * CMP_STRIDE + np.arange(CMP_LEN)[None, :]
    overlap = ((cmp_idx // SEL_LEN)[:, :, None] == np.arange(nsel)[None, None, :]).sum(1) / CMP_LEN
    ovt = np.zeros((nsel, ncp), np.float32)
    ovt[:, :n_cmp] = overlap.T
    nb = tq // CMP_STRIDE
    assert 2 * nb + 1 <= CMP_BAND and CMP_STRIDE * (nb + 1) - (CMP_LEN - 1) >= REL_MAX_DIST
    i = np.arange(tq)[:, None]
    r = np.arange(CMP_BAND)[None, :]
    rel = np.where(r < 2 * nb, i - CMP_STRIDE * (r - nb) - (CMP_LEN - 1), REL_MAX_DIST)
    tb = jnp.where(jnp.asarray(r <= 2 * nb), _bias_lookup(rel_bias, jnp.asarray(rel, jnp.int32)), 0.0)
    tb_hi = tb.astype(BF16)
    tb_lo = (tb - tb_hi.astype(F32)).astype(BF16)
    tb = jnp.concatenate([tb_hi, tb_lo], axis=-1)
    n = np.arange(ncp)[None, None, :]
    first = (np.arange(nq) * nb - nb)[:, None, None]
    rr = np.arange(CMP_BAND)[None, :, None]
    pp = np.where(rr < 2 * nb, n == first + rr, (rr == 2 * nb) & (n < first)).astype(np.float32)
    pp = np.concatenate([pp, pp], axis=1)
    qcol = IN_OFF[4] // (hpg * dh)
    return pl.pallas_call(
        _cmpattn_kernel,
        grid=(batch, g, nq),
        in_specs=[
            pl.BlockSpec((tq, hpg * dh), lambda b, gg, t: (b * nq + t, qcol + gg)),
            pl.BlockSpec((1, 1, ncp, dh), lambda b, gg, t: (b, gg, 0, 0)),
            pl.BlockSpec((1, 1, ncp, dh), lambda b, gg, t: (b, gg, 0, 0)),
            pl.BlockSpec((hpg, tq, 2 * CMP_BAND), lambda b, gg, t: (gg, 0, 0)),
            pl.BlockSpec((1, 2 * CMP_BAND, ncp), lambda b, gg, t: (t, 0, 0)),
            pl.BlockSpec((nsel, ncp), lambda b, gg, t: (0, 0)),
            pl.BlockSpec((1, 1, tq, GATE_ROWS), lambda b, gg, t: (b, gg, t, 0)),
        ],
        out_specs=[
            pl.BlockSpec((tq, hpg * dh), lambda b, gg, t: (b * nq + t, gg)),
            pl.BlockSpec((1, 1, SEL_PAD, tq), lambda b, gg, t: (b, gg, 0, t)),
        ],
        out_shape=[jax.ShapeDtypeStruct((batch * seq, g * hpg * dh), BF16),
                   jax.ShapeDtypeStruct((batch, g, SEL_PAD, seq), BF16)],
        scratch_shapes=[pltpu.VMEM((nsel, tq), jnp.int32)],
        compiler_params=pltpu.CompilerParams(
            dimension_semantics=("parallel", "parallel", "parallel"), vmem_limit_bytes=VMEM_LIMIT),
        name="cmp_attn",
    )(proj, k_cmp, v_cmp, tb, jnp.asarray(pp, BF16), jnp.asarray(ovt, BF16),
      gates[:, :, 0].transpose(0, 1, 3, 2))


def _band_kernel(qi_ref, ki_ref, var_ref, first_ref, last_ref, q_ref, k_ref, v_ref, bias_ref, gate_ref,
                 *rest, use_sel):
    if use_sel:
        selb_ref, e_ref, o_ref, qt_ref, m_ref, acc_ref = rest
    else:
        o_ref, qt_ref, m_ref, acc_ref = rest
    t = pl.program_id(2)
    dh = NSA_DH
    tq = q_ref.shape[0]

    @pl.when(first_ref[t] == 1)
    def _():
        m_ref[...] = jnp.full_like(m_ref, NEG)
        acc_ref[...] = jnp.zeros_like(acc_ref)
        scale = dh ** -0.5 * LOG2E
        for hh in range(NSA_HPG):
            q = q_ref[:, hh * dh:(hh + 1) * dh].astype(F32)
            qt_ref[0:dh, hh * tq:(hh + 1) * tq] = (q * scale).T.astype(BF16)
            if use_sel:
                qt_ref[dh:, hh * tq:(hh + 1) * tq] = selb_ref[0, 0]

    k = k_ref[...].astype(BF16)
    if use_sel:
        k = jnp.concatenate([k, e_ref[...]], axis=1)
    vt = jnp.concatenate([v_ref[...].astype(F32).T, jnp.ones((ACC_PAD, k.shape[0]), F32)],
                         axis=0).astype(BF16)
    s = _dot(k, qt_ref[...]) + bias_ref[0, var_ref[t]]
    m_prev = m_ref[...]
    m_new = jnp.maximum(m_prev, jnp.max(s, axis=0, keepdims=True))
    alpha = jnp.exp2(m_prev - m_new)
    p = jnp.exp2(s - m_new).astype(BF16)
    acc_ref[...] = alpha * acc_ref[...] + _dot(vt, p)
    m_ref[...] = m_new

    @pl.when(last_ref[t] == 1)
    def _():
        gate = _sigmoid(gate_ref[0, 0, 0])
        for hh in range(NSA_HPG):
            cols = slice(hh * tq, (hh + 1) * tq)
            o = acc_ref[0:dh, cols] * (gate[hh:hh + 1, :] / acc_ref[dh:dh + 1, cols])
            o_ref[:, hh * dh:(hh + 1) * dh] = o.T.astype(o_ref.dtype)


def _band_steps(seq, window):
    tq, tk = ATT_T, ATT_TK
    steps = []
    for qi in range(seq // tq):
        k_hi = qi * tq // tk
        k_lo = max(0, (qi * tq - window + 1) // tk) if window else 0
        steps += [(qi, ki, min((qi * tq - ki * tk) // tq, ATT_NVAR - 1)) for ki in range(k_lo, k_hi + 1)]
    return steps


def _band_attention(proj, bias_tiles, steps, kcol, vcol, gates, branch, batch, seq, selb=None, e_mat=None):
    dh, g, hpg = NSA_DH, NSA_KV_GROUPS, NSA_HPG
    tq, tk = ATT_T, ATT_TK
    nq, nk = seq // tq, seq // tk
    qi = np.array([s[0] for s in steps], np.int32)
    ki = np.array([s[1] for s in steps], np.int32)
    var = np.array([s[2] for s in steps], np.int32)
    first = np.concatenate([[1], (qi[1:] != qi[:-1]).astype(np.int32)]).astype(np.int32)
    last = np.concatenate([(qi[1:] != qi[:-1]).astype(np.int32), [1]]).astype(np.int32)
    qcol = IN_OFF[4] // (hpg * dh)
    use_sel = selb is not None
    in_specs = [
        pl.BlockSpec((tq, hpg * dh), lambda b, gg, t, qi_r, ki_r, v_r, f_r, l_r: (b * nq + qi_r[t], qcol + gg)),
        pl.BlockSpec((tk, dh), lambda b, gg, t, qi_r, ki_r, v_r, f_r, l_r: (b * nk + ki_r[t], kcol + gg)),
        pl.BlockSpec((tk, dh), lambda b, gg, t, qi_r, ki_r, v_r, f_r, l_r: (b * nk + ki_r[t], vcol + gg)),
        pl.BlockSpec((1, bias_tiles.shape[1], tk, hpg * tq),
                     lambda b, gg, t, qi_r, ki_r, v_r, f_r, l_r: (gg, 0, 0, 0)),
        pl.BlockSpec((1, 1, 1, GATE_ROWS, tq),
                     lambda b, gg, t, qi_r, ki_r, v_r, f_r, l_r: (b, gg, branch, 0, qi_r[t])),
    ]
    args = [proj, proj, proj, bias_tiles, gates]
    if use_sel:
        in_specs += [
            pl.BlockSpec((1, 1, SEL_PAD, tq), lambda b, gg, t, qi_r, ki_r, v_r, f_r, l_r: (b, gg, 0, qi_r[t])),
            pl.BlockSpec((tk, SEL_PAD), lambda b, gg, t, qi_r, ki_r, v_r, f_r, l_r: (ki_r[t], 0)),
        ]
        args += [selb, e_mat]
    kdim = dh + SEL_PAD if use_sel else dh
    grid_spec = pltpu.PrefetchScalarGridSpec(
        num_scalar_prefetch=5,
        grid=(batch, g, len(steps)),
        in_specs=in_specs,
        out_specs=pl.BlockSpec((tq, hpg * dh),
                               lambda b, gg, t, qi_r, ki_r, v_r, f_r, l_r: (b * nq + qi_r[t], gg)),
        scratch_shapes=[pltpu.VMEM((kdim, hpg * tq), BF16), pltpu.VMEM((1, hpg * tq), F32),
                        pltpu.VMEM((dh + ACC_PAD, hpg * tq), F32)],
    )
    return pl.pallas_call(
        functools.partial(_band_kernel, use_sel=use_sel),
        grid_spec=grid_spec,
        out_shape=jax.ShapeDtypeStruct((batch * seq, g * hpg * dh), BF16),
        compiler_params=pltpu.CompilerParams(
            dimension_semantics=("parallel", "parallel", "arbitrary"), vmem_limit_bytes=VMEM_LIMIT),
        name="sel_attn" if use_sel else "win_attn",
    )(jnp.asarray(qi), jnp.asarray(ki), jnp.asarray(var), jnp.asarray(first), jnp.asarray(last), *args)


def _out_kernel(x_ref, yr_ref, oc_ref, os_ref, ow_ref, w_ref, o_ref):
    y_nsa = oc_ref[...].astype(F32) + os_ref[...].astype(F32) + ow_ref[...].astype(F32)
    y = jnp.concatenate([yr_ref[...], y_nsa.astype(BF16)], axis=-1)
    o_ref[...] = x_ref[...] + _dot(y, w_ref[...])


def _out_proj(x2d, y_ret, o_cmp, o_sel, o_win, w_out):
    tok, d = x2d.shape
    tm = OUT_TM
    wmix = y_ret.shape[1]
    row = lambda shape: pl.BlockSpec(shape, lambda i: (i, 0))
    return pl.pallas_call(
        _out_kernel,
        grid=(tok // tm,),
        in_specs=[row((tm, d)), row((tm, wmix)), row((tm, wmix)), row((tm, wmix)), row((tm, wmix)),
                  pl.BlockSpec(w_out.shape, lambda i: (0, 0))],
        out_specs=row((tm, d)),
        out_shape=jax.ShapeDtypeStruct((tok, d), F32),
        compiler_params=pltpu.CompilerParams(
            dimension_semantics=("parallel",), vmem_limit_bytes=VMEM_LIMIT),
        name="out_proj",
    )(x2d, y_ret, o_cmp, o_sel, o_win, w_out)


def _t5_bucket_of(rel):
    n = jnp.maximum(rel, 0)
    max_exact = REL_BUCKETS // 2
    nf = jnp.maximum(n, 1).astype(F32)
    large = max_exact + (jnp.log(nf / max_exact) / math.log(REL_MAX_DIST / max_exact)
                         * (REL_BUCKETS - max_exact)).astype(jnp.int32)
    large = jnp.minimum(large, REL_BUCKETS - 1)
    return jnp.where(n < max_exact, n, large)


def _bias_lookup(rel_bias, rel):
    bucket = _t5_bucket_of(rel)[None]
    tab = rel_bias.astype(F32).reshape((rel_bias.shape[0], REL_BUCKETS) + (1,) * rel.ndim)
    out = jnp.zeros((rel_bias.shape[0],) + rel.shape, F32)
    for b in range(REL_BUCKETS):
        out = jnp.where(bucket == b, tab[:, b], out)
    return out


def _bias_tiles(rel_bias, window):
    t, g, hpg = ATT_T, NSA_KV_GROUPS, NSA_HPG
    assert ATT_TK == t and t >= REL_MAX_DIST and (not window or window == t)
    j = np.arange(t)[:, None]
    i = np.tile(np.arange(t), hpg)[None, :]
    bucket = _t5_bucket_of(jnp.asarray((i - j) % t, jnp.int32))[None]
    val = jnp.repeat(rel_bias.astype(F32).reshape(g, 1, hpg, REL_BUCKETS) * LOG2E, t, axis=2)
    cyc = jnp.zeros((g, t, hpg * t), F32)
    for b in range(REL_BUCKETS):
        cyc = jnp.where(bucket == b, val[..., b], cyc)
    far = val[..., REL_BUCKETS - 1]
    upper = jnp.asarray(i >= j)[None]
    tile0 = jnp.where(upper, cyc, NEG)
    if window:
        tiles = [tile0, jnp.where(upper, NEG, cyc)]
    else:
        tiles = [tile0, jnp.where(upper, far, cyc), jnp.broadcast_to(far, cyc.shape)]
    return jnp.stack(tiles, axis=1)


def _token_mix(x2d, mix_norm, w_in, ret_gn_gain, pe_k, w1_k, w2_k, pe_v, w1_v, w2_v, w_out, rel_bias,
               batch, seq):
    d = x2d.shape[1]
    dh = NSA_DH
    w_pad = jnp.concatenate([w_in.astype(BF16), jnp.zeros((d, D_IN_PAD - D_IN), BF16)], axis=1)
    proj, cmp_kv = _inproj(x2d, mix_norm, w_pad)
    y_ret = _retention(proj, ret_gn_gain, batch, seq)

    gates = proj[:, IN_OFF[11]:IN_OFF[11] + IN_SPLITS[11]].astype(F32)
    gates = gates.reshape(batch, seq, NSA_KV_GROUPS, NSA_HPG, N_BRANCH).transpose(0, 2, 4, 3, 1)
    gates = jnp.pad(gates, ((0, 0), (0, 0), (0, 0), (0, GATE_ROWS - NSA_HPG), (0, 0)))

    k_cmp, v_cmp = _compress(cmp_kv, pe_k, w1_k, w2_k, pe_v, w1_v, w2_v, batch, seq)
    o_cmp, selb = _cmp_attention(proj, k_cmp, v_cmp, rel_bias, gates, batch, seq)

    e_np = np.zeros((seq, SEL_PAD), np.float32)
    e_np[np.arange(seq), np.arange(seq) // SEL_LEN] = 1.0
    o_sel = _band_attention(proj, _bias_tiles(rel_bias, 0), _band_steps(seq, 0),
                            IN_OFF[7] // dh, IN_OFF[8] // dh, gates, 1, batch, seq,
                            selb=selb, e_mat=jnp.asarray(e_np, BF16))
    o_win = _band_attention(proj, _bias_tiles(rel_bias, WIN), _band_steps(seq, WIN),
                            IN_OFF[9] // dh, IN_OFF[10] // dh, gates, 2, batch, seq)
    return _out_proj(x2d, y_ret, o_cmp, o_sel, o_win, _to_bf16(w_out))


def _cast_kernel(x_ref, o_ref):
    o_ref[...] = x_ref[...].astype(o_ref.dtype)


def _to_bf16(w):
    r, c = w.shape
    tr = 1 << ((CAST_BLOCK_BYTES // (4 * c)).bit_length() - 1)
    while r % tr:
        tr //= 2
    return pl.pallas_call(
        _cast_kernel,
        grid=(r // tr,),
        in_specs=[pl.BlockSpec((tr, c), lambda i: (i, 0))],
        out_specs=pl.BlockSpec((tr, c), lambda i: (i, 0)),
        out_shape=jax.ShapeDtypeStruct((r, c), BF16),
        compiler_params=pltpu.CompilerParams(
            dimension_semantics=("parallel",), vmem_limit_bytes=VMEM_LIMIT),
        name="to_bf16",
    )(w)


def kernel(x, ffn1_norm, ffn1_w1, ffn1_w3, ffn1_w2, mix_norm, w_in, ret_gn_gain, cmp_pe_k, cmp_w1_k,
           cmp_w2_k, cmp_pe_v, cmp_w1_v, cmp_w2_v, w_out, ffn2_norm, ffn2_w1, ffn2_w3, ffn2_w2,
           rel_bias, final_norm):
    batch, seq, d = x.shape
    depth = ffn1_norm.shape[0]
    h = x.reshape(batch * seq, d)
    for l in range(depth):
        last = l == depth - 1
        h = _ffn(h, ffn1_norm[l], _to_bf16(ffn1_w1[l]), _to_bf16(ffn1_w3[l]), _to_bf16(ffn1_w2[l]),
                 final_norm, False)
        h = _token_mix(h, mix_norm[l], w_in[l], ret_gn_gain[l], cmp_pe_k[l], cmp_w1_k[l], cmp_w2_k[l],
                       cmp_pe_v[l], cmp_w1_v[l], cmp_w2_v[l], w_out[l], rel_bias, batch, seq)
        h = _ffn(h, ffn2_norm[l], _to_bf16(ffn2_w1[l]), _to_bf16(ffn2_w3[l]), _to_bf16(ffn2_w2[l]),
                 final_norm, last)
    if depth == 0:
        raise ValueError("depth must be positive")
    return h.reshape(batch, seq, d)
```

```python
import functools
import math

import jax
import jax.numpy as jnp
import numpy as np
from jax import lax
from jax.experimental import pallas as pl
from jax.experimental.pallas import tpu as pltpu

F32 = jnp.float32
BF16 = jnp.bfloat16

RET_HEADS = 4
RET_DK = 256
RET_DV = 256
ROPE_BASE = 10000.0
NSA_HEADS = 8
NSA_KV_GROUPS = 2
NSA_HPG = NSA_HEADS // NSA_KV_GROUPS
NSA_DH = 128
CMP_LEN = 32
CMP_STRIDE = 16
SEL_LEN = 64
SEL_TOPK = 16
WIN = 512
N_BRANCH = 3
REL_BUCKETS = 32
REL_MAX_DIST = 128
EPS = 1e-6
NEG = -1e30
FORCE = 1e4

RET_W = RET_HEADS * RET_DV
NSA_W = NSA_HEADS * NSA_DH
KV_W = NSA_KV_GROUPS * NSA_DH
IN_SPLITS = [RET_HEADS * RET_DK, RET_HEADS * RET_DK, RET_W, RET_W, NSA_W,
             KV_W, KV_W, KV_W, KV_W, KV_W, KV_W, NSA_HEADS * N_BRANCH]
D_IN = sum(IN_SPLITS)
IN_OFF = [sum(IN_SPLITS[:i]) for i in range(len(IN_SPLITS))]

LANE = 128
IN_TN = 2304
D_IN_PAD = 3 * IN_TN
CMP_KV_TILE = IN_OFF[5] // IN_TN
CMP_KV_LOCAL = IN_OFF[5] % IN_TN
assert CMP_KV_LOCAL + 2 * KV_W <= IN_TN and IN_OFF[6] == IN_OFF[5] + KV_W
GATE_ROWS = 8
FFN_TM = 1024
FFN_TF = 512
IN_TM = 512
RET_C = 256
CMP_TQ = 512
ATT_T = 512
ATT_TK = 512
ATT_NVAR = 3
OUT_TM = 512
SEL_PAD = 128
ACC_PAD = 8
CMP_BAND = 128
VMEM_LIMIT = 56 * 1024 * 1024
CAST_BLOCK_BYTES = 6 * 1024 * 1024
LOG2E = math.log2(math.e)


def _dot(a, b):
    return jnp.dot(a, b, preferred_element_type=F32)


def _dot_nt(a, b):
    return lax.dot_general(a, b, (((1,), (1,)), ((), ())), preferred_element_type=F32)


def _dot_tn(a, b):
    return lax.dot_general(a, b, (((0,), (0,)), ((), ())), preferred_element_type=F32)


def _sigmoid(x):
    return 1.0 / (1.0 + jnp.exp(-x))


def _rms(x, g):
    ms = jnp.mean(x * x, axis=-1, keepdims=True)
    return x * lax.rsqrt(ms + EPS) * g


def _ffn_kernel(x_ref, g_ref, w1_ref, w3_ref, w2_ref, fg_ref, o_ref, n_ref, *, final_norm):
    j = pl.program_id(1)

    @pl.when(j == 0)
    def _():
        x = x_ref[...]
        n_ref[...] = _rms(x, g_ref[...]).astype(BF16)
        o_ref[...] = x

    n = n_ref[...]
    a = _dot(n, w1_ref[...])
    b = _dot(n, w3_ref[...])
    h = (0.5 * a * _sigmoid(a) * b).astype(BF16)
    o_ref[...] += _dot(h, w2_ref[...])

    if final_norm:
        @pl.when(j == pl.num_programs(1) - 1)
        def _():
            o_ref[...] = _rms(o_ref[...], fg_ref[...])


def _ffn(x2d, g, w1, w3, w2, fg, final_norm):
    tok, d = x2d.shape
    dff = w1.shape[1]
    tm, tf = FFN_TM, FFN_TF
    return pl.pallas_call(
        functools.partial(_ffn_kernel, final_norm=final_norm),
        grid=(tok // tm, dff // tf),
        in_specs=[
            pl.BlockSpec((tm, d), lambda i, j: (i, 0)),
            pl.BlockSpec((1, d), lambda i, j: (0, 0)),
            pl.BlockSpec((d, tf), lambda i, j: (0, j)),
            pl.BlockSpec((d, tf), lambda i, j: (0, j)),
            pl.BlockSpec((tf, d), lambda i, j: (j, 0)),
            pl.BlockSpec((1, d), lambda i, j: (0, 0)),
        ],
        out_specs=pl.BlockSpec((tm, d), lambda i, j: (i, 0)),
        out_shape=jax.ShapeDtypeStruct((tok, d), F32),
        scratch_shapes=[pltpu.VMEM((tm, d), BF16)],
        compiler_params=pltpu.CompilerParams(
            dimension_semantics=("parallel", "arbitrary"), vmem_limit_bytes=VMEM_LIMIT),
        name="ffn",
    )(x2d, g.reshape(1, d), w1, w3, w2, fg.reshape(1, d))


def _inproj_kernel(x_ref, g_ref, w_ref, o_ref, kv_ref, n_ref):
    j = pl.program_id(1)

    @pl.when(j == 0)
    def _():
        n_ref[...] = _rms(x_ref[...], g_ref[...]).astype(BF16)

    res = _dot(n_ref[...], w_ref[...])
    o_ref[...] = res.astype(o_ref.dtype)

    @pl.when(j == CMP_KV_TILE)
    def _():
        kv_ref[...] = res[:, CMP_KV_LOCAL:CMP_KV_LOCAL + 2 * KV_W]


def _inproj(x2d, g, w_pad):
    tok, d = x2d.shape
    n_out = w_pad.shape[1]
    tm, tn = IN_TM, IN_TN
    return pl.pallas_call(
        _inproj_kernel,
        grid=(tok // tm, n_out // tn),
        in_specs=[
            pl.BlockSpec((tm, d), lambda i, j: (i, 0)),
            pl.BlockSpec((1, d), lambda i, j: (0, 0)),
            pl.BlockSpec((d, tn), lambda i, j: (0, j)),
        ],
        out_specs=[pl.BlockSpec((tm, tn), lambda i, j: (i, j)),
                   pl.BlockSpec((tm, 2 * KV_W), lambda i, j: (i, 0))],
        out_shape=[jax.ShapeDtypeStruct((tok, n_out), BF16),
                   jax.ShapeDtypeStruct((tok, 2 * KV_W), F32)],
        scratch_shapes=[pltpu.VMEM((tm, d), BF16)],
        compiler_params=pltpu.CompilerParams(
            dimension_semantics=("parallel", "arbitrary"), vmem_limit_bytes=VMEM_LIMIT),
        name="in_proj",
    )(x2d, g.reshape(1, d), w_pad)


def _ret_kernel(cd_ref, q_ref, k_ref, v_ref, g_ref, cos_ref, sin_ref, dec_ref, xi_ref, zeta_ref,
                gain_ref, o_ref, state_ref):
    @pl.when(pl.program_id(1) == 0)
    def _():
        state_ref[...] = jnp.zeros_like(state_ref)

    cos = cos_ref[...]
    sin = sin_ref[...]
    half = RET_DK // 2

    def rot(t):
        t1, t2 = t[:, :half], t[:, half:]
        return jnp.concatenate([t1 * cos - t2 * sin, t1 * sin + t2 * cos], axis=-1)

    for h in range(RET_HEADS):
        cs = slice(h * RET_DK, (h + 1) * RET_DK)
        q = rot(q_ref[:, cs].astype(F32))
        k = rot(k_ref[:, cs].astype(F32)) * (RET_DK ** -0.5)
        v = v_ref[:, cs].astype(BF16)
        qb = q.astype(BF16)
        s = _dot_nt(qb, k.astype(BF16)) * dec_ref[h]
        state = state_ref[h]
        y = _dot(s.astype(BF16), v) + _dot((q * xi_ref[h]).astype(BF16), state.astype(BF16))
        state_ref[h] = cd_ref[h] * state + _dot_tn((k * zeta_ref[h]).astype(BF16), v)
        mu = jnp.mean(y, axis=-1, keepdims=True)
        yc = y - mu
        var = jnp.mean(yc * yc, axis=-1, keepdims=True)
        yn = yc * lax.rsqrt(var + EPS) * gain_ref[:, cs]
        gate = g_ref[:, cs].astype(F32)
        o_ref[:, cs] = (gate * _sigmoid(gate) * yn).astype(o_ref.dtype)


def _retention(proj, gain, batch, seq):
    tok = proj.shape[0]
    c = RET_C
    n_chunks = seq // c
    h, dk = RET_HEADS, RET_DK
    half = dk // 2
    inv = ROPE_BASE ** (-jnp.arange(half, dtype=F32) / half)
    ang = jnp.arange(seq, dtype=F32)[:, None] * inv[None, :]
    cos, sin = jnp.cos(ang), jnp.sin(ang)
    log_g = jnp.log(1.0 - 2.0 ** (-5.0 - jnp.arange(h, dtype=F32)))
    idx = jnp.arange(c, dtype=F32)
    diff = idx[:, None] - idx[None, :]
    dec = jnp.where(diff >= 0.0, jnp.exp(jnp.maximum(diff, 0.0)[None] * log_g[:, None, None]), 0.0)
    xi = jnp.broadcast_to(jnp.exp((idx + 1.0)[None] * log_g[:, None])[:, :, None], (h, c, dk))
    zeta = jnp.broadcast_to(jnp.exp((c - 1.0 - idx)[None] * log_g[:, None])[:, :, None], (h, c, dk))
    cd = jnp.exp(c * log_g)

    w = h * dk
    row = lambda b, n: (b * n_chunks + n)
    return pl.pallas_call(
        _ret_kernel,
        grid=(batch, n_chunks),
        in_specs=[
            pl.BlockSpec(memory_space=pltpu.SMEM),
            pl.BlockSpec((c, w), lambda b, n: (row(b, n), 0)),
            pl.BlockSpec((c, w), lambda b, n: (row(b, n), 1)),
            pl.BlockSpec((c, w), lambda b, n: (row(b, n), 2)),
            pl.BlockSpec((c, w), lambda b, n: (row(b, n), 3)),
            pl.BlockSpec((c, half), lambda b, n: (n, 0)),
            pl.BlockSpec((c, half), lambda b, n: (n, 0)),
            pl.BlockSpec((h, c, c), lambda b, n: (0, 0, 0)),
            pl.BlockSpec((h, c, dk), lambda b, n: (0, 0, 0)),
            pl.BlockSpec((h, c, dk), lambda b, n: (0, 0, 0)),
            pl.BlockSpec((1, w), lambda b, n: (0, 0)),
        ],
        out_specs=pl.BlockSpec((c, w), lambda b, n: (row(b, n), 0)),
        out_shape=jax.ShapeDtypeStruct((tok, w), BF16),
        scratch_shapes=[pltpu.VMEM((h, dk, RET_DV), F32)],
        compiler_params=pltpu.CompilerParams(
            dimension_semantics=("parallel", "arbitrary"), vmem_limit_bytes=VMEM_LIMIT),
        name="retention",
    )(cd, proj, proj, proj, proj, cos, sin, dec, xi, zeta, gain.reshape(1, w))


def _compress_kernel(k_ref, v_ref, pek_ref, w1k_ref, w1kf_ref, w2k_ref, pev_ref, w1v_ref, w1vf_ref,
                     w2v_ref, ok_ref, ov_ref):
    dh = NSA_DH
    nblk = k_ref.shape[0] // CMP_STRIDE

    def one(x_ref, pe_ref, w1_ref, w1f_ref, w2_ref, o_ref):
        ab = jnp.zeros((nblk, 2 * dh), F32)
        for l in range(CMP_STRIDE):
            rows = x_ref[pl.ds(l, nblk, stride=CMP_STRIDE), :].astype(BF16)
            ab = ab + _dot(rows, w1_ref[l])
        a, b = ab[:, :dh], ab[:, dh:]
        b_next = pltpu.roll(b, nblk - 1, axis=0)
        ridx = lax.broadcasted_iota(jnp.int32, (nblk, dh), 0)
        b_next = jnp.where(ridx < nblk - 1, b_next, 0.0)
        pe_term = _dot(pe_ref[...], w1f_ref[...])[0:1, :]
        hdn = a + b_next + pe_term
        o_ref[0, 0] = _dot((hdn * _sigmoid(hdn)).astype(BF16), w2_ref[...])

    one(k_ref, pek_ref, w1k_ref, w1kf_ref, w2k_ref, ok_ref)
    one(v_ref, pev_ref, w1v_ref, w1vf_ref, w2v_ref, ov_ref)


def _compress(kv, pe_k, w1_k, w2_k, pe_v, w1_v, w2_v, batch, seq):
    dh, g = NSA_DH, NSA_KV_GROUPS
    nblk = seq // CMP_STRIDE
    half = CMP_LEN // 2

    def prep(pe, w1, w2):
        w1b = w1.astype(BF16)
        w1_pair = jnp.concatenate([w1b[:half], w1b[half:]], axis=-1)
        pe_flat = jnp.broadcast_to(pe.reshape(1, CMP_LEN * dh), (8, CMP_LEN * dh)).astype(BF16)
        return pe_flat, w1_pair, w1b.reshape(CMP_LEN * dh, dh), w2.astype(BF16)

    args_k = prep(pe_k, w1_k, w2_k)
    args_v = prep(pe_v, w1_v, w2_v)
    kcol = 0
    vcol = KV_W // dh
    const = lambda shape: pl.BlockSpec(shape, lambda b, gg: (0,) * len(shape))
    wspecs = [const((8, CMP_LEN * dh)), const((half, dh, 2 * dh)), const((CMP_LEN * dh, dh)),
              const((dh, dh))]
    out_spec = pl.BlockSpec((1, 1, nblk, dh), lambda b, gg: (b, gg, 0, 0))
    return pl.pallas_call(
        _compress_kernel,
        grid=(batch, g),
        in_specs=[pl.BlockSpec((seq, dh), lambda b, gg: (b, kcol + gg)),
                  pl.BlockSpec((seq, dh), lambda b, gg: (b, vcol + gg))] + wspecs + wspecs,
        out_specs=[out_spec, out_spec],
        out_shape=[jax.ShapeDtypeStruct((batch, g, nblk, dh), F32)] * 2,
        compiler_params=pltpu.CompilerParams(
            dimension_semantics=("parallel", "parallel"), vmem_limit_bytes=VMEM_LIMIT),
        name="compress",
    )(kv, kv, *args_k, *args_v)


def _cmpattn_kernel(q_ref, kc_ref, vc_ref, tb_ref, pp_ref, ovt_ref, gate_ref, o_ref, sel_ref, rank_ref):
    tq = q_ref.shape[0]
    ncp = kc_ref.shape[2]
    dh = NSA_DH
    s0 = pl.program_id(2) * tq
    hpg = NSA_HPG
    pos_r = s0 + (lax.broadcasted_iota(jnp.int32, (hpg * tq, ncp), 0) & (tq - 1))
    n_c = lax.broadcasted_iota(jnp.int32, (hpg * tq, ncp), 1)
    mask = pos_r >= n_c * CMP_STRIDE + (CMP_LEN - 1)
    kc = kc_ref[0, 0].astype(BF16)
    vc = vc_ref[0, 0].astype(BF16)
    scale = dh ** -0.5
    q = jnp.concatenate([q_ref[:, hh * dh:(hh + 1) * dh] for hh in range(hpg)], axis=0).astype(BF16)
    bias = _dot(tb_ref[...].reshape(hpg * tq, tb_ref.shape[2]), pp_ref[0])
    lm = jnp.where(mask, _dot_nt(q, kc) * scale + bias, NEG)
    m = jnp.max(lm, axis=-1, keepdims=True)
    e = jnp.where(mask, jnp.exp(lm - m), 0.0)
    den = jnp.sum(e, axis=-1, keepdims=True)
    p = e / jnp.where(den > 0.0, den, 1.0)
    o = _dot(p.astype(BF16), vc)
    gate = _sigmoid(gate_ref[0, 0])
    psum = p[0:tq]
    for hh in range(hpg):
        o_ref[:, hh * dh:(hh + 1) * dh] = (o[hh * tq:(hh + 1) * tq] * gate[:, hh:hh + 1]).astype(o_ref.dtype)
        if hh:
            psum = psum + p[hh * tq:(hh + 1) * tq]
    nsel = ovt_ref.shape[0]
    imp_t = _dot_nt(ovt_ref[...], psum.astype(BF16))
    jb = lax.broadcasted_iota(jnp.int32, (nsel, tq), 0)
    pos = s0 + lax.broadcasted_iota(jnp.int32, (nsel, tq), 1)
    cur = jnp.right_shift(pos, SEL_LEN.bit_length() - 1)
    causal = jb * SEL_LEN <= pos
    forced = (jb == 0) | (jb == cur) | (jb == cur - 1)
    score = jnp.where(forced, FORCE, jnp.where(causal, imp_t, NEG))
    per_tile = tq // SEL_LEN
    rank_ref[...] = jnp.zeros_like(rank_ref)
    for c in range(nsel // per_tile):
        @pl.when(c <= pl.program_id(2))
        def _():
            rank = rank_ref[...]
            for kk in range(c * per_tile, (c + 1) * per_tile):
                row = score[kk:kk + 1, :]
                beats = (row > score) | ((row == score) & (jb > kk))
                rank = rank + jnp.where(beats, 1, 0)
            rank_ref[...] = rank
    rank = rank_ref[...]
    selb = jnp.where((rank < SEL_TOPK) & causal, 0.0, NEG)
    selb = jnp.concatenate([selb, jnp.zeros((SEL_PAD - nsel, tq), F32)], axis=0)
    sel_ref[0, 0] = selb.astype(BF16)


def _cmp_attention(proj, k_cmp, v_cmp, rel_bias, gates, batch, seq):
    dh, g, hpg = NSA_DH, NSA_KV_GROUPS, NSA_HPG
    tq = CMP_TQ
    nq = seq // tq
    ncp = k_cmp.shape[2]
    nsel = seq // SEL_LEN
    n_cmp = (seq - CMP_LEN) // CMP_STRIDE + 1
    cmp_idx = np.arange(n_cmp)[:, None] * CMP_STRIDE + np.arange(CMP_LEN)[None, :]
    overlap = ((cmp_idx // SEL_LEN)[:, :, None] == np.arange(nsel)[None, None, :]).sum(1) / CMP_LEN
    ovt = np.zeros((nsel, ncp), np.float32)
    ovt[:, :n_cmp] = overlap.T
    nb = tq // CMP_STRIDE
    assert 2 * nb + 1 <= CMP_BAND and CMP_STRIDE * (nb + 1) - (CMP_LEN - 1) >= REL_MAX_DIST
    i = np.arange(tq)[:, None]
    r = np.arange(CMP_BAND)[None, :]
    rel = np.where(r < 2 * nb, i - CMP_STRIDE * (r - nb) - (CMP_LEN - 1), REL_MAX_DIST)
    tb = jnp.where(jnp.asarray(r <= 2 * nb), _bias_lookup(rel_bias, jnp.asarray(rel, jnp.int32)), 0.0)
    tb_hi = tb.astype(BF16)
    tb_lo = (tb - tb_hi.astype(F32)).astype(BF16)
    tb = jnp.concatenate([tb_hi, tb_lo], axis=-1)
    n = np.arange(ncp)[None, None, :]
    first = (np.arange(nq) * nb - nb)[:, None, None]
    rr = np.arange(CMP_BAND)[None, :, None]
    pp = np.where(rr < 2 * nb, n == first + rr, (rr == 2 * nb) & (n < first)).astype(np.float32)
    pp = np.concatenate([pp, pp], axis=1)
    qcol = IN_OFF[4] // (hpg * dh)
    return pl.pallas_call(
        _cmpattn_kernel,
        grid=(batch, g, nq),
        in_specs=[
            pl.BlockSpec((tq, hpg * dh), lambda b, gg, t: (b * nq + t, qcol + gg)),
            pl.BlockSpec((1, 1, ncp, dh), lambda b, gg, t: (b, gg, 0, 0)),
            pl.BlockSpec((1, 1, ncp, dh), lambda b, gg, t: (b, gg, 0, 0)),
            pl.BlockSpec((hpg, tq, 2 * CMP_BAND), lambda b, gg, t: (gg, 0, 0)),
            pl.BlockSpec((1, 2 * CMP_BAND, ncp), lambda b, gg, t: (t, 0, 0)),
            pl.BlockSpec((nsel, ncp), lambda b, gg, t: (0, 0)),
            pl.BlockSpec((1, 1, tq, GATE_ROWS), lambda b, gg, t: (b, gg, t, 0)),
        ],
        out_specs=[
            pl.BlockSpec((tq, hpg * dh), lambda b, gg, t: (b * nq + t, gg)),
            pl.BlockSpec((1, 1, SEL_PAD, tq), lambda b, gg, t: (b, gg, 0, t)),
        ],
        out_shape=[jax.ShapeDtypeStruct((batch * seq, g * hpg * dh), BF16),
                   jax.ShapeDtypeStruct((batch, g, SEL_PAD, seq), BF16)],
        scratch_shapes=[pltpu.VMEM((nsel, tq), jnp.int32)],
        compiler_params=pltpu.CompilerParams(
            dimension_semantics=("parallel", "parallel", "parallel"), vmem_limit_bytes=VMEM_LIMIT),
        name="cmp_attn",
    )(proj, k_cmp, v_cmp, tb, jnp.asarray(pp, BF16), jnp.asarray(ovt, BF16),
      gates[:, :, 0].transpose(0, 1, 3, 2))


def _band_kernel(qi_ref, ki_ref, var_ref, first_ref, last_ref, q_ref, k_ref, v_ref, bias_ref, gate_ref,
                 *rest, use_sel):
    if use_sel:
        selb_ref, e_ref, o_ref, qt_ref, m_ref, acc_ref = rest
    else:
        o_ref, qt_ref, m_ref, acc_ref = rest
    t = pl.program_id(1)
    dh, hpg, ng = NSA_DH, NSA_HPG, NSA_KV_GROUPS
    tq, tk = q_ref.shape[0], k_ref.shape[0]

    @pl.when(first_ref[t] == 1)
    def _():
        m_ref[...] = jnp.full_like(m_ref, NEG)
        acc_ref[...] = jnp.zeros_like(acc_ref)
        scale = dh ** -0.5 * LOG2E
        for gg in range(ng):
            for hh in range(hpg):
                h = gg * hpg + hh
                q = q_ref[:, h * dh:(h + 1) * dh].astype(F32)
                qt_ref[gg, 0:dh, hh * tq:(hh + 1) * tq] = (q * scale).T.astype(BF16)
                if use_sel:
                    qt_ref[gg, dh:, hh * tq:(hh + 1) * tq] = selb_ref[0, gg]

    var = var_ref[t]
    ones = jnp.ones((ACC_PAD, tk), F32)
    for gg in range(ng):
        k = k_ref[:, gg * dh:(gg + 1) * dh]
        if use_sel:
            k = jnp.concatenate([k, e_ref[...]], axis=1)
        vt = jnp.concatenate([v_ref[:, gg * dh:(gg + 1) * dh].astype(F32).T, ones], axis=0).astype(BF16)
        s = _dot(k, qt_ref[gg]) + bias_ref[gg, var]
        m_prev = m_ref[gg]
        m_new = jnp.maximum(m_prev, jnp.max(s, axis=0, keepdims=True))
        alpha = jnp.exp2(m_prev - m_new)
        p = jnp.exp2(s - m_new).astype(BF16)
        acc_ref[gg] = alpha * acc_ref[gg] + _dot(vt, p)
        m_ref[gg] = m_new

    @pl.when(last_ref[t] == 1)
    def _():
        for gg in range(ng):
            gate = _sigmoid(gate_ref[0, gg, 0])
            for hh in range(hpg):
                h = gg * hpg + hh
                cols = slice(hh * tq, (hh + 1) * tq)
                o = acc_ref[gg, 0:dh, cols] * (gate[hh:hh + 1, :] / acc_ref[gg, dh:dh + 1, cols])
                o_ref[:, h * dh:(h + 1) * dh] = o.T.astype(o_ref.dtype)


def _band_steps(seq, window):
    tq, tk = ATT_T, ATT_TK
    steps = []
    for qi in range(seq // tq):
        k_hi = qi * tq // tk
        k_lo = max(0, (qi * tq - window + 1) // tk) if window else 0
        steps += [(qi, ki, min((qi * tq - ki * tk) // tq, ATT_NVAR - 1)) for ki in range(k_lo, k_hi + 1)]
    return steps


def _band_attention(proj, bias_tiles, steps, koff, voff, gates, branch, batch, seq, selb=None, e_mat=None):
    dh, g, hpg = NSA_DH, NSA_KV_GROUPS, NSA_HPG
    tq, tk = ATT_T, ATT_TK
    nq, nk = seq // tq, seq // tk
    qi = np.array([s[0] for s in steps], np.int32)
    ki = np.array([s[1] for s in steps], np.int32)
    var = np.array([s[2] for s in steps], np.int32)
    first = np.concatenate([[1], (qi[1:] != qi[:-1]).astype(np.int32)]).astype(np.int32)
    last = np.concatenate([(qi[1:] != qi[:-1]).astype(np.int32), [1]]).astype(np.int32)
    qcol, kcol, vcol = IN_OFF[4] // NSA_W, koff // KV_W, voff // KV_W
    use_sel = selb is not None
    in_specs = [
        pl.BlockSpec((tq, NSA_W), lambda b, t, qi_r, ki_r, v_r, f_r, l_r: (b * nq + qi_r[t], qcol)),
        pl.BlockSpec((tk, KV_W), lambda b, t, qi_r, ki_r, v_r, f_r, l_r: (b * nk + ki_r[t], kcol)),
        pl.BlockSpec((tk, KV_W), lambda b, t, qi_r, ki_r, v_r, f_r, l_r: (b * nk + ki_r[t], vcol)),
        pl.BlockSpec(bias_tiles.shape, lambda b, t, qi_r, ki_r, v_r, f_r, l_r: (0, 0, 0, 0),
                     pipeline_mode=pl.Buffered(1)),
        pl.BlockSpec((1, g, 1, GATE_ROWS, tq),
                     lambda b, t, qi_r, ki_r, v_r, f_r, l_r: (b, 0, branch, 0, qi_r[t])),
    ]
    args = [proj, proj, proj, bias_tiles, gates]
    if use_sel:
        in_specs += [
            pl.BlockSpec((1, g, SEL_PAD, tq), lambda b, t, qi_r, ki_r, v_r, f_r, l_r: (b, 0, 0, qi_r[t])),
            pl.BlockSpec((tk, SEL_PAD), lambda b, t, qi_r, ki_r, v_r, f_r, l_r: (ki_r[t], 0)),
        ]
        args += [selb, e_mat]
    kdim = dh + SEL_PAD if use_sel else dh
    grid_spec = pltpu.PrefetchScalarGridSpec(
        num_scalar_prefetch=5,
        grid=(batch, len(steps)),
        in_specs=in_specs,
        out_specs=pl.BlockSpec((tq, NSA_W), lambda b, t, qi_r, ki_r, v_r, f_r, l_r: (b * nq + qi_r[t], 0)),
        scratch_shapes=[pltpu.VMEM((g, kdim, hpg * tq), BF16), pltpu.VMEM((g, 1, hpg * tq), F32),
                        pltpu.VMEM((g, dh + ACC_PAD, hpg * tq), F32)],
    )
    return pl.pallas_call(
        functools.partial(_band_kernel, use_sel=use_sel),
        grid_spec=grid_spec,
        out_shape=jax.ShapeDtypeStruct((batch * seq, NSA_W), BF16),
        compiler_params=pltpu.CompilerParams(
            dimension_semantics=("parallel", "arbitrary"), vmem_limit_bytes=VMEM_LIMIT),
        name="sel_attn" if use_sel else "win_attn",
    )(jnp.asarray(qi), jnp.asarray(ki), jnp.asarray(var), jnp.asarray(first), jnp.asarray(last), *args)


def _out_kernel(x_ref, yr_ref, oc_ref, os_ref, ow_ref, w_ref, o_ref):
    y_nsa = oc_ref[...].astype(F32) + os_ref[...].astype(F32) + ow_ref[...].astype(F32)
    y = jnp.concatenate([yr_ref[...], y_nsa.astype(BF16)], axis=-1)
    o_ref[...] = x_ref[...] + _dot(y, w_ref[...])


def _out_proj(x2d, y_ret, o_cmp, o_sel, o_win, w_out):
    tok, d = x2d.shape
    tm = OUT_TM
    wmix = y_ret.shape[1]
    row = lambda shape: pl.BlockSpec(shape, lambda i: (i, 0))
    return pl.pallas_call(
        _out_kernel,
        grid=(tok // tm,),
        in_specs=[row((tm, d)), row((tm, wmix)), row((tm, wmix)), row((tm, wmix)), row((tm, wmix)),
                  pl.BlockSpec(w_out.shape, lambda i: (0, 0))],
        out_specs=row((tm, d)),
        out_shape=jax.ShapeDtypeStruct((tok, d), F32),
        compiler_params=pltpu.CompilerParams(
            dimension_semantics=("parallel",), vmem_limit_bytes=VMEM_LIMIT),
        name="out_proj",
    )(x2d, y_ret, o_cmp, o_sel, o_win, w_out)


def _t5_bucket_of(rel):
    n = jnp.maximum(rel, 0)
    max_exact = REL_BUCKETS // 2
    nf = jnp.maximum(n, 1).astype(F32)
    large = max_exact + (jnp.log(nf / max_exact) / math.log(REL_MAX_DIST / max_exact)
                         * (REL_BUCKETS - max_exact)).astype(jnp.int32)
    large = jnp.minimum(large, REL_BUCKETS - 1)
    return jnp.where(n < max_exact, n, large)


def _bias_lookup(rel_bias, rel):
    bucket = _t5_bucket_of(rel)[None]
    tab = rel_bias.astype(F32).reshape((rel_bias.shape[0], REL_BUCKETS) + (1,) * rel.ndim)
    out = jnp.zeros((rel_bias.shape[0],) + rel.shape, F32)
    for b in range(REL_BUCKETS):
        out = jnp.where(bucket == b, tab[:, b], out)
    return out


def _bias_tiles(rel_bias, window):
    t, g, hpg = ATT_T, NSA_KV_GROUPS, NSA_HPG
    assert ATT_TK == t and t >= REL_MAX_DIST and (not window or window == t)
    j = np.arange(t)[:, None]
    i = np.tile(np.arange(t), hpg)[None, :]
    bucket = _t5_bucket_of(jnp.asarray((i - j) % t, jnp.int32))[None]
    val = jnp.repeat(rel_bias.astype(F32).reshape(g, 1, hpg, REL_BUCKETS) * LOG2E, t, axis=2)
    cyc = jnp.zeros((g, t, hpg * t), F32)
    for b in range(REL_BUCKETS):
        cyc = jnp.where(bucket == b, val[..., b], cyc)
    far = val[..., REL_BUCKETS - 1]
    upper = jnp.asarray(i >= j)[None]
    tile0 = jnp.where(upper, cyc, NEG)
    if window:
        tiles = [tile0, jnp.where(upper, NEG, cyc)]
    else:
        tiles = [tile0, jnp.where(upper, far, cyc), jnp.broadcast_to(far, cyc.shape)]
    return jnp.stack(tiles, axis=1)


def _token_mix(x2d, mix_norm, w_in, ret_gn_gain, pe_k, w1_k, w2_k, pe_v, w1_v, w2_v, w_out, rel_bias,
               batch, seq):
    d = x2d.shape[1]
    dh = NSA_DH
    w_pad = jnp.concatenate([w_in.astype(BF16), jnp.zeros((d, D_IN_PAD - D_IN), BF16)], axis=1)
    proj, cmp_kv = _inproj(x2d, mix_norm, w_pad)
    y_ret = _retention(proj, ret_gn_gain, batch, seq)

    gates = proj[:, IN_OFF[11]:IN_OFF[11] + IN_SPLITS[11]].astype(F32)
    gates = gates.reshape(batch, seq, NSA_KV_GROUPS, NSA_HPG, N_BRANCH).transpose(0, 2, 4, 3, 1)
    gates = jnp.pad(gates, ((0, 0), (0, 0), (0, 0), (0, GATE_ROWS - NSA_HPG), (0, 0)))

    k_cmp, v_cmp = _compress(cmp_kv, pe_k, w1_k, w2_k, pe_v, w1_v, w2_v, batch, seq)
    o_cmp, selb = _cmp_attention(proj, k_cmp, v_cmp, rel_bias, gates, batch, seq)

    e_np = np.zeros((seq, SEL_PAD), np.float32)
    e_np[np.arange(seq), np.arange(seq) // SEL_LEN] = 1.0
    o_sel = _band_attention(proj, _bias_tiles(rel_bias, 0), _band_steps(seq, 0),
                            IN_OFF[7], IN_OFF[8], gates, 1, batch, seq,
                            selb=selb, e_mat=jnp.asarray(e_np, BF16))
    o_win = _band_attention(proj, _bias_tiles(rel_bias, WIN), _band_steps(seq, WIN),
                            IN_OFF[9], IN_OFF[10], gates, 2, batch, seq)
    return _out_proj(x2d, y_ret, o_cmp, o_sel, o_win, _to_bf16(w_out))


def _cast_kernel(x_ref, o_ref):
    o_ref[...] = x_ref[...].astype(o_ref.dtype)


def _to_bf16(w):
    r, c = w.shape
    tr = 1 << ((CAST_BLOCK_BYTES // (4 * c)).bit_length() - 1)
    while r % tr:
        tr //= 2
    return pl.pallas_call(
        _cast_kernel,
        grid=(r // tr,),
        in_specs=[pl.BlockSpec((tr, c), lambda i: (i, 0))],
        out_specs=pl.BlockSpec((tr, c), lambda i: (i, 0)),
        out_shape=jax.ShapeDtypeStruct((r, c), BF16),
        compiler_params=pltpu.CompilerParams(
            dimension_semantics=("parallel",), vmem_limit_bytes=VMEM_LIMIT),
        name="to_bf16",
    )(w)


def kernel(x, ffn1_norm, ffn1_w1, ffn1_w3, ffn1_w2, mix_norm, w_in, ret_gn_gain, cmp_pe_k, cmp_w1_k,
           cmp_w2_k, cmp_pe_v, cmp_w1_v, cmp_w2_v, w_out, ffn2_norm, ffn2_w1, ffn2_w3, ffn2_w2,
           rel_bias, final_norm):
    batch, seq, d = x.shape
    depth = ffn1_norm.shape[0]
    h = x.reshape(batch * seq, d)
    for l in range(depth):
        last = l == depth - 1
        h = _ffn(h, ffn1_norm[l], _to_bf16(ffn1_w1[l]), _to_bf16(ffn1_w3[l]), _to_bf16(ffn1_w2[l]),
                 final_norm, False)
        h = _token_mix(h, mix_norm[l], w_in[l], ret_gn_gain[l], cmp_pe_k[l], cmp_w1_k[l], cmp_w2_k[l],
                       cmp_pe_v[l], cmp_w1_v[l], cmp_w2_v[l], w_out[l], rel_bias, batch, seq)
        h = _ffn(h, ffn2_norm[l], _to_bf16(ffn2_w1[l]), _to_bf16(ffn2_w3[l]), _to_bf16(ffn2_w2[l]),
                 final_norm, last)
    if depth == 0:
        raise ValueError("depth must be positive")
    return h.reshape(batch, seq, d)
```

```python
import functools
import math

import jax
import jax.numpy as jnp
import numpy as np
from jax import lax
from jax.experimental import pallas as pl
from jax.experimental.pallas import tpu as pltpu

F32 = jnp.float32
BF16 = jnp.bfloat16

RET_HEADS = 4
RET_DK = 256
RET_DV = 256
ROPE_BASE = 10000.0
NSA_HEADS = 8
NSA_KV_GROUPS = 2
NSA_HPG = NSA_HEADS // NSA_KV_GROUPS
NSA_DH = 128
CMP_LEN = 32
CMP_STRIDE = 16
SEL_LEN = 64
SEL_TOPK = 16
WIN = 512
N_BRANCH = 3
REL_BUCKETS = 32
REL_MAX_DIST = 128
EPS = 1e-6
NEG = -1e30
FORCE = 1e4

RET_W = RET_HEADS * RET_DV
NSA_W = NSA_HEADS * NSA_DH
KV_W = NSA_KV_GROUPS * NSA_DH
IN_SPLITS = [RET_HEADS * RET_DK, RET_HEADS * RET_DK, RET_W, RET_W, NSA_W,
             KV_W, KV_W, KV_W, KV_W, KV_W, KV_W, NSA_HEADS * N_BRANCH]
D_IN = sum(IN_SPLITS)
IN_OFF = [sum(IN_SPLITS[:i]) for i in range(len(IN_SPLITS))]

LANE = 128
IN_TN = 2304
D_IN_PAD = 3 * IN_TN
CMP_KV_TILE = IN_OFF[5] // IN_TN
CMP_KV_LOCAL = IN_OFF[5] % IN_TN
assert CMP_KV_LOCAL + 2 * KV_W <= IN_TN and IN_OFF[6] == IN_OFF[5] + KV_W
GATE_ROWS = 8
FFN_TM = 1024
FFN_TF = 512
IN_TM = 512
RET_C = 256
CMP_TQ = 512
ATT_T = 512
ATT_TK = 512
ATT_NVAR = 3
OUT_TM = 512
SEL_PAD = 128
ACC_PAD = 8
CMP_BAND = 128
VMEM_LIMIT = 56 * 1024 * 1024
CAST_BLOCK_BYTES = 6 * 1024 * 1024
LOG2E = math.log2(math.e)


def _dot(a, b):
    return jnp.dot(a, b, preferred_element_type=F32)


def _dot_nt(a, b):
    return lax.dot_general(a, b, (((1,), (1,)), ((), ())), preferred_element_type=F32)


def _dot_tn(a, b):
    return lax.dot_general(a, b, (((0,), (0,)), ((), ())), preferred_element_type=F32)


def _sigmoid(x):
    return 1.0 / (1.0 + jnp.exp(-x))


def _rms(x, g):
    ms = jnp.mean(x * x, axis=-1, keepdims=True)
    return x * lax.rsqrt(ms + EPS) * g


def _ffn_kernel(x_ref, g_ref, w1_ref, w3_ref, w2_ref, fg_ref, o_ref, n_ref, *, final_norm):
    j = pl.program_id(1)

    @pl.when(j == 0)
    def _():
        x = x_ref[...]
        n_ref[...] = _rms(x, g_ref[...]).astype(BF16)
        o_ref[...] = x

    n = n_ref[...]
    a = _dot(n, w1_ref[...])
    b = _dot(n, w3_ref[...])
    h = (0.5 * a * _sigmoid(a) * b).astype(BF16)
    o_ref[...] += _dot(h, w2_ref[...])

    if final_norm:
        @pl.when(j == pl.num_programs(1) - 1)
        def _():
            o_ref[...] = _rms(o_ref[...], fg_ref[...])


def _ffn(x2d, g, w1, w3, w2, fg, final_norm):
    tok, d = x2d.shape
    dff = w1.shape[1]
    tm, tf = FFN_TM, FFN_TF
    return pl.pallas_call(
        functools.partial(_ffn_kernel, final_norm=final_norm),
        grid=(tok // tm, dff // tf),
        in_specs=[
            pl.BlockSpec((tm, d), lambda i, j: (i, 0)),
            pl.BlockSpec((1, d), lambda i, j: (0, 0)),
            pl.BlockSpec((d, tf), lambda i, j: (0, j)),
            pl.BlockSpec((d, tf), lambda i, j: (0, j)),
            pl.BlockSpec((tf, d), lambda i, j: (j, 0)),
            pl.BlockSpec((1, d), lambda i, j: (0, 0)),
        ],
        out_specs=pl.BlockSpec((tm, d), lambda i, j: (i, 0)),
        out_shape=jax.ShapeDtypeStruct((tok, d), F32),
        scratch_shapes=[pltpu.VMEM((tm, d), BF16)],
        compiler_params=pltpu.CompilerParams(
            dimension_semantics=("parallel", "arbitrary"), vmem_limit_bytes=VMEM_LIMIT),
        name="ffn",
    )(x2d, g.reshape(1, d), w1, w3, w2, fg.reshape(1, d))


def _inproj_kernel(x_ref, g_ref, w_ref, o_ref, kv_ref, n_ref):
    j = pl.program_id(1)

    @pl.when(j == 0)
    def _():
        n_ref[...] = _rms(x_ref[...], g_ref[...]).astype(BF16)

    res = _dot(n_ref[...], w_ref[...])
    o_ref[...] = res.astype(o_ref.dtype)

    @pl.when(j == CMP_KV_TILE)
    def _():
        kv_ref[...] = res[:, CMP_KV_LOCAL:CMP_KV_LOCAL + 2 * KV_W]


def _inproj(x2d, g, w_pad):
    tok, d = x2d.shape
    n_out = w_pad.shape[1]
    tm, tn = IN_TM, IN_TN
    return pl.pallas_call(
        _inproj_kernel,
        grid=(tok // tm, n_out // tn),
        in_specs=[
            pl.BlockSpec((tm, d), lambda i, j: (i, 0)),
            pl.BlockSpec((1, d), lambda i, j: (0, 0)),
            pl.BlockSpec((d, tn), lambda i, j: (0, j)),
        ],
        out_specs=[pl.BlockSpec((tm, tn), lambda i, j: (i, j)),
                   pl.BlockSpec((tm, 2 * KV_W), lambda i, j: (i, 0))],
        out_shape=[jax.ShapeDtypeStruct((tok, n_out), BF16),
                   jax.ShapeDtypeStruct((tok, 2 * KV_W), F32)],
        scratch_shapes=[pltpu.VMEM((tm, d), BF16)],
        compiler_params=pltpu.CompilerParams(
            dimension_semantics=("parallel", "arbitrary"), vmem_limit_bytes=VMEM_LIMIT),
        name="in_proj",
    )(x2d, g.reshape(1, d), w_pad)


def _ret_kernel(cd_ref, q_ref, k_ref, v_ref, g_ref, cos_ref, sin_ref, dec_ref, xi_ref, zeta_ref,
                gain_ref, o_ref, state_ref):
    @pl.when(pl.program_id(1) == 0)
    def _():
        state_ref[...] = jnp.zeros_like(state_ref)

    cos = cos_ref[...]
    sin = sin_ref[...]
    half = RET_DK // 2

    def rot(t):
        t1, t2 = t[:, :half], t[:, half:]
        return jnp.concatenate([t1 * cos - t2 * sin, t1 * sin + t2 * cos], axis=-1)

    for h in range(RET_HEADS):
        cs = slice(h * RET_DK, (h + 1) * RET_DK)
        q = rot(q_ref[:, cs].astype(F32))
        k = rot(k_ref[:, cs].astype(F32)) * (RET_DK ** -0.5)
        v = v_ref[:, cs].astype(BF16)
        qb = q.astype(BF16)
        s = _dot_nt(qb, k.astype(BF16)) * dec_ref[h]
        state = state_ref[h]
        y = _dot(s.astype(BF16), v) + _dot((q * xi_ref[h]).astype(BF16), state.astype(BF16))
        state_ref[h] = cd_ref[h] * state + _dot_tn((k * zeta_ref[h]).astype(BF16), v)
        mu = jnp.mean(y, axis=-1, keepdims=True)
        yc = y - mu
        var = jnp.mean(yc * yc, axis=-1, keepdims=True)
        yn = yc * lax.rsqrt(var + EPS) * gain_ref[:, cs]
        gate = g_ref[:, cs].astype(F32)
        o_ref[:, cs] = (gate * _sigmoid(gate) * yn).astype(o_ref.dtype)


def _retention(proj, gain, batch, seq):
    tok = proj.shape[0]
    c = RET_C
    n_chunks = seq // c
    h, dk = RET_HEADS, RET_DK
    half = dk // 2
    inv = ROPE_BASE ** (-jnp.arange(half, dtype=F32) / half)
    ang = jnp.arange(seq, dtype=F32)[:, None] * inv[None, :]
    cos, sin = jnp.cos(ang), jnp.sin(ang)
    log_g = jnp.log(1.0 - 2.0 ** (-5.0 - jnp.arange(h, dtype=F32)))
    idx = jnp.arange(c, dtype=F32)
    diff = idx[:, None] - idx[None, :]
    dec = jnp.where(diff >= 0.0, jnp.exp(jnp.maximum(diff, 0.0)[None] * log_g[:, None, None]), 0.0)
    xi = jnp.broadcast_to(jnp.exp((idx + 1.0)[None] * log_g[:, None])[:, :, None], (h, c, dk))
    zeta = jnp.broadcast_to(jnp.exp((c - 1.0 - idx)[None] * log_g[:, None])[:, :, None], (h, c, dk))
    cd = jnp.exp(c * log_g)

    w = h * dk
    row = lambda b, n: (b * n_chunks + n)
    return pl.pallas_call(
        _ret_kernel,
        grid=(batch, n_chunks),
        in_specs=[
            pl.BlockSpec(memory_space=pltpu.SMEM),
            pl.BlockSpec((c, w), lambda b, n: (row(b, n), 0)),
            pl.BlockSpec((c, w), lambda b, n: (row(b, n), 1)),
            pl.BlockSpec((c, w), lambda b, n: (row(b, n), 2)),
            pl.BlockSpec((c, w), lambda b, n: (row(b, n), 3)),
            pl.BlockSpec((c, half), lambda b, n: (n, 0)),
            pl.BlockSpec((c, half), lambda b, n: (n, 0)),
            pl.BlockSpec((h, c, c), lambda b, n: (0, 0, 0)),
            pl.BlockSpec((h, c, dk), lambda b, n: (0, 0, 0)),
            pl.BlockSpec((h, c, dk), lambda b, n: (0, 0, 0)),
            pl.BlockSpec((1, w), lambda b, n: (0, 0)),
        ],
        out_specs=pl.BlockSpec((c, w), lambda b, n: (row(b, n), 0)),
        out_shape=jax.ShapeDtypeStruct((tok, w), BF16),
        scratch_shapes=[pltpu.VMEM((h, dk, RET_DV), F32)],
        compiler_params=pltpu.CompilerParams(
            dimension_semantics=("parallel", "arbitrary"), vmem_limit_bytes=VMEM_LIMIT),
        name="retention",
    )(cd, proj, proj, proj, proj, cos, sin, dec, xi, zeta, gain.reshape(1, w))


def _compress_kernel(k_ref, v_ref, pek_ref, w1k_ref, w1kf_ref, w2k_ref, pev_ref, w1v_ref, w1vf_ref,
                     w2v_ref, ok_ref, ov_ref):
    dh = NSA_DH
    nblk = k_ref.shape[0] // CMP_STRIDE

    def one(x_ref, pe_ref, w1_ref, w1f_ref, w2_ref, o_ref):
        ab = jnp.zeros((nblk, 2 * dh), F32)
        for l in range(CMP_STRIDE):
            rows = x_ref[pl.ds(l, nblk, stride=CMP_STRIDE), :].astype(BF16)
            ab = ab + _dot(rows, w1_ref[l])
        a, b = ab[:, :dh], ab[:, dh:]
        b_next = pltpu.roll(b, nblk - 1, axis=0)
        ridx = lax.broadcasted_iota(jnp.int32, (nblk, dh), 0)
        b_next = jnp.where(ridx < nblk - 1, b_next, 0.0)
        pe_term = _dot(pe_ref[...], w1f_ref[...])[0:1, :]
        hdn = a + b_next + pe_term
        o_ref[0, 0] = _dot((hdn * _sigmoid(hdn)).astype(BF16), w2_ref[...])

    one(k_ref, pek_ref, w1k_ref, w1kf_ref, w2k_ref, ok_ref)
    one(v_ref, pev_ref, w1v_ref, w1vf_ref, w2v_ref, ov_ref)


def _compress(kv, pe_k, w1_k, w2_k, pe_v, w1_v, w2_v, batch, seq):
    dh, g = NSA_DH, NSA_KV_GROUPS
    nblk = seq // CMP_STRIDE
    half = CMP_LEN // 2

    def prep(pe, w1, w2):
        w1b = w1.astype(BF16)
        w1_pair = jnp.concatenate([w1b[:half], w1b[half:]], axis=-1)
        pe_flat = jnp.broadcast_to(pe.reshape(1, CMP_LEN * dh), (8, CMP_LEN * dh)).astype(BF16)
        return pe_flat, w1_pair, w1b.reshape(CMP_LEN * dh, dh), w2.astype(BF16)

    args_k = prep(pe_k, w1_k, w2_k)
    args_v = prep(pe_v, w1_v, w2_v)
    kcol = 0
    vcol = KV_W // dh
    const = lambda shape: pl.BlockSpec(shape, lambda b, gg: (0,) * len(shape))
    wspecs = [const((8, CMP_LEN * dh)), const((half, dh, 2 * dh)), const((CMP_LEN * dh, dh)),
              const((dh, dh))]
    out_spec = pl.BlockSpec((1, 1, nblk, dh), lambda b, gg: (b, gg, 0, 0))
    return pl.pallas_call(
        _compress_kernel,
        grid=(batch, g),
        in_specs=[pl.BlockSpec((seq, dh), lambda b, gg: (b, kcol + gg)),
                  pl.BlockSpec((seq, dh), lambda b, gg: (b, vcol + gg))] + wspecs + wspecs,
        out_specs=[out_spec, out_spec],
        out_shape=[jax.ShapeDtypeStruct((batch, g, nblk, dh), F32)] * 2,
        compiler_params=pltpu.CompilerParams(
            dimension_semantics=("parallel", "parallel"), vmem_limit_bytes=VMEM_LIMIT),
        name="compress",
    )(kv, kv, *args_k, *args_v)


def _cmpattn_kernel(q_ref, kc_ref, vc_ref, tb_ref, pp_ref, ovt_ref, gate_ref, o_ref, sel_ref, rank_ref):
    tq = q_ref.shape[0]
    ncp = kc_ref.shape[2]
    dh = NSA_DH
    s0 = pl.program_id(2) * tq
    hpg = NSA_HPG
    pos_r = s0 + (lax.broadcasted_iota(jnp.int32, (hpg * tq, ncp), 0) & (tq - 1))
    n_c = lax.broadcasted_iota(jnp.int32, (hpg * tq, ncp), 1)
    mask = pos_r >= n_c * CMP_STRIDE + (CMP_LEN - 1)
    kc = kc_ref[0, 0].astype(BF16)
    vc = vc_ref[0, 0].astype(BF16)
    scale = dh ** -0.5
    q = jnp.concatenate([q_ref[:, hh * dh:(hh + 1) * dh] for hh in range(hpg)], axis=0).astype(BF16)
    bias = _dot(tb_ref[...].reshape(hpg * tq, tb_ref.shape[2]), pp_ref[0])
    lm = jnp.where(mask, _dot_nt(q, kc) * scale + bias, NEG)
    m = jnp.max(lm, axis=-1, keepdims=True)
    e = jnp.where(mask, jnp.exp(lm - m), 0.0)
    den = jnp.sum(e, axis=-1, keepdims=True)
    p = e / jnp.where(den > 0.0, den, 1.0)
    o = _dot(p.astype(BF16), vc)
    gate = _sigmoid(gate_ref[0, 0])
    psum = p[0:tq]
    for hh in range(hpg):
        o_ref[:, hh * dh:(hh + 1) * dh] = (o[hh * tq:(hh + 1) * tq] * gate[:, hh:hh + 1]).astype(o_ref.dtype)
        if hh:
            psum = psum + p[hh * tq:(hh + 1) * tq]
    nsel = ovt_ref.shape[0]
    imp_t = _dot_nt(ovt_ref[...], psum.astype(BF16))
    jb = lax.broadcasted_iota(jnp.int32, (nsel, tq), 0)
    pos = s0 + lax.broadcasted_iota(jnp.int32, (nsel, tq), 1)
    cur = jnp.right_shift(pos, SEL_LEN.bit_length() - 1)
    causal = jb * SEL_LEN <= pos
    forced = (jb == 0) | (jb == cur) | (jb == cur - 1)
    score = jnp.where(forced, FORCE, jnp.where(causal, imp_t, NEG))
    per_tile = tq // SEL_LEN
    rank_ref[...] = jnp.zeros_like(rank_ref)
    for c in range(nsel // per_tile):
        @pl.when(c <= pl.program_id(2))
        def _():
            rank = rank_ref[...]
            for kk in range(c * per_tile, (c + 1) * per_tile):
                row = score[kk:kk + 1, :]
                beats = (row > score) | ((row == score) & (jb > kk))
                rank = rank + jnp.where(beats, 1, 0)
            rank_ref[...] = rank
    rank = rank_ref[...]
    selb = jnp.where((rank < SEL_TOPK) & causal, 0.0, NEG)
    selb = jnp.concatenate([selb, jnp.zeros((SEL_PAD - nsel, tq), F32)], axis=0)
    sel_ref[0, 0] = selb.astype(BF16)


def _cmp_attention(proj, k_cmp, v_cmp, rel_bias, gates, batch, seq):
    dh, g, hpg = NSA_DH, NSA_KV_GROUPS, NSA_HPG
    tq = CMP_TQ
    nq = seq // tq
    ncp = k_cmp.shape[2]
    nsel = seq // SEL_LEN
    n_cmp = (seq - CMP_LEN) // CMP_STRIDE + 1
    cmp_idx = np.arange(n_cmp)[:, None] * CMP_STRIDE + np.arange(CMP_LEN)[None, :]
    overlap = ((cmp_idx // SEL_LEN)[:, :, None] == np.arange(nsel)[None, None, :]).sum(1) / CMP_LEN
    ovt = np.zeros((nsel, ncp), np.float32)
    ovt[:, :n_cmp] = overlap.T
    nb = tq // CMP_STRIDE
    assert 2 * nb + 1 <= CMP_BAND and CMP_STRIDE * (nb + 1) - (CMP_LEN - 1) >= REL_MAX_DIST
    i = np.arange(tq)[:, None]
    r = np.arange(CMP_BAND)[None, :]
    rel = np.where(r < 2 * nb, i - CMP_STRIDE * (r - nb) - (CMP_LEN - 1), REL_MAX_DIST)
    tb = jnp.where(jnp.asarray(r <= 2 * nb), _bias_lookup(rel_bias, jnp.asarray(rel, jnp.int32)), 0.0)
    tb_hi = tb.astype(BF16)
    tb_lo = (tb - tb_hi.astype(F32)).astype(BF16)
    tb = jnp.concatenate([tb_hi, tb_lo], axis=-1)
    n = np.arange(ncp)[None, None, :]
    first = (np.arange(nq) * nb - nb)[:, None, None]
    rr = np.arange(CMP_BAND)[None, :, None]
    pp = np.where(rr < 2 * nb, n == first + rr, (rr == 2 * nb) & (n < first)).astype(np.float32)
    pp = np.concatenate([pp, pp], axis=1)
    qcol = IN_OFF[4] // (hpg * dh)
    return pl.pallas_call(
        _cmpattn_kernel,
        grid=(batch, g, nq),
        in_specs=[
            pl.BlockSpec((tq, hpg * dh), lambda b, gg, t: (b * nq + t, qcol + gg)),
            pl.BlockSpec((1, 1, ncp, dh), lambda b, gg, t: (b, gg, 0, 0)),
            pl.BlockSpec((1, 1, ncp, dh), lambda b, gg, t: (b, gg, 0, 0)),
            pl.BlockSpec((hpg, tq, 2 * CMP_BAND), lambda b, gg, t: (gg, 0, 0)),
            pl.BlockSpec((1, 2 * CMP_BAND, ncp), lambda b, gg, t: (t, 0, 0)),
            pl.BlockSpec((nsel, ncp), lambda b, gg, t: (0, 0)),
            pl.BlockSpec((1, 1, tq, GATE_ROWS), lambda b, gg, t: (b, gg, t, 0)),
        ],
        out_specs=[
            pl.BlockSpec((tq, hpg * dh), lambda b, gg, t: (b * nq + t, gg)),
            pl.BlockSpec((1, 1, SEL_PAD, tq), lambda b, gg, t: (b, gg, 0, t)),
        ],
        out_shape=[jax.ShapeDtypeStruct((batch * seq, g * hpg * dh), BF16),
                   jax.ShapeDtypeStruct((batch, g, SEL_PAD, seq), BF16)],
        scratch_shapes=[pltpu.VMEM((nsel, tq), jnp.int32)],
        compiler_params=pltpu.CompilerParams(
            dimension_semantics=("parallel", "parallel", "parallel"), vmem_limit_bytes=VMEM_LIMIT),
        name="cmp_attn",
    )(proj, k_cmp, v_cmp, tb, jnp.asarray(pp, BF16), jnp.asarray(ovt, BF16),
      gates[:, :, 0].transpose(0, 1, 3, 2))


def _band_kernel(qi_ref, ki_ref, var_ref, first_ref, last_ref, q_ref, k_ref, v_ref, fd_ref, gate_ref,
                 *rest, use_sel):
    if use_sel:
        selb_ref, e_ref, o_ref, qt_ref, m_ref, acc_ref, bias_ref = rest
    else:
        o_ref, qt_ref, m_ref, acc_ref, bias_ref = rest
    t = pl.program_id(1)
    dh, hpg, ng = NSA_DH, NSA_HPG, NSA_KV_GROUPS
    tq, tk = q_ref.shape[0], k_ref.shape[0]

    @pl.when((pl.program_id(0) == 0) & (t == 0))
    def _():
        upper = (lax.broadcasted_iota(jnp.int32, (tk, tq), 1) >= lax.broadcasted_iota(jnp.int32, (tk, tq), 0))
        for gg in range(ng):
            for hh in range(hpg):
                cols = slice(hh * tq, (hh + 1) * tq)
                by_dist = jnp.broadcast_to(fd_ref[gg, hh, 0, 0:1, :], (tk, tq))
                far = jnp.broadcast_to(fd_ref[gg, hh, 1, 0:1, :], (tk, tq))
                cyc = pltpu.roll(by_dist, 0, 1, stride=1, stride_axis=0)
                bias_ref[gg, 0, :, cols] = jnp.where(upper, cyc, NEG)
                if use_sel:
                    bias_ref[gg, 1, :, cols] = jnp.where(upper, far, cyc)
                    bias_ref[gg, 2, :, cols] = far
                else:
                    bias_ref[gg, 1, :, cols] = jnp.where(upper, NEG, cyc)

    @pl.when(first_ref[t] == 1)
    def _():
        m_ref[...] = jnp.full_like(m_ref, NEG)
        acc_ref[...] = jnp.zeros_like(acc_ref)
        scale = dh ** -0.5 * LOG2E
        for gg in range(ng):
            for hh in range(hpg):
                h = gg * hpg + hh
                q = q_ref[:, h * dh:(h + 1) * dh].astype(F32)
                qt_ref[gg, 0:dh, hh * tq:(hh + 1) * tq] = (q * scale).T.astype(BF16)
                if use_sel:
                    qt_ref[gg, dh:, hh * tq:(hh + 1) * tq] = selb_ref[0, gg]

    var = var_ref[t]
    ones = jnp.ones((ACC_PAD, tk), F32)
    for gg in range(ng):
        k = k_ref[:, gg * dh:(gg + 1) * dh]
        if use_sel:
            k = jnp.concatenate([k, e_ref[...]], axis=1)
        vt = jnp.concatenate([v_ref[:, gg * dh:(gg + 1) * dh].astype(F32).T, ones], axis=0).astype(BF16)
        s = _dot(k, qt_ref[gg]) + bias_ref[gg, var]
        m_prev = m_ref[gg]
        m_new = jnp.maximum(m_prev, jnp.max(s, axis=0, keepdims=True))
        alpha = jnp.exp2(m_prev - m_new)
        p = jnp.exp2(s - m_new).astype(BF16)
        acc_ref[gg] = alpha * acc_ref[gg] + _dot(vt, p)
        m_ref[gg] = m_new

    @pl.when(last_ref[t] == 1)
    def _():
        for gg in range(ng):
            gate = _sigmoid(gate_ref[0, gg, 0])
            for hh in range(hpg):
                h = gg * hpg + hh
                cols = slice(hh * tq, (hh + 1) * tq)
                o = acc_ref[gg, 0:dh, cols] * (gate[hh:hh + 1, :] / acc_ref[gg, dh:dh + 1, cols])
                o_ref[:, h * dh:(h + 1) * dh] = o.T.astype(o_ref.dtype)


def _band_steps(seq, window):
    tq, tk = ATT_T, ATT_TK
    steps = []
    for qi in range(seq // tq):
        k_hi = qi * tq // tk
        k_lo = max(0, (qi * tq - window + 1) // tk) if window else 0
        steps += [(qi, ki, min((qi * tq - ki * tk) // tq, ATT_NVAR - 1)) for ki in range(k_lo, k_hi + 1)]
    return steps


def _band_attention(proj, bias_by_dist, steps, koff, voff, gates, branch, batch, seq, selb=None, e_mat=None):
    dh, g, hpg = NSA_DH, NSA_KV_GROUPS, NSA_HPG
    tq, tk = ATT_T, ATT_TK
    nq, nk = seq // tq, seq // tk
    qi = np.array([s[0] for s in steps], np.int32)
    ki = np.array([s[1] for s in steps], np.int32)
    var = np.array([s[2] for s in steps], np.int32)
    first = np.concatenate([[1], (qi[1:] != qi[:-1]).astype(np.int32)]).astype(np.int32)
    last = np.concatenate([(qi[1:] != qi[:-1]).astype(np.int32), [1]]).astype(np.int32)
    qcol, kcol, vcol = IN_OFF[4] // NSA_W, koff // KV_W, voff // KV_W
    use_sel = selb is not None
    in_specs = [
        pl.BlockSpec((tq, NSA_W), lambda b, t, qi_r, ki_r, v_r, f_r, l_r: (b * nq + qi_r[t], qcol)),
        pl.BlockSpec((tk, KV_W), lambda b, t, qi_r, ki_r, v_r, f_r, l_r: (b * nk + ki_r[t], kcol)),
        pl.BlockSpec((tk, KV_W), lambda b, t, qi_r, ki_r, v_r, f_r, l_r: (b * nk + ki_r[t], vcol)),
        pl.BlockSpec(bias_by_dist.shape, lambda b, t, qi_r, ki_r, v_r, f_r, l_r: (0, 0, 0, 0, 0)),
        pl.BlockSpec((1, g, 1, GATE_ROWS, tq),
                     lambda b, t, qi_r, ki_r, v_r, f_r, l_r: (b, 0, branch, 0, qi_r[t])),
    ]
    args = [proj, proj, proj, bias_by_dist, gates]
    if use_sel:
        in_specs += [
            pl.BlockSpec((1, g, SEL_PAD, tq), lambda b, t, qi_r, ki_r, v_r, f_r, l_r: (b, 0, 0, qi_r[t])),
            pl.BlockSpec((tk, SEL_PAD), lambda b, t, qi_r, ki_r, v_r, f_r, l_r: (ki_r[t], 0)),
        ]
        args += [selb, e_mat]
    kdim = dh + SEL_PAD if use_sel else dh
    grid_spec = pltpu.PrefetchScalarGridSpec(
        num_scalar_prefetch=5,
        grid=(batch, len(steps)),
        in_specs=in_specs,
        out_specs=pl.BlockSpec((tq, NSA_W), lambda b, t, qi_r, ki_r, v_r, f_r, l_r: (b * nq + qi_r[t], 0)),
        scratch_shapes=[pltpu.VMEM((g, kdim, hpg * tq), BF16), pltpu.VMEM((g, 1, hpg * tq), F32),
                        pltpu.VMEM((g, dh + ACC_PAD, hpg * tq), F32),
                        pltpu.VMEM((g, ATT_NVAR if use_sel else ATT_NVAR - 1, tk, hpg * tq), F32)],
    )
    return pl.pallas_call(
        functools.partial(_band_kernel, use_sel=use_sel),
        grid_spec=grid_spec,
        out_shape=jax.ShapeDtypeStruct((batch * seq, NSA_W), BF16),
        compiler_params=pltpu.CompilerParams(
            dimension_semantics=("arbitrary", "arbitrary"), vmem_limit_bytes=VMEM_LIMIT),
        name="sel_attn" if use_sel else "win_attn",
    )(jnp.asarray(qi), jnp.asarray(ki), jnp.asarray(var), jnp.asarray(first), jnp.asarray(last), *args)


def _out_kernel(x_ref, yr_ref, oc_ref, os_ref, ow_ref, w_ref, o_ref):
    y_nsa = oc_ref[...].astype(F32) + os_ref[...].astype(F32) + ow_ref[...].astype(F32)
    y = jnp.concatenate([yr_ref[...], y_nsa.astype(BF16)], axis=-1)
    o_ref[...] = x_ref[...] + _dot(y, w_ref[...])


def _out_proj(x2d, y_ret, o_cmp, o_sel, o_win, w_out):
    tok, d = x2d.shape
    tm = OUT_TM
    wmix = y_ret.shape[1]
    row = lambda shape: pl.BlockSpec(shape, lambda i: (i, 0))
    return pl.pallas_call(
        _out_kernel,
        grid=(tok // tm,),
        in_specs=[row((tm, d)), row((tm, wmix)), row((tm, wmix)), row((tm, wmix)), row((tm, wmix)),
                  pl.BlockSpec(w_out.shape, lambda i: (0, 0))],
        out_specs=row((tm, d)),
        out_shape=jax.ShapeDtypeStruct((tok, d), F32),
        compiler_params=pltpu.CompilerParams(
            dimension_semantics=("parallel",), vmem_limit_bytes=VMEM_LIMIT),
        name="out_proj",
    )(x2d, y_ret, o_cmp, o_sel, o_win, w_out)


def _t5_bucket_of(rel):
    n = jnp.maximum(rel, 0)
    max_exact = REL_BUCKETS // 2
    nf = jnp.maximum(n, 1).astype(F32)
    large = max_exact + (jnp.log(nf / max_exact) / math.log(REL_MAX_DIST / max_exact)
                         * (REL_BUCKETS - max_exact)).astype(jnp.int32)
    large = jnp.minimum(large, REL_BUCKETS - 1)
    return jnp.where(n < max_exact, n, large)


def _bias_lookup(rel_bias, rel):
    bucket = _t5_bucket_of(rel)[None]
    tab = rel_bias.astype(F32).reshape((rel_bias.shape[0], REL_BUCKETS) + (1,) * rel.ndim)
    out = jnp.zeros((rel_bias.shape[0],) + rel.shape, F32)
    for b in range(REL_BUCKETS):
        out = jnp.where(bucket == b, tab[:, b], out)
    return out


def _bias_by_distance(rel_bias):
    t, g, hpg = ATT_T, NSA_KV_GROUPS, NSA_HPG
    assert ATT_TK == t and t >= REL_MAX_DIST and WIN == t
    by_dist = _bias_lookup(rel_bias, jnp.arange(t, dtype=jnp.int32)) * LOG2E
    far = jnp.broadcast_to(by_dist[:, t - 1:t], by_dist.shape)
    tab = jnp.stack([by_dist, far], axis=1).reshape(g, hpg, 2, 1, t)
    return jnp.broadcast_to(tab, (g, hpg, 2, 8, t))


def _token_mix(x2d, mix_norm, w_in, ret_gn_gain, pe_k, w1_k, w2_k, pe_v, w1_v, w2_v, w_out, rel_bias,
               batch, seq):
    d = x2d.shape[1]
    dh = NSA_DH
    w_pad = _to_bf16(w_in, D_IN_PAD)
    proj, cmp_kv = _inproj(x2d, mix_norm, w_pad)
    y_ret = _retention(proj, ret_gn_gain, batch, seq)

    gates = proj[:, IN_OFF[11]:IN_OFF[11] + IN_SPLITS[11]].astype(F32)
    gates = gates.reshape(batch, seq, NSA_KV_GROUPS, NSA_HPG, N_BRANCH).transpose(0, 2, 4, 3, 1)
    gates = jnp.pad(gates, ((0, 0), (0, 0), (0, 0), (0, GATE_ROWS - NSA_HPG), (0, 0)))

    k_cmp, v_cmp = _compress(cmp_kv, pe_k, w1_k, w2_k, pe_v, w1_v, w2_v, batch, seq)
    o_cmp, selb = _cmp_attention(proj, k_cmp, v_cmp, rel_bias, gates, batch, seq)

    bias_by_dist = _bias_by_distance(rel_bias)
    e_np = np.zeros((seq, SEL_PAD), np.float32)
    e_np[np.arange(seq), np.arange(seq) // SEL_LEN] = 1.0
    o_sel = _band_attention(proj, bias_by_dist, _band_steps(seq, 0),
                            IN_OFF[7], IN_OFF[8], gates, 1, batch, seq,
                            selb=selb, e_mat=jnp.asarray(e_np, BF16))
    o_win = _band_attention(proj, bias_by_dist, _band_steps(seq, WIN),
                            IN_OFF[9], IN_OFF[10], gates, 2, batch, seq)
    return _out_proj(x2d, y_ret, o_cmp, o_sel, o_win, _to_bf16(w_out))


def _cast_kernel(x_ref, o_ref):
    c = x_ref.shape[1]
    o_ref[:, 0:c] = x_ref[...].astype(o_ref.dtype)
    if o_ref.shape[1] > c:
        o_ref[:, c:] = jnp.zeros((o_ref.shape[0], o_ref.shape[1] - c), o_ref.dtype)


def _to_bf16(w, cols=None):
    r, c = w.shape
    cols = c if cols is None else cols
    tr = 1 << ((CAST_BLOCK_BYTES // (4 * c)).bit_length() - 1)
    while r % tr:
        tr //= 2
    return pl.pallas_call(
        _cast_kernel,
        grid=(r // tr,),
        in_specs=[pl.BlockSpec((tr, c), lambda i: (i, 0))],
        out_specs=pl.BlockSpec((tr, cols), lambda i: (i, 0)),
        out_shape=jax.ShapeDtypeStruct((r, cols), BF16),
        compiler_params=pltpu.CompilerParams(
            dimension_semantics=("parallel",), vmem_limit_bytes=VMEM_LIMIT),
        name="to_bf16",
    )(w)


def kernel(x, ffn1_norm, ffn1_w1, ffn1_w3, ffn1_w2, mix_norm, w_in, ret_gn_gain, cmp_pe_k, cmp_w1_k,
           cmp_w2_k, cmp_pe_v, cmp_w1_v, cmp_w2_v, w_out, ffn2_norm, ffn2_w1, ffn2_w3, ffn2_w2,
           rel_bias, final_norm):
    batch, seq, d = x.shape
    depth = ffn1_norm.shape[0]
    h = x.reshape(batch * seq, d)
    for l in range(depth):
        last = l == depth - 1
        h = _ffn(h, ffn1_norm[l], _to_bf16(ffn1_w1[l]), _to_bf16(ffn1_w3[l]), _to_bf16(ffn1_w2[l]),
                 final_norm, False)
        h = _token_mix(h, mix_norm[l], w_in[l], ret_gn_gain[l], cmp_pe_k[l], cmp_w1_k[l], cmp_w2_k[l],
                       cmp_pe_v[l], cmp_w1_v[l], cmp_w2_v[l], w_out[l], rel_bias, batch, seq)
        h = _ffn(h, ffn2_norm[l], _to_bf16(ffn2_w1[l]), _to_bf16(ffn2_w3[l]), _to_bf16(ffn2_w2[l]),
                 final_norm, last)
    if depth == 0:
        raise ValueError("depth must be positive")
    return h.reshape(batch, seq, d)
```

```python
import functools
import math

import jax
import jax.numpy as jnp
import numpy as np
from jax import lax
from jax.experimental import pallas as pl
from jax.experimental.pallas import tpu as pltpu

F32 = jnp.float32
BF16 = jnp.bfloat16

RET_HEADS = 4
RET_DK = 256
RET_DV = 256
ROPE_BASE = 10000.0
NSA_HEADS = 8
NSA_KV_GROUPS = 2
NSA_HPG = NSA_HEADS // NSA_KV_GROUPS
NSA_DH = 128
CMP_LEN = 32
CMP_STRIDE = 16
SEL_LEN = 64
SEL_TOPK = 16
WIN = 512
N_BRANCH = 3
REL_BUCKETS = 32
REL_MAX_DIST = 128
EPS = 1e-6
NEG = -1e30
FORCE = 1e4

RET_W = RET_HEADS * RET_DV
NSA_W = NSA_HEADS * NSA_DH
KV_W = NSA_KV_GROUPS * NSA_DH
IN_SPLITS = [RET_HEADS * RET_DK, RET_HEADS * RET_DK, RET_W, RET_W, NSA_W,
             KV_W, KV_W, KV_W, KV_W, KV_W, KV_W, NSA_HEADS * N_BRANCH]
D_IN = sum(IN_SPLITS)
IN_OFF = [sum(IN_SPLITS[:i]) for i in range(len(IN_SPLITS))]

LANE = 128
IN_TN = 2304
D_IN_PAD = 3 * IN_TN
CMP_KV_TILE = IN_OFF[5] // IN_TN
CMP_KV_LOCAL = IN_OFF[5] % IN_TN
assert CMP_KV_LOCAL + 2 * KV_W <= IN_TN and IN_OFF[6] == IN_OFF[5] + KV_W
GATE_ROWS = 8
FFN_TM = 1024
FFN_TF = 512
IN_TM = 512
RET_C = 256
CMP_TQ = 512
ATT_T = 512
ATT_TK = 512
ATT_NVAR = 3
OUT_TM = 512
SEL_PAD = 128
ACC_PAD = 8
CMP_BAND = 128
VMEM_LIMIT = 56 * 1024 * 1024
CAST_BLOCK_BYTES = 6 * 1024 * 1024
CAST_T_ROWS = 256
LOG2E = math.log2(math.e)


def _dot(a, b):
    return jnp.dot(a, b, preferred_element_type=F32)


def _dot_nt(a, b):
    return lax.dot_general(a, b, (((1,), (1,)), ((), ())), preferred_element_type=F32)


def _dot_tn(a, b):
    return lax.dot_general(a, b, (((0,), (0,)), ((), ())), preferred_element_type=F32)


def _sigmoid(x):
    return 1.0 / (1.0 + jnp.exp(-x))


def _rms(x, g):
    ms = jnp.mean(x * x, axis=-1, keepdims=True)
    return x * lax.rsqrt(ms + EPS) * g


def _ffn_kernel(x_ref, g_ref, w1_ref, w3_ref, w2_ref, fg_ref, o_ref, n_ref, *, final_norm):
    j = pl.program_id(1)

    @pl.when(j == 0)
    def _():
        x = x_ref[...]
        n_ref[...] = _rms(x, g_ref[...]).astype(BF16)
        o_ref[...] = x

    n = n_ref[...]
    a = _dot(n, w1_ref[...])
    b = _dot(n, w3_ref[...])
    h = (0.5 * a * _sigmoid(a) * b).astype(BF16)
    o_ref[...] += _dot(h, w2_ref[...])

    if final_norm:
        @pl.when(j == pl.num_programs(1) - 1)
        def _():
            o_ref[...] = _rms(o_ref[...], fg_ref[...])


def _ffn(x2d, g, w1, w3, w2, fg, final_norm):
    tok, d = x2d.shape
    dff = w1.shape[1]
    tm, tf = FFN_TM, FFN_TF
    return pl.pallas_call(
        functools.partial(_ffn_kernel, final_norm=final_norm),
        grid=(tok // tm, dff // tf),
        in_specs=[
            pl.BlockSpec((tm, d), lambda i, j: (i, 0)),
            pl.BlockSpec((1, d), lambda i, j: (0, 0)),
            pl.BlockSpec((d, tf), lambda i, j: (0, j)),
            pl.BlockSpec((d, tf), lambda i, j: (0, j)),
            pl.BlockSpec((tf, d), lambda i, j: (j, 0)),
            pl.BlockSpec((1, d), lambda i, j: (0, 0)),
        ],
        out_specs=pl.BlockSpec((tm, d), lambda i, j: (i, 0)),
        out_shape=jax.ShapeDtypeStruct((tok, d), F32),
        scratch_shapes=[pltpu.VMEM((tm, d), BF16)],
        compiler_params=pltpu.CompilerParams(
            dimension_semantics=("parallel", "arbitrary"), vmem_limit_bytes=VMEM_LIMIT),
        name="ffn",
    )(x2d, g.reshape(1, d), w1, w3, w2, fg.reshape(1, d))


def _inproj_kernel(x_ref, g_ref, w_ref, o_ref, kv_ref, n_ref):
    j = pl.program_id(1)

    @pl.when(j == 0)
    def _():
        n_ref[...] = _rms(x_ref[...], g_ref[...]).astype(BF16)

    res = _dot(n_ref[...], w_ref[...])
    o_ref[...] = res.astype(o_ref.dtype)

    @pl.when(j == CMP_KV_TILE)
    def _():
        kv_ref[...] = res[:, CMP_KV_LOCAL:CMP_KV_LOCAL + 2 * KV_W]


def _inproj(x2d, g, w_pad):
    tok, d = x2d.shape
    n_out = w_pad.shape[1]
    tm, tn = IN_TM, IN_TN
    return pl.pallas_call(
        _inproj_kernel,
        grid=(tok // tm, n_out // tn),
        in_specs=[
            pl.BlockSpec((tm, d), lambda i, j: (i, 0)),
            pl.BlockSpec((1, d), lambda i, j: (0, 0)),
            pl.BlockSpec((d, tn), lambda i, j: (0, j)),
        ],
        out_specs=[pl.BlockSpec((tm, tn), lambda i, j: (i, j)),
                   pl.BlockSpec((tm, 2 * KV_W), lambda i, j: (i, 0))],
        out_shape=[jax.ShapeDtypeStruct((tok, n_out), BF16),
                   jax.ShapeDtypeStruct((tok, 2 * KV_W), F32)],
        scratch_shapes=[pltpu.VMEM((tm, d), BF16)],
        compiler_params=pltpu.CompilerParams(
            dimension_semantics=("parallel", "arbitrary"), vmem_limit_bytes=VMEM_LIMIT),
        name="in_proj",
    )(x2d, g.reshape(1, d), w_pad)


def _ret_kernel(cd_ref, q_ref, k_ref, v_ref, g_ref, cos_ref, sin_ref, dec_ref, xi_ref, zeta_ref,
                gain_ref, o_ref, state_ref):
    @pl.when(pl.program_id(1) == 0)
    def _():
        state_ref[...] = jnp.zeros_like(state_ref)

    cos = cos_ref[...]
    sin = sin_ref[...]
    half = RET_DK // 2

    def rot(t):
        t1, t2 = t[:, :half], t[:, half:]
        return jnp.concatenate([t1 * cos - t2 * sin, t1 * sin + t2 * cos], axis=-1)

    for h in range(RET_HEADS):
        cs = slice(h * RET_DK, (h + 1) * RET_DK)
        q = rot(q_ref[:, cs].astype(F32))
        k = rot(k_ref[:, cs].astype(F32)) * (RET_DK ** -0.5)
        v = v_ref[:, cs].astype(BF16)
        qb = q.astype(BF16)
        s = _dot_nt(qb, k.astype(BF16)) * dec_ref[h]
        state = state_ref[h]
        y = _dot(s.astype(BF16), v) + _dot((q * xi_ref[h]).astype(BF16), state.astype(BF16))
        state_ref[h] = cd_ref[h] * state + _dot_tn((k * zeta_ref[h]).astype(BF16), v)
        mu = jnp.mean(y, axis=-1, keepdims=True)
        yc = y - mu
        var = jnp.mean(yc * yc, axis=-1, keepdims=True)
        yn = yc * lax.rsqrt(var + EPS) * gain_ref[:, cs]
        gate = g_ref[:, cs].astype(F32)
        o_ref[:, cs] = (gate * _sigmoid(gate) * yn).astype(o_ref.dtype)


def _retention(proj, gain, batch, seq):
    tok = proj.shape[0]
    c = RET_C
    n_chunks = seq // c
    h, dk = RET_HEADS, RET_DK
    half = dk // 2
    inv = ROPE_BASE ** (-jnp.arange(half, dtype=F32) / half)
    ang = jnp.arange(seq, dtype=F32)[:, None] * inv[None, :]
    cos, sin = jnp.cos(ang), jnp.sin(ang)
    log_g = jnp.log(1.0 - 2.0 ** (-5.0 - jnp.arange(h, dtype=F32)))
    idx = jnp.arange(c, dtype=F32)
    diff = idx[:, None] - idx[None, :]
    dec = jnp.where(diff >= 0.0, jnp.exp(jnp.maximum(diff, 0.0)[None] * log_g[:, None, None]), 0.0)
    xi = jnp.broadcast_to(jnp.exp((idx + 1.0)[None] * log_g[:, None])[:, :, None], (h, c, dk))
    zeta = jnp.broadcast_to(jnp.exp((c - 1.0 - idx)[None] * log_g[:, None])[:, :, None], (h, c, dk))
    cd = jnp.exp(c * log_g)

    w = h * dk
    row = lambda b, n: (b * n_chunks + n)
    return pl.pallas_call(
        _ret_kernel,
        grid=(batch, n_chunks),
        in_specs=[
            pl.BlockSpec(memory_space=pltpu.SMEM),
            pl.BlockSpec((c, w), lambda b, n: (row(b, n), 0)),
            pl.BlockSpec((c, w), lambda b, n: (row(b, n), 1)),
            pl.BlockSpec((c, w), lambda b, n: (row(b, n), 2)),
            pl.BlockSpec((c, w), lambda b, n: (row(b, n), 3)),
            pl.BlockSpec((c, half), lambda b, n: (n, 0)),
            pl.BlockSpec((c, half), lambda b, n: (n, 0)),
            pl.BlockSpec((h, c, c), lambda b, n: (0, 0, 0)),
            pl.BlockSpec((h, c, dk), lambda b, n: (0, 0, 0)),
            pl.BlockSpec((h, c, dk), lambda b, n: (0, 0, 0)),
            pl.BlockSpec((1, w), lambda b, n: (0, 0)),
        ],
        out_specs=pl.BlockSpec((c, w), lambda b, n: (row(b, n), 0)),
        out_shape=jax.ShapeDtypeStruct((tok, w), BF16),
        scratch_shapes=[pltpu.VMEM((h, dk, RET_DV), F32)],
        compiler_params=pltpu.CompilerParams(
            dimension_semantics=("parallel", "arbitrary"), vmem_limit_bytes=VMEM_LIMIT),
        name="retention",
    )(cd, proj, proj, proj, proj, cos, sin, dec, xi, zeta, gain.reshape(1, w))


def _compress_kernel(k_ref, v_ref, pek_ref, w1k_ref, w1kf_ref, w2k_ref, pev_ref, w1v_ref, w1vf_ref,
                     w2v_ref, ok_ref, ov_ref):
    dh = NSA_DH
    nblk = k_ref.shape[0] // CMP_STRIDE

    def one(x_ref, pe_ref, w1_ref, w1f_ref, w2_ref, o_ref):
        ab = jnp.zeros((nblk, 2 * dh), F32)
        for l in range(CMP_STRIDE):
            rows = x_ref[pl.ds(l, nblk, stride=CMP_STRIDE), :].astype(BF16)
            ab = ab + _dot(rows, w1_ref[l])
        a, b = ab[:, :dh], ab[:, dh:]
        b_next = pltpu.roll(b, nblk - 1, axis=0)
        ridx = lax.broadcasted_iota(jnp.int32, (nblk, dh), 0)
        b_next = jnp.where(ridx < nblk - 1, b_next, 0.0)
        pe_term = _dot(pe_ref[...], w1f_ref[...])[0:1, :]
        hdn = a + b_next + pe_term
        o_ref[0, 0] = _dot((hdn * _sigmoid(hdn)).astype(BF16), w2_ref[...])

    one(k_ref, pek_ref, w1k_ref, w1kf_ref, w2k_ref, ok_ref)
    one(v_ref, pev_ref, w1v_ref, w1vf_ref, w2v_ref, ov_ref)


def _compress(kv, pe_k, w1_k, w2_k, pe_v, w1_v, w2_v, batch, seq):
    dh, g = NSA_DH, NSA_KV_GROUPS
    nblk = seq // CMP_STRIDE
    half = CMP_LEN // 2

    def prep(pe, w1, w2):
        w1b = w1.astype(BF16)
        w1_pair = jnp.concatenate([w1b[:half], w1b[half:]], axis=-1)
        pe_flat = jnp.broadcast_to(pe.reshape(1, CMP_LEN * dh), (8, CMP_LEN * dh)).astype(BF16)
        return pe_flat, w1_pair, w1b.reshape(CMP_LEN * dh, dh), w2.astype(BF16)

    args_k = prep(pe_k, w1_k, w2_k)
    args_v = prep(pe_v, w1_v, w2_v)
    kcol = 0
    vcol = KV_W // dh
    const = lambda shape: pl.BlockSpec(shape, lambda b, gg: (0,) * len(shape))
    wspecs = [const((8, CMP_LEN * dh)), const((half, dh, 2 * dh)), const((CMP_LEN * dh, dh)),
              const((dh, dh))]
    out_spec = pl.BlockSpec((1, 1, nblk, dh), lambda b, gg: (b, gg, 0, 0))
    return pl.pallas_call(
        _compress_kernel,
        grid=(batch, g),
        in_specs=[pl.BlockSpec((seq, dh), lambda b, gg: (b, kcol + gg)),
                  pl.BlockSpec((seq, dh), lambda b, gg: (b, vcol + gg))] + wspecs + wspecs,
        out_specs=[out_spec, out_spec],
        out_shape=[jax.ShapeDtypeStruct((batch, g, nblk, dh), F32)] * 2,
        compiler_params=pltpu.CompilerParams(
            dimension_semantics=("parallel", "parallel"), vmem_limit_bytes=VMEM_LIMIT),
        name="compress",
    )(kv, kv, *args_k, *args_v)


def _cmpattn_kernel(q_ref, kc_ref, vc_ref, tb_ref, pp_ref, ovt_ref, gate_ref, o_ref, sel_ref, rank_ref):
    tq = q_ref.shape[0]
    ncp = kc_ref.shape[2]
    dh = NSA_DH
    s0 = pl.program_id(2) * tq
    hpg = NSA_HPG
    pos_r = s0 + (lax.broadcasted_iota(jnp.int32, (hpg * tq, ncp), 0) & (tq - 1))
    n_c = lax.broadcasted_iota(jnp.int32, (hpg * tq, ncp), 1)
    mask = pos_r >= n_c * CMP_STRIDE + (CMP_LEN - 1)
    kc = kc_ref[0, 0].astype(BF16)
    vc = vc_ref[0, 0].astype(BF16)
    scale = dh ** -0.5
    q = jnp.concatenate([q_ref[:, hh * dh:(hh + 1) * dh] for hh in range(hpg)], axis=0).astype(BF16)
    bias = _dot(tb_ref[...].reshape(hpg * tq, tb_ref.shape[2]), pp_ref[0])
    lm = jnp.where(mask, _dot_nt(q, kc) * scale + bias, NEG)
    m = jnp.max(lm, axis=-1, keepdims=True)
    e = jnp.where(mask, jnp.exp(lm - m), 0.0)
    den = jnp.sum(e, axis=-1, keepdims=True)
    p = e / jnp.where(den > 0.0, den, 1.0)
    o = _dot(p.astype(BF16), vc)
    gate = _sigmoid(gate_ref[0, 0])
    psum = p[0:tq]
    for hh in range(hpg):
        o_ref[:, hh * dh:(hh + 1) * dh] = (o[hh * tq:(hh + 1) * tq] * gate[:, hh:hh + 1]).astype(o_ref.dtype)
        if hh:
            psum = psum + p[hh * tq:(hh + 1) * tq]
    nsel = ovt_ref.shape[0]
    imp_t = _dot_nt(ovt_ref[...], psum.astype(BF16))
    jb = lax.broadcasted_iota(jnp.int32, (nsel, tq), 0)
    pos = s0 + lax.broadcasted_iota(jnp.int32, (nsel, tq), 1)
    cur = jnp.right_shift(pos, SEL_LEN.bit_length() - 1)
    causal = jb * SEL_LEN <= pos
    forced = (jb == 0) | (jb == cur) | (jb == cur - 1)
    score = jnp.where(forced, FORCE, jnp.where(causal, imp_t, NEG))
    per_tile = tq // SEL_LEN
    rank_ref[...] = jnp.zeros_like(rank_ref)
    for c in range(nsel // per_tile):
        @pl.when(c <= pl.program_id(2))
        def _():
            rank = rank_ref[...]
            for kk in range(c * per_tile, (c + 1) * per_tile):
                row = score[kk:kk + 1, :]
                beats = (row > score) | ((row == score) & (jb > kk))
                rank = rank + jnp.where(beats, 1, 0)
            rank_ref[...] = rank
    rank = rank_ref[...]
    selb = jnp.where((rank < SEL_TOPK) & causal, 0.0, NEG)
    selb = jnp.concatenate([selb, jnp.zeros((SEL_PAD - nsel, tq), F32)], axis=0)
    sel_ref[0, 0] = selb.astype(BF16)


def _cmp_attention(proj, k_cmp, v_cmp, rel_bias, gates, batch, seq):
    dh, g, hpg = NSA_DH, NSA_KV_GROUPS, NSA_HPG
    tq = CMP_TQ
    nq = seq // tq
    ncp = k_cmp.shape[2]
    nsel = seq // SEL_LEN
    n_cmp = (seq - CMP_LEN) // CMP_STRIDE + 1
    cmp_idx = np.arange(n_cmp)[:, None] * CMP_STRIDE + np.arange(CMP_LEN)[None, :]
    overlap = ((cmp_idx // SEL_LEN)[:, :, None] == np.arange(nsel)[None, None, :]).sum(1) / CMP_LEN
    ovt = np.zeros((nsel, ncp), np.float32)
    ovt[:, :n_cmp] = overlap.T
    nb = tq // CMP_STRIDE
    assert 2 * nb + 1 <= CMP_BAND and CMP_STRIDE * (nb + 1) - (CMP_LEN - 1) >= REL_MAX_DIST
    i = np.arange(tq)[:, None]
    r = np.arange(CMP_BAND)[None, :]
    rel = np.where(r < 2 * nb, i - CMP_STRIDE * (r - nb) - (CMP_LEN - 1), REL_MAX_DIST)
    tb = jnp.where(jnp.asarray(r <= 2 * nb), _bias_lookup(rel_bias, jnp.asarray(rel, jnp.int32)), 0.0)
    tb_hi = tb.astype(BF16)
    tb_lo = (tb - tb_hi.astype(F32)).astype(BF16)
    tb = jnp.concatenate([tb_hi, tb_lo], axis=-1)
    n = np.arange(ncp)[None, None, :]
    first = (np.arange(nq) * nb - nb)[:, None, None]
    rr = np.arange(CMP_BAND)[None, :, None]
    pp = np.where(rr < 2 * nb, n == first + rr, (rr == 2 * nb) & (n < first)).astype(np.float32)
    pp = np.concatenate([pp, pp], axis=1)
    qcol = IN_OFF[4] // (hpg * dh)
    return pl.pallas_call(
        _cmpattn_kernel,
        grid=(batch, g, nq),
        in_specs=[
            pl.BlockSpec((tq, hpg * dh), lambda b, gg, t: (b * nq + t, qcol + gg)),
            pl.BlockSpec((1, 1, ncp, dh), lambda b, gg, t: (b, gg, 0, 0)),
            pl.BlockSpec((1, 1, ncp, dh), lambda b, gg, t: (b, gg, 0, 0)),
            pl.BlockSpec((hpg, tq, 2 * CMP_BAND), lambda b, gg, t: (gg, 0, 0)),
            pl.BlockSpec((1, 2 * CMP_BAND, ncp), lambda b, gg, t: (t, 0, 0)),
            pl.BlockSpec((nsel, ncp), lambda b, gg, t: (0, 0)),
            pl.BlockSpec((1, 1, tq, GATE_ROWS), lambda b, gg, t: (b, gg, t, 0)),
        ],
        out_specs=[
            pl.BlockSpec((tq, hpg * dh), lambda b, gg, t: (b * nq + t, gg)),
            pl.BlockSpec((1, 1, SEL_PAD, tq), lambda b, gg, t: (b, gg, 0, t)),
        ],
        out_shape=[jax.ShapeDtypeStruct((batch * seq, g * hpg * dh), BF16),
                   jax.ShapeDtypeStruct((batch, g, SEL_PAD, seq), BF16)],
        scratch_shapes=[pltpu.VMEM((nsel, tq), jnp.int32)],
        compiler_params=pltpu.CompilerParams(
            dimension_semantics=("parallel", "parallel", "parallel"), vmem_limit_bytes=VMEM_LIMIT),
        name="cmp_attn",
    )(proj, k_cmp, v_cmp, tb, jnp.asarray(pp, BF16), jnp.asarray(ovt, BF16),
      gates[:, :, 0].transpose(0, 1, 3, 2))


def _band_kernel(qi_ref, ki_ref, var_ref, first_ref, last_ref, q_ref, k_ref, v_ref, fd_ref, gate_ref,
                 *rest, use_sel):
    if use_sel:
        selb_ref, e_ref, o_ref, qt_ref, m_ref, acc_ref, bias_ref = rest
    else:
        o_ref, qt_ref, m_ref, acc_ref, bias_ref = rest
    t = pl.program_id(1)
    dh, hpg, ng = NSA_DH, NSA_HPG, NSA_KV_GROUPS
    tq, tk = q_ref.shape[0], k_ref.shape[0]

    @pl.when((pl.program_id(0) == 0) & (t == 0))
    def _():
        upper = (lax.broadcasted_iota(jnp.int32, (tk, tq), 1) >= lax.broadcasted_iota(jnp.int32, (tk, tq), 0))
        for gg in range(ng):
            for hh in range(hpg):
                cols = slice(hh * tq, (hh + 1) * tq)
                by_dist = jnp.broadcast_to(fd_ref[gg, hh, 0, 0:1, :], (tk, tq))
                far = jnp.broadcast_to(fd_ref[gg, hh, 1, 0:1, :], (tk, tq))
                cyc = pltpu.roll(by_dist, 0, 1, stride=1, stride_axis=0)
                bias_ref[gg, 0, :, cols] = jnp.where(upper, cyc, NEG)
                if use_sel:
                    bias_ref[gg, 1, :, cols] = jnp.where(upper, far, cyc)
                    bias_ref[gg, 2, :, cols] = far
                else:
                    bias_ref[gg, 1, :, cols] = jnp.where(upper, NEG, cyc)

    @pl.when(first_ref[t] == 1)
    def _():
        m_ref[...] = jnp.full_like(m_ref, NEG)
        acc_ref[...] = jnp.zeros_like(acc_ref)
        scale = dh ** -0.5 * LOG2E
        for gg in range(ng):
            for hh in range(hpg):
                h = gg * hpg + hh
                q = q_ref[:, h * dh:(h + 1) * dh].astype(F32)
                qt_ref[gg, 0:dh, hh * tq:(hh + 1) * tq] = (q * scale).T.astype(BF16)
                if use_sel:
                    qt_ref[gg, dh:, hh * tq:(hh + 1) * tq] = selb_ref[0, gg]

    var = var_ref[t]
    ones = jnp.ones((ACC_PAD, tk), F32)
    scores = []
    for gg in range(ng):
        k = k_ref[:, gg * dh:(gg + 1) * dh]
        if use_sel:
            k = jnp.concatenate([k, e_ref[...]], axis=1)
        scores.append(_dot(k, qt_ref[gg]) + bias_ref[gg, var])
    for gg in range(ng):
        vt = jnp.concatenate([v_ref[:, gg * dh:(gg + 1) * dh].astype(F32).T, ones], axis=0).astype(BF16)
        s = scores[gg]
        m_prev = m_ref[gg]
        m_new = jnp.maximum(m_prev, jnp.max(s, axis=0, keepdims=True))
        alpha = jnp.exp2(m_prev - m_new)
        p = jnp.exp2(s - m_new).astype(BF16)
        acc_ref[gg] = alpha * acc_ref[gg] + _dot(vt, p)
        m_ref[gg] = m_new

    @pl.when(last_ref[t] == 1)
    def _():
        for gg in range(ng):
            gate = _sigmoid(gate_ref[0, gg, 0])
            for hh in range(hpg):
                h = gg * hpg + hh
                cols = slice(hh * tq, (hh + 1) * tq)
                o = acc_ref[gg, 0:dh, cols] * (gate[hh:hh + 1, :] / acc_ref[gg, dh:dh + 1, cols])
                o_ref[:, h * dh:(h + 1) * dh] = o.T.astype(o_ref.dtype)


def _band_steps(seq, window):
    tq, tk = ATT_T, ATT_TK
    steps = []
    for qi in range(seq // tq):
        k_hi = qi * tq // tk
        k_lo = max(0, (qi * tq - window + 1) // tk) if window else 0
        steps += [(qi, ki, min((qi * tq - ki * tk) // tq, ATT_NVAR - 1)) for ki in range(k_lo, k_hi + 1)]
    return steps


def _band_attention(proj, bias_by_dist, steps, koff, voff, gates, branch, batch, seq, selb=None, e_mat=None):
    dh, g, hpg = NSA_DH, NSA_KV_GROUPS, NSA_HPG
    tq, tk = ATT_T, ATT_TK
    nq, nk = seq // tq, seq // tk
    qi = np.array([s[0] for s in steps], np.int32)
    ki = np.array([s[1] for s in steps], np.int32)
    var = np.array([s[2] for s in steps], np.int32)
    first = np.concatenate([[1], (qi[1:] != qi[:-1]).astype(np.int32)]).astype(np.int32)
    last = np.concatenate([(qi[1:] != qi[:-1]).astype(np.int32), [1]]).astype(np.int32)
    qcol, kcol, vcol = IN_OFF[4] // NSA_W, koff // KV_W, voff // KV_W
    use_sel = selb is not None
    in_specs = [
        pl.BlockSpec((tq, NSA_W), lambda b, t, qi_r, ki_r, v_r, f_r, l_r: (b * nq + qi_r[t], qcol)),
        pl.BlockSpec((tk, KV_W), lambda b, t, qi_r, ki_r, v_r, f_r, l_r: (b * nk + ki_r[t], kcol)),
        pl.BlockSpec((tk, KV_W), lambda b, t, qi_r, ki_r, v_r, f_r, l_r: (b * nk + ki_r[t], vcol)),
        pl.BlockSpec(bias_by_dist.shape, lambda b, t, qi_r, ki_r, v_r, f_r, l_r: (0, 0, 0, 0, 0)),
        pl.BlockSpec((1, g, 1, GATE_ROWS, tq),
                     lambda b, t, qi_r, ki_r, v_r, f_r, l_r: (b, 0, branch, 0, qi_r[t])),
    ]
    args = [proj, proj, proj, bias_by_dist, gates]
    if use_sel:
        in_specs += [
            pl.BlockSpec((1, g, SEL_PAD, tq), lambda b, t, qi_r, ki_r, v_r, f_r, l_r: (b, 0, 0, qi_r[t])),
            pl.BlockSpec((tk, SEL_PAD), lambda b, t, qi_r, ki_r, v_r, f_r, l_r: (ki_r[t], 0)),
        ]
        args += [selb, e_mat]
    kdim = dh + SEL_PAD if use_sel else dh
    grid_spec = pltpu.PrefetchScalarGridSpec(
        num_scalar_prefetch=5,
        grid=(batch, len(steps)),
        in_specs=in_specs,
        out_specs=pl.BlockSpec((tq, NSA_W), lambda b, t, qi_r, ki_r, v_r, f_r, l_r: (b * nq + qi_r[t], 0)),
        scratch_shapes=[pltpu.VMEM((g, kdim, hpg * tq), BF16), pltpu.VMEM((g, 1, hpg * tq), F32),
                        pltpu.VMEM((g, dh + ACC_PAD, hpg * tq), F32),
                        pltpu.VMEM((g, ATT_NVAR if use_sel else ATT_NVAR - 1, tk, hpg * tq), F32)],
    )
    return pl.pallas_call(
        functools.partial(_band_kernel, use_sel=use_sel),
        grid_spec=grid_spec,
        out_shape=jax.ShapeDtypeStruct((batch * seq, NSA_W), BF16),
        compiler_params=pltpu.CompilerParams(
            dimension_semantics=("arbitrary", "arbitrary"), vmem_limit_bytes=VMEM_LIMIT),
        name="sel_attn" if use_sel else "win_attn",
    )(jnp.asarray(qi), jnp.asarray(ki), jnp.asarray(var), jnp.asarray(first), jnp.asarray(last), *args)


def _out_kernel(x_ref, yr_ref, oc_ref, os_ref, ow_ref, w_ref, o_ref):
    y_nsa = oc_ref[...].astype(F32) + os_ref[...].astype(F32) + ow_ref[...].astype(F32)
    y = jnp.concatenate([yr_ref[...], y_nsa.astype(BF16)], axis=-1)
    o_ref[...] = x_ref[...] + _dot(y, w_ref[...])


def _out_proj(x2d, y_ret, o_cmp, o_sel, o_win, w_out):
    tok, d = x2d.shape
    tm = OUT_TM
    wmix = y_ret.shape[1]
    row = lambda shape: pl.BlockSpec(shape, lambda i: (i, 0))
    return pl.pallas_call(
        _out_kernel,
        grid=(tok // tm,),
        in_specs=[row((tm, d)), row((tm, wmix)), row((tm, wmix)), row((tm, wmix)), row((tm, wmix)),
                  pl.BlockSpec(w_out.shape, lambda i: (0, 0))],
        out_specs=row((tm, d)),
        out_shape=jax.ShapeDtypeStruct((tok, d), F32),
        compiler_params=pltpu.CompilerParams(
            dimension_semantics=("parallel",), vmem_limit_bytes=VMEM_LIMIT),
        name="out_proj",
    )(x2d, y_ret, o_cmp, o_sel, o_win, w_out)


def _t5_bucket_of(rel):
    n = jnp.maximum(rel, 0)
    max_exact = REL_BUCKETS // 2
    nf = jnp.maximum(n, 1).astype(F32)
    large = max_exact + (jnp.log(nf / max_exact) / math.log(REL_MAX_DIST / max_exact)
                         * (REL_BUCKETS - max_exact)).astype(jnp.int32)
    large = jnp.minimum(large, REL_BUCKETS - 1)
    return jnp.where(n < max_exact, n, large)


def _bias_lookup(rel_bias, rel):
    bucket = _t5_bucket_of(rel)[None]
    tab = rel_bias.astype(F32).reshape((rel_bias.shape[0], REL_BUCKETS) + (1,) * rel.ndim)
    out = jnp.zeros((rel_bias.shape[0],) + rel.shape, F32)
    for b in range(REL_BUCKETS):
        out = jnp.where(bucket == b, tab[:, b], out)
    return out


def _bias_by_distance(rel_bias):
    t, g, hpg = ATT_T, NSA_KV_GROUPS, NSA_HPG
    assert ATT_TK == t and t >= REL_MAX_DIST and WIN == t
    by_dist = _bias_lookup(rel_bias, jnp.arange(t, dtype=jnp.int32)) * LOG2E
    far = jnp.broadcast_to(by_dist[:, t - 1:t], by_dist.shape)
    tab = jnp.stack([by_dist, far], axis=1).reshape(g, hpg, 2, 1, t)
    return jnp.broadcast_to(tab, (g, hpg, 2, 8, t))


def _token_mix(x2d, mix_norm, w_in, ret_gn_gain, pe_k, w1_k, w2_k, pe_v, w1_v, w2_v, w_out, rel_bias,
               batch, seq):
    d = x2d.shape[1]
    dh = NSA_DH
    w_pad = _to_bf16_transposed(w_in.T, D_IN_PAD)
    proj, cmp_kv = _inproj(x2d, mix_norm, w_pad)
    y_ret = _retention(proj, ret_gn_gain, batch, seq)

    gates = proj[:, IN_OFF[11]:IN_OFF[11] + IN_SPLITS[11]].astype(F32)
    gates = gates.reshape(batch, seq, NSA_KV_GROUPS, NSA_HPG, N_BRANCH).transpose(0, 2, 4, 3, 1)
    gates = jnp.pad(gates, ((0, 0), (0, 0), (0, 0), (0, GATE_ROWS - NSA_HPG), (0, 0)))

    k_cmp, v_cmp = _compress(cmp_kv, pe_k, w1_k, w2_k, pe_v, w1_v, w2_v, batch, seq)
    o_cmp, selb = _cmp_attention(proj, k_cmp, v_cmp, rel_bias, gates, batch, seq)

    bias_by_dist = _bias_by_distance(rel_bias)
    e_np = np.zeros((seq, SEL_PAD), np.float32)
    e_np[np.arange(seq), np.arange(seq) // SEL_LEN] = 1.0
    o_sel = _band_attention(proj, bias_by_dist, _band_steps(seq, 0),
                            IN_OFF[7], IN_OFF[8], gates, 1, batch, seq,
                            selb=selb, e_mat=jnp.asarray(e_np, BF16))
    o_win = _band_attention(proj, bias_by_dist, _band_steps(seq, WIN),
                            IN_OFF[9], IN_OFF[10], gates, 2, batch, seq)
    return _out_proj(x2d, y_ret, o_cmp, o_sel, o_win, _to_bf16(w_out))


def _cast_t_kernel(x_ref, o_ref, *, valid_rows):
    tr = x_ref.shape[0]
    row = pl.program_id(0) * tr + lax.broadcasted_iota(jnp.int32, x_ref.shape, 0)
    o_ref[...] = jnp.where(row < valid_rows, x_ref[...], 0.0).T.astype(o_ref.dtype)


def _to_bf16_transposed(w_t, cols):
    c, r = w_t.shape
    tr = CAST_T_ROWS
    return pl.pallas_call(
        functools.partial(_cast_t_kernel, valid_rows=c),
        grid=(cols // tr,),
        in_specs=[pl.BlockSpec((tr, r), lambda i: (i, 0))],
        out_specs=pl.BlockSpec((r, tr), lambda i: (0, i)),
        out_shape=jax.ShapeDtypeStruct((r, cols), BF16),
        compiler_params=pltpu.CompilerParams(
            dimension_semantics=("parallel",), vmem_limit_bytes=VMEM_LIMIT),
        name="to_bf16_t",
    )(w_t)


def _cast_kernel(x_ref, o_ref):
    o_ref[...] = x_ref[...].astype(o_ref.dtype)


def _to_bf16(w):
    r, c = w.shape
    tr = 1 << ((CAST_BLOCK_BYTES // (4 * c)).bit_length() - 1)
    while r % tr:
        tr //= 2
    return pl.pallas_call(
        _cast_kernel,
        grid=(r // tr,),
        in_specs=[pl.BlockSpec((tr, c), lambda i: (i, 0))],
        out_specs=pl.BlockSpec((tr, c), lambda i: (i, 0)),
        out_shape=jax.ShapeDtypeStruct((r, c), BF16),
        compiler_params=pltpu.CompilerParams(
            dimension_semantics=("parallel",), vmem_limit_bytes=VMEM_LIMIT),
        name="to_bf16",
    )(w)


def kernel(x, ffn1_norm, ffn1_w1, ffn1_w3, ffn1_w2, mix_norm, w_in, ret_gn_gain, cmp_pe_k, cmp_w1_k,
           cmp_w2_k, cmp_pe_v, cmp_w1_v, cmp_w2_v, w_out, ffn2_norm, ffn2_w1, ffn2_w3, ffn2_w2,
           rel_bias, final_norm):
    batch, seq, d = x.shape
    depth = ffn1_norm.shape[0]
    h = x.reshape(batch * seq, d)
    for l in range(depth):
        last = l == depth - 1
        h = _ffn(h, ffn1_norm[l], _to_bf16(ffn1_w1[l]), _to_bf16(ffn1_w3[l]), _to_bf16(ffn1_w2[l]),
                 final_norm, False)
        h = _token_mix(h, mix_norm[l], w_in[l], ret_gn_gain[l], cmp_pe_k[l], cmp_w1_k[l], cmp_w2_k[l],
                       cmp_pe_v[l], cmp_w1_v[l], cmp_w2_v[l], w_out[l], rel_bias, batch, seq)
        h = _ffn(h, ffn2_norm[l], _to_bf16(ffn2_w1[l]), _to_bf16(ffn2_w3[l]), _to_bf16(ffn2_w2[l]),
                 final_norm, last)
    if depth == 0:
        raise ValueError("depth must be positive")
    return h.reshape(batch, seq, d)
```

```python
import functools
import math

import jax
import jax.numpy as jnp
import numpy as np
from jax import lax
from jax.experimental import pallas as pl
from jax.experimental.pallas import tpu as pltpu

F32 = jnp.float32
BF16 = jnp.bfloat16

RET_HEADS = 4
RET_DK = 256
RET_DV = 256
ROPE_BASE = 10000.0
NSA_HEADS = 8
NSA_KV_GROUPS = 2
NSA_HPG = NSA_HEADS // NSA_KV_GROUPS
NSA_DH = 128
CMP_LEN = 32
CMP_STRIDE = 16
SEL_LEN = 64
SEL_TOPK = 16
WIN = 512
N_BRANCH = 3
REL_BUCKETS = 32
REL_MAX_DIST = 128
EPS = 1e-6
NEG = -1e30
FORCE = 1e4

RET_W = RET_HEADS * RET_DV
NSA_W = NSA_HEADS * NSA_DH
KV_W = NSA_KV_GROUPS * NSA_DH
IN_SPLITS = [RET_HEADS * RET_DK, RET_HEADS * RET_DK, RET_W, RET_W, NSA_W,
             KV_W, KV_W, KV_W, KV_W, KV_W, KV_W, NSA_HEADS * N_BRANCH]
D_IN = sum(IN_SPLITS)
IN_OFF = [sum(IN_SPLITS[:i]) for i in range(len(IN_SPLITS))]

LANE = 128
IN_TN = 2304
D_IN_PAD = 3 * IN_TN
CMP_KV_TILE = IN_OFF[5] // IN_TN
CMP_KV_LOCAL = IN_OFF[5] % IN_TN
assert CMP_KV_TILE > 0 and CMP_KV_LOCAL + 2 * KV_W <= IN_TN and IN_OFF[6] == IN_OFF[5] + KV_W
GATE_ROWS = 8
FFN_TM = 1024
FFN_TF = 512
IN_TM = 512
RET_C = 256
CMP_TQ = 512
ATT_T = 512
ATT_TK = 512
ATT_NVAR = 3
OUT_TM = 512
SEL_PAD = 128
ACC_PAD = 8
CMP_BAND = 128
VMEM_LIMIT = 56 * 1024 * 1024
CAST_BLOCK_BYTES = 6 * 1024 * 1024
CAST_T_ROWS = 256
LOG2E = math.log2(math.e)


def _dot(a, b):
    return jnp.dot(a, b, preferred_element_type=F32)


def _dot_nt(a, b):
    return lax.dot_general(a, b, (((1,), (1,)), ((), ())), preferred_element_type=F32)


def _dot_tn(a, b):
    return lax.dot_general(a, b, (((0,), (0,)), ((), ())), preferred_element_type=F32)


def _sigmoid(x):
    return 1.0 / (1.0 + jnp.exp(-x))


def _rms(x, g):
    ms = jnp.mean(x * x, axis=-1, keepdims=True)
    return x * lax.rsqrt(ms + EPS) * g


def _ffn_kernel(x_ref, g_ref, w1_ref, w3_ref, w2_ref, fg_ref, o_ref, n_ref, *, final_norm):
    j = pl.program_id(1)

    def step(first):
        if first:
            n = _rms(x_ref[...], g_ref[...]).astype(BF16)
            n_ref[...] = n
        else:
            n = n_ref[...]
        a = _dot(n, w1_ref[...])
        b = _dot(n, w3_ref[...])
        h = (0.5 * a * _sigmoid(a) * b).astype(BF16)
        upd = _dot(h, w2_ref[...])
        if first:
            o_ref[...] = x_ref[...] + upd
        else:
            o_ref[...] += upd

    pl.when(j == 0)(functools.partial(step, True))
    pl.when(j > 0)(functools.partial(step, False))

    if final_norm:
        @pl.when(j == pl.num_programs(1) - 1)
        def _():
            o_ref[...] = _rms(o_ref[...], fg_ref[...])


def _ffn(x2d, g, w1, w3, w2, fg, final_norm):
    tok, d = x2d.shape
    dff = w1.shape[1]
    tm, tf = FFN_TM, FFN_TF
    return pl.pallas_call(
        functools.partial(_ffn_kernel, final_norm=final_norm),
        grid=(tok // tm, dff // tf),
        in_specs=[
            pl.BlockSpec((tm, d), lambda i, j: (i, 0)),
            pl.BlockSpec((1, d), lambda i, j: (0, 0)),
            pl.BlockSpec((d, tf), lambda i, j: (0, j)),
            pl.BlockSpec((d, tf), lambda i, j: (0, j)),
            pl.BlockSpec((tf, d), lambda i, j: (j, 0)),
            pl.BlockSpec((1, d), lambda i, j: (0, 0)),
        ],
        out_specs=pl.BlockSpec((tm, d), lambda i, j: (i, 0)),
        out_shape=jax.ShapeDtypeStruct((tok, d), F32),
        scratch_shapes=[pltpu.VMEM((tm, d), BF16)],
        compiler_params=pltpu.CompilerParams(
            dimension_semantics=("parallel", "arbitrary"), vmem_limit_bytes=VMEM_LIMIT),
        name="ffn",
    )(x2d, g.reshape(1, d), w1, w3, w2, fg.reshape(1, d))


def _inproj_kernel(x_ref, g_ref, w_ref, o_ref, kv_ref, n_ref):
    j = pl.program_id(1)

    def step(first):
        if first:
            n = _rms(x_ref[...], g_ref[...]).astype(BF16)
            n_ref[...] = n
        else:
            n = n_ref[...]
        res = _dot(n, w_ref[...])
        o_ref[...] = res.astype(o_ref.dtype)
        return res

    @pl.when(j == 0)
    def _():
        step(True)

    @pl.when(j > 0)
    def _():
        res = step(False)

        @pl.when(j == CMP_KV_TILE)
        def _():
            kv_ref[...] = res[:, CMP_KV_LOCAL:CMP_KV_LOCAL + 2 * KV_W]


def _inproj(x2d, g, w_pad):
    tok, d = x2d.shape
    n_out = w_pad.shape[1]
    tm, tn = IN_TM, IN_TN
    return pl.pallas_call(
        _inproj_kernel,
        grid=(tok // tm, n_out // tn),
        in_specs=[
            pl.BlockSpec((tm, d), lambda i, j: (i, 0)),
            pl.BlockSpec((1, d), lambda i, j: (0, 0)),
            pl.BlockSpec((d, tn), lambda i, j: (0, j)),
        ],
        out_specs=[pl.BlockSpec((tm, tn), lambda i, j: (i, j)),
                   pl.BlockSpec((tm, 2 * KV_W), lambda i, j: (i, 0))],
        out_shape=[jax.ShapeDtypeStruct((tok, n_out), BF16),
                   jax.ShapeDtypeStruct((tok, 2 * KV_W), F32)],
        scratch_shapes=[pltpu.VMEM((tm, d), BF16)],
        compiler_params=pltpu.CompilerParams(
            dimension_semantics=("parallel", "arbitrary"), vmem_limit_bytes=VMEM_LIMIT),
        name="in_proj",
    )(x2d, g.reshape(1, d), w_pad)


def _ret_kernel(cd_ref, q_ref, k_ref, v_ref, g_ref, cos_ref, sin_ref, dec_ref, xi_ref, zeta_ref,
                gain_ref, o_ref, state_ref):
    @pl.when(pl.program_id(1) == 0)
    def _():
        state_ref[...] = jnp.zeros_like(state_ref)

    cos = cos_ref[...]
    sin = sin_ref[...]
    half = RET_DK // 2

    def rot(t):
        t1, t2 = t[:, :half], t[:, half:]
        return jnp.concatenate([t1 * cos - t2 * sin, t1 * sin + t2 * cos], axis=-1)

    for h in range(RET_HEADS):
        cs = slice(h * RET_DK, (h + 1) * RET_DK)
        q = rot(q_ref[:, cs].astype(F32))
        k = rot(k_ref[:, cs].astype(F32)) * (RET_DK ** -0.5)
        v = v_ref[:, cs].astype(BF16)
        qb = q.astype(BF16)
        s = _dot_nt(qb, k.astype(BF16)) * dec_ref[h]
        state = state_ref[h]
        y = _dot(s.astype(BF16), v) + _dot((q * xi_ref[h]).astype(BF16), state.astype(BF16))
        state_ref[h] = cd_ref[h] * state + _dot_tn((k * zeta_ref[h]).astype(BF16), v)
        mu = jnp.mean(y, axis=-1, keepdims=True)
        yc = y - mu
        var = jnp.mean(yc * yc, axis=-1, keepdims=True)
        yn = yc * lax.rsqrt(var + EPS) * gain_ref[:, cs]
        gate = g_ref[:, cs].astype(F32)
        o_ref[:, cs] = (gate * _sigmoid(gate) * yn).astype(o_ref.dtype)


def _retention(proj, gain, batch, seq):
    tok = proj.shape[0]
    c = RET_C
    n_chunks = seq // c
    h, dk = RET_HEADS, RET_DK
    half = dk // 2
    inv = ROPE_BASE ** (-jnp.arange(half, dtype=F32) / half)
    ang = jnp.arange(seq, dtype=F32)[:, None] * inv[None, :]
    cos, sin = jnp.cos(ang), jnp.sin(ang)
    log_g = jnp.log(1.0 - 2.0 ** (-5.0 - jnp.arange(h, dtype=F32)))
    idx = jnp.arange(c, dtype=F32)
    diff = idx[:, None] - idx[None, :]
    dec = jnp.where(diff >= 0.0, jnp.exp(jnp.maximum(diff, 0.0)[None] * log_g[:, None, None]), 0.0)
    xi = jnp.broadcast_to(jnp.exp((idx + 1.0)[None] * log_g[:, None])[:, :, None], (h, c, dk))
    zeta = jnp.broadcast_to(jnp.exp((c - 1.0 - idx)[None] * log_g[:, None])[:, :, None], (h, c, dk))
    cd = jnp.exp(c * log_g)

    w = h * dk
    row = lambda b, n: (b * n_chunks + n)
    return pl.pallas_call(
        _ret_kernel,
        grid=(batch, n_chunks),
        in_specs=[
            pl.BlockSpec(memory_space=pltpu.SMEM),
            pl.BlockSpec((c, w), lambda b, n: (row(b, n), 0)),
            pl.BlockSpec((c, w), lambda b, n: (row(b, n), 1)),
            pl.BlockSpec((c, w), lambda b, n: (row(b, n), 2)),
            pl.BlockSpec((c, w), lambda b, n: (row(b, n), 3)),
            pl.BlockSpec((c, half), lambda b, n: (n, 0)),
            pl.BlockSpec((c, half), lambda b, n: (n, 0)),
            pl.BlockSpec((h, c, c), lambda b, n: (0, 0, 0)),
            pl.BlockSpec((h, c, dk), lambda b, n: (0, 0, 0)),
            pl.BlockSpec((h, c, dk), lambda b, n: (0, 0, 0)),
            pl.BlockSpec((1, w), lambda b, n: (0, 0)),
        ],
        out_specs=pl.BlockSpec((c, w), lambda b, n: (row(b, n), 0)),
        out_shape=jax.ShapeDtypeStruct((tok, w), BF16),
        scratch_shapes=[pltpu.VMEM((h, dk, RET_DV), F32)],
        compiler_params=pltpu.CompilerParams(
            dimension_semantics=("parallel", "arbitrary"), vmem_limit_bytes=VMEM_LIMIT),
        name="retention",
    )(cd, proj, proj, proj, proj, cos, sin, dec, xi, zeta, gain.reshape(1, w))


def _compress_kernel(k_ref, v_ref, pek_ref, w1k_ref, w1kf_ref, w2k_ref, pev_ref, w1v_ref, w1vf_ref,
                     w2v_ref, ok_ref, ov_ref):
    dh = NSA_DH
    nblk = k_ref.shape[0] // CMP_STRIDE

    def one(x_ref, pe_ref, w1_ref, w1f_ref, w2_ref, o_ref):
        ab = jnp.zeros((nblk, 2 * dh), F32)
        for l in range(CMP_STRIDE):
            rows = x_ref[pl.ds(l, nblk, stride=CMP_STRIDE), :].astype(BF16)
            ab = ab + _dot(rows, w1_ref[l])
        a, b = ab[:, :dh], ab[:, dh:]
        b_next = pltpu.roll(b, nblk - 1, axis=0)
        ridx = lax.broadcasted_iota(jnp.int32, (nblk, dh), 0)
        b_next = jnp.where(ridx < nblk - 1, b_next, 0.0)
        pe_term = _dot(pe_ref[...], w1f_ref[...])[0:1, :]
        hdn = a + b_next + pe_term
        o_ref[0, 0] = _dot((hdn * _sigmoid(hdn)).astype(BF16), w2_ref[...])

    one(k_ref, pek_ref, w1k_ref, w1kf_ref, w2k_ref, ok_ref)
    one(v_ref, pev_ref, w1v_ref, w1vf_ref, w2v_ref, ov_ref)


def _compress(kv, pe_k, w1_k, w2_k, pe_v, w1_v, w2_v, batch, seq):
    dh, g = NSA_DH, NSA_KV_GROUPS
    nblk = seq // CMP_STRIDE
    half = CMP_LEN // 2

    def prep(pe, w1, w2):
        w1b = w1.astype(BF16)
        w1_pair = jnp.concatenate([w1b[:half], w1b[half:]], axis=-1)
        pe_flat = jnp.broadcast_to(pe.reshape(1, CMP_LEN * dh), (8, CMP_LEN * dh)).astype(BF16)
        return pe_flat, w1_pair, w1b.reshape(CMP_LEN * dh, dh), w2.astype(BF16)

    args_k = prep(pe_k, w1_k, w2_k)
    args_v = prep(pe_v, w1_v, w2_v)
    kcol = 0
    vcol = KV_W // dh
    const = lambda shape: pl.BlockSpec(shape, lambda b, gg: (0,) * len(shape))
    wspecs = [const((8, CMP_LEN * dh)), const((half, dh, 2 * dh)), const((CMP_LEN * dh, dh)),
              const((dh, dh))]
    out_spec = pl.BlockSpec((1, 1, nblk, dh), lambda b, gg: (b, gg, 0, 0))
    return pl.pallas_call(
        _compress_kernel,
        grid=(batch, g),
        in_specs=[pl.BlockSpec((seq, dh), lambda b, gg: (b, kcol + gg)),
                  pl.BlockSpec((seq, dh), lambda b, gg: (b, vcol + gg))] + wspecs + wspecs,
        out_specs=[out_spec, out_spec],
        out_shape=[jax.ShapeDtypeStruct((batch, g, nblk, dh), F32)] * 2,
        compiler_params=pltpu.CompilerParams(
            dimension_semantics=("parallel", "parallel"), vmem_limit_bytes=VMEM_LIMIT),
        name="compress",
    )(kv, kv, *args_k, *args_v)


def _cmpattn_kernel(q_ref, kc_ref, vc_ref, tb_ref, pp_ref, ovt_ref, gate_ref, o_ref, sel_ref, rank_ref):
    tq = q_ref.shape[0]
    ncp = kc_ref.shape[2]
    dh = NSA_DH
    s0 = pl.program_id(2) * tq
    hpg = NSA_HPG
    pos_r = s0 + (lax.broadcasted_iota(jnp.int32, (hpg * tq, ncp), 0) & (tq - 1))
    n_c = lax.broadcasted_iota(jnp.int32, (hpg * tq, ncp), 1)
    mask = pos_r >= n_c * CMP_STRIDE + (CMP_LEN - 1)
    kc = kc_ref[0, 0].astype(BF16)
    vc = vc_ref[0, 0].astype(BF16)
    scale = dh ** -0.5
    q = jnp.concatenate([q_ref[:, hh * dh:(hh + 1) * dh] for hh in range(hpg)], axis=0).astype(BF16)
    bias = _dot(tb_ref[...].reshape(hpg * tq, tb_ref.shape[2]), pp_ref[0])
    lm = jnp.where(mask, _dot_nt(q, kc) * scale + bias, NEG)
    m = jnp.max(lm, axis=-1, keepdims=True)
    e = jnp.where(mask, jnp.exp(lm - m), 0.0)
    den = jnp.sum(e, axis=-1, keepdims=True)
    p = e / jnp.where(den > 0.0, den, 1.0)
    o = _dot(p.astype(BF16), vc)
    gate = _sigmoid(gate_ref[0, 0])
    psum = p[0:tq]
    for hh in range(hpg):
        o_ref[:, hh * dh:(hh + 1) * dh] = (o[hh * tq:(hh + 1) * tq] * gate[:, hh:hh + 1]).astype(o_ref.dtype)
        if hh:
            psum = psum + p[hh * tq:(hh + 1) * tq]
    nsel = ovt_ref.shape[0]
    imp_t = _dot_nt(ovt_ref[...], psum.astype(BF16))
    jb = lax.broadcasted_iota(jnp.int32, (nsel, tq), 0)
    pos = s0 + lax.broadcasted_iota(jnp.int32, (nsel, tq), 1)
    cur = jnp.right_shift(pos, SEL_LEN.bit_length() - 1)
    causal = jb * SEL_LEN <= pos
    forced = (jb == 0) | (jb == cur) | (jb == cur - 1)
    score = jnp.where(forced, FORCE, jnp.where(causal, imp_t, NEG))
    per_tile = tq // SEL_LEN
    rank_ref[...] = jnp.zeros_like(rank_ref)
    for c in range(nsel // per_tile):
        @pl.when(c <= pl.program_id(2))
        def _():
            rank = rank_ref[...]
            for kk in range(c * per_tile, (c + 1) * per_tile):
                row = score[kk:kk + 1, :]
                beats = (row > score) | ((row == score) & (jb > kk))
                rank = rank + jnp.where(beats, 1, 0)
            rank_ref[...] = rank
    rank = rank_ref[...]
    selb = jnp.where((rank < SEL_TOPK) & causal, 0.0, NEG)
    selb = jnp.concatenate([selb, jnp.zeros((SEL_PAD - nsel, tq), F32)], axis=0)
    sel_ref[0, 0] = selb.astype(BF16)


def _cmp_attention(proj, k_cmp, v_cmp, rel_bias, gates, batch, seq):
    dh, g, hpg = NSA_DH, NSA_KV_GROUPS, NSA_HPG
    tq = CMP_TQ
    nq = seq // tq
    ncp = k_cmp.shape[2]
    nsel = seq // SEL_LEN
    n_cmp = (seq - CMP_LEN) // CMP_STRIDE + 1
    cmp_idx = np.arange(n_cmp)[:, None] * CMP_STRIDE + np.arange(CMP_LEN)[None, :]
    overlap = ((cmp_idx // SEL_LEN)[:, :, None] == np.arange(nsel)[None, None, :]).sum(1) / CMP_LEN
    ovt = np.zeros((nsel, ncp), np.float32)
    ovt[:, :n_cmp] = overlap.T
    nb = tq // CMP_STRIDE
    assert 2 * nb + 1 <= CMP_BAND and CMP_STRIDE * (nb + 1) - (CMP_LEN - 1) >= REL_MAX_DIST
    i = np.arange(tq)[:, None]
    r = np.arange(CMP_BAND)[None, :]
    rel = np.where(r < 2 * nb, i - CMP_STRIDE * (r - nb) - (CMP_LEN - 1), REL_MAX_DIST)
    tb = jnp.where(jnp.asarray(r <= 2 * nb), _bias_lookup(rel_bias, jnp.asarray(rel, jnp.int32)), 0.0)
    tb_hi = tb.astype(BF16)
    tb_lo = (tb - tb_hi.astype(F32)).astype(BF16)
    tb = jnp.concatenate([tb_hi, tb_lo], axis=-1)
    n = np.arange(ncp)[None, None, :]
    first = (np.arange(nq) * nb - nb)[:, None, None]
    rr = np.arange(CMP_BAND)[None, :, None]
    pp = np.where(rr < 2 * nb, n == first + rr, (rr == 2 * nb) & (n < first)).astype(np.float32)
    pp = np.concatenate([pp, pp], axis=1)
    qcol = IN_OFF[4] // (hpg * dh)
    return pl.pallas_call(
        _cmpattn_kernel,
        grid=(batch, g, nq),
        in_specs=[
            pl.BlockSpec((tq, hpg * dh), lambda b, gg, t: (b * nq + t, qcol + gg)),
            pl.BlockSpec((1, 1, ncp, dh), lambda b, gg, t: (b, gg, 0, 0)),
            pl.BlockSpec((1, 1, ncp, dh), lambda b, gg, t: (b, gg, 0, 0)),
            pl.BlockSpec((hpg, tq, 2 * CMP_BAND), lambda b, gg, t: (gg, 0, 0)),
            pl.BlockSpec((1, 2 * CMP_BAND, ncp), lambda b, gg, t: (t, 0, 0)),
            pl.BlockSpec((nsel, ncp), lambda b, gg, t: (0, 0)),
            pl.BlockSpec((1, 1, tq, GATE_ROWS), lambda b, gg, t: (b, gg, t, 0)),
        ],
        out_specs=[
            pl.BlockSpec((tq, hpg * dh), lambda b, gg, t: (b * nq + t, gg)),
            pl.BlockSpec((1, 1, SEL_PAD, tq), lambda b, gg, t: (b, gg, 0, t)),
        ],
        out_shape=[jax.ShapeDtypeStruct((batch * seq, g * hpg * dh), BF16),
                   jax.ShapeDtypeStruct((batch, g, SEL_PAD, seq), BF16)],
        scratch_shapes=[pltpu.VMEM((nsel, tq), jnp.int32)],
        compiler_params=pltpu.CompilerParams(
            dimension_semantics=("parallel", "parallel", "parallel"), vmem_limit_bytes=VMEM_LIMIT),
        name="cmp_attn",
    )(proj, k_cmp, v_cmp, tb, jnp.asarray(pp, BF16), jnp.asarray(ovt, BF16),
      gates[:, :, 0].transpose(0, 1, 3, 2))


def _band_kernel(qi_ref, ki_ref, var_ref, first_ref, last_ref, q_ref, k_ref, v_ref, fd_ref, gate_ref,
                 *rest, use_sel):
    if use_sel:
        selb_ref, e_ref, o_ref, qt_ref, m_ref, acc_ref, bias_ref = rest
    else:
        o_ref, qt_ref, m_ref, acc_ref, bias_ref = rest
    t = pl.program_id(1)
    dh, hpg, ng = NSA_DH, NSA_HPG, NSA_KV_GROUPS
    tq, tk = q_ref.shape[0], k_ref.shape[0]

    @pl.when((pl.program_id(0) == 0) & (t == 0))
    def _():
        upper = (lax.broadcasted_iota(jnp.int32, (tk, tq), 1) >= lax.broadcasted_iota(jnp.int32, (tk, tq), 0))
        for gg in range(ng):
            for hh in range(hpg):
                cols = slice(hh * tq, (hh + 1) * tq)
                by_dist = jnp.broadcast_to(fd_ref[gg, hh, 0, 0:1, :], (tk, tq))
                far = jnp.broadcast_to(fd_ref[gg, hh, 1, 0:1, :], (tk, tq))
                cyc = pltpu.roll(by_dist, 0, 1, stride=1, stride_axis=0)
                bias_ref[gg, 0, :, cols] = jnp.where(upper, cyc, NEG)
                if use_sel:
                    bias_ref[gg, 1, :, cols] = jnp.where(upper, far, cyc)
                    bias_ref[gg, 2, :, cols] = far
                else:
                    bias_ref[gg, 1, :, cols] = jnp.where(upper, NEG, cyc)

    @pl.when(first_ref[t] == 1)
    def _():
        m_ref[...] = jnp.full_like(m_ref, NEG)
        acc_ref[...] = jnp.zeros_like(acc_ref)
        scale = dh ** -0.5 * LOG2E
        for gg in range(ng):
            for hh in range(hpg):
                h = gg * hpg + hh
                q = q_ref[:, h * dh:(h + 1) * dh].astype(F32)
                qt_ref[gg, 0:dh, hh * tq:(hh + 1) * tq] = (q * scale).T.astype(BF16)
                if use_sel:
                    qt_ref[gg, dh:, hh * tq:(hh + 1) * tq] = selb_ref[0, gg]

    var = var_ref[t]
    ones = jnp.ones((ACC_PAD, tk), F32)
    scores = []
    for gg in range(ng):
        k = k_ref[:, gg * dh:(gg + 1) * dh]
        if use_sel:
            k = jnp.concatenate([k, e_ref[...]], axis=1)
        scores.append(_dot(k, qt_ref[gg]) + bias_ref[gg, var])
    for gg in range(ng):
        vt = jnp.concatenate([v_ref[:, gg * dh:(gg + 1) * dh].astype(F32).T, ones], axis=0).astype(BF16)
        s = scores[gg]
        m_prev = m_ref[gg]
        m_new = jnp.maximum(m_prev, jnp.max(s, axis=0, keepdims=True))
        alpha = jnp.exp2(m_prev - m_new)
        p = jnp.exp2(s - m_new).astype(BF16)
        acc_ref[gg] = alpha * acc_ref[gg] + _dot(vt, p)
        m_ref[gg] = m_new

    @pl.when(last_ref[t] == 1)
    def _():
        for gg in range(ng):
            gate = _sigmoid(gate_ref[0, gg, 0])
            for hh in range(hpg):
                h = gg * hpg + hh
                cols = slice(hh * tq, (hh + 1) * tq)
                o = acc_ref[gg, 0:dh, cols] * (gate[hh:hh + 1, :] / acc_ref[gg, dh:dh + 1, cols])
                o_ref[:, h * dh:(h + 1) * dh] = o.T.astype(o_ref.dtype)


def _band_steps(seq, window):
    tq, tk = ATT_T, ATT_TK
    steps = []
    for qi in range(seq // tq):
        k_hi = qi * tq // tk
        k_lo = max(0, (qi * tq - window + 1) // tk) if window else 0
        steps += [(qi, ki, min((qi * tq - ki * tk) // tq, ATT_NVAR - 1)) for ki in range(k_lo, k_hi + 1)]
    return steps


def _band_attention(proj, bias_by_dist, steps, koff, voff, gates, branch, batch, seq, selb=None, e_mat=None):
    dh, g, hpg = NSA_DH, NSA_KV_GROUPS, NSA_HPG
    tq, tk = ATT_T, ATT_TK
    nq, nk = seq // tq, seq // tk
    qi = np.array([s[0] for s in steps], np.int32)
    ki = np.array([s[1] for s in steps], np.int32)
    var = np.array([s[2] for s in steps], np.int32)
    first = np.concatenate([[1], (qi[1:] != qi[:-1]).astype(np.int32)]).astype(np.int32)
    last = np.concatenate([(qi[1:] != qi[:-1]).astype(np.int32), [1]]).astype(np.int32)
    qcol, kcol, vcol = IN_OFF[4] // NSA_W, koff // KV_W, voff // KV_W
    use_sel = selb is not None
    in_specs = [
        pl.BlockSpec((tq, NSA_W), lambda b, t, qi_r, ki_r, v_r, f_r, l_r: (b * nq + qi_r[t], qcol)),
        pl.BlockSpec((tk, KV_W), lambda b, t, qi_r, ki_r, v_r, f_r, l_r: (b * nk + ki_r[t], kcol)),
        pl.BlockSpec((tk, KV_W), lambda b, t, qi_r, ki_r, v_r, f_r, l_r: (b * nk + ki_r[t], vcol)),
        pl.BlockSpec(bias_by_dist.shape, lambda b, t, qi_r, ki_r, v_r, f_r, l_r: (0, 0, 0, 0, 0)),
        pl.BlockSpec((1, g, 1, GATE_ROWS, tq),
                     lambda b, t, qi_r, ki_r, v_r, f_r, l_r: (b, 0, branch, 0, qi_r[t])),
    ]
    args = [proj, proj, proj, bias_by_dist, gates]
    if use_sel:
        in_specs += [
            pl.BlockSpec((1, g, SEL_PAD, tq), lambda b, t, qi_r, ki_r, v_r, f_r, l_r: (b, 0, 0, qi_r[t])),
            pl.BlockSpec((tk, SEL_PAD), lambda b, t, qi_r, ki_r, v_r, f_r, l_r: (ki_r[t], 0)),
        ]
        args += [selb, e_mat]
    kdim = dh + SEL_PAD if use_sel else dh
    grid_spec = pltpu.PrefetchScalarGridSpec(
        num_scalar_prefetch=5,
        grid=(batch, len(steps)),
        in_specs=in_specs,
        out_specs=pl.BlockSpec((tq, NSA_W), lambda b, t, qi_r, ki_r, v_r, f_r, l_r: (b * nq + qi_r[t], 0)),
        scratch_shapes=[pltpu.VMEM((g, kdim, hpg * tq), BF16), pltpu.VMEM((g, 1, hpg * tq), F32),
                        pltpu.VMEM((g, dh + ACC_PAD, hpg * tq), F32),
                        pltpu.VMEM((g, ATT_NVAR if use_sel else ATT_NVAR - 1, tk, hpg * tq), F32)],
    )
    return pl.pallas_call(
        functools.partial(_band_kernel, use_sel=use_sel),
        grid_spec=grid_spec,
        out_shape=jax.ShapeDtypeStruct((batch * seq, NSA_W), BF16),
        compiler_params=pltpu.CompilerParams(
            dimension_semantics=("arbitrary", "arbitrary"), vmem_limit_bytes=VMEM_LIMIT),
        name="sel_attn" if use_sel else "win_attn",
    )(jnp.asarray(qi), jnp.asarray(ki), jnp.asarray(var), jnp.asarray(first), jnp.asarray(last), *args)


def _out_kernel(x_ref, yr_ref, oc_ref, os_ref, ow_ref, w_ref, o_ref):
    y_nsa = oc_ref[...].astype(F32) + os_ref[...].astype(F32) + ow_ref[...].astype(F32)
    y = jnp.concatenate([yr_ref[...], y_nsa.astype(BF16)], axis=-1)
    o_ref[...] = x_ref[...] + _dot(y, w_ref[...])


def _out_proj(x2d, y_ret, o_cmp, o_sel, o_win, w_out):
    tok, d = x2d.shape
    tm = OUT_TM
    wmix = y_ret.shape[1]
    row = lambda shape: pl.BlockSpec(shape, lambda i: (i, 0))
    return pl.pallas_call(
        _out_kernel,
        grid=(tok // tm,),
        in_specs=[row((tm, d)), row((tm, wmix)), row((tm, wmix)), row((tm, wmix)), row((tm, wmix)),
                  pl.BlockSpec(w_out.shape, lambda i: (0, 0))],
        out_specs=row((tm, d)),
        out_shape=jax.ShapeDtypeStruct((tok, d), F32),
        compiler_params=pltpu.CompilerParams(
            dimension_semantics=("parallel",), vmem_limit_bytes=VMEM_LIMIT),
        name="out_proj",
    )(x2d, y_ret, o_cmp, o_sel, o_win, w_out)


def _t5_bucket_of(rel):
    n = jnp.maximum(rel, 0)
    max_exact = REL_BUCKETS // 2
    nf = jnp.maximum(n, 1).astype(F32)
    large = max_exact + (jnp.log(nf / max_exact) / math.log(REL_MAX_DIST / max_exact)
                         * (REL_BUCKETS - max_exact)).astype(jnp.int32)
    large = jnp.minimum(large, REL_BUCKETS - 1)
    return jnp.where(n < max_exact, n, large)


def _bias_lookup(rel_bias, rel):
    bucket = _t5_bucket_of(rel)[None]
    tab = rel_bias.astype(F32).reshape((rel_bias.shape[0], REL_BUCKETS) + (1,) * rel.ndim)
    out = jnp.zeros((rel_bias.shape[0],) + rel.shape, F32)
    for b in range(REL_BUCKETS):
        out = jnp.where(bucket == b, tab[:, b], out)
    return out


def _bias_by_distance(rel_bias):
    t, g, hpg = ATT_T, NSA_KV_GROUPS, NSA_HPG
    assert ATT_TK == t and t >= REL_MAX_DIST and WIN == t
    by_dist = _bias_lookup(rel_bias, jnp.arange(t, dtype=jnp.int32)) * LOG2E
    far = jnp.broadcast_to(by_dist[:, t - 1:t], by_dist.shape)
    tab = jnp.stack([by_dist, far], axis=1).reshape(g, hpg, 2, 1, t)
    return jnp.broadcast_to(tab, (g, hpg, 2, 8, t))


def _token_mix(x2d, mix_norm, w_in, ret_gn_gain, pe_k, w1_k, w2_k, pe_v, w1_v, w2_v, w_out, rel_bias,
               batch, seq):
    d = x2d.shape[1]
    dh = NSA_DH
    w_pad = _to_bf16_transposed(w_in.T, D_IN_PAD)
    proj, cmp_kv = _inproj(x2d, mix_norm, w_pad)
    y_ret = _retention(proj, ret_gn_gain, batch, seq)

    gates = proj[:, IN_OFF[11]:IN_OFF[11] + IN_SPLITS[11]].astype(F32)
    gates = gates.reshape(batch, seq, NSA_KV_GROUPS, NSA_HPG, N_BRANCH).transpose(0, 2, 4, 3, 1)
    gates = jnp.pad(gates, ((0, 0), (0, 0), (0, 0), (0, GATE_ROWS - NSA_HPG), (0, 0)))

    k_cmp, v_cmp = _compress(cmp_kv, pe_k, w1_k, w2_k, pe_v, w1_v, w2_v, batch, seq)
    o_cmp, selb = _cmp_attention(proj, k_cmp, v_cmp, rel_bias, gates, batch, seq)

    bias_by_dist = _bias_by_distance(rel_bias)
    e_np = np.zeros((seq, SEL_PAD), np.float32)
    e_np[np.arange(seq), np.arange(seq) // SEL_LEN] = 1.0
    o_sel = _band_attention(proj, bias_by_dist, _band_steps(seq, 0),
                            IN_OFF[7], IN_OFF[8], gates, 1, batch, seq,
                            selb=selb, e_mat=jnp.asarray(e_np, BF16))
    o_win = _band_attention(proj, bias_by_dist, _band_steps(seq, WIN),
                            IN_OFF[9], IN_OFF[10], gates, 2, batch, seq)
    return _out_proj(x2d, y_ret, o_cmp, o_sel, o_win, _to_bf16(w_out))


def _cast_t_kernel(x_ref, o_ref, *, valid_rows):
    tr = x_ref.shape[0]
    row = pl.program_id(0) * tr + lax.broadcasted_iota(jnp.int32, x_ref.shape, 0)
    o_ref[...] = jnp.where(row < valid_rows, x_ref[...], 0.0).T.astype(o_ref.dtype)


def _to_bf16_transposed(w_t, cols):
    c, r = w_t.shape
    tr = CAST_T_ROWS
    return pl.pallas_call(
        functools.partial(_cast_t_kernel, valid_rows=c),
        grid=(cols // tr,),
        in_specs=[pl.BlockSpec((tr, r), lambda i: (i, 0))],
        out_specs=pl.BlockSpec((r, tr), lambda i: (0, i)),
        out_shape=jax.ShapeDtypeStruct((r, cols), BF16),
        compiler_params=pltpu.CompilerParams(
            dimension_semantics=("parallel",), vmem_limit_bytes=VMEM_LIMIT),
        name="to_bf16_t",
    )(w_t)


def _cast_kernel(x_ref, o_ref):
    o_ref[...] = x_ref[...].astype(o_ref.dtype)


def _to_bf16(w):
    r, c = w.shape
    tr = 1 << ((CAST_BLOCK_BYTES // (4 * c)).bit_length() - 1)
    while r % tr:
        tr //= 2
    return pl.pallas_call(
        _cast_kernel,
        grid=(r // tr,),
        in_specs=[pl.BlockSpec((tr, c), lambda i: (i, 0))],
        out_specs=pl.BlockSpec((tr, c), lambda i: (i, 0)),
        out_shape=jax.ShapeDtypeStruct((r, c), BF16),
        compiler_params=pltpu.CompilerParams(
            dimension_semantics=("parallel",), vmem_limit_bytes=VMEM_LIMIT),
        name="to_bf16",
    )(w)


def kernel(x, ffn1_norm, ffn1_w1, ffn1_w3, ffn1_w2, mix_norm, w_in, ret_gn_gain, cmp_pe_k, cmp_w1_k,
           cmp_w2_k, cmp_pe_v, cmp_w1_v, cmp_w2_v, w_out, ffn2_norm, ffn2_w1, ffn2_w3, ffn2_w2,
           rel_bias, final_norm):
    batch, seq, d = x.shape
    depth = ffn1_norm.shape[0]
    h = x.reshape(batch * seq, d)
    for l in range(depth):
        last = l == depth - 1
        h = _ffn(h, ffn1_norm[l], _to_bf16(ffn1_w1[l]), _to_bf16(ffn1_w3[l]), _to_bf16(ffn1_w2[l]),
                 final_norm, False)
        h = _token_mix(h, mix_norm[l], w_in[l], ret_gn_gain[l], cmp_pe_k[l], cmp_w1_k[l], cmp_w2_k[l],
                       cmp_pe_v[l], cmp_w1_v[l], cmp_w2_v[l], w_out[l], rel_bias, batch, seq)
        h = _ffn(h, ffn2_norm[l], _to_bf16(ffn2_w1[l]), _to_bf16(ffn2_w3[l]), _to_bf16(ffn2_w2[l]),
                 final_norm, last)
    if depth == 0:
        raise ValueError("depth must be positive")
    return h.reshape(batch, seq, d)
```

```python
import functools
import math

import jax
import jax.numpy as jnp
import numpy as np
from jax import lax
from jax.experimental import pallas as pl
from jax.experimental.pallas import tpu as pltpu

F32 = jnp.float32
BF16 = jnp.bfloat16

RET_HEADS = 4
RET_DK = 256
RET_DV = 256
ROPE_BASE = 10000.0
NSA_HEADS = 8
NSA_KV_GROUPS = 2
NSA_HPG = NSA_HEADS // NSA_KV_GROUPS
NSA_DH = 128
CMP_LEN = 32
CMP_STRIDE = 16
SEL_LEN = 64
SEL_TOPK = 16
WIN = 512
N_BRANCH = 3
REL_BUCKETS = 32
REL_MAX_DIST = 128
EPS = 1e-6
NEG = -1e30
FORCE = 1e4

RET_W = RET_HEADS * RET_DV
NSA_W = NSA_HEADS * NSA_DH
KV_W = NSA_KV_GROUPS * NSA_DH
IN_SPLITS = [RET_HEADS * RET_DK, RET_HEADS * RET_DK, RET_W, RET_W, NSA_W,
             KV_W, KV_W, KV_W, KV_W, KV_W, KV_W, NSA_HEADS * N_BRANCH]
D_IN = sum(IN_SPLITS)
IN_OFF = [sum(IN_SPLITS[:i]) for i in range(len(IN_SPLITS))]

LANE = 128
IN_TN = 2304
D_IN_PAD = 3 * IN_TN
CMP_KV_TILE = IN_OFF[5] // IN_TN
CMP_KV_LOCAL = IN_OFF[5] % IN_TN
assert CMP_KV_TILE > 0 and CMP_KV_LOCAL + 2 * KV_W <= IN_TN and IN_OFF[6] == IN_OFF[5] + KV_W
GATE_ROWS = 8
FFN_TM = 1024
FFN_TF = 512
IN_TM = 512
RET_C = 256
CMP_TQ = 512
ATT_T = 512
ATT_TK = 512
ATT_NVAR = 3
OUT_TM = 512
SEL_PAD = 128
ACC_PAD = 8
CMP_BAND = 128
VMEM_LIMIT = 56 * 1024 * 1024
CAST_BLOCK_BYTES = 6 * 1024 * 1024
CAST_T_ROWS = 256
LOG2E = math.log2(math.e)


def _dot(a, b):
    return jnp.dot(a, b, preferred_element_type=F32)


def _dot_nt(a, b):
    return lax.dot_general(a, b, (((1,), (1,)), ((), ())), preferred_element_type=F32)


def _dot_tn(a, b):
    return lax.dot_general(a, b, (((0,), (0,)), ((), ())), preferred_element_type=F32)


def _sigmoid(x):
    return 1.0 / (1.0 + jnp.exp(-x))


def _rms(x, g):
    ms = jnp.mean(x * x, axis=-1, keepdims=True)
    return x * lax.rsqrt(ms + EPS) * g


def _ffn_kernel(x_ref, g_ref, w1_ref, w3_ref, w2_ref, fg_ref, o_ref, n_ref, *, final_norm):
    j = pl.program_id(1)

    def step(first):
        if first:
            n = _rms(x_ref[...], g_ref[...]).astype(BF16)
            n_ref[...] = n
        else:
            n = n_ref[...]
        a = _dot(n, w1_ref[...])
        b = _dot(n, w3_ref[...])
        h = (0.5 * a * _sigmoid(a) * b).astype(BF16)
        upd = _dot(h, w2_ref[...])
        if first:
            o_ref[...] = x_ref[...] + upd
        else:
            o_ref[...] += upd

    pl.when(j == 0)(functools.partial(step, True))
    pl.when(j > 0)(functools.partial(step, False))

    if final_norm:
        @pl.when(j == pl.num_programs(1) - 1)
        def _():
            o_ref[...] = _rms(o_ref[...], fg_ref[...])


def _ffn(x2d, g, w1, w3, w2, fg, final_norm):
    tok, d = x2d.shape
    dff = w1.shape[1]
    tm, tf = FFN_TM, FFN_TF
    return pl.pallas_call(
        functools.partial(_ffn_kernel, final_norm=final_norm),
        grid=(tok // tm, dff // tf),
        in_specs=[
            pl.BlockSpec((tm, d), lambda i, j: (i, 0)),
            pl.BlockSpec((1, d), lambda i, j: (0, 0)),
            pl.BlockSpec((d, tf), lambda i, j: (0, j)),
            pl.BlockSpec((d, tf), lambda i, j: (0, j)),
            pl.BlockSpec((tf, d), lambda i, j: (j, 0)),
            pl.BlockSpec((1, d), lambda i, j: (0, 0)),
        ],
        out_specs=pl.BlockSpec((tm, d), lambda i, j: (i, 0)),
        out_shape=jax.ShapeDtypeStruct((tok, d), F32),
        scratch_shapes=[pltpu.VMEM((tm, d), BF16)],
        compiler_params=pltpu.CompilerParams(
            dimension_semantics=("parallel", "arbitrary"), vmem_limit_bytes=VMEM_LIMIT),
        name="ffn",
    )(x2d, g.reshape(1, d), w1, w3, w2, fg.reshape(1, d))


def _inproj_kernel(x_ref, g_ref, w_ref, o_ref, kv_ref, n_ref):
    j = pl.program_id(1)
    tn = o_ref.shape[1]
    for jj in range(w_ref.shape[1] // tn):
        @pl.when(j == jj)
        def _():
            if jj == 0:
                n = _rms(x_ref[...], g_ref[...]).astype(BF16)
                n_ref[...] = n
            else:
                n = n_ref[...]
            res = _dot(n, w_ref[:, jj * tn:(jj + 1) * tn])
            o_ref[...] = res.astype(o_ref.dtype)
            if jj == CMP_KV_TILE:
                kv_ref[...] = res[:, CMP_KV_LOCAL:CMP_KV_LOCAL + 2 * KV_W]


def _inproj(x2d, g, w_pad):
    tok, d = x2d.shape
    n_out = w_pad.shape[1]
    tm, tn = IN_TM, IN_TN
    return pl.pallas_call(
        _inproj_kernel,
        grid=(tok // tm, n_out // tn),
        in_specs=[
            pl.BlockSpec((tm, d), lambda i, j: (i, 0)),
            pl.BlockSpec((1, d), lambda i, j: (0, 0)),
            pl.BlockSpec(w_pad.shape, lambda i, j: (0, 0), pipeline_mode=pl.Buffered(1)),
        ],
        out_specs=[pl.BlockSpec((tm, tn), lambda i, j: (i, j)),
                   pl.BlockSpec((tm, 2 * KV_W), lambda i, j: (i, 0))],
        out_shape=[jax.ShapeDtypeStruct((tok, n_out), BF16),
                   jax.ShapeDtypeStruct((tok, 2 * KV_W), F32)],
        scratch_shapes=[pltpu.VMEM((tm, d), BF16)],
        compiler_params=pltpu.CompilerParams(
            dimension_semantics=("parallel", "arbitrary"), vmem_limit_bytes=VMEM_LIMIT),
        name="in_proj",
    )(x2d, g.reshape(1, d), w_pad)


def _ret_kernel(cd_ref, q_ref, k_ref, v_ref, g_ref, cos_ref, sin_ref, dec_ref, xi_ref, zeta_ref,
                gain_ref, o_ref, state_ref):
    @pl.when(pl.program_id(1) == 0)
    def _():
        state_ref[...] = jnp.zeros_like(state_ref)

    cos = cos_ref[...]
    sin = sin_ref[...]
    half = RET_DK // 2

    def rot(t):
        t1, t2 = t[:, :half], t[:, half:]
        return jnp.concatenate([t1 * cos - t2 * sin, t1 * sin + t2 * cos], axis=-1)

    for h in range(RET_HEADS):
        cs = slice(h * RET_DK, (h + 1) * RET_DK)
        q = rot(q_ref[:, cs].astype(F32))
        k = rot(k_ref[:, cs].astype(F32)) * (RET_DK ** -0.5)
        v = v_ref[:, cs].astype(BF16)
        qb = q.astype(BF16)
        s = _dot_nt(qb, k.astype(BF16)) * dec_ref[h]
        state = state_ref[h]
        y = _dot(s.astype(BF16), v) + _dot((q * xi_ref[h]).astype(BF16), state.astype(BF16))
        state_ref[h] = cd_ref[h] * state + _dot_tn((k * zeta_ref[h]).astype(BF16), v)
        mu = jnp.mean(y, axis=-1, keepdims=True)
        yc = y - mu
        var = jnp.mean(yc * yc, axis=-1, keepdims=True)
        yn = yc * lax.rsqrt(var + EPS) * gain_ref[:, cs]
        gate = g_ref[:, cs].astype(F32)
        o_ref[:, cs] = (gate * _sigmoid(gate) * yn).astype(o_ref.dtype)


def _retention(proj, gain, batch, seq):
    tok = proj.shape[0]
    c = RET_C
    n_chunks = seq // c
    h, dk = RET_HEADS, RET_DK
    half = dk // 2
    inv = ROPE_BASE ** (-jnp.arange(half, dtype=F32) / half)
    ang = jnp.arange(seq, dtype=F32)[:, None] * inv[None, :]
    cos, sin = jnp.cos(ang), jnp.sin(ang)
    log_g = jnp.log(1.0 - 2.0 ** (-5.0 - jnp.arange(h, dtype=F32)))
    idx = jnp.arange(c, dtype=F32)
    diff = idx[:, None] - idx[None, :]
    dec = jnp.where(diff >= 0.0, jnp.exp(jnp.maximum(diff, 0.0)[None] * log_g[:, None, None]), 0.0)
    xi = jnp.broadcast_to(jnp.exp((idx + 1.0)[None] * log_g[:, None])[:, :, None], (h, c, dk))
    zeta = jnp.broadcast_to(jnp.exp((c - 1.0 - idx)[None] * log_g[:, None])[:, :, None], (h, c, dk))
    cd = jnp.exp(c * log_g)

    w = h * dk
    row = lambda b, n: (b * n_chunks + n)
    return pl.pallas_call(
        _ret_kernel,
        grid=(batch, n_chunks),
        in_specs=[
            pl.BlockSpec(memory_space=pltpu.SMEM),
            pl.BlockSpec((c, w), lambda b, n: (row(b, n), 0)),
            pl.BlockSpec((c, w), lambda b, n: (row(b, n), 1)),
            pl.BlockSpec((c, w), lambda b, n: (row(b, n), 2)),
            pl.BlockSpec((c, w), lambda b, n: (row(b, n), 3)),
            pl.BlockSpec((c, half), lambda b, n: (n, 0)),
            pl.BlockSpec((c, half), lambda b, n: (n, 0)),
            pl.BlockSpec((h, c, c), lambda b, n: (0, 0, 0)),
            pl.BlockSpec((h, c, dk), lambda b, n: (0, 0, 0)),
            pl.BlockSpec((h, c, dk), lambda b, n: (0, 0, 0)),
            pl.BlockSpec((1, w), lambda b, n: (0, 0)),
        ],
        out_specs=pl.BlockSpec((c, w), lambda b, n: (row(b, n), 0)),
        out_shape=jax.ShapeDtypeStruct((tok, w), BF16),
        scratch_shapes=[pltpu.VMEM((h, dk, RET_DV), F32)],
        compiler_params=pltpu.CompilerParams(
            dimension_semantics=("parallel", "arbitrary"), vmem_limit_bytes=VMEM_LIMIT),
        name="retention",
    )(cd, proj, proj, proj, proj, cos, sin, dec, xi, zeta, gain.reshape(1, w))


def _compress_kernel(k_ref, v_ref, pek_ref, w1k_ref, w1kf_ref, w2k_ref, pev_ref, w1v_ref, w1vf_ref,
                     w2v_ref, ok_ref, ov_ref):
    dh = NSA_DH
    nblk = k_ref.shape[0] // CMP_STRIDE

    def one(x_ref, pe_ref, w1_ref, w1f_ref, w2_ref, o_ref):
        ab = jnp.zeros((nblk, 2 * dh), F32)
        for l in range(CMP_STRIDE):
            rows = x_ref[pl.ds(l, nblk, stride=CMP_STRIDE), :].astype(BF16)
            ab = ab + _dot(rows, w1_ref[l])
        a, b = ab[:, :dh], ab[:, dh:]
        b_next = pltpu.roll(b, nblk - 1, axis=0)
        ridx = lax.broadcasted_iota(jnp.int32, (nblk, dh), 0)
        b_next = jnp.where(ridx < nblk - 1, b_next, 0.0)
        pe_term = _dot(pe_ref[...], w1f_ref[...])[0:1, :]
        hdn = a + b_next + pe_term
        o_ref[0, 0] = _dot((hdn * _sigmoid(hdn)).astype(BF16), w2_ref[...])

    one(k_ref, pek_ref, w1k_ref, w1kf_ref, w2k_ref, ok_ref)
    one(v_ref, pev_ref, w1v_ref, w1vf_ref, w2v_ref, ov_ref)


def _compress(kv, pe_k, w1_k, w2_k, pe_v, w1_v, w2_v, batch, seq):
    dh, g = NSA_DH, NSA_KV_GROUPS
    nblk = seq // CMP_STRIDE
    half = CMP_LEN // 2

    def prep(pe, w1, w2):
        w1b = w1.astype(BF16)
        w1_pair = jnp.concatenate([w1b[:half], w1b[half:]], axis=-1)
        pe_flat = jnp.broadcast_to(pe.reshape(1, CMP_LEN * dh), (8, CMP_LEN * dh)).astype(BF16)
        return pe_flat, w1_pair, w1b.reshape(CMP_LEN * dh, dh), w2.astype(BF16)

    args_k = prep(pe_k, w1_k, w2_k)
    args_v = prep(pe_v, w1_v, w2_v)
    kcol = 0
    vcol = KV_W // dh
    const = lambda shape: pl.BlockSpec(shape, lambda b, gg: (0,) * len(shape))
    wspecs = [const((8, CMP_LEN * dh)), const((half, dh, 2 * dh)), const((CMP_LEN * dh, dh)),
              const((dh, dh))]
    out_spec = pl.BlockSpec((1, 1, nblk, dh), lambda b, gg: (b, gg, 0, 0))
    return pl.pallas_call(
        _compress_kernel,
        grid=(batch, g),
        in_specs=[pl.BlockSpec((seq, dh), lambda b, gg: (b, kcol + gg)),
                  pl.BlockSpec((seq, dh), lambda b, gg: (b, vcol + gg))] + wspecs + wspecs,
        out_specs=[out_spec, out_spec],
        out_shape=[jax.ShapeDtypeStruct((batch, g, nblk, dh), F32)] * 2,
        compiler_params=pltpu.CompilerParams(
            dimension_semantics=("parallel", "parallel"), vmem_limit_bytes=VMEM_LIMIT),
        name="compress",
    )(kv, kv, *args_k, *args_v)


def _cmpattn_kernel(q_ref, kc_ref, vc_ref, tb_ref, pp_ref, ovt_ref, gate_ref, o_ref, sel_ref, rank_ref):
    tq = q_ref.shape[0]
    ncp = kc_ref.shape[2]
    dh = NSA_DH
    s0 = pl.program_id(2) * tq
    hpg = NSA_HPG
    pos_r = s0 + (lax.broadcasted_iota(jnp.int32, (hpg * tq, ncp), 0) & (tq - 1))
    n_c = lax.broadcasted_iota(jnp.int32, (hpg * tq, ncp), 1)
    mask = pos_r >= n_c * CMP_STRIDE + (CMP_LEN - 1)
    kc = kc_ref[0, 0].astype(BF16)
    vc = vc_ref[0, 0].astype(BF16)
    scale = dh ** -0.5
    q = jnp.concatenate([q_ref[:, hh * dh:(hh + 1) * dh] for hh in range(hpg)], axis=0).astype(BF16)
    bias = _dot(tb_ref[...].reshape(hpg * tq, tb_ref.shape[2]), pp_ref[0])
    lm = jnp.where(mask, _dot_nt(q, kc) * scale + bias, NEG)
    m = jnp.max(lm, axis=-1, keepdims=True)
    e = jnp.where(mask, jnp.exp(lm - m), 0.0)
    den = jnp.sum(e, axis=-1, keepdims=True)
    p = e / jnp.where(den > 0.0, den, 1.0)
    o = _dot(p.astype(BF16), vc)
    gate = _sigmoid(gate_ref[0, 0])
    psum = p[0:tq]
    for hh in range(hpg):
        o_ref[:, hh * dh:(hh + 1) * dh] = (o[hh * tq:(hh + 1) * tq] * gate[:, hh:hh + 1]).astype(o_ref.dtype)
        if hh:
            psum = psum + p[hh * tq:(hh + 1) * tq]
    nsel = ovt_ref.shape[0]
    imp_t = _dot_nt(ovt_ref[...], psum.astype(BF16))
    jb = lax.broadcasted_iota(jnp.int32, (nsel, tq), 0)
    pos = s0 + lax.broadcasted_iota(jnp.int32, (nsel, tq), 1)
    cur = jnp.right_shift(pos, SEL_LEN.bit_length() - 1)
    causal = jb * SEL_LEN <= pos
    forced = (jb == 0) | (jb == cur) | (jb == cur - 1)
    score = jnp.where(forced, FORCE, jnp.where(causal, imp_t, NEG))
    per_tile = tq // SEL_LEN
    rank_ref[...] = jnp.zeros_like(rank_ref)
    for c in range(nsel // per_tile):
        @pl.when(c <= pl.program_id(2))
        def _():
            rank = rank_ref[...]
            for kk in range(c * per_tile, (c + 1) * per_tile):
                row = score[kk:kk + 1, :]
                beats = (row > score) | ((row == score) & (jb > kk))
                rank = rank + jnp.where(beats, 1, 0)
            rank_ref[...] = rank
    rank = rank_ref[...]
    selb = jnp.where((rank < SEL_TOPK) & causal, 0.0, NEG)
    selb = jnp.concatenate([selb, jnp.zeros((SEL_PAD - nsel, tq), F32)], axis=0)
    sel_ref[0, 0] = selb.astype(BF16)


def _cmp_attention(proj, k_cmp, v_cmp, rel_bias, gates, batch, seq):
    dh, g, hpg = NSA_DH, NSA_KV_GROUPS, NSA_HPG
    tq = CMP_TQ
    nq = seq // tq
    ncp = k_cmp.shape[2]
    nsel = seq // SEL_LEN
    n_cmp = (seq - CMP_LEN) // CMP_STRIDE + 1
    cmp_idx = np.arange(n_cmp)[:, None] * CMP_STRIDE + np.arange(CMP_LEN)[None, :]
    overlap = ((cmp_idx // SEL_LEN)[:, :, None] == np.arange(nsel)[None, None, :]).sum(1) / CMP_LEN
    ovt = np.zeros((nsel, ncp), np.float32)
    ovt[:, :n_cmp] = overlap.T
    nb = tq // CMP_STRIDE
    assert 2 * nb + 1 <= CMP_BAND and CMP_STRIDE * (nb + 1) - (CMP_LEN - 1) >= REL_MAX_DIST
    i = np.arange(tq)[:, None]
    r = np.arange(CMP_BAND)[None, :]
    rel = np.where(r < 2 * nb, i - CMP_STRIDE * (r - nb) - (CMP_LEN - 1), REL_MAX_DIST)
    tb = jnp.where(jnp.asarray(r <= 2 * nb), _bias_lookup(rel_bias, jnp.asarray(rel, jnp.int32)), 0.0)
    tb_hi = tb.astype(BF16)
    tb_lo = (tb - tb_hi.astype(F32)).astype(BF16)
    tb = jnp.concatenate([tb_hi, tb_lo], axis=-1)
    n = np.arange(ncp)[None, None, :]
    first = (np.arange(nq) * nb - nb)[:, None, None]
    rr = np.arange(CMP_BAND)[None, :, None]
    pp = np.where(rr < 2 * nb, n == first + rr, (rr == 2 * nb) & (n < first)).astype(np.float32)
    pp = np.concatenate([pp, pp], axis=1)
    qcol = IN_OFF[4] // (hpg * dh)
    return pl.pallas_call(
        _cmpattn_kernel,
        grid=(batch, g, nq),
        in_specs=[
            pl.BlockSpec((tq, hpg * dh), lambda b, gg, t: (b * nq + t, qcol + gg)),
            pl.BlockSpec((1, 1, ncp, dh), lambda b, gg, t: (b, gg, 0, 0)),
            pl.BlockSpec((1, 1, ncp, dh), lambda b, gg, t: (b, gg, 0, 0)),
            pl.BlockSpec((hpg, tq, 2 * CMP_BAND), lambda b, gg, t: (gg, 0, 0)),
            pl.BlockSpec((1, 2 * CMP_BAND, ncp), lambda b, gg, t: (t, 0, 0)),
            pl.BlockSpec((nsel, ncp), lambda b, gg, t: (0, 0)),
            pl.BlockSpec((1, 1, tq, GATE_ROWS), lambda b, gg, t: (b, gg, t, 0)),
        ],
        out_specs=[
            pl.BlockSpec((tq, hpg * dh), lambda b, gg, t: (b * nq + t, gg)),
            pl.BlockSpec((1, 1, SEL_PAD, tq), lambda b, gg, t: (b, gg, 0, t)),
        ],
        out_shape=[jax.ShapeDtypeStruct((batch * seq, g * hpg * dh), BF16),
                   jax.ShapeDtypeStruct((batch, g, SEL_PAD, seq), BF16)],
        scratch_shapes=[pltpu.VMEM((nsel, tq), jnp.int32)],
        compiler_params=pltpu.CompilerParams(
            dimension_semantics=("parallel", "parallel", "parallel"), vmem_limit_bytes=VMEM_LIMIT),
        name="cmp_attn",
    )(proj, k_cmp, v_cmp, tb, jnp.asarray(pp, BF16), jnp.asarray(ovt, BF16),
      gates[:, :, 0].transpose(0, 1, 3, 2))


def _band_kernel(qi_ref, ki_ref, var_ref, first_ref, last_ref, q_ref, k_ref, v_ref, fd_ref, gate_ref,
                 *rest, use_sel):
    if use_sel:
        selb_ref, e_ref, o_ref, qt_ref, m_ref, acc_ref, bias_ref = rest
    else:
        o_ref, qt_ref, m_ref, acc_ref, bias_ref = rest
    t = pl.program_id(1)
    dh, hpg, ng = NSA_DH, NSA_HPG, NSA_KV_GROUPS
    tq, tk = q_ref.shape[0], k_ref.shape[0]

    @pl.when((pl.program_id(0) == 0) & (t == 0))
    def _():
        upper = (lax.broadcasted_iota(jnp.int32, (tk, tq), 1) >= lax.broadcasted_iota(jnp.int32, (tk, tq), 0))
        for gg in range(ng):
            for hh in range(hpg):
                cols = slice(hh * tq, (hh + 1) * tq)
                by_dist = jnp.broadcast_to(fd_ref[gg, hh, 0, 0:1, :], (tk, tq))
                cyc = pltpu.roll(by_dist, 0, 1, stride=1, stride_axis=0)
                bias_ref[gg, 0, :, cols] = jnp.where(upper, cyc, NEG)
                if use_sel:
                    far = jnp.broadcast_to(fd_ref[gg, hh, 1, 0:1, :], (tk, tq))
                    bias_ref[gg, 1, :, cols] = jnp.where(upper, far, cyc)
                else:
                    bias_ref[gg, 1, :, cols] = jnp.where(upper, NEG, cyc)

    @pl.when(first_ref[t] == 1)
    def _():
        m_ref[...] = jnp.full_like(m_ref, NEG)
        acc_ref[...] = jnp.zeros_like(acc_ref)
        scale = dh ** -0.5 * LOG2E
        for gg in range(ng):
            for hh in range(hpg):
                h = gg * hpg + hh
                q = q_ref[:, h * dh:(h + 1) * dh].astype(F32)
                qt_ref[gg, 0:dh, hh * tq:(hh + 1) * tq] = (q * scale).T.astype(BF16)
                if use_sel:
                    qt_ref[gg, dh:, hh * tq:(hh + 1) * tq] = selb_ref[0, gg]

    var = var_ref[t]

    def update(far):
        ones = jnp.ones((ACC_PAD, tk), F32)
        scores = []
        for gg in range(ng):
            k = k_ref[:, gg * dh:(gg + 1) * dh]
            if use_sel:
                k = jnp.concatenate([k, e_ref[...]], axis=1)
            s = _dot(k, qt_ref[gg])
            scores.append(s if far else s + bias_ref[gg, var])
        for gg in range(ng):
            vt = jnp.concatenate([v_ref[:, gg * dh:(gg + 1) * dh].astype(F32).T, ones], axis=0).astype(BF16)
            s = scores[gg]
            m_prev = m_ref[gg]
            col_max = jnp.max(s, axis=0, keepdims=True)
            if far:
                c = jnp.concatenate([fd_ref[gg, hh, 1, 0:1, :] for hh in range(hpg)], axis=1)
                m_new = jnp.maximum(m_prev, col_max + c)
                shift = m_new - c
            else:
                m_new = jnp.maximum(m_prev, col_max)
                shift = m_new
            alpha = jnp.exp2(m_prev - m_new)
            p = jnp.exp2(s - shift).astype(BF16)
            acc_ref[gg] = alpha * acc_ref[gg] + _dot(vt, p)
            m_ref[gg] = m_new

    if use_sel:
        pl.when(var == ATT_NVAR - 1)(functools.partial(update, True))
        pl.when(var != ATT_NVAR - 1)(functools.partial(update, False))
    else:
        update(False)

    @pl.when(last_ref[t] == 1)
    def _():
        for gg in range(ng):
            gate = _sigmoid(gate_ref[0, gg, 0])
            for hh in range(hpg):
                h = gg * hpg + hh
                cols = slice(hh * tq, (hh + 1) * tq)
                o = acc_ref[gg, 0:dh, cols] * (gate[hh:hh + 1, :] / acc_ref[gg, dh:dh + 1, cols])
                o_ref[:, h * dh:(h + 1) * dh] = o.T.astype(o_ref.dtype)


def _band_steps(seq, window):
    tq, tk = ATT_T, ATT_TK
    steps = []
    for qi in range(seq // tq):
        k_hi = qi * tq // tk
        k_lo = max(0, (qi * tq - window + 1) // tk) if window else 0
        steps += [(qi, ki, min((qi * tq - ki * tk) // tq, ATT_NVAR - 1)) for ki in range(k_lo, k_hi + 1)]
    return steps


def _band_attention(proj, bias_by_dist, steps, koff, voff, gates, branch, batch, seq, selb=None, e_mat=None):
    dh, g, hpg = NSA_DH, NSA_KV_GROUPS, NSA_HPG
    tq, tk = ATT_T, ATT_TK
    nq, nk = seq // tq, seq // tk
    qi = np.array([s[0] for s in steps], np.int32)
    ki = np.array([s[1] for s in steps], np.int32)
    var = np.array([s[2] for s in steps], np.int32)
    first = np.concatenate([[1], (qi[1:] != qi[:-1]).astype(np.int32)]).astype(np.int32)
    last = np.concatenate([(qi[1:] != qi[:-1]).astype(np.int32), [1]]).astype(np.int32)
    qcol, kcol, vcol = IN_OFF[4] // NSA_W, koff // KV_W, voff // KV_W
    use_sel = selb is not None
    in_specs = [
        pl.BlockSpec((tq, NSA_W), lambda b, t, qi_r, ki_r, v_r, f_r, l_r: (b * nq + qi_r[t], qcol)),
        pl.BlockSpec((tk, KV_W), lambda b, t, qi_r, ki_r, v_r, f_r, l_r: (b * nk + ki_r[t], kcol)),
        pl.BlockSpec((tk, KV_W), lambda b, t, qi_r, ki_r, v_r, f_r, l_r: (b * nk + ki_r[t], vcol)),
        pl.BlockSpec(bias_by_dist.shape, lambda b, t, qi_r, ki_r, v_r, f_r, l_r: (0, 0, 0, 0, 0)),
        pl.BlockSpec((1, g, 1, GATE_ROWS, tq),
                     lambda b, t, qi_r, ki_r, v_r, f_r, l_r: (b, 0, branch, 0, qi_r[t])),
    ]
    args = [proj, proj, proj, bias_by_dist, gates]
    if use_sel:
        in_specs += [
            pl.BlockSpec((1, g, SEL_PAD, tq), lambda b, t, qi_r, ki_r, v_r, f_r, l_r: (b, 0, 0, qi_r[t])),
            pl.BlockSpec((tk, SEL_PAD), lambda b, t, qi_r, ki_r, v_r, f_r, l_r: (ki_r[t], 0)),
        ]
        args += [selb, e_mat]
    kdim = dh + SEL_PAD if use_sel else dh
    grid_spec = pltpu.PrefetchScalarGridSpec(
        num_scalar_prefetch=5,
        grid=(batch, len(steps)),
        in_specs=in_specs,
        out_specs=pl.BlockSpec((tq, NSA_W), lambda b, t, qi_r, ki_r, v_r, f_r, l_r: (b * nq + qi_r[t], 0)),
        scratch_shapes=[pltpu.VMEM((g, kdim, hpg * tq), BF16), pltpu.VMEM((g, 1, hpg * tq), F32),
                        pltpu.VMEM((g, dh + ACC_PAD, hpg * tq), F32),
                        pltpu.VMEM((g, ATT_NVAR - 1, tk, hpg * tq), F32)],
    )
    return pl.pallas_call(
        functools.partial(_band_kernel, use_sel=use_sel),
        grid_spec=grid_spec,
        out_shape=jax.ShapeDtypeStruct((batch * seq, NSA_W), BF16),
        compiler_params=pltpu.CompilerParams(
            dimension_semantics=("arbitrary", "arbitrary"), vmem_limit_bytes=VMEM_LIMIT),
        name="sel_attn" if use_sel else "win_attn",
    )(jnp.asarray(qi), jnp.asarray(ki), jnp.asarray(var), jnp.asarray(first), jnp.asarray(last), *args)


def _out_kernel(x_ref, yr_ref, oc_ref, os_ref, ow_ref, w_ref, o_ref):
    y_nsa = oc_ref[...].astype(F32) + os_ref[...].astype(F32) + ow_ref[...].astype(F32)
    y = jnp.concatenate([yr_ref[...], y_nsa.astype(BF16)], axis=-1)
    o_ref[...] = x_ref[...] + _dot(y, w_ref[...])


def _out_proj(x2d, y_ret, o_cmp, o_sel, o_win, w_out):
    tok, d = x2d.shape
    tm = OUT_TM
    wmix = y_ret.shape[1]
    row = lambda shape: pl.BlockSpec(shape, lambda i: (i, 0))
    return pl.pallas_call(
        _out_kernel,
        grid=(tok // tm,),
        in_specs=[row((tm, d)), row((tm, wmix)), row((tm, wmix)), row((tm, wmix)), row((tm, wmix)),
                  pl.BlockSpec(w_out.shape, lambda i: (0, 0))],
        out_specs=row((tm, d)),
        out_shape=jax.ShapeDtypeStruct((tok, d), F32),
        compiler_params=pltpu.CompilerParams(
            dimension_semantics=("parallel",), vmem_limit_bytes=VMEM_LIMIT),
        name="out_proj",
    )(x2d, y_ret, o_cmp, o_sel, o_win, w_out)


def _t5_bucket_of(rel):
    n = jnp.maximum(rel, 0)
    max_exact = REL_BUCKETS // 2
    nf = jnp.maximum(n, 1).astype(F32)
    large = max_exact + (jnp.log(nf / max_exact) / math.log(REL_MAX_DIST / max_exact)
                         * (REL_BUCKETS - max_exact)).astype(jnp.int32)
    large = jnp.minimum(large, REL_BUCKETS - 1)
    return jnp.where(n < max_exact, n, large)


def _bias_lookup(rel_bias, rel):
    bucket = _t5_bucket_of(rel)[None]
    tab = rel_bias.astype(F32).reshape((rel_bias.shape[0], REL_BUCKETS) + (1,) * rel.ndim)
    out = jnp.zeros((rel_bias.shape[0],) + rel.shape, F32)
    for b in range(REL_BUCKETS):
        out = jnp.where(bucket == b, tab[:, b], out)
    return out


def _bias_by_distance(rel_bias):
    t, g, hpg = ATT_T, NSA_KV_GROUPS, NSA_HPG
    assert ATT_TK == t and t >= REL_MAX_DIST and WIN == t
    by_dist = _bias_lookup(rel_bias, jnp.arange(t, dtype=jnp.int32)) * LOG2E
    far = jnp.broadcast_to(by_dist[:, t - 1:t], by_dist.shape)
    tab = jnp.stack([by_dist, far], axis=1).reshape(g, hpg, 2, 1, t)
    return jnp.broadcast_to(tab, (g, hpg, 2, 8, t))


def _token_mix(x2d, mix_norm, w_in, ret_gn_gain, pe_k, w1_k, w2_k, pe_v, w1_v, w2_v, w_out, rel_bias,
               batch, seq):
    d = x2d.shape[1]
    dh = NSA_DH
    w_pad = _to_bf16_transposed(w_in.T, D_IN_PAD)
    proj, cmp_kv = _inproj(x2d, mix_norm, w_pad)
    y_ret = _retention(proj, ret_gn_gain, batch, seq)

    gates = proj[:, IN_OFF[11]:IN_OFF[11] + IN_SPLITS[11]].astype(F32)
    gates = gates.reshape(batch, seq, NSA_KV_GROUPS, NSA_HPG, N_BRANCH).transpose(0, 2, 4, 3, 1)
    gates = jnp.pad(gates, ((0, 0), (0, 0), (0, 0), (0, GATE_ROWS - NSA_HPG), (0, 0)))

    k_cmp, v_cmp = _compress(cmp_kv, pe_k, w1_k, w2_k, pe_v, w1_v, w2_v, batch, seq)
    o_cmp, selb = _cmp_attention(proj, k_cmp, v_cmp, rel_bias, gates, batch, seq)

    bias_by_dist = _bias_by_distance(rel_bias)
    e_np = np.zeros((seq, SEL_PAD), np.float32)
    e_np[np.arange(seq), np.arange(seq) // SEL_LEN] = 1.0
    o_sel = _band_attention(proj, bias_by_dist, _band_steps(seq, 0),
                            IN_OFF[7], IN_OFF[8], gates, 1, batch, seq,
                            selb=selb, e_mat=jnp.asarray(e_np, BF16))
    o_win = _band_attention(proj, bias_by_dist, _band_steps(seq, WIN),
                            IN_OFF[9], IN_OFF[10], gates, 2, batch, seq)
    return _out_proj(x2d, y_ret, o_cmp, o_sel, o_win, _to_bf16(w_out))


def _cast_t_kernel(x_ref, o_ref, *, valid_rows):
    tr = x_ref.shape[0]
    row = pl.program_id(0) * tr + lax.broadcasted_iota(jnp.int32, x_ref.shape, 0)
    o_ref[...] = jnp.where(row < valid_rows, x_ref[...], 0.0).T.astype(o_ref.dtype)


def _to_bf16_transposed(w_t, cols):
    c, r = w_t.shape
    tr = CAST_T_ROWS
    return pl.pallas_call(
        functools.partial(_cast_t_kernel, valid_rows=c),
        grid=(cols // tr,),
        in_specs=[pl.BlockSpec((tr, r), lambda i: (i, 0))],
        out_specs=pl.BlockSpec((r, tr), lambda i: (0, i)),
        out_shape=jax.ShapeDtypeStruct((r, cols), BF16),
        compiler_params=pltpu.CompilerParams(
            dimension_semantics=("parallel",), vmem_limit_bytes=VMEM_LIMIT),
        name="to_bf16_t",
    )(w_t)


def _cast_kernel(x_ref, o_ref):
    o_ref[...] = x_ref[...].astype(o_ref.dtype)


def _to_bf16(w):
    r, c = w.shape
    tr = 1 << ((CAST_BLOCK_BYTES // (4 * c)).bit_length() - 1)
    while r % tr:
        tr //= 2
    return pl.pallas_call(
        _cast_kernel,
        grid=(r // tr,),
        in_specs=[pl.BlockSpec((tr, c), lambda i: (i, 0))],
        out_specs=pl.BlockSpec((tr, c), lambda i: (i, 0)),
        out_shape=jax.ShapeDtypeStruct((r, c), BF16),
        compiler_params=pltpu.CompilerParams(
            dimension_semantics=("parallel",), vmem_limit_bytes=VMEM_LIMIT),
        name="to_bf16",
    )(w)


def kernel(x, ffn1_norm, ffn1_w1, ffn1_w3, ffn1_w2, mix_norm, w_in, ret_gn_gain, cmp_pe_k, cmp_w1_k,
           cmp_w2_k, cmp_pe_v, cmp_w1_v, cmp_w2_v, w_out, ffn2_norm, ffn2_w1, ffn2_w3, ffn2_w2,
           rel_bias, final_norm):
    batch, seq, d = x.shape
    depth = ffn1_norm.shape[0]
    h = x.reshape(batch * seq, d)
    for l in range(depth):
        last = l == depth - 1
        h = _ffn(h, ffn1_norm[l], _to_bf16(ffn1_w1[l]), _to_bf16(ffn1_w3[l]), _to_bf16(ffn1_w2[l]),
                 final_norm, False)
        h = _token_mix(h, mix_norm[l], w_in[l], ret_gn_gain[l], cmp_pe_k[l], cmp_w1_k[l], cmp_w2_k[l],
                       cmp_pe_v[l], cmp_w1_v[l], cmp_w2_v[l], w_out[l], rel_bias, batch, seq)
        h = _ffn(h, ffn2_norm[l], _to_bf16(ffn2_w1[l]), _to_bf16(ffn2_w3[l]), _to_bf16(ffn2_w2[l]),
                 final_norm, last)
    if depth == 0:
        raise ValueError("depth must be positive")
    return h.reshape(batch, seq, d)
```

```python
import functools
import math

import jax
import jax.numpy as jnp
import numpy as np
from jax import lax
from jax.experimental import pallas as pl
from jax.experimental.pallas import tpu as pltpu

F32 = jnp.float32
BF16 = jnp.bfloat16

RET_HEADS = 4
RET_DK = 256
RET_DV = 256
ROPE_BASE = 10000.0
NSA_HEADS = 8
NSA_KV_GROUPS = 2
NSA_HPG = NSA_HEADS // NSA_KV_GROUPS
NSA_DH = 128
CMP_LEN = 32
CMP_STRIDE = 16
SEL_LEN = 64
SEL_TOPK = 16
WIN = 512
N_BRANCH = 3
REL_BUCKETS = 32
REL_MAX_DIST = 128
EPS = 1e-6
NEG = -1e30
FORCE = 1e4

RET_W = RET_HEADS * RET_DV
NSA_W = NSA_HEADS * NSA_DH
KV_W = NSA_KV_GROUPS * NSA_DH
IN_SPLITS = [RET_HEADS * RET_DK, RET_HEADS * RET_DK, RET_W, RET_W, NSA_W,
             KV_W, KV_W, KV_W, KV_W, KV_W, KV_W, NSA_HEADS * N_BRANCH]
D_IN = sum(IN_SPLITS)
IN_OFF = [sum(IN_SPLITS[:i]) for i in range(len(IN_SPLITS))]

LANE = 128
IN_TN = 2304
D_IN_PAD = 3 * IN_TN
CMP_KV_TILE = IN_OFF[5] // IN_TN
CMP_KV_LOCAL = IN_OFF[5] % IN_TN
assert CMP_KV_TILE > 0 and CMP_KV_LOCAL + 2 * KV_W <= IN_TN and IN_OFF[6] == IN_OFF[5] + KV_W
GATE_ROWS = 8
FFN_TM = 1024
FFN_TF = 512
IN_TM = 512
RET_C = 256
CMP_TQ = 512
ATT_T = 512
ATT_TK = 512
ATT_NVAR = 3
OUT_TM = 512
SEL_PAD = 128
ACC_PAD = 8
CMP_BAND = 128
VMEM_LIMIT = 56 * 1024 * 1024
CAST_BLOCK_BYTES = 6 * 1024 * 1024
CAST_T_ROWS = 256
LOG2E = math.log2(math.e)


def _dot(a, b):
    return jnp.dot(a, b, preferred_element_type=F32)


def _dot_nt(a, b):
    return lax.dot_general(a, b, (((1,), (1,)), ((), ())), preferred_element_type=F32)


def _dot_tn(a, b):
    return lax.dot_general(a, b, (((0,), (0,)), ((), ())), preferred_element_type=F32)


def _sigmoid(x):
    return 1.0 / (1.0 + jnp.exp(-x))


def _rms(x, g):
    ms = jnp.mean(x * x, axis=-1, keepdims=True)
    return x * lax.rsqrt(ms + EPS) * g


def _ffn_kernel(x_ref, g_ref, w1_ref, w3_ref, w2_ref, fg_ref, o_ref, n_ref, *, final_norm):
    j = pl.program_id(1)

    last = pl.num_programs(1) - 1

    def step(first, norm_out):
        if first:
            n = _rms(x_ref[...], g_ref[...]).astype(BF16)
            n_ref[...] = n
        else:
            n = n_ref[...]
        a = _dot(n, w1_ref[...])
        b = _dot(n, w3_ref[...])
        h = (0.5 * a * _sigmoid(a) * b).astype(BF16)
        y = (x_ref[...] if first else o_ref[...]) + _dot(h, w2_ref[...])
        o_ref[...] = _rms(y, fg_ref[...]) if norm_out else y

    pl.when(j == 0)(functools.partial(step, True, False))
    if final_norm:
        pl.when((j > 0) & (j < last))(functools.partial(step, False, False))
        pl.when(j == last)(functools.partial(step, False, True))
    else:
        pl.when(j > 0)(functools.partial(step, False, False))


def _ffn(x2d, g, w1, w3, w2, fg, final_norm):
    tok, d = x2d.shape
    dff = w1.shape[1]
    tm, tf = FFN_TM, FFN_TF
    assert dff // tf >= 2
    return pl.pallas_call(
        functools.partial(_ffn_kernel, final_norm=final_norm),
        grid=(tok // tm, dff // tf),
        in_specs=[
            pl.BlockSpec((tm, d), lambda i, j: (i, 0)),
            pl.BlockSpec((1, d), lambda i, j: (0, 0)),
            pl.BlockSpec((d, tf), lambda i, j: (0, j)),
            pl.BlockSpec((d, tf), lambda i, j: (0, j)),
            pl.BlockSpec((tf, d), lambda i, j: (j, 0)),
            pl.BlockSpec((1, d), lambda i, j: (0, 0)),
        ],
        out_specs=pl.BlockSpec((tm, d), lambda i, j: (i, 0)),
        out_shape=jax.ShapeDtypeStruct((tok, d), F32),
        scratch_shapes=[pltpu.VMEM((tm, d), BF16)],
        compiler_params=pltpu.CompilerParams(
            dimension_semantics=("parallel", "arbitrary"), vmem_limit_bytes=VMEM_LIMIT),
        name="ffn",
    )(x2d, g.reshape(1, d), w1, w3, w2, fg.reshape(1, d))


def _inproj_kernel(x_ref, g_ref, w_ref, o_ref, kv_ref, n_ref):
    j = pl.program_id(1)
    tn = o_ref.shape[1]
    for jj in range(w_ref.shape[1] // tn):
        @pl.when(j == jj)
        def _():
            if jj == 0:
                n = _rms(x_ref[...], g_ref[...]).astype(BF16)
                n_ref[...] = n
            else:
                n = n_ref[...]
            res = _dot(n, w_ref[:, jj * tn:(jj + 1) * tn])
            o_ref[...] = res.astype(o_ref.dtype)
            if jj == CMP_KV_TILE:
                kv_ref[...] = res[:, CMP_KV_LOCAL:CMP_KV_LOCAL + 2 * KV_W]


def _inproj(x2d, g, w_pad):
    tok, d = x2d.shape
    n_out = w_pad.shape[1]
    tm, tn = IN_TM, IN_TN
    return pl.pallas_call(
        _inproj_kernel,
        grid=(tok // tm, n_out // tn),
        in_specs=[
            pl.BlockSpec((tm, d), lambda i, j: (i, 0)),
            pl.BlockSpec((1, d), lambda i, j: (0, 0)),
            pl.BlockSpec(w_pad.shape, lambda i, j: (0, 0), pipeline_mode=pl.Buffered(1)),
        ],
        out_specs=[pl.BlockSpec((tm, tn), lambda i, j: (i, j)),
                   pl.BlockSpec((tm, 2 * KV_W), lambda i, j: (i, 0))],
        out_shape=[jax.ShapeDtypeStruct((tok, n_out), BF16),
                   jax.ShapeDtypeStruct((tok, 2 * KV_W), F32)],
        scratch_shapes=[pltpu.VMEM((tm, d), BF16)],
        compiler_params=pltpu.CompilerParams(
            dimension_semantics=("parallel", "arbitrary"), vmem_limit_bytes=VMEM_LIMIT),
        name="in_proj",
    )(x2d, g.reshape(1, d), w_pad)


def _ret_kernel(cd_ref, q_ref, k_ref, v_ref, g_ref, cos_ref, sin_ref, dec_ref, xi_ref, zeta_ref,
                gain_ref, o_ref, state_ref):
    @pl.when(pl.program_id(1) == 0)
    def _():
        state_ref[...] = jnp.zeros_like(state_ref)

    cos = cos_ref[...]
    sin = sin_ref[...]
    half = RET_DK // 2

    def rot(t):
        t1, t2 = t[:, :half], t[:, half:]
        return jnp.concatenate([t1 * cos - t2 * sin, t1 * sin + t2 * cos], axis=-1)

    for h in range(RET_HEADS):
        cs = slice(h * RET_DK, (h + 1) * RET_DK)
        q = rot(q_ref[:, cs].astype(F32))
        k = rot(k_ref[:, cs].astype(F32)) * (RET_DK ** -0.5)
        v = v_ref[:, cs].astype(BF16)
        qb = q.astype(BF16)
        s = _dot_nt(qb, k.astype(BF16)) * dec_ref[h]
        state = state_ref[h]
        y = _dot(s.astype(BF16), v) + _dot((q * xi_ref[h]).astype(BF16), state.astype(BF16))
        state_ref[h] = cd_ref[h] * state + _dot_tn((k * zeta_ref[h]).astype(BF16), v)
        mu = jnp.mean(y, axis=-1, keepdims=True)
        yc = y - mu
        var = jnp.mean(yc * yc, axis=-1, keepdims=True)
        yn = yc * lax.rsqrt(var + EPS) * gain_ref[:, cs]
        gate = g_ref[:, cs].astype(F32)
        o_ref[:, cs] = (gate * _sigmoid(gate) * yn).astype(o_ref.dtype)


def _retention(proj, gain, batch, seq):
    tok = proj.shape[0]
    c = RET_C
    n_chunks = seq // c
    h, dk = RET_HEADS, RET_DK
    half = dk // 2
    inv = np.float32(ROPE_BASE) ** (-np.arange(half, dtype=np.float32) / np.float32(half))
    ang = np.arange(seq, dtype=np.float32)[:, None] * inv[None, :]
    cos, sin = np.cos(ang).astype(np.float32), np.sin(ang).astype(np.float32)
    log_g = np.log(1.0 - 2.0 ** (-5.0 - np.arange(h, dtype=np.float64)))
    idx = np.arange(c, dtype=np.float64)
    diff = idx[:, None] - idx[None, :]
    dec = np.where(diff >= 0.0, np.exp(np.maximum(diff, 0.0)[None] * log_g[:, None, None]), 0.0)
    dec = dec.astype(np.float32)
    xi = np.broadcast_to(np.exp((idx + 1.0)[None] * log_g[:, None])[:, :, None], (h, c, dk)).astype(np.float32)
    zeta = np.broadcast_to(np.exp((c - 1.0 - idx)[None] * log_g[:, None])[:, :, None],
                           (h, c, dk)).astype(np.float32)
    cd = np.exp(c * log_g).astype(np.float32)

    w = h * dk
    row = lambda b, n: (b * n_chunks + n)
    return pl.pallas_call(
        _ret_kernel,
        grid=(batch, n_chunks),
        in_specs=[
            pl.BlockSpec(memory_space=pltpu.SMEM),
            pl.BlockSpec((c, w), lambda b, n: (row(b, n), 0)),
            pl.BlockSpec((c, w), lambda b, n: (row(b, n), 1)),
            pl.BlockSpec((c, w), lambda b, n: (row(b, n), 2)),
            pl.BlockSpec((c, w), lambda b, n: (row(b, n), 3)),
            pl.BlockSpec((c, half), lambda b, n: (n, 0)),
            pl.BlockSpec((c, half), lambda b, n: (n, 0)),
            pl.BlockSpec((h, c, c), lambda b, n: (0, 0, 0)),
            pl.BlockSpec((h, c, dk), lambda b, n: (0, 0, 0)),
            pl.BlockSpec((h, c, dk), lambda b, n: (0, 0, 0)),
            pl.BlockSpec((1, w), lambda b, n: (0, 0)),
        ],
        out_specs=pl.BlockSpec((c, w), lambda b, n: (row(b, n), 0)),
        out_shape=jax.ShapeDtypeStruct((tok, w), BF16),
        scratch_shapes=[pltpu.VMEM((h, dk, RET_DV), F32)],
        compiler_params=pltpu.CompilerParams(
            dimension_semantics=("parallel", "arbitrary"), vmem_limit_bytes=VMEM_LIMIT),
        name="retention",
    )(cd, proj, proj, proj, proj, cos, sin, dec, xi, zeta, gain.reshape(1, w))


def _compress_kernel(k_ref, v_ref, pek_ref, w1k_ref, w1kf_ref, w2k_ref, pev_ref, w1v_ref, w1vf_ref,
                     w2v_ref, ok_ref, ov_ref):
    dh = NSA_DH
    nblk = k_ref.shape[0] // CMP_STRIDE

    def one(x_ref, pe_ref, w1_ref, w1f_ref, w2_ref, o_ref):
        ab = jnp.zeros((nblk, 2 * dh), F32)
        for l in range(CMP_STRIDE):
            rows = x_ref[pl.ds(l, nblk, stride=CMP_STRIDE), :].astype(BF16)
            ab = ab + _dot(rows, w1_ref[l])
        a, b = ab[:, :dh], ab[:, dh:]
        b_next = pltpu.roll(b, nblk - 1, axis=0)
        ridx = lax.broadcasted_iota(jnp.int32, (nblk, dh), 0)
        b_next = jnp.where(ridx < nblk - 1, b_next, 0.0)
        pe_term = _dot(pe_ref[...], w1f_ref[...])[0:1, :]
        hdn = a + b_next + pe_term
        o_ref[0, 0] = _dot((hdn * _sigmoid(hdn)).astype(BF16), w2_ref[...])

    one(k_ref, pek_ref, w1k_ref, w1kf_ref, w2k_ref, ok_ref)
    one(v_ref, pev_ref, w1v_ref, w1vf_ref, w2v_ref, ov_ref)


def _compress(kv, pe_k, w1_k, w2_k, pe_v, w1_v, w2_v, batch, seq):
    dh, g = NSA_DH, NSA_KV_GROUPS
    nblk = seq // CMP_STRIDE
    half = CMP_LEN // 2

    def prep(pe, w1, w2):
        w1b = w1.astype(BF16)
        w1_pair = jnp.concatenate([w1b[:half], w1b[half:]], axis=-1)
        pe_flat = jnp.broadcast_to(pe.reshape(1, CMP_LEN * dh), (8, CMP_LEN * dh)).astype(BF16)
        return pe_flat, w1_pair, w1b.reshape(CMP_LEN * dh, dh), w2.astype(BF16)

    args_k = prep(pe_k, w1_k, w2_k)
    args_v = prep(pe_v, w1_v, w2_v)
    kcol = 0
    vcol = KV_W // dh
    const = lambda shape: pl.BlockSpec(shape, lambda b, gg: (0,) * len(shape))
    wspecs = [const((8, CMP_LEN * dh)), const((half, dh, 2 * dh)), const((CMP_LEN * dh, dh)),
              const((dh, dh))]
    out_spec = pl.BlockSpec((1, 1, nblk, dh), lambda b, gg: (b, gg, 0, 0))
    return pl.pallas_call(
        _compress_kernel,
        grid=(batch, g),
        in_specs=[pl.BlockSpec((seq, dh), lambda b, gg: (b, kcol + gg)),
                  pl.BlockSpec((seq, dh), lambda b, gg: (b, vcol + gg))] + wspecs + wspecs,
        out_specs=[out_spec, out_spec],
        out_shape=[jax.ShapeDtypeStruct((batch, g, nblk, dh), F32)] * 2,
        compiler_params=pltpu.CompilerParams(
            dimension_semantics=("parallel", "parallel"), vmem_limit_bytes=VMEM_LIMIT),
        name="compress",
    )(kv, kv, *args_k, *args_v)


def _cmpattn_kernel(q_ref, kc_ref, vc_ref, tb_ref, pp_ref, ovt_ref, gate_ref, o_ref, sel_ref, rank_ref):
    tq = q_ref.shape[0]
    ncp = kc_ref.shape[2]
    dh = NSA_DH
    s0 = pl.program_id(2) * tq
    hpg = NSA_HPG
    pos_r = s0 + (lax.broadcasted_iota(jnp.int32, (hpg * tq, ncp), 0) & (tq - 1))
    n_c = lax.broadcasted_iota(jnp.int32, (hpg * tq, ncp), 1)
    mask = pos_r >= n_c * CMP_STRIDE + (CMP_LEN - 1)
    kc = kc_ref[0, 0].astype(BF16)
    vc = vc_ref[0, 0].astype(BF16)
    scale = dh ** -0.5
    q = jnp.concatenate([q_ref[:, hh * dh:(hh + 1) * dh] for hh in range(hpg)], axis=0).astype(BF16)
    bias = _dot(tb_ref[...].reshape(hpg * tq, tb_ref.shape[2]), pp_ref[0])
    lm = jnp.where(mask, _dot_nt(q, kc) * scale + bias, NEG)
    m = jnp.max(lm, axis=-1, keepdims=True)
    e = jnp.where(mask, jnp.exp(lm - m), 0.0)
    den = jnp.sum(e, axis=-1, keepdims=True)
    p = e / jnp.where(den > 0.0, den, 1.0)
    o = _dot(p.astype(BF16), vc)
    gate = _sigmoid(gate_ref[0, 0])
    psum = p[0:tq]
    for hh in range(hpg):
        o_ref[:, hh * dh:(hh + 1) * dh] = (o[hh * tq:(hh + 1) * tq] * gate[:, hh:hh + 1]).astype(o_ref.dtype)
        if hh:
            psum = psum + p[hh * tq:(hh + 1) * tq]
    nsel = ovt_ref.shape[0]
    imp_t = _dot_nt(ovt_ref[...], psum.astype(BF16))
    jb = lax.broadcasted_iota(jnp.int32, (nsel, tq), 0)
    pos = s0 + lax.broadcasted_iota(jnp.int32, (nsel, tq), 1)
    cur = jnp.right_shift(pos, SEL_LEN.bit_length() - 1)
    causal = jb * SEL_LEN <= pos
    forced = (jb == 0) | (jb == cur) | (jb == cur - 1)
    score = jnp.where(forced, FORCE, jnp.where(causal, imp_t, NEG))
    per_tile = tq // SEL_LEN
    rank_ref[...] = jnp.zeros_like(rank_ref)
    for c in range(nsel // per_tile):
        @pl.when(c <= pl.program_id(2))
        def _():
            rank = rank_ref[...]
            for kk in range(c * per_tile, (c + 1) * per_tile):
                row = score[kk:kk + 1, :]
                beats = (row > score) | ((row == score) & (jb > kk))
                rank = rank + jnp.where(beats, 1, 0)
            rank_ref[...] = rank
    rank = rank_ref[...]
    selb = jnp.where((rank < SEL_TOPK) & causal, 0.0, NEG)
    selb = jnp.concatenate([selb, jnp.zeros((SEL_PAD - nsel, tq), F32)], axis=0)
    sel_ref[0, 0] = selb.astype(BF16)


def _cmp_attention(proj, k_cmp, v_cmp, rel_bias, gates, batch, seq):
    dh, g, hpg = NSA_DH, NSA_KV_GROUPS, NSA_HPG
    tq = CMP_TQ
    nq = seq // tq
    ncp = k_cmp.shape[2]
    nsel = seq // SEL_LEN
    n_cmp = (seq - CMP_LEN) // CMP_STRIDE + 1
    cmp_idx = np.arange(n_cmp)[:, None] * CMP_STRIDE + np.arange(CMP_LEN)[None, :]
    overlap = ((cmp_idx // SEL_LEN)[:, :, None] == np.arange(nsel)[None, None, :]).sum(1) / CMP_LEN
    ovt = np.zeros((nsel, ncp), np.float32)
    ovt[:, :n_cmp] = overlap.T
    nb = tq // CMP_STRIDE
    assert 2 * nb + 1 <= CMP_BAND and CMP_STRIDE * (nb + 1) - (CMP_LEN - 1) >= REL_MAX_DIST
    i = np.arange(tq)[:, None]
    r = np.arange(CMP_BAND)[None, :]
    rel = np.where(r < 2 * nb, i - CMP_STRIDE * (r - nb) - (CMP_LEN - 1), REL_MAX_DIST)
    tb = jnp.where(jnp.asarray(r <= 2 * nb), _bias_lookup(rel_bias, jnp.asarray(rel, jnp.int32)), 0.0)
    tb_hi = tb.astype(BF16)
    tb_lo = (tb - tb_hi.astype(F32)).astype(BF16)
    tb = jnp.concatenate([tb_hi, tb_lo], axis=-1)
    n = np.arange(ncp)[None, None, :]
    first = (np.arange(nq) * nb - nb)[:, None, None]
    rr = np.arange(CMP_BAND)[None, :, None]
    pp = np.where(rr < 2 * nb, n == first + rr, (rr == 2 * nb) & (n < first)).astype(np.float32)
    pp = np.concatenate([pp, pp], axis=1)
    qcol = IN_OFF[4] // (hpg * dh)
    return pl.pallas_call(
        _cmpattn_kernel,
        grid=(batch, g, nq),
        in_specs=[
            pl.BlockSpec((tq, hpg * dh), lambda b, gg, t: (b * nq + t, qcol + gg)),
            pl.BlockSpec((1, 1, ncp, dh), lambda b, gg, t: (b, gg, 0, 0)),
            pl.BlockSpec((1, 1, ncp, dh), lambda b, gg, t: (b, gg, 0, 0)),
            pl.BlockSpec((hpg, tq, 2 * CMP_BAND), lambda b, gg, t: (gg, 0, 0)),
            pl.BlockSpec((1, 2 * CMP_BAND, ncp), lambda b, gg, t: (t, 0, 0)),
            pl.BlockSpec((nsel, ncp), lambda b, gg, t: (0, 0)),
            pl.BlockSpec((1, 1, tq, GATE_ROWS), lambda b, gg, t: (b, gg, t, 0)),
        ],
        out_specs=[
            pl.BlockSpec((tq, hpg * dh), lambda b, gg, t: (b * nq + t, gg)),
            pl.BlockSpec((1, 1, SEL_PAD, tq), lambda b, gg, t: (b, gg, 0, t)),
        ],
        out_shape=[jax.ShapeDtypeStruct((batch * seq, g * hpg * dh), BF16),
                   jax.ShapeDtypeStruct((batch, g, SEL_PAD, seq), BF16)],
        scratch_shapes=[pltpu.VMEM((nsel, tq), jnp.int32)],
        compiler_params=pltpu.CompilerParams(
            dimension_semantics=("parallel", "parallel", "parallel"), vmem_limit_bytes=VMEM_LIMIT),
        name="cmp_attn",
    )(proj, k_cmp, v_cmp, tb, jnp.asarray(pp, BF16), jnp.asarray(ovt, BF16),
      gates[:, :, 0].transpose(0, 1, 3, 2))


def _band_kernel(qi_ref, ki_ref, var_ref, first_ref, last_ref, q_ref, k_ref, v_ref, fd_ref, gate_ref,
                 *rest, use_sel):
    if use_sel:
        selb_ref, e_ref, o_ref, qt_ref, m_ref, acc_ref, bias_ref = rest
    else:
        o_ref, qt_ref, m_ref, acc_ref, bias_ref = rest
    t = pl.program_id(1)
    dh, hpg, ng = NSA_DH, NSA_HPG, NSA_KV_GROUPS
    tq, tk = q_ref.shape[0], k_ref.shape[0]

    @pl.when((pl.program_id(0) == 0) & (t == 0))
    def _():
        upper = (lax.broadcasted_iota(jnp.int32, (tk, tq), 1) >= lax.broadcasted_iota(jnp.int32, (tk, tq), 0))
        for gg in range(ng):
            for hh in range(hpg):
                cols = slice(hh * tq, (hh + 1) * tq)
                by_dist = jnp.broadcast_to(fd_ref[gg, hh, 0, 0:1, :], (tk, tq))
                cyc = pltpu.roll(by_dist, 0, 1, stride=1, stride_axis=0)
                bias_ref[gg, 0, :, cols] = jnp.where(upper, cyc, NEG)
                if use_sel:
                    far = jnp.broadcast_to(fd_ref[gg, hh, 1, 0:1, :], (tk, tq))
                    bias_ref[gg, 1, :, cols] = jnp.where(upper, far, cyc)
                else:
                    bias_ref[gg, 1, :, cols] = jnp.where(upper, NEG, cyc)

    @pl.when(first_ref[t] == 1)
    def _():
        m_ref[...] = jnp.full_like(m_ref, NEG)
        acc_ref[...] = jnp.zeros_like(acc_ref)
        scale = dh ** -0.5 * LOG2E
        for gg in range(ng):
            for hh in range(hpg):
                h = gg * hpg + hh
                q = q_ref[:, h * dh:(h + 1) * dh].astype(F32)
                qt_ref[gg, 0:dh, hh * tq:(hh + 1) * tq] = (q * scale).T.astype(BF16)
                if use_sel:
                    qt_ref[gg, dh:, hh * tq:(hh + 1) * tq] = selb_ref[0, gg]

    var = var_ref[t]

    def update(far):
        ones = jnp.ones((ACC_PAD, tk), F32)
        scores = []
        for gg in range(ng):
            k = k_ref[:, gg * dh:(gg + 1) * dh]
            if use_sel:
                k = jnp.concatenate([k, e_ref[...]], axis=1)
            s = _dot(k, qt_ref[gg])
            scores.append(s if far else s + bias_ref[gg, var])
        for gg in range(ng):
            vt = jnp.concatenate([v_ref[:, gg * dh:(gg + 1) * dh].astype(F32).T, ones], axis=0).astype(BF16)
            s = scores[gg]
            m_prev = m_ref[gg]
            col_max = jnp.max(s, axis=0, keepdims=True)
            if far:
                c = jnp.concatenate([fd_ref[gg, hh, 1, 0:1, :] for hh in range(hpg)], axis=1)
                m_new = jnp.maximum(m_prev, col_max + c)
                shift = m_new - c
            else:
                m_new = jnp.maximum(m_prev, col_max)
                shift = m_new
            alpha = jnp.exp2(m_prev - m_new)
            p = jnp.exp2(s - shift).astype(BF16)
            acc_ref[gg] = alpha * acc_ref[gg] + _dot(vt, p)
            m_ref[gg] = m_new

    if use_sel:
        pl.when(var == ATT_NVAR - 1)(functools.partial(update, True))
        pl.when(var != ATT_NVAR - 1)(functools.partial(update, False))
    else:
        update(False)

    @pl.when(last_ref[t] == 1)
    def _():
        for gg in range(ng):
            gate = _sigmoid(gate_ref[0, gg, 0])
            for hh in range(hpg):
                h = gg * hpg + hh
                cols = slice(hh * tq, (hh + 1) * tq)
                o = acc_ref[gg, 0:dh, cols] * (gate[hh:hh + 1, :] / acc_ref[gg, dh:dh + 1, cols])
                o_ref[:, h * dh:(h + 1) * dh] = o.T.astype(o_ref.dtype)


def _band_steps(seq, window):
    tq, tk = ATT_T, ATT_TK
    steps = []
    for qi in range(seq // tq):
        k_hi = qi * tq // tk
        k_lo = max(0, (qi * tq - window + 1) // tk) if window else 0
        steps += [(qi, ki, min((qi * tq - ki * tk) // tq, ATT_NVAR - 1)) for ki in range(k_lo, k_hi + 1)]
    return steps


def _band_attention(proj, bias_by_dist, steps, koff, voff, gates, branch, batch, seq, selb=None, e_mat=None):
    dh, g, hpg = NSA_DH, NSA_KV_GROUPS, NSA_HPG
    tq, tk = ATT_T, ATT_TK
    nq, nk = seq // tq, seq // tk
    qi = np.array([s[0] for s in steps], np.int32)
    ki = np.array([s[1] for s in steps], np.int32)
    var = np.array([s[2] for s in steps], np.int32)
    first = np.concatenate([[1], (qi[1:] != qi[:-1]).astype(np.int32)]).astype(np.int32)
    last = np.concatenate([(qi[1:] != qi[:-1]).astype(np.int32), [1]]).astype(np.int32)
    qcol, kcol, vcol = IN_OFF[4] // NSA_W, koff // KV_W, voff // KV_W
    use_sel = selb is not None
    in_specs = [
        pl.BlockSpec((tq, NSA_W), lambda b, t, qi_r, ki_r, v_r, f_r, l_r: (b * nq + qi_r[t], qcol)),
        pl.BlockSpec((tk, KV_W), lambda b, t, qi_r, ki_r, v_r, f_r, l_r: (b * nk + ki_r[t], kcol)),
        pl.BlockSpec((tk, KV_W), lambda b, t, qi_r, ki_r, v_r, f_r, l_r: (b * nk + ki_r[t], vcol)),
        pl.BlockSpec(bias_by_dist.shape, lambda b, t, qi_r, ki_r, v_r, f_r, l_r: (0, 0, 0, 0, 0)),
        pl.BlockSpec((1, g, 1, GATE_ROWS, tq),
                     lambda b, t, qi_r, ki_r, v_r, f_r, l_r: (b, 0, branch, 0, qi_r[t])),
    ]
    args = [proj, proj, proj, bias_by_dist, gates]
    if use_sel:
        in_specs += [
            pl.BlockSpec((1, g, SEL_PAD, tq), lambda b, t, qi_r, ki_r, v_r, f_r, l_r: (b, 0, 0, qi_r[t])),
            pl.BlockSpec((tk, SEL_PAD), lambda b, t, qi_r, ki_r, v_r, f_r, l_r: (ki_r[t], 0)),
        ]
        args += [selb, e_mat]
    kdim = dh + SEL_PAD if use_sel else dh
    grid_spec = pltpu.PrefetchScalarGridSpec(
        num_scalar_prefetch=5,
        grid=(batch, len(steps)),
        in_specs=in_specs,
        out_specs=pl.BlockSpec((tq, NSA_W), lambda b, t, qi_r, ki_r, v_r, f_r, l_r: (b * nq + qi_r[t], 0)),
        scratch_shapes=[pltpu.VMEM((g, kdim, hpg * tq), BF16), pltpu.VMEM((g, 1, hpg * tq), F32),
                        pltpu.VMEM((g, dh + ACC_PAD, hpg * tq), F32),
                        pltpu.VMEM((g, ATT_NVAR - 1, tk, hpg * tq), F32)],
    )
    return pl.pallas_call(
        functools.partial(_band_kernel, use_sel=use_sel),
        grid_spec=grid_spec,
        out_shape=jax.ShapeDtypeStruct((batch * seq, NSA_W), BF16),
        compiler_params=pltpu.CompilerParams(
            dimension_semantics=("arbitrary", "arbitrary"), vmem_limit_bytes=VMEM_LIMIT),
        name="sel_attn" if use_sel else "win_attn",
    )(jnp.asarray(qi), jnp.asarray(ki), jnp.asarray(var), jnp.asarray(first), jnp.asarray(last), *args)


def _out_kernel(x_ref, yr_ref, oc_ref, os_ref, ow_ref, w_ref, o_ref):
    y_nsa = oc_ref[...].astype(F32) + os_ref[...].astype(F32) + ow_ref[...].astype(F32)
    y = jnp.concatenate([yr_ref[...], y_nsa.astype(BF16)], axis=-1)
    o_ref[...] = x_ref[...] + _dot(y, w_ref[...])


def _out_proj(x2d, y_ret, o_cmp, o_sel, o_win, w_out):
    tok, d = x2d.shape
    tm = OUT_TM
    wmix = y_ret.shape[1]
    row = lambda shape: pl.BlockSpec(shape, lambda i: (i, 0))
    return pl.pallas_call(
        _out_kernel,
        grid=(tok // tm,),
        in_specs=[row((tm, d)), row((tm, wmix)), row((tm, wmix)), row((tm, wmix)), row((tm, wmix)),
                  pl.BlockSpec(w_out.shape, lambda i: (0, 0))],
        out_specs=row((tm, d)),
        out_shape=jax.ShapeDtypeStruct((tok, d), F32),
        compiler_params=pltpu.CompilerParams(
            dimension_semantics=("parallel",), vmem_limit_bytes=VMEM_LIMIT),
        name="out_proj",
    )(x2d, y_ret, o_cmp, o_sel, o_win, w_out)


def _t5_bucket_of(rel):
    n = jnp.maximum(rel, 0)
    max_exact = REL_BUCKETS // 2
    nf = jnp.maximum(n, 1).astype(F32)
    large = max_exact + (jnp.log(nf / max_exact) / math.log(REL_MAX_DIST / max_exact)
                         * (REL_BUCKETS - max_exact)).astype(jnp.int32)
    large = jnp.minimum(large, REL_BUCKETS - 1)
    return jnp.where(n < max_exact, n, large)


def _bias_lookup(rel_bias, rel):
    bucket = _t5_bucket_of(rel)[None]
    tab = rel_bias.astype(F32).reshape((rel_bias.shape[0], REL_BUCKETS) + (1,) * rel.ndim)
    out = jnp.zeros((rel_bias.shape[0],) + rel.shape, F32)
    for b in range(REL_BUCKETS):
        out = jnp.where(bucket == b, tab[:, b], out)
    return out


def _bias_by_distance(rel_bias):
    t, g, hpg = ATT_T, NSA_KV_GROUPS, NSA_HPG
    assert ATT_TK == t and t >= REL_MAX_DIST and WIN == t
    by_dist = _bias_lookup(rel_bias, jnp.arange(t, dtype=jnp.int32)) * LOG2E
    far = jnp.broadcast_to(by_dist[:, t - 1:t], by_dist.shape)
    tab = jnp.stack([by_dist, far], axis=1).reshape(g, hpg, 2, 1, t)
    return jnp.broadcast_to(tab, (g, hpg, 2, 8, t))


def _token_mix(x2d, mix_norm, w_in, ret_gn_gain, pe_k, w1_k, w2_k, pe_v, w1_v, w2_v, w_out, rel_bias,
               batch, seq):
    d = x2d.shape[1]
    dh = NSA_DH
    w_pad = _to_bf16_transposed(w_in.T, D_IN_PAD)
    proj, cmp_kv = _inproj(x2d, mix_norm, w_pad)
    y_ret = _retention(proj, ret_gn_gain, batch, seq)

    gates = proj[:, IN_OFF[11]:IN_OFF[11] + IN_SPLITS[11]].astype(F32)
    gates = gates.reshape(batch, seq, NSA_KV_GROUPS, NSA_HPG, N_BRANCH).transpose(0, 2, 4, 3, 1)
    gates = jnp.pad(gates, ((0, 0), (0, 0), (0, 0), (0, GATE_ROWS - NSA_HPG), (0, 0)))

    k_cmp, v_cmp = _compress(cmp_kv, pe_k, w1_k, w2_k, pe_v, w1_v, w2_v, batch, seq)
    o_cmp, selb = _cmp_attention(proj, k_cmp, v_cmp, rel_bias, gates, batch, seq)

    bias_by_dist = _bias_by_distance(rel_bias)
    e_np = np.zeros((seq, SEL_PAD), np.float32)
    e_np[np.arange(seq), np.arange(seq) // SEL_LEN] = 1.0
    o_sel = _band_attention(proj, bias_by_dist, _band_steps(seq, 0),
                            IN_OFF[7], IN_OFF[8], gates, 1, batch, seq,
                            selb=selb, e_mat=jnp.asarray(e_np, BF16))
    o_win = _band_attention(proj, bias_by_dist, _band_steps(seq, WIN),
                            IN_OFF[9], IN_OFF[10], gates, 2, batch, seq)
    return _out_proj(x2d, y_ret, o_cmp, o_sel, o_win, _to_bf16(w_out))


def _cast_t_kernel(x_ref, o_ref, *, valid_rows):
    tr = x_ref.shape[0]
    row = pl.program_id(0) * tr + lax.broadcasted_iota(jnp.int32, x_ref.shape, 0)
    o_ref[...] = jnp.where(row < valid_rows, x_ref[...], 0.0).T.astype(o_ref.dtype)


def _to_bf16_transposed(w_t, cols):
    c, r = w_t.shape
    tr = CAST_T_ROWS
    return pl.pallas_call(
        functools.partial(_cast_t_kernel, valid_rows=c),
        grid=(cols // tr,),
        in_specs=[pl.BlockSpec((tr, r), lambda i: (i, 0))],
        out_specs=pl.BlockSpec((r, tr), lambda i: (0, i)),
        out_shape=jax.ShapeDtypeStruct((r, cols), BF16),
        compiler_params=pltpu.CompilerParams(
            dimension_semantics=("parallel",), vmem_limit_bytes=VMEM_LIMIT),
        name="to_bf16_t",
    )(w_t)


def _cast_kernel(x_ref, o_ref):
    o_ref[...] = x_ref[...].astype(o_ref.dtype)


def _to_bf16(w):
    r, c = w.shape
    tr = 1 << ((CAST_BLOCK_BYTES // (4 * c)).bit_length() - 1)
    while r % tr:
        tr //= 2
    return pl.pallas_call(
        _cast_kernel,
        grid=(r // tr,),
        in_specs=[pl.BlockSpec((tr, c), lambda i: (i, 0))],
        out_specs=pl.BlockSpec((tr, c), lambda i: (i, 0)),
        out_shape=jax.ShapeDtypeStruct((r, c), BF16),
        compiler_params=pltpu.CompilerParams(
            dimension_semantics=("parallel",), vmem_limit_bytes=VMEM_LIMIT),
        name="to_bf16",
    )(w)


def kernel(x, ffn1_norm, ffn1_w1, ffn1_w3, ffn1_w2, mix_norm, w_in, ret_gn_gain, cmp_pe_k, cmp_w1_k,
           cmp_w2_k, cmp_pe_v, cmp_w1_v, cmp_w2_v, w_out, ffn2_norm, ffn2_w1, ffn2_w3, ffn2_w2,
           rel_bias, final_norm):
    batch, seq, d = x.shape
    depth = ffn1_norm.shape[0]
    h = x.reshape(batch * seq, d)
    for l in range(depth):
        last = l == depth - 1
        h = _ffn(h, ffn1_norm[l], _to_bf16(ffn1_w1[l]), _to_bf16(ffn1_w3[l]), _to_bf16(ffn1_w2[l]),
                 final_norm, False)
        h = _token_mix(h, mix_norm[l], w_in[l], ret_gn_gain[l], cmp_pe_k[l], cmp_w1_k[l], cmp_w2_k[l],
                       cmp_pe_v[l], cmp_w1_v[l], cmp_w2_v[l], w_out[l], rel_bias, batch, seq)
        h = _ffn(h, ffn2_norm[l], _to_bf16(ffn2_w1[l]), _to_bf16(ffn2_w3[l]), _to_bf16(ffn2_w2[l]),
                 final_norm, last)
    return h.reshape(batch, seq, d)
```

```python
import functools
import math

import jax
import jax.numpy as jnp
import numpy as np
from jax import lax
from jax.experimental import pallas as pl
from jax.experimental.pallas import tpu as pltpu

F32 = jnp.float32
BF16 = jnp.bfloat16

RET_HEADS = 4
RET_DK = 256
RET_DV = 256
ROPE_BASE = 10000.0
NSA_HEADS = 8
NSA_KV_GROUPS = 2
NSA_HPG = NSA_HEADS // NSA_KV_GROUPS
NSA_DH = 128
CMP_LEN = 32
CMP_STRIDE = 16
SEL_LEN = 64
SEL_TOPK = 16
WIN = 512
N_BRANCH = 3
REL_BUCKETS = 32
REL_MAX_DIST = 128
EPS = 1e-6
NEG = -1e30
FORCE = 1e4

RET_W = RET_HEADS * RET_DV
NSA_W = NSA_HEADS * NSA_DH
KV_W = NSA_KV_GROUPS * NSA_DH
IN_SPLITS = [RET_HEADS * RET_DK, RET_HEADS * RET_DK, RET_W, RET_W, NSA_W,
             KV_W, KV_W, KV_W, KV_W, KV_W, KV_W, NSA_HEADS * N_BRANCH]
D_IN = sum(IN_SPLITS)
IN_OFF = [sum(IN_SPLITS[:i]) for i in range(len(IN_SPLITS))]

LANE = 128
IN_TN = 2304
D_IN_PAD = 3 * IN_TN
CMP_KV_TILE = IN_OFF[5] // IN_TN
CMP_KV_LOCAL = IN_OFF[5] % IN_TN
assert CMP_KV_TILE > 0 and CMP_KV_LOCAL + 2 * KV_W <= IN_TN and IN_OFF[6] == IN_OFF[5] + KV_W
GATE_ROWS = 8
FFN_TM = 1024
FFN_TF = 512
IN_TM = 512
RET_C = 256
CMP_TQ = 512
ATT_T = 512
ATT_TK = 512
ATT_NVAR = 3
OUT_TM = 512
SEL_PAD = 128
ACC_PAD = 8
CMP_BAND = 128
VMEM_LIMIT = 56 * 1024 * 1024
CAST_BLOCK_BYTES = 6 * 1024 * 1024
CAST_T_ROWS = 256
LOG2E = math.log2(math.e)


def _dot(a, b):
    return jnp.dot(a, b, preferred_element_type=F32)


def _dot_nt(a, b):
    return lax.dot_general(a, b, (((1,), (1,)), ((), ())), preferred_element_type=F32)


def _dot_tn(a, b):
    return lax.dot_general(a, b, (((0,), (0,)), ((), ())), preferred_element_type=F32)


def _sigmoid(x):
    return 1.0 / (1.0 + jnp.exp(-x))


def _rms(x, g):
    ms = jnp.mean(x * x, axis=-1, keepdims=True)
    return x * lax.rsqrt(ms + EPS) * g


def _ffn_kernel(x_ref, g_ref, w1_ref, w3_ref, w2_ref, fg_ref, o_ref, n_ref, *, final_norm):
    j = pl.program_id(1)

    last = pl.num_programs(1) - 1

    def step(first, norm_out):
        if first:
            n = _rms(x_ref[...], g_ref[...]).astype(BF16)
            n_ref[...] = n
        else:
            n = n_ref[...]
        a = _dot(n, w1_ref[...])
        b = _dot(n, w3_ref[...])
        h = (0.5 * a * _sigmoid(a) * b).astype(BF16)
        y = (x_ref[...] if first else o_ref[...]) + _dot(h, w2_ref[...])
        o_ref[...] = _rms(y, fg_ref[...]) if norm_out else y

    pl.when(j == 0)(functools.partial(step, True, False))
    if final_norm:
        pl.when((j > 0) & (j < last))(functools.partial(step, False, False))
        pl.when(j == last)(functools.partial(step, False, True))
    else:
        pl.when(j > 0)(functools.partial(step, False, False))


def _ffn(x2d, g, w1, w3, w2, fg, final_norm):
    tok, d = x2d.shape
    dff = w1.shape[1]
    tm, tf = FFN_TM, FFN_TF
    assert dff // tf >= 2
    return pl.pallas_call(
        functools.partial(_ffn_kernel, final_norm=final_norm),
        grid=(tok // tm, dff // tf),
        in_specs=[
            pl.BlockSpec((tm, d), lambda i, j: (i, 0)),
            pl.BlockSpec((1, d), lambda i, j: (0, 0)),
            pl.BlockSpec((d, tf), lambda i, j: (0, j)),
            pl.BlockSpec((d, tf), lambda i, j: (0, j)),
            pl.BlockSpec((tf, d), lambda i, j: (j, 0)),
            pl.BlockSpec((1, d), lambda i, j: (0, 0)),
        ],
        out_specs=pl.BlockSpec((tm, d), lambda i, j: (i, 0)),
        out_shape=jax.ShapeDtypeStruct((tok, d), F32),
        scratch_shapes=[pltpu.VMEM((tm, d), BF16)],
        compiler_params=pltpu.CompilerParams(
            dimension_semantics=("parallel", "arbitrary"), vmem_limit_bytes=VMEM_LIMIT),
        name="ffn",
    )(x2d, g.reshape(1, d), w1, w3, w2, fg.reshape(1, d))


def _inproj_kernel(x_ref, g_ref, w_ref, o_ref, kv_ref, n_ref):
    j = pl.program_id(1)
    tn = o_ref.shape[1]
    for jj in range(w_ref.shape[1] // tn):
        @pl.when(j == jj)
        def _():
            if jj == 0:
                n = _rms(x_ref[...], g_ref[...]).astype(BF16)
                n_ref[...] = n
            else:
                n = n_ref[...]
            res = _dot(n, w_ref[:, jj * tn:(jj + 1) * tn])
            o_ref[...] = res.astype(o_ref.dtype)
            if jj == CMP_KV_TILE:
                kv_ref[...] = res[:, CMP_KV_LOCAL:CMP_KV_LOCAL + 2 * KV_W]


def _inproj(x2d, g, w_pad):
    tok, d = x2d.shape
    n_out = w_pad.shape[1]
    tm, tn = IN_TM, IN_TN
    return pl.pallas_call(
        _inproj_kernel,
        grid=(tok // tm, n_out // tn),
        in_specs=[
            pl.BlockSpec((tm, d), lambda i, j: (i, 0)),
            pl.BlockSpec((1, d), lambda i, j: (0, 0)),
            pl.BlockSpec(w_pad.shape, lambda i, j: (0, 0), pipeline_mode=pl.Buffered(1)),
        ],
        out_specs=[pl.BlockSpec((tm, tn), lambda i, j: (i, j)),
                   pl.BlockSpec((tm, 2 * KV_W), lambda i, j: (i, 0))],
        out_shape=[jax.ShapeDtypeStruct((tok, n_out), BF16),
                   jax.ShapeDtypeStruct((tok, 2 * KV_W), F32)],
        scratch_shapes=[pltpu.VMEM((tm, d), BF16)],
        compiler_params=pltpu.CompilerParams(
            dimension_semantics=("parallel", "arbitrary"), vmem_limit_bytes=VMEM_LIMIT),
        name="in_proj",
    )(x2d, g.reshape(1, d), w_pad)


def _ret_kernel(cd_ref, q_ref, k_ref, v_ref, g_ref, cos_ref, sin_ref, dec_ref, xi_ref, zeta_ref,
                gain_ref, o_ref, state_ref):
    @pl.when(pl.program_id(1) == 0)
    def _():
        state_ref[...] = jnp.zeros_like(state_ref)

    cos = cos_ref[...]
    sin = sin_ref[...]
    half = RET_DK // 2

    def rot(t):
        t1, t2 = t[:, :half], t[:, half:]
        return jnp.concatenate([t1 * cos - t2 * sin, t1 * sin + t2 * cos], axis=-1)

    for h in range(RET_HEADS):
        cs = slice(h * RET_DK, (h + 1) * RET_DK)
        q = rot(q_ref[:, cs].astype(F32))
        k = rot(k_ref[:, cs].astype(F32)) * (RET_DK ** -0.5)
        v = v_ref[:, cs].astype(BF16)
        qb = q.astype(BF16)
        s = _dot_nt(qb, k.astype(BF16)) * dec_ref[h]
        state = state_ref[h]
        y = _dot(s.astype(BF16), v) + _dot((q * xi_ref[h]).astype(BF16), state.astype(BF16))
        state_ref[h] = cd_ref[h] * state + _dot_tn((k * zeta_ref[h]).astype(BF16), v)
        mu = jnp.mean(y, axis=-1, keepdims=True)
        yc = y - mu
        var = jnp.mean(yc * yc, axis=-1, keepdims=True)
        yn = yc * lax.rsqrt(var + EPS) * gain_ref[:, cs]
        gate = g_ref[:, cs].astype(F32)
        o_ref[:, cs] = (gate * _sigmoid(gate) * yn).astype(o_ref.dtype)


def _retention(proj, gain, batch, seq):
    tok = proj.shape[0]
    c = RET_C
    n_chunks = seq // c
    h, dk = RET_HEADS, RET_DK
    half = dk // 2
    inv = np.float32(ROPE_BASE) ** (-np.arange(half, dtype=np.float32) / np.float32(half))
    ang = np.arange(seq, dtype=np.float32)[:, None] * inv[None, :]
    cos, sin = np.cos(ang).astype(np.float32), np.sin(ang).astype(np.float32)
    log_g = np.log(1.0 - 2.0 ** (-5.0 - np.arange(h, dtype=np.float64)))
    idx = np.arange(c, dtype=np.float64)
    diff = idx[:, None] - idx[None, :]
    dec = np.where(diff >= 0.0, np.exp(np.maximum(diff, 0.0)[None] * log_g[:, None, None]), 0.0)
    dec = dec.astype(np.float32)
    xi = np.broadcast_to(np.exp((idx + 1.0)[None] * log_g[:, None])[:, :, None], (h, c, dk)).astype(np.float32)
    zeta = np.broadcast_to(np.exp((c - 1.0 - idx)[None] * log_g[:, None])[:, :, None],
                           (h, c, dk)).astype(np.float32)
    cd = np.exp(c * log_g).astype(np.float32)

    w = h * dk
    row = lambda b, n: (b * n_chunks + n)
    return pl.pallas_call(
        _ret_kernel,
        grid=(batch, n_chunks),
        in_specs=[
            pl.BlockSpec(memory_space=pltpu.SMEM),
            pl.BlockSpec((c, w), lambda b, n: (row(b, n), 0)),
            pl.BlockSpec((c, w), lambda b, n: (row(b, n), 1)),
            pl.BlockSpec((c, w), lambda b, n: (row(b, n), 2)),
            pl.BlockSpec((c, w), lambda b, n: (row(b, n), 3)),
            pl.BlockSpec((c, half), lambda b, n: (n, 0)),
            pl.BlockSpec((c, half), lambda b, n: (n, 0)),
            pl.BlockSpec((h, c, c), lambda b, n: (0, 0, 0)),
            pl.BlockSpec((h, c, dk), lambda b, n: (0, 0, 0)),
            pl.BlockSpec((h, c, dk), lambda b, n: (0, 0, 0)),
            pl.BlockSpec((1, w), lambda b, n: (0, 0)),
        ],
        out_specs=pl.BlockSpec((c, w), lambda b, n: (row(b, n), 0)),
        out_shape=jax.ShapeDtypeStruct((tok, w), BF16),
        scratch_shapes=[pltpu.VMEM((h, dk, RET_DV), F32)],
        compiler_params=pltpu.CompilerParams(
            dimension_semantics=("parallel", "arbitrary"), vmem_limit_bytes=VMEM_LIMIT),
        name="retention",
    )(cd, proj, proj, proj, proj, cos, sin, dec, xi, zeta, gain.reshape(1, w))


def _compress_kernel(k_ref, v_ref, pek_ref, w1k_ref, w1kf_ref, w2k_ref, pev_ref, w1v_ref, w1vf_ref,
                     w2v_ref, ok_ref, ov_ref):
    dh = NSA_DH
    nblk = k_ref.shape[0] // CMP_STRIDE

    def one(x_ref, pe_ref, w1_ref, w1f_ref, w2_ref, o_ref):
        ab = jnp.zeros((nblk, 2 * dh), F32)
        for l in range(CMP_STRIDE):
            rows = x_ref[pl.ds(l, nblk, stride=CMP_STRIDE), :].astype(BF16)
            ab = ab + _dot(rows, w1_ref[l])
        a, b = ab[:, :dh], ab[:, dh:]
        b_next = pltpu.roll(b, nblk - 1, axis=0)
        ridx = lax.broadcasted_iota(jnp.int32, (nblk, dh), 0)
        b_next = jnp.where(ridx < nblk - 1, b_next, 0.0)
        pe_term = _dot(pe_ref[...], w1f_ref[...])[0:1, :]
        hdn = a + b_next + pe_term
        o_ref[0, 0] = _dot((hdn * _sigmoid(hdn)).astype(BF16), w2_ref[...])

    one(k_ref, pek_ref, w1k_ref, w1kf_ref, w2k_ref, ok_ref)
    one(v_ref, pev_ref, w1v_ref, w1vf_ref, w2v_ref, ov_ref)


def _compress(kv, pe_k, w1_k, w2_k, pe_v, w1_v, w2_v, batch, seq):
    dh, g = NSA_DH, NSA_KV_GROUPS
    nblk = seq // CMP_STRIDE
    half = CMP_LEN // 2

    def prep(pe, w1, w2):
        w1b = w1.astype(BF16)
        w1_pair = jnp.concatenate([w1b[:half], w1b[half:]], axis=-1)
        pe_flat = jnp.broadcast_to(pe.reshape(1, CMP_LEN * dh), (8, CMP_LEN * dh)).astype(BF16)
        return pe_flat, w1_pair, w1b.reshape(CMP_LEN * dh, dh), w2.astype(BF16)

    args_k = prep(pe_k, w1_k, w2_k)
    args_v = prep(pe_v, w1_v, w2_v)
    kcol = 0
    vcol = KV_W // dh
    const = lambda shape: pl.BlockSpec(shape, lambda b, gg: (0,) * len(shape))
    wspecs = [const((8, CMP_LEN * dh)), const((half, dh, 2 * dh)), const((CMP_LEN * dh, dh)),
              const((dh, dh))]
    out_spec = pl.BlockSpec((1, 1, nblk, dh), lambda b, gg: (b, gg, 0, 0))
    return pl.pallas_call(
        _compress_kernel,
        grid=(batch, g),
        in_specs=[pl.BlockSpec((seq, dh), lambda b, gg: (b, kcol + gg)),
                  pl.BlockSpec((seq, dh), lambda b, gg: (b, vcol + gg))] + wspecs + wspecs,
        out_specs=[out_spec, out_spec],
        out_shape=[jax.ShapeDtypeStruct((batch, g, nblk, dh), F32)] * 2,
        compiler_params=pltpu.CompilerParams(
            dimension_semantics=("parallel", "parallel"), vmem_limit_bytes=VMEM_LIMIT),
        name="compress",
    )(kv, kv, *args_k, *args_v)


def _cmpattn_kernel(q_ref, kc_ref, vc_ref, tb_ref, pp_ref, ovt_ref, gate_ref, o_ref, sel_ref, rank_ref):
    tq = q_ref.shape[0]
    ncp = kc_ref.shape[2]
    dh = NSA_DH
    s0 = pl.program_id(2) * tq
    hpg = NSA_HPG
    kc = kc_ref[0, 0].astype(BF16)
    vc = vc_ref[0, 0].astype(BF16)
    scale = dh ** -0.5
    q = jnp.concatenate([q_ref[:, hh * dh:(hh + 1) * dh] for hh in range(hpg)], axis=0).astype(BF16)
    bias = _dot(tb_ref[...].reshape(hpg * tq, tb_ref.shape[2]), pp_ref[0])
    lm = _dot_nt(q, kc) * scale + bias
    m = jnp.max(lm, axis=-1, keepdims=True)
    e = jnp.exp(lm - m)
    den = jnp.sum(e, axis=-1, keepdims=True)
    p = e * jnp.where(m > 0.5 * NEG, 1.0 / den, 0.0)
    o = _dot(p.astype(BF16), vc)
    gate = _sigmoid(gate_ref[0, 0])
    psum = p[0:tq]
    for hh in range(hpg):
        o_ref[:, hh * dh:(hh + 1) * dh] = (o[hh * tq:(hh + 1) * tq] * gate[:, hh:hh + 1]).astype(o_ref.dtype)
        if hh:
            psum = psum + p[hh * tq:(hh + 1) * tq]
    nsel = ovt_ref.shape[0]
    imp_t = _dot_nt(ovt_ref[...], psum.astype(BF16))
    jb = lax.broadcasted_iota(jnp.int32, (nsel, tq), 0)
    pos = s0 + lax.broadcasted_iota(jnp.int32, (nsel, tq), 1)
    cur = jnp.right_shift(pos, SEL_LEN.bit_length() - 1)
    causal = jb * SEL_LEN <= pos
    forced = (jb == 0) | (jb == cur) | (jb == cur - 1)
    score = jnp.where(forced, FORCE, jnp.where(causal, imp_t, NEG))
    per_tile = tq // SEL_LEN
    rank_ref[...] = jnp.zeros_like(rank_ref)
    for c in range(nsel // per_tile):
        @pl.when(c <= pl.program_id(2))
        def _():
            rank = rank_ref[...]
            for kk in range(c * per_tile, (c + 1) * per_tile):
                row = score[kk:kk + 1, :]
                beats = (row > score) | ((row == score) & (jb > kk))
                rank = rank + jnp.where(beats, 1, 0)
            rank_ref[...] = rank
    rank = rank_ref[...]
    selb = jnp.where((rank < SEL_TOPK) & causal, 0.0, NEG)
    selb = jnp.concatenate([selb, jnp.zeros((SEL_PAD - nsel, tq), F32)], axis=0)
    sel_ref[0, 0] = selb.astype(BF16)


def _cmp_attention(proj, k_cmp, v_cmp, rel_bias, gates, batch, seq):
    dh, g, hpg = NSA_DH, NSA_KV_GROUPS, NSA_HPG
    tq = CMP_TQ
    nq = seq // tq
    ncp = k_cmp.shape[2]
    nsel = seq // SEL_LEN
    n_cmp = (seq - CMP_LEN) // CMP_STRIDE + 1
    cmp_idx = np.arange(n_cmp)[:, None] * CMP_STRIDE + np.arange(CMP_LEN)[None, :]
    overlap = ((cmp_idx // SEL_LEN)[:, :, None] == np.arange(nsel)[None, None, :]).sum(1) / CMP_LEN
    ovt = np.zeros((nsel, ncp), np.float32)
    ovt[:, :n_cmp] = overlap.T
    nb = tq // CMP_STRIDE
    assert 2 * nb + 2 <= CMP_BAND and CMP_STRIDE * (nb + 1) - (CMP_LEN - 1) >= REL_MAX_DIST
    i = np.arange(tq)[:, None]
    r = np.arange(CMP_BAND)[None, :]
    rel = np.where(r < 2 * nb, i - CMP_STRIDE * (r - nb) - (CMP_LEN - 1), REL_MAX_DIST)
    tb = jnp.where(jnp.asarray(r <= 2 * nb), _bias_lookup(rel_bias, jnp.asarray(rel, jnp.int32)), 0.0)
    tb = jnp.where(jnp.asarray((rel < 0) | (r == 2 * nb + 1)), NEG, tb)
    tb_hi = tb.astype(BF16)
    tb_lo = (tb - tb_hi.astype(F32)).astype(BF16)
    tb = jnp.concatenate([tb_hi, tb_lo], axis=-1)
    n = np.arange(ncp)[None, None, :]
    first = (np.arange(nq) * nb - nb)[:, None, None]
    rr = np.arange(CMP_BAND)[None, :, None]
    pp = np.where(rr < 2 * nb, n == first + rr,
                  ((rr == 2 * nb) & (n < first)) | ((rr == 2 * nb + 1) & (n >= first + 2 * nb)))
    pp = pp.astype(np.float32)
    pp = np.concatenate([pp, pp], axis=1)
    qcol = IN_OFF[4] // (hpg * dh)
    return pl.pallas_call(
        _cmpattn_kernel,
        grid=(batch, g, nq),
        in_specs=[
            pl.BlockSpec((tq, hpg * dh), lambda b, gg, t: (b * nq + t, qcol + gg)),
            pl.BlockSpec((1, 1, ncp, dh), lambda b, gg, t: (b, gg, 0, 0)),
            pl.BlockSpec((1, 1, ncp, dh), lambda b, gg, t: (b, gg, 0, 0)),
            pl.BlockSpec((hpg, tq, 2 * CMP_BAND), lambda b, gg, t: (gg, 0, 0)),
            pl.BlockSpec((1, 2 * CMP_BAND, ncp), lambda b, gg, t: (t, 0, 0)),
            pl.BlockSpec((nsel, ncp), lambda b, gg, t: (0, 0)),
            pl.BlockSpec((1, 1, tq, GATE_ROWS), lambda b, gg, t: (b, gg, t, 0)),
        ],
        out_specs=[
            pl.BlockSpec((tq, hpg * dh), lambda b, gg, t: (b * nq + t, gg)),
            pl.BlockSpec((1, 1, SEL_PAD, tq), lambda b, gg, t: (b, gg, 0, t)),
        ],
        out_shape=[jax.ShapeDtypeStruct((batch * seq, g * hpg * dh), BF16),
                   jax.ShapeDtypeStruct((batch, g, SEL_PAD, seq), BF16)],
        scratch_shapes=[pltpu.VMEM((nsel, tq), jnp.int32)],
        compiler_params=pltpu.CompilerParams(
            dimension_semantics=("parallel", "parallel", "parallel"), vmem_limit_bytes=VMEM_LIMIT),
        name="cmp_attn",
    )(proj, k_cmp, v_cmp, tb, jnp.asarray(pp, BF16), jnp.asarray(ovt, BF16),
      gates[:, :, 0].transpose(0, 1, 3, 2))


def _band_kernel(qi_ref, ki_ref, var_ref, first_ref, last_ref, q_ref, k_ref, v_ref, fd_ref, gate_ref,
                 *rest, use_sel):
    if use_sel:
        selb_ref, e_ref, o_ref, qt_ref, m_ref, acc_ref, bias_ref = rest
    else:
        o_ref, qt_ref, m_ref, acc_ref, bias_ref = rest
    t = pl.program_id(1)
    dh, hpg, ng = NSA_DH, NSA_HPG, NSA_KV_GROUPS
    tq, tk = q_ref.shape[0], k_ref.shape[0]

    @pl.when((pl.program_id(0) == 0) & (t == 0))
    def _():
        upper = (lax.broadcasted_iota(jnp.int32, (tk, tq), 1) >= lax.broadcasted_iota(jnp.int32, (tk, tq), 0))
        for gg in range(ng):
            for hh in range(hpg):
                cols = slice(hh * tq, (hh + 1) * tq)
                by_dist = jnp.broadcast_to(fd_ref[gg, hh, 0, 0:1, :], (tk, tq))
                cyc = pltpu.roll(by_dist, 0, 1, stride=1, stride_axis=0)
                bias_ref[gg, 0, :, cols] = jnp.where(upper, cyc, NEG)
                if use_sel:
                    far = jnp.broadcast_to(fd_ref[gg, hh, 1, 0:1, :], (tk, tq))
                    bias_ref[gg, 1, :, cols] = jnp.where(upper, far, cyc)
                else:
                    bias_ref[gg, 1, :, cols] = jnp.where(upper, NEG, cyc)

    @pl.when(first_ref[t] == 1)
    def _():
        m_ref[...] = jnp.full_like(m_ref, NEG)
        acc_ref[...] = jnp.zeros_like(acc_ref)
        scale = dh ** -0.5 * LOG2E
        for gg in range(ng):
            for hh in range(hpg):
                h = gg * hpg + hh
                q = q_ref[:, h * dh:(h + 1) * dh].astype(F32)
                qt_ref[gg, 0:dh, hh * tq:(hh + 1) * tq] = (q * scale).T.astype(BF16)
                if use_sel:
                    qt_ref[gg, dh:, hh * tq:(hh + 1) * tq] = selb_ref[0, gg]

    var = var_ref[t]

    def update(far):
        ones = jnp.ones((ACC_PAD, tk), F32)
        scores = []
        for gg in range(ng):
            k = k_ref[:, gg * dh:(gg + 1) * dh]
            if use_sel:
                k = jnp.concatenate([k, e_ref[...]], axis=1)
            s = _dot(k, qt_ref[gg])
            scores.append(s if far else s + bias_ref[gg, var])
        for gg in range(ng):
            vt = jnp.concatenate([v_ref[:, gg * dh:(gg + 1) * dh].astype(F32).T, ones], axis=0).astype(BF16)
            s = scores[gg]
            m_prev = m_ref[gg]
            col_max = jnp.max(s, axis=0, keepdims=True)
            if far:
                c = jnp.concatenate([fd_ref[gg, hh, 1, 0:1, :] for hh in range(hpg)], axis=1)
                m_new = jnp.maximum(m_prev, col_max + c)
                shift = m_new - c
            else:
                m_new = jnp.maximum(m_prev, col_max)
                shift = m_new
            alpha = jnp.exp2(m_prev - m_new)
            p = jnp.exp2(s - shift).astype(BF16)
            acc_ref[gg] = alpha * acc_ref[gg] + _dot(vt, p)
            m_ref[gg] = m_new

    if use_sel:
        pl.when(var == ATT_NVAR - 1)(functools.partial(update, True))
        pl.when(var != ATT_NVAR - 1)(functools.partial(update, False))
    else:
        update(False)

    @pl.when(last_ref[t] == 1)
    def _():
        for gg in range(ng):
            gate = _sigmoid(gate_ref[0, gg, 0])
            for hh in range(hpg):
                h = gg * hpg + hh
                cols = slice(hh * tq, (hh + 1) * tq)
                o = acc_ref[gg, 0:dh, cols] * (gate[hh:hh + 1, :] / acc_ref[gg, dh:dh + 1, cols])
                o_ref[:, h * dh:(h + 1) * dh] = o.T.astype(o_ref.dtype)


def _band_steps(seq, window):
    tq, tk = ATT_T, ATT_TK
    steps = []
    for qi in range(seq // tq):
        k_hi = qi * tq // tk
        k_lo = max(0, (qi * tq - window + 1) // tk) if window else 0
        steps += [(qi, ki, min((qi * tq - ki * tk) // tq, ATT_NVAR - 1)) for ki in range(k_lo, k_hi + 1)]
    return steps


def _band_attention(proj, bias_by_dist, steps, koff, voff, gates, branch, batch, seq, selb=None, e_mat=None):
    dh, g, hpg = NSA_DH, NSA_KV_GROUPS, NSA_HPG
    tq, tk = ATT_T, ATT_TK
    nq, nk = seq // tq, seq // tk
    qi = np.array([s[0] for s in steps], np.int32)
    ki = np.array([s[1] for s in steps], np.int32)
    var = np.array([s[2] for s in steps], np.int32)
    first = np.concatenate([[1], (qi[1:] != qi[:-1]).astype(np.int32)]).astype(np.int32)
    last = np.concatenate([(qi[1:] != qi[:-1]).astype(np.int32), [1]]).astype(np.int32)
    qcol, kcol, vcol = IN_OFF[4] // NSA_W, koff // KV_W, voff // KV_W
    use_sel = selb is not None
    in_specs = [
        pl.BlockSpec((tq, NSA_W), lambda b, t, qi_r, ki_r, v_r, f_r, l_r: (b * nq + qi_r[t], qcol)),
        pl.BlockSpec((tk, KV_W), lambda b, t, qi_r, ki_r, v_r, f_r, l_r: (b * nk + ki_r[t], kcol)),
        pl.BlockSpec((tk, KV_W), lambda b, t, qi_r, ki_r, v_r, f_r, l_r: (b * nk + ki_r[t], vcol)),
        pl.BlockSpec(bias_by_dist.shape, lambda b, t, qi_r, ki_r, v_r, f_r, l_r: (0, 0, 0, 0, 0)),
        pl.BlockSpec((1, g, 1, GATE_ROWS, tq),
                     lambda b, t, qi_r, ki_r, v_r, f_r, l_r: (b, 0, branch, 0, qi_r[t])),
    ]
    args = [proj, proj, proj, bias_by_dist, gates]
    if use_sel:
        in_specs += [
            pl.BlockSpec((1, g, SEL_PAD, tq), lambda b, t, qi_r, ki_r, v_r, f_r, l_r: (b, 0, 0, qi_r[t])),
            pl.BlockSpec((tk, SEL_PAD), lambda b, t, qi_r, ki_r, v_r, f_r, l_r: (ki_r[t], 0)),
        ]
        args += [selb, e_mat]
    kdim = dh + SEL_PAD if use_sel else dh
    grid_spec = pltpu.PrefetchScalarGridSpec(
        num_scalar_prefetch=5,
        grid=(batch, len(steps)),
        in_specs=in_specs,
        out_specs=pl.BlockSpec((tq, NSA_W), lambda b, t, qi_r, ki_r, v_r, f_r, l_r: (b * nq + qi_r[t], 0)),
        scratch_shapes=[pltpu.VMEM((g, kdim, hpg * tq), BF16), pltpu.VMEM((g, 1, hpg * tq), F32),
                        pltpu.VMEM((g, dh + ACC_PAD, hpg * tq), F32),
                        pltpu.VMEM((g, ATT_NVAR - 1, tk, hpg * tq), F32)],
    )
    return pl.pallas_call(
        functools.partial(_band_kernel, use_sel=use_sel),
        grid_spec=grid_spec,
        out_shape=jax.ShapeDtypeStruct((batch * seq, NSA_W), BF16),
        compiler_params=pltpu.CompilerParams(
            dimension_semantics=("arbitrary", "arbitrary"), vmem_limit_bytes=VMEM_LIMIT),
        name="sel_attn" if use_sel else "win_attn",
    )(jnp.asarray(qi), jnp.asarray(ki), jnp.asarray(var), jnp.asarray(first), jnp.asarray(last), *args)


def _out_kernel(x_ref, yr_ref, oc_ref, os_ref, ow_ref, w_ref, o_ref):
    y_nsa = oc_ref[...].astype(F32) + os_ref[...].astype(F32) + ow_ref[...].astype(F32)
    y = jnp.concatenate([yr_ref[...], y_nsa.astype(BF16)], axis=-1)
    o_ref[...] = x_ref[...] + _dot(y, w_ref[...])


def _out_proj(x2d, y_ret, o_cmp, o_sel, o_win, w_out):
    tok, d = x2d.shape
    tm = OUT_TM
    wmix = y_ret.shape[1]
    row = lambda shape: pl.BlockSpec(shape, lambda i: (i, 0))
    return pl.pallas_call(
        _out_kernel,
        grid=(tok // tm,),
        in_specs=[row((tm, d)), row((tm, wmix)), row((tm, wmix)), row((tm, wmix)), row((tm, wmix)),
                  pl.BlockSpec(w_out.shape, lambda i: (0, 0))],
        out_specs=row((tm, d)),
        out_shape=jax.ShapeDtypeStruct((tok, d), F32),
        compiler_params=pltpu.CompilerParams(
            dimension_semantics=("parallel",), vmem_limit_bytes=VMEM_LIMIT),
        name="out_proj",
    )(x2d, y_ret, o_cmp, o_sel, o_win, w_out)


def _t5_bucket_of(rel):
    n = jnp.maximum(rel, 0)
    max_exact = REL_BUCKETS // 2
    nf = jnp.maximum(n, 1).astype(F32)
    large = max_exact + (jnp.log(nf / max_exact) / math.log(REL_MAX_DIST / max_exact)
                         * (REL_BUCKETS - max_exact)).astype(jnp.int32)
    large = jnp.minimum(large, REL_BUCKETS - 1)
    return jnp.where(n < max_exact, n, large)


def _bias_lookup(rel_bias, rel):
    bucket = _t5_bucket_of(rel)[None]
    tab = rel_bias.astype(F32).reshape((rel_bias.shape[0], REL_BUCKETS) + (1,) * rel.ndim)
    out = jnp.zeros((rel_bias.shape[0],) + rel.shape, F32)
    for b in range(REL_BUCKETS):
        out = jnp.where(bucket == b, tab[:, b], out)
    return out


def _bias_by_distance(rel_bias):
    t, g, hpg = ATT_T, NSA_KV_GROUPS, NSA_HPG
    assert ATT_TK == t and t >= REL_MAX_DIST and WIN == t
    by_dist = _bias_lookup(rel_bias, jnp.arange(t, dtype=jnp.int32)) * LOG2E
    far = jnp.broadcast_to(by_dist[:, t - 1:t], by_dist.shape)
    tab = jnp.stack([by_dist, far], axis=1).reshape(g, hpg, 2, 1, t)
    return jnp.broadcast_to(tab, (g, hpg, 2, 8, t))


def _token_mix(x2d, mix_norm, w_in, ret_gn_gain, pe_k, w1_k, w2_k, pe_v, w1_v, w2_v, w_out, rel_bias,
               batch, seq):
    d = x2d.shape[1]
    dh = NSA_DH
    w_pad = _to_bf16_transposed(w_in.T, D_IN_PAD)
    proj, cmp_kv = _inproj(x2d, mix_norm, w_pad)
    y_ret = _retention(proj, ret_gn_gain, batch, seq)

    gates = proj[:, IN_OFF[11]:IN_OFF[11] + IN_SPLITS[11]].astype(F32)
    gates = gates.reshape(batch, seq, NSA_KV_GROUPS, NSA_HPG, N_BRANCH).transpose(0, 2, 4, 3, 1)
    gates = jnp.pad(gates, ((0, 0), (0, 0), (0, 0), (0, GATE_ROWS - NSA_HPG), (0, 0)))

    k_cmp, v_cmp = _compress(cmp_kv, pe_k, w1_k, w2_k, pe_v, w1_v, w2_v, batch, seq)
    o_cmp, selb = _cmp_attention(proj, k_cmp, v_cmp, rel_bias, gates, batch, seq)

    bias_by_dist = _bias_by_distance(rel_bias)
    e_np = np.zeros((seq, SEL_PAD), np.float32)
    e_np[np.arange(seq), np.arange(seq) // SEL_LEN] = 1.0
    o_sel = _band_attention(proj, bias_by_dist, _band_steps(seq, 0),
                            IN_OFF[7], IN_OFF[8], gates, 1, batch, seq,
                            selb=selb, e_mat=jnp.asarray(e_np, BF16))
    o_win = _band_attention(proj, bias_by_dist, _band_steps(seq, WIN),
                            IN_OFF[9], IN_OFF[10], gates, 2, batch, seq)
    return _out_proj(x2d, y_ret, o_cmp, o_sel, o_win, _to_bf16(w_out))


def _cast_t_kernel(x_ref, o_ref, *, valid_rows):
    tr = x_ref.shape[0]
    row = pl.program_id(0) * tr + lax.broadcasted_iota(jnp.int32, x_ref.shape, 0)
    o_ref[...] = jnp.where(row < valid_rows, x_ref[...], 0.0).T.astype(o_ref.dtype)


def _to_bf16_transposed(w_t, cols):
    c, r = w_t.shape
    tr = CAST_T_ROWS
    return pl.pallas_call(
        functools.partial(_cast_t_kernel, valid_rows=c),
        grid=(cols // tr,),
        in_specs=[pl.BlockSpec((tr, r), lambda i: (i, 0))],
        out_specs=pl.BlockSpec((r, tr), lambda i: (0, i)),
        out_shape=jax.ShapeDtypeStruct((r, cols), BF16),
        compiler_params=pltpu.CompilerParams(
            dimension_semantics=("parallel",), vmem_limit_bytes=VMEM_LIMIT),
        name="to_bf16_t",
    )(w_t)


def _cast_kernel(x_ref, o_ref):
    o_ref[...] = x_ref[...].astype(o_ref.dtype)


def _to_bf16(w):
    r, c = w.shape
    tr = 1 << ((CAST_BLOCK_BYTES // (4 * c)).bit_length() - 1)
    while r % tr:
        tr //= 2
    return pl.pallas_call(
        _cast_kernel,
        grid=(r // tr,),
        in_specs=[pl.BlockSpec((tr, c), lambda i: (i, 0))],
        out_specs=pl.BlockSpec((tr, c), lambda i: (i, 0)),
        out_shape=jax.ShapeDtypeStruct((r, c), BF16),
        compiler_params=pltpu.CompilerParams(
            dimension_semantics=("parallel",), vmem_limit_bytes=VMEM_LIMIT),
        name="to_bf16",
    )(w)


def kernel(x, ffn1_norm, ffn1_w1, ffn1_w3, ffn1_w2, mix_norm, w_in, ret_gn_gain, cmp_pe_k, cmp_w1_k,
           cmp_w2_k, cmp_pe_v, cmp_w1_v, cmp_w2_v, w_out, ffn2_norm, ffn2_w1, ffn2_w3, ffn2_w2,
           rel_bias, final_norm):
    batch, seq, d = x.shape
    depth = ffn1_norm.shape[0]
    h = x.reshape(batch * seq, d)
    for l in range(depth):
        last = l == depth - 1
        h = _ffn(h, ffn1_norm[l], _to_bf16(ffn1_w1[l]), _to_bf16(ffn1_w3[l]), _to_bf16(ffn1_w2[l]),
                 final_norm, False)
        h = _token_mix(h, mix_norm[l], w_in[l], ret_gn_gain[l], cmp_pe_k[l], cmp_w1_k[l], cmp_w2_k[l],
                       cmp_pe_v[l], cmp_w1_v[l], cmp_w2_v[l], w_out[l], rel_bias, batch, seq)
        h = _ffn(h, ffn2_norm[l], _to_bf16(ffn2_w1[l]), _to_bf16(ffn2_w3[l]), _to_bf16(ffn2_w2[l]),
                 final_norm, last)
    return h.reshape(batch, seq, d)
```

```python
import functools
import math

import jax
import jax.numpy as jnp
import numpy as np
from jax import lax
from jax.experimental import pallas as pl
from jax.experimental.pallas import tpu as pltpu

F32 = jnp.float32
BF16 = jnp.bfloat16

RET_HEADS = 4
RET_DK = 256
RET_DV = 256
ROPE_BASE = 10000.0
NSA_HEADS = 8
NSA_KV_GROUPS = 2
NSA_HPG = NSA_HEADS // NSA_KV_GROUPS
NSA_DH = 128
CMP_LEN = 32
CMP_STRIDE = 16
SEL_LEN = 64
SEL_TOPK = 16
WIN = 512
N_BRANCH = 3
REL_BUCKETS = 32
REL_MAX_DIST = 128
EPS = 1e-6
NEG = -1e30
FORCE = 1e4

RET_W = RET_HEADS * RET_DV
NSA_W = NSA_HEADS * NSA_DH
KV_W = NSA_KV_GROUPS * NSA_DH
IN_SPLITS = [RET_HEADS * RET_DK, RET_HEADS * RET_DK, RET_W, RET_W, NSA_W,
             KV_W, KV_W, KV_W, KV_W, KV_W, KV_W, NSA_HEADS * N_BRANCH]
D_IN = sum(IN_SPLITS)
IN_OFF = [sum(IN_SPLITS[:i]) for i in range(len(IN_SPLITS))]

LANE = 128
IN_TN = 2304
D_IN_PAD = 3 * IN_TN
CMP_KV_TILE = IN_OFF[5] // IN_TN
CMP_KV_LOCAL = IN_OFF[5] % IN_TN
assert CMP_KV_TILE > 0 and CMP_KV_LOCAL + 2 * KV_W <= IN_TN and IN_OFF[6] == IN_OFF[5] + KV_W
GATE_ROWS = 8
FFN_TM = 1024
FFN_TF = 512
IN_TM = 512
RET_C = 256
CMP_TQ = 512
ATT_T = 512
ATT_TK = 512
ATT_NVAR = 3
OUT_TM = 512
SEL_PAD = 128
ACC_PAD = 8
CMP_BAND = 128
VMEM_LIMIT = 56 * 1024 * 1024
CAST_BLOCK_BYTES = 6 * 1024 * 1024
CAST_T_ROWS = 256
LOG2E = math.log2(math.e)


def _dot(a, b):
    return jnp.dot(a, b, preferred_element_type=F32)


def _dot_nt(a, b):
    return lax.dot_general(a, b, (((1,), (1,)), ((), ())), preferred_element_type=F32)


def _dot_tn(a, b):
    return lax.dot_general(a, b, (((0,), (0,)), ((), ())), preferred_element_type=F32)


def _sigmoid(x):
    return 1.0 / (1.0 + jnp.exp(-x))


def _rms(x, g):
    ms = jnp.mean(x * x, axis=-1, keepdims=True)
    return x * lax.rsqrt(ms + EPS) * g


def _ffn_kernel(x_ref, g_ref, w1_ref, w3_ref, w2_ref, fg_ref, o_ref, n_ref, *, final_norm):
    j = pl.program_id(1)

    last = pl.num_programs(1) - 1

    def step(first, norm_out):
        if first:
            n = _rms(x_ref[...], g_ref[...]).astype(BF16)
            n_ref[...] = n
        else:
            n = n_ref[...]
        a = _dot(n, w1_ref[...])
        b = _dot(n, w3_ref[...])
        h = (0.5 * a * _sigmoid(a) * b).astype(BF16)
        y = (x_ref[...] if first else o_ref[...]) + _dot(h, w2_ref[...])
        o_ref[...] = _rms(y, fg_ref[...]) if norm_out else y

    pl.when(j == 0)(functools.partial(step, True, False))
    if final_norm:
        pl.when((j > 0) & (j < last))(functools.partial(step, False, False))
        pl.when(j == last)(functools.partial(step, False, True))
    else:
        pl.when(j > 0)(functools.partial(step, False, False))


def _ffn(x2d, g, w1, w3, w2, fg, final_norm):
    tok, d = x2d.shape
    dff = w1.shape[1]
    tm, tf = FFN_TM, FFN_TF
    assert dff // tf >= 2
    return pl.pallas_call(
        functools.partial(_ffn_kernel, final_norm=final_norm),
        grid=(tok // tm, dff // tf),
        in_specs=[
            pl.BlockSpec((tm, d), lambda i, j: (i, 0)),
            pl.BlockSpec((1, d), lambda i, j: (0, 0)),
            pl.BlockSpec((d, tf), lambda i, j: (0, j)),
            pl.BlockSpec((d, tf), lambda i, j: (0, j)),
            pl.BlockSpec((tf, d), lambda i, j: (j, 0)),
            pl.BlockSpec((1, d), lambda i, j: (0, 0)),
        ],
        out_specs=pl.BlockSpec((tm, d), lambda i, j: (i, 0)),
        out_shape=jax.ShapeDtypeStruct((tok, d), F32),
        scratch_shapes=[pltpu.VMEM((tm, d), BF16)],
        compiler_params=pltpu.CompilerParams(
            dimension_semantics=("parallel", "arbitrary"), vmem_limit_bytes=VMEM_LIMIT),
        name="ffn",
    )(x2d, g.reshape(1, d), w1, w3, w2, fg.reshape(1, d))


def _inproj_kernel(x_ref, g_ref, w_ref, o_ref, kv_ref, n_ref):
    j = pl.program_id(1)
    tn = o_ref.shape[1]
    for jj in range(w_ref.shape[1] // tn):
        @pl.when(j == jj)
        def _():
            if jj == 0:
                n = _rms(x_ref[...], g_ref[...]).astype(BF16)
                n_ref[...] = n
            else:
                n = n_ref[...]
            res = _dot(n, w_ref[:, jj * tn:(jj + 1) * tn])
            o_ref[...] = res.astype(o_ref.dtype)
            if jj == CMP_KV_TILE:
                kv_ref[...] = res[:, CMP_KV_LOCAL:CMP_KV_LOCAL + 2 * KV_W]


def _inproj(x2d, g, w_pad):
    tok, d = x2d.shape
    n_out = w_pad.shape[1]
    tm, tn = IN_TM, IN_TN
    return pl.pallas_call(
        _inproj_kernel,
        grid=(tok // tm, n_out // tn),
        in_specs=[
            pl.BlockSpec((tm, d), lambda i, j: (i, 0)),
            pl.BlockSpec((1, d), lambda i, j: (0, 0)),
            pl.BlockSpec(w_pad.shape, lambda i, j: (0, 0), pipeline_mode=pl.Buffered(1)),
        ],
        out_specs=[pl.BlockSpec((tm, tn), lambda i, j: (i, j)),
                   pl.BlockSpec((tm, 2 * KV_W), lambda i, j: (i, 0))],
        out_shape=[jax.ShapeDtypeStruct((tok, n_out), BF16),
                   jax.ShapeDtypeStruct((tok, 2 * KV_W), F32)],
        scratch_shapes=[pltpu.VMEM((tm, d), BF16)],
        compiler_params=pltpu.CompilerParams(
            dimension_semantics=("parallel", "arbitrary"), vmem_limit_bytes=VMEM_LIMIT),
        name="in_proj",
    )(x2d, g.reshape(1, d), w_pad)


def _ret_kernel(cd_ref, q_ref, k_ref, v_ref, g_ref, cos_ref, sin_ref, dec_ref, xi_ref, zeta_ref,
                gain_ref, o_ref, state_ref):
    @pl.when(pl.program_id(0) == 0)
    def _():
        state_ref[...] = jnp.zeros_like(state_ref)

    cos = cos_ref[...]
    sin = sin_ref[...]
    half = RET_DK // 2

    def rot(t):
        t1, t2 = t[:, :half], t[:, half:]
        return jnp.concatenate([t1 * cos - t2 * sin, t1 * sin + t2 * cos], axis=-1)

    for bi, h in [(bi, h) for bi in range(q_ref.shape[0]) for h in range(RET_HEADS)]:
        cs = slice(h * RET_DK, (h + 1) * RET_DK)
        q = rot(q_ref[bi, :, cs].astype(F32))
        k = rot(k_ref[bi, :, cs].astype(F32)) * (RET_DK ** -0.5)
        v = v_ref[bi, :, cs].astype(BF16)
        qb = q.astype(BF16)
        s = _dot_nt(qb, k.astype(BF16)) * dec_ref[h]
        state = state_ref[bi, h]
        y = _dot(s.astype(BF16), v) + _dot((q * xi_ref[h]).astype(BF16), state.astype(BF16))
        state_ref[bi, h] = cd_ref[h] * state + _dot_tn((k * zeta_ref[h]).astype(BF16), v)
        mu = jnp.mean(y, axis=-1, keepdims=True)
        yc = y - mu
        var = jnp.mean(yc * yc, axis=-1, keepdims=True)
        yn = yc * lax.rsqrt(var + EPS) * gain_ref[:, cs]
        gate = g_ref[bi, :, cs].astype(F32)
        o_ref[bi, :, cs] = (gate * _sigmoid(gate) * yn).astype(o_ref.dtype)


def _retention(proj, gain, batch, seq):
    tok = proj.shape[0]
    c = RET_C
    n_chunks = seq // c
    h, dk = RET_HEADS, RET_DK
    half = dk // 2
    inv = np.float32(ROPE_BASE) ** (-np.arange(half, dtype=np.float32) / np.float32(half))
    ang = np.arange(seq, dtype=np.float32)[:, None] * inv[None, :]
    cos, sin = np.cos(ang).astype(np.float32), np.sin(ang).astype(np.float32)
    log_g = np.log(1.0 - 2.0 ** (-5.0 - np.arange(h, dtype=np.float64)))
    idx = np.arange(c, dtype=np.float64)
    diff = idx[:, None] - idx[None, :]
    dec = np.where(diff >= 0.0, np.exp(np.maximum(diff, 0.0)[None] * log_g[:, None, None]), 0.0)
    dec = dec.astype(np.float32)
    xi = np.broadcast_to(np.exp((idx + 1.0)[None] * log_g[:, None])[:, :, None], (h, c, dk)).astype(np.float32)
    zeta = np.broadcast_to(np.exp((c - 1.0 - idx)[None] * log_g[:, None])[:, :, None],
                           (h, c, dk)).astype(np.float32)
    cd = np.exp(c * log_g).astype(np.float32)

    w = h * dk
    proj3 = proj.reshape(batch, seq, proj.shape[1])
    col = lambda j: pl.BlockSpec((batch, c, w), lambda n: (0, n, j))
    y = pl.pallas_call(
        _ret_kernel,
        grid=(n_chunks,),
        in_specs=[
            pl.BlockSpec(memory_space=pltpu.SMEM),
            col(0), col(1), col(2), col(3),
            pl.BlockSpec((c, half), lambda n: (n, 0)),
            pl.BlockSpec((c, half), lambda n: (n, 0)),
            pl.BlockSpec((h, c, c), lambda n: (0, 0, 0)),
            pl.BlockSpec((h, c, dk), lambda n: (0, 0, 0)),
            pl.BlockSpec((h, c, dk), lambda n: (0, 0, 0)),
            pl.BlockSpec((1, w), lambda n: (0, 0)),
        ],
        out_specs=pl.BlockSpec((batch, c, w), lambda n: (0, n, 0)),
        out_shape=jax.ShapeDtypeStruct((batch, seq, w), BF16),
        scratch_shapes=[pltpu.VMEM((batch, h, dk, RET_DV), F32)],
        compiler_params=pltpu.CompilerParams(
            dimension_semantics=("arbitrary",), vmem_limit_bytes=VMEM_LIMIT),
        name="retention",
    )(cd, proj3, proj3, proj3, proj3, cos, sin, dec, xi, zeta, gain.reshape(1, w))
    return y.reshape(tok, w)


def _compress_kernel(k_ref, v_ref, pek_ref, w1k_ref, w1kf_ref, w2k_ref, pev_ref, w1v_ref, w1vf_ref,
                     w2v_ref, ok_ref, ov_ref):
    dh = NSA_DH
    nblk = k_ref.shape[0] // CMP_STRIDE

    def one(x_ref, pe_ref, w1_ref, w1f_ref, w2_ref, o_ref):
        ab = jnp.zeros((nblk, 2 * dh), F32)
        for l in range(CMP_STRIDE):
            rows = x_ref[pl.ds(l, nblk, stride=CMP_STRIDE), :].astype(BF16)
            ab = ab + _dot(rows, w1_ref[l])
        a, b = ab[:, :dh], ab[:, dh:]
        b_next = pltpu.roll(b, nblk - 1, axis=0)
        ridx = lax.broadcasted_iota(jnp.int32, (nblk, dh), 0)
        b_next = jnp.where(ridx < nblk - 1, b_next, 0.0)
        pe_term = _dot(pe_ref[...], w1f_ref[...])[0:1, :]
        hdn = a + b_next + pe_term
        o_ref[0, 0] = _dot((hdn * _sigmoid(hdn)).astype(BF16), w2_ref[...])

    one(k_ref, pek_ref, w1k_ref, w1kf_ref, w2k_ref, ok_ref)
    one(v_ref, pev_ref, w1v_ref, w1vf_ref, w2v_ref, ov_ref)


def _compress(kv, pe_k, w1_k, w2_k, pe_v, w1_v, w2_v, batch, seq):
    dh, g = NSA_DH, NSA_KV_GROUPS
    nblk = seq // CMP_STRIDE
    half = CMP_LEN // 2

    def prep(pe, w1, w2):
        w1b = w1.astype(BF16)
        w1_pair = jnp.concatenate([w1b[:half], w1b[half:]], axis=-1)
        pe_flat = jnp.broadcast_to(pe.reshape(1, CMP_LEN * dh), (8, CMP_LEN * dh)).astype(BF16)
        return pe_flat, w1_pair, w1b.reshape(CMP_LEN * dh, dh), w2.astype(BF16)

    args_k = prep(pe_k, w1_k, w2_k)
    args_v = prep(pe_v, w1_v, w2_v)
    kcol = 0
    vcol = KV_W // dh
    const = lambda shape: pl.BlockSpec(shape, lambda b, gg: (0,) * len(shape))
    wspecs = [const((8, CMP_LEN * dh)), const((half, dh, 2 * dh)), const((CMP_LEN * dh, dh)),
              const((dh, dh))]
    out_spec = pl.BlockSpec((1, 1, nblk, dh), lambda b, gg: (b, gg, 0, 0))
    return pl.pallas_call(
        _compress_kernel,
        grid=(batch, g),
        in_specs=[pl.BlockSpec((seq, dh), lambda b, gg: (b, kcol + gg)),
                  pl.BlockSpec((seq, dh), lambda b, gg: (b, vcol + gg))] + wspecs + wspecs,
        out_specs=[out_spec, out_spec],
        out_shape=[jax.ShapeDtypeStruct((batch, g, nblk, dh), F32)] * 2,
        compiler_params=pltpu.CompilerParams(
            dimension_semantics=("parallel", "parallel"), vmem_limit_bytes=VMEM_LIMIT),
        name="compress",
    )(kv, kv, *args_k, *args_v)


def _cmpattn_kernel(q_ref, kc_ref, vc_ref, tb_ref, pp_ref, ovt_ref, gate_ref, o_ref, sel_ref, rank_ref):
    tq = q_ref.shape[0]
    ncp = kc_ref.shape[2]
    dh = NSA_DH
    s0 = pl.program_id(2) * tq
    hpg = NSA_HPG
    kc = kc_ref[0, 0].astype(BF16)
    vc = vc_ref[0, 0].astype(BF16)
    scale = dh ** -0.5
    q = jnp.concatenate([q_ref[:, hh * dh:(hh + 1) * dh] for hh in range(hpg)], axis=0).astype(BF16)
    bias = _dot(tb_ref[...].reshape(hpg * tq, tb_ref.shape[2]), pp_ref[0])
    lm = _dot_nt(q, kc) * scale + bias
    m = jnp.max(lm, axis=-1, keepdims=True)
    e = jnp.exp(lm - m)
    den = jnp.sum(e, axis=-1, keepdims=True)
    p = e * jnp.where(m > 0.5 * NEG, 1.0 / den, 0.0)
    o = _dot(p.astype(BF16), vc)
    gate = _sigmoid(gate_ref[0, 0])
    psum = p[0:tq]
    for hh in range(hpg):
        o_ref[:, hh * dh:(hh + 1) * dh] = (o[hh * tq:(hh + 1) * tq] * gate[:, hh:hh + 1]).astype(o_ref.dtype)
        if hh:
            psum = psum + p[hh * tq:(hh + 1) * tq]
    nsel = ovt_ref.shape[0]
    imp_t = _dot_nt(ovt_ref[...], psum.astype(BF16))
    jb = lax.broadcasted_iota(jnp.int32, (nsel, tq), 0)
    pos = s0 + lax.broadcasted_iota(jnp.int32, (nsel, tq), 1)
    cur = jnp.right_shift(pos, SEL_LEN.bit_length() - 1)
    causal = jb * SEL_LEN <= pos
    forced = (jb == 0) | (jb == cur) | (jb == cur - 1)
    score = jnp.where(forced, FORCE, jnp.where(causal, imp_t, NEG))
    per_tile = tq // SEL_LEN
    rank_ref[...] = jnp.zeros_like(rank_ref)
    for c in range(nsel // per_tile):
        @pl.when(c <= pl.program_id(2))
        def _():
            rank = rank_ref[...]
            for kk in range(c * per_tile, (c + 1) * per_tile):
                row = score[kk:kk + 1, :]
                beats = (row > score) | ((row == score) & (jb > kk))
                rank = rank + jnp.where(beats, 1, 0)
            rank_ref[...] = rank
    rank = rank_ref[...]
    selb = jnp.where((rank < SEL_TOPK) & causal, 0.0, NEG)
    selb = jnp.concatenate([selb, jnp.zeros((SEL_PAD - nsel, tq), F32)], axis=0)
    sel_ref[0, 0] = selb.astype(BF16)


def _cmp_attention(proj, k_cmp, v_cmp, rel_bias, gates, batch, seq):
    dh, g, hpg = NSA_DH, NSA_KV_GROUPS, NSA_HPG
    tq = CMP_TQ
    nq = seq // tq
    ncp = k_cmp.shape[2]
    nsel = seq // SEL_LEN
    n_cmp = (seq - CMP_LEN) // CMP_STRIDE + 1
    cmp_idx = np.arange(n_cmp)[:, None] * CMP_STRIDE + np.arange(CMP_LEN)[None, :]
    overlap = ((cmp_idx // SEL_LEN)[:, :, None] == np.arange(nsel)[None, None, :]).sum(1) / CMP_LEN
    ovt = np.zeros((nsel, ncp), np.float32)
    ovt[:, :n_cmp] = overlap.T
    nb = tq // CMP_STRIDE
    assert 2 * nb + 2 <= CMP_BAND and CMP_STRIDE * (nb + 1) - (CMP_LEN - 1) >= REL_MAX_DIST
    i = np.arange(tq)[:, None]
    r = np.arange(CMP_BAND)[None, :]
    rel = np.where(r < 2 * nb, i - CMP_STRIDE * (r - nb) - (CMP_LEN - 1), REL_MAX_DIST)
    tb = jnp.where(jnp.asarray(r <= 2 * nb), _bias_lookup(rel_bias, jnp.asarray(rel, jnp.int32)), 0.0)
    tb = jnp.where(jnp.asarray((rel < 0) | (r == 2 * nb + 1)), NEG, tb)
    tb_hi = tb.astype(BF16)
    tb_lo = (tb - tb_hi.astype(F32)).astype(BF16)
    tb = jnp.concatenate([tb_hi, tb_lo], axis=-1)
    n = np.arange(ncp)[None, None, :]
    first = (np.arange(nq) * nb - nb)[:, None, None]
    rr = np.arange(CMP_BAND)[None, :, None]
    pp = np.where(rr < 2 * nb, n == first + rr,
                  ((rr == 2 * nb) & (n < first)) | ((rr == 2 * nb + 1) & (n >= first + 2 * nb)))
    pp = pp.astype(np.float32)
    pp = np.concatenate([pp, pp], axis=1)
    qcol = IN_OFF[4] // (hpg * dh)
    return pl.pallas_call(
        _cmpattn_kernel,
        grid=(batch, g, nq),
        in_specs=[
            pl.BlockSpec((tq, hpg * dh), lambda b, gg, t: (b * nq + t, qcol + gg)),
            pl.BlockSpec((1, 1, ncp, dh), lambda b, gg, t: (b, gg, 0, 0)),
            pl.BlockSpec((1, 1, ncp, dh), lambda b, gg, t: (b, gg, 0, 0)),
            pl.BlockSpec((hpg, tq, 2 * CMP_BAND), lambda b, gg, t: (gg, 0, 0)),
            pl.BlockSpec((1, 2 * CMP_BAND, ncp), lambda b, gg, t: (t, 0, 0)),
            pl.BlockSpec((nsel, ncp), lambda b, gg, t: (0, 0)),
            pl.BlockSpec((1, 1, tq, GATE_ROWS), lambda b, gg, t: (b, gg, t, 0)),
        ],
        out_specs=[
            pl.BlockSpec((tq, hpg * dh), lambda b, gg, t: (b * nq + t, gg)),
            pl.BlockSpec((1, 1, SEL_PAD, tq), lambda b, gg, t: (b, gg, 0, t)),
        ],
        out_shape=[jax.ShapeDtypeStruct((batch * seq, g * hpg * dh), BF16),
                   jax.ShapeDtypeStruct((batch, g, SEL_PAD, seq), BF16)],
        scratch_shapes=[pltpu.VMEM((nsel, tq), jnp.int32)],
        compiler_params=pltpu.CompilerParams(
            dimension_semantics=("parallel", "parallel", "parallel"), vmem_limit_bytes=VMEM_LIMIT),
        name="cmp_attn",
    )(proj, k_cmp, v_cmp, tb, jnp.asarray(pp, BF16), jnp.asarray(ovt, BF16),
      gates[:, :, 0].transpose(0, 1, 3, 2))


def _band_kernel(qi_ref, ki_ref, var_ref, first_ref, last_ref, q_ref, k_ref, v_ref, fd_ref, gate_ref,
                 *rest, use_sel):
    if use_sel:
        selb_ref, e_ref, o_ref, qt_ref, m_ref, acc_ref, bias_ref = rest
    else:
        o_ref, qt_ref, m_ref, acc_ref, bias_ref = rest
    t = pl.program_id(1)
    dh, hpg, ng = NSA_DH, NSA_HPG, NSA_KV_GROUPS
    tq, tk = q_ref.shape[0], k_ref.shape[0]

    @pl.when((pl.program_id(0) == 0) & (t == 0))
    def _():
        upper = (lax.broadcasted_iota(jnp.int32, (tk, tq), 1) >= lax.broadcasted_iota(jnp.int32, (tk, tq), 0))
        for gg in range(ng):
            for hh in range(hpg):
                cols = slice(hh * tq, (hh + 1) * tq)
                by_dist = jnp.broadcast_to(fd_ref[gg, hh, 0, 0:1, :], (tk, tq))
                cyc = pltpu.roll(by_dist, 0, 1, stride=1, stride_axis=0)
                bias_ref[gg, 0, :, cols] = jnp.where(upper, cyc, NEG)
                if use_sel:
                    far = jnp.broadcast_to(fd_ref[gg, hh, 1, 0:1, :], (tk, tq))
                    bias_ref[gg, 1, :, cols] = jnp.where(upper, far, cyc)
                else:
                    bias_ref[gg, 1, :, cols] = jnp.where(upper, NEG, cyc)

    @pl.when(first_ref[t] == 1)
    def _():
        m_ref[...] = jnp.full_like(m_ref, NEG)
        acc_ref[...] = jnp.zeros_like(acc_ref)
        scale = dh ** -0.5 * LOG2E
        for gg in range(ng):
            for hh in range(hpg):
                h = gg * hpg + hh
                q = q_ref[:, h * dh:(h + 1) * dh].astype(F32)
                qt_ref[gg, 0:dh, hh * tq:(hh + 1) * tq] = (q * scale).T.astype(BF16)
                if use_sel:
                    qt_ref[gg, dh:, hh * tq:(hh + 1) * tq] = selb_ref[0, gg]

    var = var_ref[t]

    def update(far):
        ones = jnp.ones((ACC_PAD, tk), F32)
        scores = []
        for gg in range(ng):
            k = k_ref[:, gg * dh:(gg + 1) * dh]
            if use_sel:
                k = jnp.concatenate([k, e_ref[...]], axis=1)
            s = _dot(k, qt_ref[gg])
            scores.append(s if far else s + bias_ref[gg, var])
        for gg in range(ng):
            vt = jnp.concatenate([v_ref[:, gg * dh:(gg + 1) * dh].astype(F32).T, ones], axis=0).astype(BF16)
            s = scores[gg]
            m_prev = m_ref[gg]
            col_max = jnp.max(s, axis=0, keepdims=True)
            if far:
                c = jnp.concatenate([fd_ref[gg, hh, 1, 0:1, :] for hh in range(hpg)], axis=1)
                m_new = jnp.maximum(m_prev, col_max + c)
                shift = m_new - c
            else:
                m_new = jnp.maximum(m_prev, col_max)
                shift = m_new
            alpha = jnp.exp2(m_prev - m_new)
            p = jnp.exp2(s - shift).astype(BF16)
            acc_ref[gg] = alpha * acc_ref[gg] + _dot(vt, p)
            m_ref[gg] = m_new

    if use_sel:
        pl.when(var == ATT_NVAR - 1)(functools.partial(update, True))
        pl.when(var != ATT_NVAR - 1)(functools.partial(update, False))
    else:
        update(False)

    @pl.when(last_ref[t] == 1)
    def _():
        for gg in range(ng):
            gate = _sigmoid(gate_ref[0, gg, 0])
            for hh in range(hpg):
                h = gg * hpg + hh
                cols = slice(hh * tq, (hh + 1) * tq)
                o = acc_ref[gg, 0:dh, cols] * (gate[hh:hh + 1, :] / acc_ref[gg, dh:dh + 1, cols])
                o_ref[:, h * dh:(h + 1) * dh] = o.T.astype(o_ref.dtype)


def _band_steps(seq, window):
    tq, tk = ATT_T, ATT_TK
    steps = []
    for qi in range(seq // tq):
        k_hi = qi * tq // tk
        k_lo = max(0, (qi * tq - window + 1) // tk) if window else 0
        steps += [(qi, ki, min((qi * tq - ki * tk) // tq, ATT_NVAR - 1)) for ki in range(k_lo, k_hi + 1)]
    return steps


def _band_attention(proj, bias_by_dist, steps, koff, voff, gates, branch, batch, seq, selb=None, e_mat=None):
    dh, g, hpg = NSA_DH, NSA_KV_GROUPS, NSA_HPG
    tq, tk = ATT_T, ATT_TK
    nq, nk = seq // tq, seq // tk
    qi = np.array([s[0] for s in steps], np.int32)
    ki = np.array([s[1] for s in steps], np.int32)
    var = np.array([s[2] for s in steps], np.int32)
    first = np.concatenate([[1], (qi[1:] != qi[:-1]).astype(np.int32)]).astype(np.int32)
    last = np.concatenate([(qi[1:] != qi[:-1]).astype(np.int32), [1]]).astype(np.int32)
    qcol, kcol, vcol = IN_OFF[4] // NSA_W, koff // KV_W, voff // KV_W
    use_sel = selb is not None
    in_specs = [
        pl.BlockSpec((tq, NSA_W), lambda b, t, qi_r, ki_r, v_r, f_r, l_r: (b * nq + qi_r[t], qcol)),
        pl.BlockSpec((tk, KV_W), lambda b, t, qi_r, ki_r, v_r, f_r, l_r: (b * nk + ki_r[t], kcol)),
        pl.BlockSpec((tk, KV_W), lambda b, t, qi_r, ki_r, v_r, f_r, l_r: (b * nk + ki_r[t], vcol)),
        pl.BlockSpec(bias_by_dist.shape, lambda b, t, qi_r, ki_r, v_r, f_r, l_r: (0, 0, 0, 0, 0)),
        pl.BlockSpec((1, g, 1, GATE_ROWS, tq),
                     lambda b, t, qi_r, ki_r, v_r, f_r, l_r: (b, 0, branch, 0, qi_r[t])),
    ]
    args = [proj, proj, proj, bias_by_dist, gates]
    if use_sel:
        in_specs += [
            pl.BlockSpec((1, g, SEL_PAD, tq), lambda b, t, qi_r, ki_r, v_r, f_r, l_r: (b, 0, 0, qi_r[t])),
            pl.BlockSpec((tk, SEL_PAD), lambda b, t, qi_r, ki_r, v_r, f_r, l_r: (ki_r[t], 0)),
        ]
        args += [selb, e_mat]
    kdim = dh + SEL_PAD if use_sel else dh
    grid_spec = pltpu.PrefetchScalarGridSpec(
        num_scalar_prefetch=5,
        grid=(batch, len(steps)),
        in_specs=in_specs,
        out_specs=pl.BlockSpec((tq, NSA_W), lambda b, t, qi_r, ki_r, v_r, f_r, l_r: (b * nq + qi_r[t], 0)),
        scratch_shapes=[pltpu.VMEM((g, kdim, hpg * tq), BF16), pltpu.VMEM((g, 1, hpg * tq), F32),
                        pltpu.VMEM((g, dh + ACC_PAD, hpg * tq), F32),
                        pltpu.VMEM((g, ATT_NVAR - 1, tk, hpg * tq), F32)],
    )
    return pl.pallas_call(
        functools.partial(_band_kernel, use_sel=use_sel),
        grid_spec=grid_spec,
        out_shape=jax.ShapeDtypeStruct((batch * seq, NSA_W), BF16),
        compiler_params=pltpu.CompilerParams(
            dimension_semantics=("arbitrary", "arbitrary"), vmem_limit_bytes=VMEM_LIMIT),
        name="sel_attn" if use_sel else "win_attn",
    )(jnp.asarray(qi), jnp.asarray(ki), jnp.asarray(var), jnp.asarray(first), jnp.asarray(last), *args)


def _out_kernel(x_ref, yr_ref, oc_ref, os_ref, ow_ref, w_ref, o_ref):
    y_nsa = oc_ref[...].astype(F32) + os_ref[...].astype(F32) + ow_ref[...].astype(F32)
    y = jnp.concatenate([yr_ref[...], y_nsa.astype(BF16)], axis=-1)
    o_ref[...] = x_ref[...] + _dot(y, w_ref[...])


def _out_proj(x2d, y_ret, o_cmp, o_sel, o_win, w_out):
    tok, d = x2d.shape
    tm = OUT_TM
    wmix = y_ret.shape[1]
    row = lambda shape: pl.BlockSpec(shape, lambda i: (i, 0))
    return pl.pallas_call(
        _out_kernel,
        grid=(tok // tm,),
        in_specs=[row((tm, d)), row((tm, wmix)), row((tm, wmix)), row((tm, wmix)), row((tm, wmix)),
                  pl.BlockSpec(w_out.shape, lambda i: (0, 0))],
        out_specs=row((tm, d)),
        out_shape=jax.ShapeDtypeStruct((tok, d), F32),
        compiler_params=pltpu.CompilerParams(
            dimension_semantics=("parallel",), vmem_limit_bytes=VMEM_LIMIT),
        name="out_proj",
    )(x2d, y_ret, o_cmp, o_sel, o_win, w_out)


def _t5_bucket_of(rel):
    n = jnp.maximum(rel, 0)
    max_exact = REL_BUCKETS // 2
    nf = jnp.maximum(n, 1).astype(F32)
    large = max_exact + (jnp.log(nf / max_exact) / math.log(REL_MAX_DIST / max_exact)
                         * (REL_BUCKETS - max_exact)).astype(jnp.int32)
    large = jnp.minimum(large, REL_BUCKETS - 1)
    return jnp.where(n < max_exact, n, large)


def _bias_lookup(rel_bias, rel):
    bucket = _t5_bucket_of(rel)[None]
    tab = rel_bias.astype(F32).reshape((rel_bias.shape[0], REL_BUCKETS) + (1,) * rel.ndim)
    out = jnp.zeros((rel_bias.shape[0],) + rel.shape, F32)
    for b in range(REL_BUCKETS):
        out = jnp.where(bucket == b, tab[:, b], out)
    return out


def _bias_by_distance(rel_bias):
    t, g, hpg = ATT_T, NSA_KV_GROUPS, NSA_HPG
    assert ATT_TK == t and t >= REL_MAX_DIST and WIN == t
    by_dist = _bias_lookup(rel_bias, jnp.arange(t, dtype=jnp.int32)) * LOG2E
    far = jnp.broadcast_to(by_dist[:, t - 1:t], by_dist.shape)
    tab = jnp.stack([by_dist, far], axis=1).reshape(g, hpg, 2, 1, t)
    return jnp.broadcast_to(tab, (g, hpg, 2, 8, t))


def _token_mix(x2d, mix_norm, w_in, ret_gn_gain, pe_k, w1_k, w2_k, pe_v, w1_v, w2_v, w_out, rel_bias,
               batch, seq):
    d = x2d.shape[1]
    dh = NSA_DH
    w_pad = _to_bf16_transposed(w_in.T, D_IN_PAD)
    proj, cmp_kv = _inproj(x2d, mix_norm, w_pad)
    y_ret = _retention(proj, ret_gn_gain, batch, seq)

    gates = proj[:, IN_OFF[11]:IN_OFF[11] + IN_SPLITS[11]].astype(F32)
    gates = gates.reshape(batch, seq, NSA_KV_GROUPS, NSA_HPG, N_BRANCH).transpose(0, 2, 4, 3, 1)
    gates = jnp.pad(gates, ((0, 0), (0, 0), (0, 0), (0, GATE_ROWS - NSA_HPG), (0, 0)))

    k_cmp, v_cmp = _compress(cmp_kv, pe_k, w1_k, w2_k, pe_v, w1_v, w2_v, batch, seq)
    o_cmp, selb = _cmp_attention(proj, k_cmp, v_cmp, rel_bias, gates, batch, seq)

    bias_by_dist = _bias_by_distance(rel_bias)
    e_np = np.zeros((seq, SEL_PAD), np.float32)
    e_np[np.arange(seq), np.arange(seq) // SEL_LEN] = 1.0
    o_sel = _band_attention(proj, bias_by_dist, _band_steps(seq, 0),
                            IN_OFF[7], IN_OFF[8], gates, 1, batch, seq,
                            selb=selb, e_mat=jnp.asarray(e_np, BF16))
    o_win = _band_attention(proj, bias_by_dist, _band_steps(seq, WIN),
                            IN_OFF[9], IN_OFF[10], gates, 2, batch, seq)
    return _out_proj(x2d, y_ret, o_cmp, o_sel, o_win, _to_bf16(w_out))


def _cast_t_kernel(x_ref, o_ref, *, valid_rows):
    tr = x_ref.shape[0]
    row = pl.program_id(0) * tr + lax.broadcasted_iota(jnp.int32, x_ref.shape, 0)
    o_ref[...] = jnp.where(row < valid_rows, x_ref[...], 0.0).T.astype(o_ref.dtype)


def _to_bf16_transposed(w_t, cols):
    c, r = w_t.shape
    tr = CAST_T_ROWS
    return pl.pallas_call(
        functools.partial(_cast_t_kernel, valid_rows=c),
        grid=(cols // tr,),
        in_specs=[pl.BlockSpec((tr, r), lambda i: (i, 0))],
        out_specs=pl.BlockSpec((r, tr), lambda i: (0, i)),
        out_shape=jax.ShapeDtypeStruct((r, cols), BF16),
        compiler_params=pltpu.CompilerParams(
            dimension_semantics=("parallel",), vmem_limit_bytes=VMEM_LIMIT),
        name="to_bf16_t",
    )(w_t)


def _cast_kernel(x_ref, o_ref):
    o_ref[...] = x_ref[...].astype(o_ref.dtype)


def _to_bf16(w):
    r, c = w.shape
    tr = 1 << ((CAST_BLOCK_BYTES // (4 * c)).bit_length() - 1)
    while r % tr:
        tr //= 2
    return pl.pallas_call(
        _cast_kernel,
        grid=(r // tr,),
        in_specs=[pl.BlockSpec((tr, c), lambda i: (i, 0))],
        out_specs=pl.BlockSpec((tr, c), lambda i: (i, 0)),
        out_shape=jax.ShapeDtypeStruct((r, c), BF16),
        compiler_params=pltpu.CompilerParams(
            dimension_semantics=("parallel",), vmem_limit_bytes=VMEM_LIMIT),
        name="to_bf16",
    )(w)


def kernel(x, ffn1_norm, ffn1_w1, ffn1_w3, ffn1_w2, mix_norm, w_in, ret_gn_gain, cmp_pe_k, cmp_w1_k,
           cmp_w2_k, cmp_pe_v, cmp_w1_v, cmp_w2_v, w_out, ffn2_norm, ffn2_w1, ffn2_w3, ffn2_w2,
           rel_bias, final_norm):
    batch, seq, d = x.shape
    depth = ffn1_norm.shape[0]
    h = x.reshape(batch * seq, d)
    for l in range(depth):
        last = l == depth - 1
        h = _ffn(h, ffn1_norm[l], _to_bf16(ffn1_w1[l]), _to_bf16(ffn1_w3[l]), _to_bf16(ffn1_w2[l]),
                 final_norm, False)
        h = _token_mix(h, mix_norm[l], w_in[l], ret_gn_gain[l], cmp_pe_k[l], cmp_w1_k[l], cmp_w2_k[l],
                       cmp_pe_v[l], cmp_w1_v[l], cmp_w2_v[l], w_out[l], rel_bias, batch, seq)
        h = _ffn(h, ffn2_norm[l], _to_bf16(ffn2_w1[l]), _to_bf16(ffn2_w3[l]), _to_bf16(ffn2_w2[l]),
                 final_norm, last)
    return h.reshape(batch, seq, d)
```

```python
import functools
import math

import jax
import jax.numpy as jnp
import numpy as np
from jax import lax
from jax.experimental import pallas as pl
from jax.experimental.pallas import tpu as pltpu

F32 = jnp.float32
BF16 = jnp.bfloat16

RET_HEADS = 4
RET_DK = 256
RET_DV = 256
ROPE_BASE = 10000.0
NSA_HEADS = 8
NSA_KV_GROUPS = 2
NSA_HPG = NSA_HEADS // NSA_KV_GROUPS
NSA_DH = 128
CMP_LEN = 32
CMP_STRIDE = 16
SEL_LEN = 64
SEL_TOPK = 16
WIN = 512
N_BRANCH = 3
REL_BUCKETS = 32
REL_MAX_DIST = 128
EPS = 1e-6
NEG = -1e30
FORCE = 1e4

RET_W = RET_HEADS * RET_DV
NSA_W = NSA_HEADS * NSA_DH
KV_W = NSA_KV_GROUPS * NSA_DH
IN_SPLITS = [RET_HEADS * RET_DK, RET_HEADS * RET_DK, RET_W, RET_W, NSA_W,
             KV_W, KV_W, KV_W, KV_W, KV_W, KV_W, NSA_HEADS * N_BRANCH]
D_IN = sum(IN_SPLITS)
IN_OFF = [sum(IN_SPLITS[:i]) for i in range(len(IN_SPLITS))]

LANE = 128
IN_TN = 2304
D_IN_PAD = 3 * IN_TN
CMP_KV_TILE = IN_OFF[5] // IN_TN
CMP_KV_LOCAL = IN_OFF[5] % IN_TN
assert CMP_KV_TILE > 0 and CMP_KV_LOCAL + 2 * KV_W <= IN_TN and IN_OFF[6] == IN_OFF[5] + KV_W
GATE_ROWS = 8
FFN_TM = 1024
FFN_TF = 512
IN_TM = 512
RET_C = 256
CMP_TQ = 512
ATT_T = 512
ATT_TK = 512
ATT_NVAR = 3
OUT_TM = 512
SEL_PAD = 128
ACC_PAD = 8
CMP_BAND = 128
VMEM_LIMIT = 56 * 1024 * 1024
CAST_BLOCK_BYTES = 6 * 1024 * 1024
CAST_T_ROWS = 256
LOG2E = math.log2(math.e)


def _dot(a, b):
    return jnp.dot(a, b, preferred_element_type=F32)


def _dot_nt(a, b):
    return lax.dot_general(a, b, (((1,), (1,)), ((), ())), preferred_element_type=F32)


def _dot_tn(a, b):
    return lax.dot_general(a, b, (((0,), (0,)), ((), ())), preferred_element_type=F32)


def _sigmoid(x):
    return 1.0 / (1.0 + jnp.exp(-x))


def _rms(x, g):
    ms = jnp.mean(x * x, axis=-1, keepdims=True)
    return x * lax.rsqrt(ms + EPS) * g


def _ffn_kernel(x_ref, g_ref, w1_ref, w3_ref, w2_ref, fg_ref, o_ref, n_ref, *, final_norm):
    j = pl.program_id(1)

    last = pl.num_programs(1) - 1

    def step(first, norm_out):
        if first:
            n = _rms(x_ref[...], g_ref[...]).astype(BF16)
            n_ref[...] = n
        else:
            n = n_ref[...]
        a = _dot(n, w1_ref[...])
        b = _dot(n, w3_ref[...])
        h = (0.5 * a * _sigmoid(a) * b).astype(BF16)
        y = (x_ref[...] if first else o_ref[...]) + _dot(h, w2_ref[...])
        o_ref[...] = _rms(y, fg_ref[...]) if norm_out else y

    pl.when(j == 0)(functools.partial(step, True, False))
    if final_norm:
        pl.when((j > 0) & (j < last))(functools.partial(step, False, False))
        pl.when(j == last)(functools.partial(step, False, True))
    else:
        pl.when(j > 0)(functools.partial(step, False, False))


def _ffn(x2d, g, w1, w3, w2, fg, final_norm):
    tok, d = x2d.shape
    dff = w1.shape[1]
    tm, tf = FFN_TM, FFN_TF
    assert dff // tf >= 2
    return pl.pallas_call(
        functools.partial(_ffn_kernel, final_norm=final_norm),
        grid=(tok // tm, dff // tf),
        in_specs=[
            pl.BlockSpec((tm, d), lambda i, j: (i, 0)),
            pl.BlockSpec((1, d), lambda i, j: (0, 0)),
            pl.BlockSpec((d, tf), lambda i, j: (0, j)),
            pl.BlockSpec((d, tf), lambda i, j: (0, j)),
            pl.BlockSpec((tf, d), lambda i, j: (j, 0)),
            pl.BlockSpec((1, d), lambda i, j: (0, 0)),
        ],
        out_specs=pl.BlockSpec((tm, d), lambda i, j: (i, 0)),
        out_shape=jax.ShapeDtypeStruct((tok, d), F32),
        scratch_shapes=[pltpu.VMEM((tm, d), BF16)],
        compiler_params=pltpu.CompilerParams(
            dimension_semantics=("parallel", "arbitrary"), vmem_limit_bytes=VMEM_LIMIT),
        name="ffn",
    )(x2d, g.reshape(1, d), w1, w3, w2, fg.reshape(1, d))


def _inproj_kernel(x_ref, g_ref, w_ref, o_ref, kv_ref, n_ref):
    j = pl.program_id(1)
    tn = o_ref.shape[1]
    for jj in range(w_ref.shape[1] // tn):
        @pl.when(j == jj)
        def _():
            if jj == 0:
                n = _rms(x_ref[...], g_ref[...]).astype(BF16)
                n_ref[...] = n
            else:
                n = n_ref[...]
            res = _dot(n, w_ref[:, jj * tn:(jj + 1) * tn])
            o_ref[...] = res.astype(o_ref.dtype)
            if jj == CMP_KV_TILE:
                kv_ref[...] = res[:, CMP_KV_LOCAL:CMP_KV_LOCAL + 2 * KV_W]


def _inproj(x2d, g, w_pad):
    tok, d = x2d.shape
    n_out = w_pad.shape[1]
    tm, tn = IN_TM, IN_TN
    return pl.pallas_call(
        _inproj_kernel,
        grid=(tok // tm, n_out // tn),
        in_specs=[
            pl.BlockSpec((tm, d), lambda i, j: (i, 0)),
            pl.BlockSpec((1, d), lambda i, j: (0, 0)),
            pl.BlockSpec(w_pad.shape, lambda i, j: (0, 0), pipeline_mode=pl.Buffered(1)),
        ],
        out_specs=[pl.BlockSpec((tm, tn), lambda i, j: (i, j)),
                   pl.BlockSpec((tm, 2 * KV_W), lambda i, j: (i, 0))],
        out_shape=[jax.ShapeDtypeStruct((tok, n_out), BF16),
                   jax.ShapeDtypeStruct((tok, 2 * KV_W), F32)],
        scratch_shapes=[pltpu.VMEM((tm, d), BF16)],
        compiler_params=pltpu.CompilerParams(
            dimension_semantics=("parallel", "arbitrary"), vmem_limit_bytes=VMEM_LIMIT),
        name="in_proj",
    )(x2d, g.reshape(1, d), w_pad)


def _ret_kernel(cd_ref, q_ref, k_ref, v_ref, g_ref, cos_ref, sin_ref, dec_ref, xi_ref, zeta_ref,
                gain_ref, o_ref, state_ref):
    @pl.when(pl.program_id(0) == 0)
    def _():
        state_ref[...] = jnp.zeros_like(state_ref)

    cos = cos_ref[...]
    sin = sin_ref[...]
    half = RET_DK // 2

    def rot(t):
        t1, t2 = t[:, :half], t[:, half:]
        return jnp.concatenate([t1 * cos - t2 * sin, t1 * sin + t2 * cos], axis=-1)

    for bi, h in [(bi, h) for bi in range(q_ref.shape[0]) for h in range(RET_HEADS)]:
        cs = slice(h * RET_DK, (h + 1) * RET_DK)
        q = rot(q_ref[bi, :, cs].astype(F32))
        k = rot(k_ref[bi, :, cs].astype(F32))
        v = v_ref[bi, :, cs].astype(BF16)
        qb = q.astype(BF16)
        s = _dot_nt(qb, k.astype(BF16)) * dec_ref[h]
        state = state_ref[bi, h]
        y = _dot(s.astype(BF16), v) + _dot((q * xi_ref[h]).astype(BF16), state.astype(BF16))
        state_ref[bi, h] = cd_ref[h] * state + _dot_tn((k * zeta_ref[h]).astype(BF16), v)
        mu = jnp.mean(y, axis=-1, keepdims=True)
        yc = y - mu
        var = jnp.mean(yc * yc, axis=-1, keepdims=True)
        yn = yc * lax.rsqrt(var + EPS) * gain_ref[:, cs]
        gate = g_ref[bi, :, cs].astype(F32)
        o_ref[bi, :, cs] = (gate * _sigmoid(gate) * yn).astype(o_ref.dtype)


def _retention(proj, gain, batch, seq):
    tok = proj.shape[0]
    c = RET_C
    n_chunks = seq // c
    h, dk = RET_HEADS, RET_DK
    half = dk // 2
    inv = np.float32(ROPE_BASE) ** (-np.arange(half, dtype=np.float32) / np.float32(half))
    ang = np.arange(seq, dtype=np.float32)[:, None] * inv[None, :]
    cos, sin = np.cos(ang).astype(np.float32), np.sin(ang).astype(np.float32)
    log_g = np.log(1.0 - 2.0 ** (-5.0 - np.arange(h, dtype=np.float64)))
    idx = np.arange(c, dtype=np.float64)
    diff = idx[:, None] - idx[None, :]
    dec = np.where(diff >= 0.0, np.exp(np.maximum(diff, 0.0)[None] * log_g[:, None, None]), 0.0)
    k_scale = dk ** -0.5
    assert math.frexp(k_scale)[0] == 0.5
    dec = dec.astype(np.float32) * np.float32(k_scale)
    xi = np.broadcast_to(np.exp((idx + 1.0)[None] * log_g[:, None])[:, :, None], (h, c, dk)).astype(np.float32)
    zeta = np.broadcast_to(np.exp((c - 1.0 - idx)[None] * log_g[:, None])[:, :, None],
                           (h, c, dk)).astype(np.float32) * np.float32(k_scale)
    cd = np.exp(c * log_g).astype(np.float32)

    w = h * dk
    proj3 = proj.reshape(batch, seq, proj.shape[1])
    col = lambda j: pl.BlockSpec((batch, c, w), lambda n: (0, n, j))
    y = pl.pallas_call(
        _ret_kernel,
        grid=(n_chunks,),
        in_specs=[
            pl.BlockSpec(memory_space=pltpu.SMEM),
            col(0), col(1), col(2), col(3),
            pl.BlockSpec((c, half), lambda n: (n, 0)),
            pl.BlockSpec((c, half), lambda n: (n, 0)),
            pl.BlockSpec((h, c, c), lambda n: (0, 0, 0)),
            pl.BlockSpec((h, c, dk), lambda n: (0, 0, 0)),
            pl.BlockSpec((h, c, dk), lambda n: (0, 0, 0)),
            pl.BlockSpec((1, w), lambda n: (0, 0)),
        ],
        out_specs=pl.BlockSpec((batch, c, w), lambda n: (0, n, 0)),
        out_shape=jax.ShapeDtypeStruct((batch, seq, w), BF16),
        scratch_shapes=[pltpu.VMEM((batch, h, dk, RET_DV), F32)],
        compiler_params=pltpu.CompilerParams(
            dimension_semantics=("arbitrary",), vmem_limit_bytes=VMEM_LIMIT),
        name="retention",
    )(cd, proj3, proj3, proj3, proj3, cos, sin, dec, xi, zeta, gain.reshape(1, w))
    return y.reshape(tok, w)


def _compress_kernel(k_ref, v_ref, pek_ref, w1k_ref, w1kf_ref, w2k_ref, pev_ref, w1v_ref, w1vf_ref,
                     w2v_ref, ok_ref, ov_ref):
    dh = NSA_DH
    nblk = k_ref.shape[0] // CMP_STRIDE

    def one(x_ref, pe_ref, w1_ref, w1f_ref, w2_ref, o_ref):
        ab = jnp.zeros((nblk, 2 * dh), F32)
        for l in range(CMP_STRIDE):
            rows = x_ref[pl.ds(l, nblk, stride=CMP_STRIDE), :].astype(BF16)
            ab = ab + _dot(rows, w1_ref[l])
        a, b = ab[:, :dh], ab[:, dh:]
        b_next = pltpu.roll(b, nblk - 1, axis=0)
        ridx = lax.broadcasted_iota(jnp.int32, (nblk, dh), 0)
        b_next = jnp.where(ridx < nblk - 1, b_next, 0.0)
        pe_term = _dot(pe_ref[...], w1f_ref[...])[0:1, :]
        hdn = a + b_next + pe_term
        o_ref[0, 0] = _dot((hdn * _sigmoid(hdn)).astype(BF16), w2_ref[...])

    one(k_ref, pek_ref, w1k_ref, w1kf_ref, w2k_ref, ok_ref)
    one(v_ref, pev_ref, w1v_ref, w1vf_ref, w2v_ref, ov_ref)


def _compress(kv, pe_k, w1_k, w2_k, pe_v, w1_v, w2_v, batch, seq):
    dh, g = NSA_DH, NSA_KV_GROUPS
    nblk = seq // CMP_STRIDE
    half = CMP_LEN // 2

    def prep(pe, w1, w2):
        w1b = w1.astype(BF16)
        w1_pair = jnp.concatenate([w1b[:half], w1b[half:]], axis=-1)
        pe_flat = jnp.broadcast_to(pe.reshape(1, CMP_LEN * dh), (8, CMP_LEN * dh)).astype(BF16)
        return pe_flat, w1_pair, w1b.reshape(CMP_LEN * dh, dh), w2.astype(BF16)

    args_k = prep(pe_k, w1_k, w2_k)
    args_v = prep(pe_v, w1_v, w2_v)
    kcol = 0
    vcol = KV_W // dh
    const = lambda shape: pl.BlockSpec(shape, lambda b, gg: (0,) * len(shape))
    wspecs = [const((8, CMP_LEN * dh)), const((half, dh, 2 * dh)), const((CMP_LEN * dh, dh)),
              const((dh, dh))]
    out_spec = pl.BlockSpec((1, 1, nblk, dh), lambda b, gg: (b, gg, 0, 0))
    return pl.pallas_call(
        _compress_kernel,
        grid=(batch, g),
        in_specs=[pl.BlockSpec((seq, dh), lambda b, gg: (b, kcol + gg)),
                  pl.BlockSpec((seq, dh), lambda b, gg: (b, vcol + gg))] + wspecs + wspecs,
        out_specs=[out_spec, out_spec],
        out_shape=[jax.ShapeDtypeStruct((batch, g, nblk, dh), F32)] * 2,
        compiler_params=pltpu.CompilerParams(
            dimension_semantics=("parallel", "parallel"), vmem_limit_bytes=VMEM_LIMIT),
        name="compress",
    )(kv, kv, *args_k, *args_v)


def _cmpattn_kernel(q_ref, kc_ref, vc_ref, tb_ref, pp_ref, ovt_ref, gate_ref, o_ref, sel_ref, rank_ref):
    tq = q_ref.shape[0]
    ncp = kc_ref.shape[2]
    dh = NSA_DH
    s0 = pl.program_id(2) * tq
    hpg = NSA_HPG
    kc = kc_ref[0, 0].astype(BF16)
    vc = vc_ref[0, 0].astype(BF16)
    scale = dh ** -0.5
    q = jnp.concatenate([q_ref[:, hh * dh:(hh + 1) * dh] for hh in range(hpg)], axis=0).astype(BF16)
    bias = _dot(tb_ref[...].reshape(hpg * tq, tb_ref.shape[2]), pp_ref[0])
    lm = _dot_nt(q, kc) * scale + bias
    m = jnp.max(lm, axis=-1, keepdims=True)
    e = jnp.exp(lm - m)
    den = jnp.sum(e, axis=-1, keepdims=True)
    p = e * jnp.where(m > 0.5 * NEG, 1.0 / den, 0.0)
    o = _dot(p.astype(BF16), vc)
    gate = _sigmoid(gate_ref[0, 0])
    psum = p[0:tq]
    for hh in range(hpg):
        o_ref[:, hh * dh:(hh + 1) * dh] = (o[hh * tq:(hh + 1) * tq] * gate[:, hh:hh + 1]).astype(o_ref.dtype)
        if hh:
            psum = psum + p[hh * tq:(hh + 1) * tq]
    nsel = ovt_ref.shape[0]
    imp_t = _dot_nt(ovt_ref[...], psum.astype(BF16))
    jb = lax.broadcasted_iota(jnp.int32, (nsel, tq), 0)
    pos = s0 + lax.broadcasted_iota(jnp.int32, (nsel, tq), 1)
    cur = jnp.right_shift(pos, SEL_LEN.bit_length() - 1)
    causal = jb * SEL_LEN <= pos
    forced = (jb == 0) | (jb == cur) | (jb == cur - 1)
    score = jnp.where(forced, FORCE, jnp.where(causal, imp_t, NEG))
    per_tile = tq // SEL_LEN
    rank_ref[...] = jnp.zeros_like(rank_ref)
    for c in range(nsel // per_tile):
        @pl.when(c <= pl.program_id(2))
        def _():
            rank = rank_ref[...]
            for kk in range(c * per_tile, (c + 1) * per_tile):
                row = score[kk:kk + 1, :]
                beats = (row > score) | ((row == score) & (jb > kk))
                rank = rank + jnp.where(beats, 1, 0)
            rank_ref[...] = rank
    rank = rank_ref[...]
    selb = jnp.where((rank < SEL_TOPK) & causal, 0.0, NEG)
    selb = jnp.concatenate([selb, jnp.zeros((SEL_PAD - nsel, tq), F32)], axis=0)
    sel_ref[0, 0] = selb.astype(BF16)


def _cmp_attention(proj, k_cmp, v_cmp, rel_bias, gates, batch, seq):
    dh, g, hpg = NSA_DH, NSA_KV_GROUPS, NSA_HPG
    tq = CMP_TQ
    nq = seq // tq
    ncp = k_cmp.shape[2]
    nsel = seq // SEL_LEN
    n_cmp = (seq - CMP_LEN) // CMP_STRIDE + 1
    cmp_idx = np.arange(n_cmp)[:, None] * CMP_STRIDE + np.arange(CMP_LEN)[None, :]
    overlap = ((cmp_idx // SEL_LEN)[:, :, None] == np.arange(nsel)[None, None, :]).sum(1) / CMP_LEN
    ovt = np.zeros((nsel, ncp), np.float32)
    ovt[:, :n_cmp] = overlap.T
    nb = tq // CMP_STRIDE
    assert 2 * nb + 2 <= CMP_BAND and CMP_STRIDE * (nb + 1) - (CMP_LEN - 1) >= REL_MAX_DIST
    i = np.arange(tq)[:, None]
    r = np.arange(CMP_BAND)[None, :]
    rel = np.where(r < 2 * nb, i - CMP_STRIDE * (r - nb) - (CMP_LEN - 1), REL_MAX_DIST)
    tb = jnp.where(jnp.asarray(r <= 2 * nb), _bias_lookup(rel_bias, jnp.asarray(rel, jnp.int32)), 0.0)
    tb = jnp.where(jnp.asarray((rel < 0) | (r == 2 * nb + 1)), NEG, tb)
    tb_hi = tb.astype(BF16)
    tb_lo = (tb - tb_hi.astype(F32)).astype(BF16)
    tb = jnp.concatenate([tb_hi, tb_lo], axis=-1)
    n = np.arange(ncp)[None, None, :]
    first = (np.arange(nq) * nb - nb)[:, None, None]
    rr = np.arange(CMP_BAND)[None, :, None]
    pp = np.where(rr < 2 * nb, n == first + rr,
                  ((rr == 2 * nb) & (n < first)) | ((rr == 2 * nb + 1) & (n >= first + 2 * nb)))
    pp = pp.astype(np.float32)
    pp = np.concatenate([pp, pp], axis=1)
    qcol = IN_OFF[4] // (hpg * dh)
    return pl.pallas_call(
        _cmpattn_kernel,
        grid=(batch, g, nq),
        in_specs=[
            pl.BlockSpec((tq, hpg * dh), lambda b, gg, t: (b * nq + t, qcol + gg)),
            pl.BlockSpec((1, 1, ncp, dh), lambda b, gg, t: (b, gg, 0, 0)),
            pl.BlockSpec((1, 1, ncp, dh), lambda b, gg, t: (b, gg, 0, 0)),
            pl.BlockSpec((hpg, tq, 2 * CMP_BAND), lambda b, gg, t: (gg, 0, 0)),
            pl.BlockSpec((1, 2 * CMP_BAND, ncp), lambda b, gg, t: (t, 0, 0)),
            pl.BlockSpec((nsel, ncp), lambda b, gg, t: (0, 0)),
            pl.BlockSpec((1, 1, tq, GATE_ROWS), lambda b, gg, t: (b, gg, t, 0)),
        ],
        out_specs=[
            pl.BlockSpec((tq, hpg * dh), lambda b, gg, t: (b * nq + t, gg)),
            pl.BlockSpec((1, 1, SEL_PAD, tq), lambda b, gg, t: (b, gg, 0, t)),
        ],
        out_shape=[jax.ShapeDtypeStruct((batch * seq, g * hpg * dh), BF16),
                   jax.ShapeDtypeStruct((batch, g, SEL_PAD, seq), BF16)],
        scratch_shapes=[pltpu.VMEM((nsel, tq), jnp.int32)],
        compiler_params=pltpu.CompilerParams(
            dimension_semantics=("parallel", "parallel", "parallel"), vmem_limit_bytes=VMEM_LIMIT),
        name="cmp_attn",
    )(proj, k_cmp, v_cmp, tb, jnp.asarray(pp, BF16), jnp.asarray(ovt, BF16),
      gates[:, :, 0].transpose(0, 1, 3, 2))


def _band_kernel(qi_ref, ki_ref, var_ref, first_ref, last_ref, q_ref, k_ref, v_ref, fd_ref, gate_ref,
                 *rest, use_sel):
    if use_sel:
        selb_ref, e_ref, o_ref, qt_ref, m_ref, acc_ref, bias_ref = rest
    else:
        o_ref, qt_ref, m_ref, acc_ref, bias_ref = rest
    t = pl.program_id(1)
    dh, hpg, ng = NSA_DH, NSA_HPG, NSA_KV_GROUPS
    tq, tk = q_ref.shape[0], k_ref.shape[0]

    @pl.when((pl.program_id(0) == 0) & (t == 0))
    def _():
        upper = (lax.broadcasted_iota(jnp.int32, (tk, tq), 1) >= lax.broadcasted_iota(jnp.int32, (tk, tq), 0))
        for gg in range(ng):
            for hh in range(hpg):
                cols = slice(hh * tq, (hh + 1) * tq)
                by_dist = jnp.broadcast_to(fd_ref[gg, hh, 0, 0:1, :], (tk, tq))
                cyc = pltpu.roll(by_dist, 0, 1, stride=1, stride_axis=0)
                bias_ref[gg, 0, :, cols] = jnp.where(upper, cyc, NEG)
                if use_sel:
                    far = jnp.broadcast_to(fd_ref[gg, hh, 1, 0:1, :], (tk, tq))
                    bias_ref[gg, 1, :, cols] = jnp.where(upper, far, cyc)
                else:
                    bias_ref[gg, 1, :, cols] = jnp.where(upper, NEG, cyc)

    @pl.when(first_ref[t] == 1)
    def _():
        m_ref[...] = jnp.full_like(m_ref, NEG)
        acc_ref[...] = jnp.zeros_like(acc_ref)
        scale = dh ** -0.5 * LOG2E
        for gg in range(ng):
            for hh in range(hpg):
                h = gg * hpg + hh
                q = q_ref[:, h * dh:(h + 1) * dh].astype(F32)
                qt_ref[gg, 0:dh, hh * tq:(hh + 1) * tq] = (q * scale).T.astype(BF16)
                if use_sel:
                    qt_ref[gg, dh:, hh * tq:(hh + 1) * tq] = selb_ref[0, gg]

    var = var_ref[t]

    def update(far):
        ones = jnp.ones((ACC_PAD, tk), F32)
        scores = []
        for gg in range(ng):
            k = k_ref[:, gg * dh:(gg + 1) * dh]
            if use_sel:
                k = jnp.concatenate([k, e_ref[...]], axis=1)
            s = _dot(k, qt_ref[gg])
            scores.append(s if far else s + bias_ref[gg, var])
        for gg in range(ng):
            vt = jnp.concatenate([v_ref[:, gg * dh:(gg + 1) * dh].astype(F32).T, ones], axis=0).astype(BF16)
            s = scores[gg]
            m_prev = m_ref[gg]
            col_max = jnp.max(s, axis=0, keepdims=True)
            if far:
                c = jnp.concatenate([fd_ref[gg, hh, 1, 0:1, :] for hh in range(hpg)], axis=1)
                m_new = jnp.maximum(m_prev, col_max + c)
                shift = m_new - c
            else:
                m_new = jnp.maximum(m_prev, col_max)
                shift = m_new
            alpha = jnp.exp2(m_prev - m_new)
            p = jnp.exp2(s - shift).astype(BF16)
            acc_ref[gg] = alpha * acc_ref[gg] + _dot(vt, p)
            m_ref[gg] = m_new

    if use_sel:
        pl.when(var == ATT_NVAR - 1)(functools.partial(update, True))
        pl.when(var != ATT_NVAR - 1)(functools.partial(update, False))
    else:
        update(False)

    @pl.when(last_ref[t] == 1)
    def _():
        for gg in range(ng):
            gate = _sigmoid(gate_ref[0, gg, 0])
            for hh in range(hpg):
                h = gg * hpg + hh
                cols = slice(hh * tq, (hh + 1) * tq)
                o = acc_ref[gg, 0:dh, cols] * (gate[hh:hh + 1, :] / acc_ref[gg, dh:dh + 1, cols])
                o_ref[:, h * dh:(h + 1) * dh] = o.T.astype(o_ref.dtype)


def _band_steps(seq, window):
    tq, tk = ATT_T, ATT_TK
    steps = []
    for qi in range(seq // tq):
        k_hi = qi * tq // tk
        k_lo = max(0, (qi * tq - window + 1) // tk) if window else 0
        steps += [(qi, ki, min((qi * tq - ki * tk) // tq, ATT_NVAR - 1)) for ki in range(k_lo, k_hi + 1)]
    return steps


def _band_attention(proj, bias_by_dist, steps, koff, voff, gates, branch, batch, seq, selb=None, e_mat=None):
    dh, g, hpg = NSA_DH, NSA_KV_GROUPS, NSA_HPG
    tq, tk = ATT_T, ATT_TK
    nq, nk = seq // tq, seq // tk
    qi = np.array([s[0] for s in steps], np.int32)
    ki = np.array([s[1] for s in steps], np.int32)
    var = np.array([s[2] for s in steps], np.int32)
    first = np.concatenate([[1], (qi[1:] != qi[:-1]).astype(np.int32)]).astype(np.int32)
    last = np.concatenate([(qi[1:] != qi[:-1]).astype(np.int32), [1]]).astype(np.int32)
    qcol, kcol, vcol = IN_OFF[4] // NSA_W, koff // KV_W, voff // KV_W
    use_sel = selb is not None
    in_specs = [
        pl.BlockSpec((tq, NSA_W), lambda b, t, qi_r, ki_r, v_r, f_r, l_r: (b * nq + qi_r[t], qcol)),
        pl.BlockSpec((tk, KV_W), lambda b, t, qi_r, ki_r, v_r, f_r, l_r: (b * nk + ki_r[t], kcol)),
        pl.BlockSpec((tk, KV_W), lambda b, t, qi_r, ki_r, v_r, f_r, l_r: (b * nk + ki_r[t], vcol)),
        pl.BlockSpec(bias_by_dist.shape, lambda b, t, qi_r, ki_r, v_r, f_r, l_r: (0, 0, 0, 0, 0)),
        pl.BlockSpec((1, g, 1, GATE_ROWS, tq),
                     lambda b, t, qi_r, ki_r, v_r, f_r, l_r: (b, 0, branch, 0, qi_r[t])),
    ]
    args = [proj, proj, proj, bias_by_dist, gates]
    if use_sel:
        in_specs += [
            pl.BlockSpec((1, g, SEL_PAD, tq), lambda b, t, qi_r, ki_r, v_r, f_r, l_r: (b, 0, 0, qi_r[t])),
            pl.BlockSpec((tk, SEL_PAD), lambda b, t, qi_r, ki_r, v_r, f_r, l_r: (ki_r[t], 0)),
        ]
        args += [selb, e_mat]
    kdim = dh + SEL_PAD if use_sel else dh
    grid_spec = pltpu.PrefetchScalarGridSpec(
        num_scalar_prefetch=5,
        grid=(batch, len(steps)),
        in_specs=in_specs,
        out_specs=pl.BlockSpec((tq, NSA_W), lambda b, t, qi_r, ki_r, v_r, f_r, l_r: (b * nq + qi_r[t], 0)),
        scratch_shapes=[pltpu.VMEM((g, kdim, hpg * tq), BF16), pltpu.VMEM((g, 1, hpg * tq), F32),
                        pltpu.VMEM((g, dh + ACC_PAD, hpg * tq), F32),
                        pltpu.VMEM((g, ATT_NVAR - 1, tk, hpg * tq), F32)],
    )
    return pl.pallas_call(
        functools.partial(_band_kernel, use_sel=use_sel),
        grid_spec=grid_spec,
        out_shape=jax.ShapeDtypeStruct((batch * seq, NSA_W), BF16),
        compiler_params=pltpu.CompilerParams(
            dimension_semantics=("arbitrary", "arbitrary"), vmem_limit_bytes=VMEM_LIMIT),
        name="sel_attn" if use_sel else "win_attn",
    )(jnp.asarray(qi), jnp.asarray(ki), jnp.asarray(var), jnp.asarray(first), jnp.asarray(last), *args)


def _out_kernel(x_ref, yr_ref, oc_ref, os_ref, ow_ref, w_ref, o_ref):
    y_nsa = oc_ref[...].astype(F32) + os_ref[...].astype(F32) + ow_ref[...].astype(F32)
    y = jnp.concatenate([yr_ref[...], y_nsa.astype(BF16)], axis=-1)
    o_ref[...] = x_ref[...] + _dot(y, w_ref[...])


def _out_proj(x2d, y_ret, o_cmp, o_sel, o_win, w_out):
    tok, d = x2d.shape
    tm = OUT_TM
    wmix = y_ret.shape[1]
    row = lambda shape: pl.BlockSpec(shape, lambda i: (i, 0))
    return pl.pallas_call(
        _out_kernel,
        grid=(tok // tm,),
        in_specs=[row((tm, d)), row((tm, wmix)), row((tm, wmix)), row((tm, wmix)), row((tm, wmix)),
                  pl.BlockSpec(w_out.shape, lambda i: (0, 0))],
        out_specs=row((tm, d)),
        out_shape=jax.ShapeDtypeStruct((tok, d), F32),
        compiler_params=pltpu.CompilerParams(
            dimension_semantics=("parallel",), vmem_limit_bytes=VMEM_LIMIT),
        name="out_proj",
    )(x2d, y_ret, o_cmp, o_sel, o_win, w_out)


def _t5_bucket_of(rel):
    n = jnp.maximum(rel, 0)
    max_exact = REL_BUCKETS // 2
    nf = jnp.maximum(n, 1).astype(F32)
    large = max_exact + (jnp.log(nf / max_exact) / math.log(REL_MAX_DIST / max_exact)
                         * (REL_BUCKETS - max_exact)).astype(jnp.int32)
    large = jnp.minimum(large, REL_BUCKETS - 1)
    return jnp.where(n < max_exact, n, large)


def _bias_lookup(rel_bias, rel):
    bucket = _t5_bucket_of(rel)[None]
    tab = rel_bias.astype(F32).reshape((rel_bias.shape[0], REL_BUCKETS) + (1,) * rel.ndim)
    out = jnp.zeros((rel_bias.shape[0],) + rel.shape, F32)
    for b in range(REL_BUCKETS):
        out = jnp.where(bucket == b, tab[:, b], out)
    return out


def _bias_by_distance(rel_bias):
    t, g, hpg = ATT_T, NSA_KV_GROUPS, NSA_HPG
    assert ATT_TK == t and t >= REL_MAX_DIST and WIN == t
    by_dist = _bias_lookup(rel_bias, jnp.arange(t, dtype=jnp.int32)) * LOG2E
    far = jnp.broadcast_to(by_dist[:, t - 1:t], by_dist.shape)
    tab = jnp.stack([by_dist, far], axis=1).reshape(g, hpg, 2, 1, t)
    return jnp.broadcast_to(tab, (g, hpg, 2, 8, t))


def _token_mix(x2d, mix_norm, w_in, ret_gn_gain, pe_k, w1_k, w2_k, pe_v, w1_v, w2_v, w_out, rel_bias,
               batch, seq):
    d = x2d.shape[1]
    dh = NSA_DH
    w_pad = _to_bf16_transposed(w_in.T, D_IN_PAD)
    proj, cmp_kv = _inproj(x2d, mix_norm, w_pad)
    y_ret = _retention(proj, ret_gn_gain, batch, seq)

    gates = proj[:, IN_OFF[11]:IN_OFF[11] + IN_SPLITS[11]].astype(F32)
    gates = gates.reshape(batch, seq, NSA_KV_GROUPS, NSA_HPG, N_BRANCH).transpose(0, 2, 4, 3, 1)
    gates = jnp.pad(gates, ((0, 0), (0, 0), (0, 0), (0, GATE_ROWS - NSA_HPG), (0, 0)))

    k_cmp, v_cmp = _compress(cmp_kv, pe_k, w1_k, w2_k, pe_v, w1_v, w2_v, batch, seq)
    o_cmp, selb = _cmp_attention(proj, k_cmp, v_cmp, rel_bias, gates, batch, seq)

    bias_by_dist = _bias_by_distance(rel_bias)
    e_np = np.zeros((seq, SEL_PAD), np.float32)
    e_np[np.arange(seq), np.arange(seq) // SEL_LEN] = 1.0
    o_sel = _band_attention(proj, bias_by_dist, _band_steps(seq, 0),
                            IN_OFF[7], IN_OFF[8], gates, 1, batch, seq,
                            selb=selb, e_mat=jnp.asarray(e_np, BF16))
    o_win = _band_attention(proj, bias_by_dist, _band_steps(seq, WIN),
                            IN_OFF[9], IN_OFF[10], gates, 2, batch, seq)
    return _out_proj(x2d, y_ret, o_cmp, o_sel, o_win, _to_bf16(w_out))


def _cast_t_kernel(x_ref, o_ref, *, valid_rows):
    tr = x_ref.shape[0]
    row = pl.program_id(0) * tr + lax.broadcasted_iota(jnp.int32, x_ref.shape, 0)
    o_ref[...] = jnp.where(row < valid_rows, x_ref[...], 0.0).T.astype(o_ref.dtype)


def _to_bf16_transposed(w_t, cols):
    c, r = w_t.shape
    tr = CAST_T_ROWS
    return pl.pallas_call(
        functools.partial(_cast_t_kernel, valid_rows=c),
        grid=(cols // tr,),
        in_specs=[pl.BlockSpec((tr, r), lambda i: (i, 0))],
        out_specs=pl.BlockSpec((r, tr), lambda i: (0, i)),
        out_shape=jax.ShapeDtypeStruct((r, cols), BF16),
        compiler_params=pltpu.CompilerParams(
            dimension_semantics=("parallel",), vmem_limit_bytes=VMEM_LIMIT),
        name="to_bf16_t",
    )(w_t)


def _cast_kernel(x_ref, o_ref):
    o_ref[...] = x_ref[...].astype(o_ref.dtype)


def _to_bf16(w):
    r, c = w.shape
    tr = 1 << ((CAST_BLOCK_BYTES // (4 * c)).bit_length() - 1)
    while r % tr:
        tr //= 2
    return pl.pallas_call(
        _cast_kernel,
        grid=(r // tr,),
        in_specs=[pl.BlockSpec((tr, c), lambda i: (i, 0))],
        out_specs=pl.BlockSpec((tr, c), lambda i: (i, 0)),
        out_shape=jax.ShapeDtypeStruct((r, c), BF16),
        compiler_params=pltpu.CompilerParams(
            dimension_semantics=("parallel",), vmem_limit_bytes=VMEM_LIMIT),
        name="to_bf16",
    )(w)


def kernel(x, ffn1_norm, ffn1_w1, ffn1_w3, ffn1_w2, mix_norm, w_in, ret_gn_gain, cmp_pe_k, cmp_w1_k,
           cmp_w2_k, cmp_pe_v, cmp_w1_v, cmp_w2_v, w_out, ffn2_norm, ffn2_w1, ffn2_w3, ffn2_w2,
           rel_bias, final_norm):
    batch, seq, d = x.shape
    depth = ffn1_norm.shape[0]
    h = x.reshape(batch * seq, d)
    for l in range(depth):
        last = l == depth - 1
        h = _ffn(h, ffn1_norm[l], _to_bf16(ffn1_w1[l]), _to_bf16(ffn1_w3[l]), _to_bf16(ffn1_w2[l]),
                 final_norm, False)
        h = _token_mix(h, mix_norm[l], w_in[l], ret_gn_gain[l], cmp_pe_k[l], cmp_w1_k[l], cmp_w2_k[l],
                       cmp_pe_v[l], cmp_w1_v[l], cmp_w2_v[l], w_out[l], rel_bias, batch, seq)
        h = _ffn(h, ffn2_norm[l], _to_bf16(ffn2_w1[l]), _to_bf16(ffn2_w3[l]), _to_bf16(ffn2_w2[l]),
                 final_norm, last)
    return h.reshape(batch, seq, d)
```

```python
import functools
import math

import jax
import jax.numpy as jnp
import numpy as np
from jax import lax
from jax.experimental import pallas as pl
from jax.experimental.pallas import tpu as pltpu

F32 = jnp.float32
BF16 = jnp.bfloat16

RET_HEADS = 4
RET_DK = 256
RET_DV = 256
ROPE_BASE = 10000.0
NSA_HEADS = 8
NSA_KV_GROUPS = 2
NSA_HPG = NSA_HEADS // NSA_KV_GROUPS
NSA_DH = 128
CMP_LEN = 32
CMP_STRIDE = 16
SEL_LEN = 64
SEL_TOPK = 16
WIN = 512
N_BRANCH = 3
REL_BUCKETS = 32
REL_MAX_DIST = 128
EPS = 1e-6
NEG = -1e30
FORCE = 1e4

RET_W = RET_HEADS * RET_DV
NSA_W = NSA_HEADS * NSA_DH
KV_W = NSA_KV_GROUPS * NSA_DH
IN_SPLITS = [RET_HEADS * RET_DK, RET_HEADS * RET_DK, RET_W, RET_W, NSA_W,
             KV_W, KV_W, KV_W, KV_W, KV_W, KV_W, NSA_HEADS * N_BRANCH]
D_IN = sum(IN_SPLITS)
IN_OFF = [sum(IN_SPLITS[:i]) for i in range(len(IN_SPLITS))]

LANE = 128
IN_TN = 2304
D_IN_PAD = 3 * IN_TN
CMP_KV_TILE = IN_OFF[5] // IN_TN
CMP_KV_LOCAL = IN_OFF[5] % IN_TN
assert CMP_KV_TILE > 0 and CMP_KV_LOCAL + 2 * KV_W <= IN_TN and IN_OFF[6] == IN_OFF[5] + KV_W
GATE_ROWS = 8
FFN_TM = 1024
FFN_TF = 512
IN_TM = 512
RET_C = 256
CMP_TQ = 512
ATT_T = 512
ATT_TK = 512
ATT_NVAR = 3
OUT_TM = 512
SEL_PAD = 128
ACC_PAD = 8
CMP_BAND = 128
VMEM_LIMIT = 56 * 1024 * 1024
CAST_BLOCK_BYTES = 6 * 1024 * 1024
CAST_T_ROWS = 256
LOG2E = math.log2(math.e)


def _dot(a, b):
    return jnp.dot(a, b, preferred_element_type=F32)


def _dot_nt(a, b):
    return lax.dot_general(a, b, (((1,), (1,)), ((), ())), preferred_element_type=F32)


def _dot_tn(a, b):
    return lax.dot_general(a, b, (((0,), (0,)), ((), ())), preferred_element_type=F32)


def _sigmoid(x):
    return 1.0 / (1.0 + jnp.exp(-x))


def _rms(x, g):
    ms = jnp.mean(x * x, axis=-1, keepdims=True)
    return x * lax.rsqrt(ms + EPS) * g


def _ffn_kernel(x_ref, g_ref, w1_ref, w3_ref, w2_ref, fg_ref, o_ref, n_ref, *, final_norm):
    j = pl.program_id(1)

    last = pl.num_programs(1) - 1

    def step(first, norm_out):
        if first:
            n = _rms(x_ref[...], g_ref[...]).astype(BF16)
            n_ref[...] = n
        else:
            n = n_ref[...]
        a = _dot(n, w1_ref[...])
        b = _dot(n, w3_ref[...])
        h = (0.5 * a * _sigmoid(a) * b).astype(BF16)
        y = (x_ref[...] if first else o_ref[...]) + _dot(h, w2_ref[...])
        o_ref[...] = _rms(y, fg_ref[...]) if norm_out else y

    pl.when(j == 0)(functools.partial(step, True, False))
    if final_norm:
        pl.when((j > 0) & (j < last))(functools.partial(step, False, False))
        pl.when(j == last)(functools.partial(step, False, True))
    else:
        pl.when(j > 0)(functools.partial(step, False, False))


def _ffn(x2d, g, w1, w3, w2, fg, final_norm):
    tok, d = x2d.shape
    dff = w1.shape[1]
    tm, tf = FFN_TM, FFN_TF
    assert dff // tf >= 2
    return pl.pallas_call(
        functools.partial(_ffn_kernel, final_norm=final_norm),
        grid=(tok // tm, dff // tf),
        in_specs=[
            pl.BlockSpec((tm, d), lambda i, j: (i, 0)),
            pl.BlockSpec((1, d), lambda i, j: (0, 0)),
            pl.BlockSpec((d, tf), lambda i, j: (0, j)),
            pl.BlockSpec((d, tf), lambda i, j: (0, j)),
            pl.BlockSpec((tf, d), lambda i, j: (j, 0)),
            pl.BlockSpec((1, d), lambda i, j: (0, 0)),
        ],
        out_specs=pl.BlockSpec((tm, d), lambda i, j: (i, 0)),
        out_shape=jax.ShapeDtypeStruct((tok, d), F32),
        scratch_shapes=[pltpu.VMEM((tm, d), BF16)],
        compiler_params=pltpu.CompilerParams(
            dimension_semantics=("parallel", "arbitrary"), vmem_limit_bytes=VMEM_LIMIT),
        name="ffn",
    )(x2d, g.reshape(1, d), w1, w3, w2, fg.reshape(1, d))


def _inproj_kernel(x_ref, g_ref, w_ref, o_ref, kv_ref, n_ref):
    j = pl.program_id(1)
    tn = o_ref.shape[1]
    for jj in range(w_ref.shape[1] // tn):
        @pl.when(j == jj)
        def _():
            if jj == 0:
                n = _rms(x_ref[...], g_ref[...]).astype(BF16)
                n_ref[...] = n
            else:
                n = n_ref[...]
            res = _dot(n, w_ref[:, jj * tn:(jj + 1) * tn])
            o_ref[...] = res.astype(o_ref.dtype)
            if jj == CMP_KV_TILE:
                kv_ref[...] = res[:, CMP_KV_LOCAL:CMP_KV_LOCAL + 2 * KV_W]


def _inproj(x2d, g, w_pad):
    tok, d = x2d.shape
    n_out = w_pad.shape[1]
    tm, tn = IN_TM, IN_TN
    return pl.pallas_call(
        _inproj_kernel,
        grid=(tok // tm, n_out // tn),
        in_specs=[
            pl.BlockSpec((tm, d), lambda i, j: (i, 0)),
            pl.BlockSpec((1, d), lambda i, j: (0, 0)),
            pl.BlockSpec(w_pad.shape, lambda i, j: (0, 0), pipeline_mode=pl.Buffered(1)),
        ],
        out_specs=[pl.BlockSpec((tm, tn), lambda i, j: (i, j)),
                   pl.BlockSpec((tm, 2 * KV_W), lambda i, j: (i, 0))],
        out_shape=[jax.ShapeDtypeStruct((tok, n_out), BF16),
                   jax.ShapeDtypeStruct((tok, 2 * KV_W), F32)],
        scratch_shapes=[pltpu.VMEM((tm, d), BF16)],
        compiler_params=pltpu.CompilerParams(
            dimension_semantics=("parallel", "arbitrary"), vmem_limit_bytes=VMEM_LIMIT),
        name="in_proj",
    )(x2d, g.reshape(1, d), w_pad)


def _ret_kernel(cd_ref, q_ref, k_ref, v_ref, g_ref, cos_ref, sin_ref, dec_ref, xi_ref, zeta_ref,
                gain_ref, o_ref, state_ref):
    @pl.when(pl.program_id(0) == 0)
    def _():
        state_ref[...] = jnp.zeros_like(state_ref)

    cos = cos_ref[...]
    sin = sin_ref[...]
    half = RET_DK // 2

    def rot(t):
        t1, t2 = t[:, :half], t[:, half:]
        return jnp.concatenate([t1 * cos - t2 * sin, t1 * sin + t2 * cos], axis=-1)

    for bi, h in [(bi, h) for bi in range(q_ref.shape[0]) for h in range(RET_HEADS)]:
        cs = slice(h * RET_DK, (h + 1) * RET_DK)
        q = rot(q_ref[bi, :, cs].astype(F32))
        k = rot(k_ref[bi, :, cs].astype(F32))
        v = v_ref[bi, :, cs].astype(BF16)
        qb = q.astype(BF16)
        s = _dot_nt(qb, k.astype(BF16)) * dec_ref[h]
        state = state_ref[bi, h]
        y = _dot(s.astype(BF16), v) + _dot((q * xi_ref[h]).astype(BF16), state.astype(BF16))
        state_ref[bi, h] = cd_ref[h] * state + _dot_tn((k * zeta_ref[h]).astype(BF16), v)
        mu = jnp.mean(y, axis=-1, keepdims=True)
        yc = y - mu
        var = jnp.mean(yc * yc, axis=-1, keepdims=True)
        yn = yc * lax.rsqrt(var + EPS) * gain_ref[:, cs]
        gate = g_ref[bi, :, cs].astype(F32)
        o_ref[bi, :, cs] = (gate * _sigmoid(gate) * yn).astype(o_ref.dtype)


def _retention(proj, gain, batch, seq):
    tok = proj.shape[0]
    c = RET_C
    n_chunks = seq // c
    h, dk = RET_HEADS, RET_DK
    half = dk // 2
    inv = np.float32(ROPE_BASE) ** (-np.arange(half, dtype=np.float32) / np.float32(half))
    ang = np.arange(seq, dtype=np.float32)[:, None] * inv[None, :]
    cos, sin = np.cos(ang).astype(np.float32), np.sin(ang).astype(np.float32)
    log_g = np.log(1.0 - 2.0 ** (-5.0 - np.arange(h, dtype=np.float64)))
    idx = np.arange(c, dtype=np.float64)
    diff = idx[:, None] - idx[None, :]
    dec = np.where(diff >= 0.0, np.exp(np.maximum(diff, 0.0)[None] * log_g[:, None, None]), 0.0)
    k_scale = dk ** -0.5
    assert math.frexp(k_scale)[0] == 0.5
    dec = dec.astype(np.float32) * np.float32(k_scale)
    xi = np.broadcast_to(np.exp((idx + 1.0)[None] * log_g[:, None])[:, :, None], (h, c, dk)).astype(np.float32)
    zeta = np.broadcast_to(np.exp((c - 1.0 - idx)[None] * log_g[:, None])[:, :, None],
                           (h, c, dk)).astype(np.float32) * np.float32(k_scale)
    cd = np.exp(c * log_g).astype(np.float32)

    w = h * dk
    proj3 = proj.reshape(batch, seq, proj.shape[1])
    col = lambda j: pl.BlockSpec((batch, c, w), lambda n: (0, n, j))
    y = pl.pallas_call(
        _ret_kernel,
        grid=(n_chunks,),
        in_specs=[
            pl.BlockSpec(memory_space=pltpu.SMEM),
            col(0), col(1), col(2), col(3),
            pl.BlockSpec((c, half), lambda n: (n, 0)),
            pl.BlockSpec((c, half), lambda n: (n, 0)),
            pl.BlockSpec((h, c, c), lambda n: (0, 0, 0)),
            pl.BlockSpec((h, c, dk), lambda n: (0, 0, 0)),
            pl.BlockSpec((h, c, dk), lambda n: (0, 0, 0)),
            pl.BlockSpec((1, w), lambda n: (0, 0)),
        ],
        out_specs=pl.BlockSpec((batch, c, w), lambda n: (0, n, 0)),
        out_shape=jax.ShapeDtypeStruct((batch, seq, w), BF16),
        scratch_shapes=[pltpu.VMEM((batch, h, dk, RET_DV), F32)],
        compiler_params=pltpu.CompilerParams(
            dimension_semantics=("arbitrary",), vmem_limit_bytes=VMEM_LIMIT),
        name="retention",
    )(cd, proj3, proj3, proj3, proj3, cos, sin, dec, xi, zeta, gain.reshape(1, w))
    return y.reshape(tok, w)


def _compress_kernel(k_ref, v_ref, pek_ref, w1k_ref, w1kf_ref, w2k_ref, pev_ref, w1v_ref, w1vf_ref,
                     w2v_ref, ok_ref, ov_ref):
    dh = NSA_DH
    nblk = k_ref.shape[0] // CMP_STRIDE

    def one(x_ref, pe_ref, w1_ref, w1f_ref, w2_ref, o_ref):
        ab = jnp.zeros((nblk, 2 * dh), F32)
        for l in range(CMP_STRIDE):
            rows = x_ref[pl.ds(l, nblk, stride=CMP_STRIDE), :].astype(BF16)
            ab = ab + _dot(rows, w1_ref[l])
        a, b = ab[:, :dh], ab[:, dh:]
        b_next = pltpu.roll(b, nblk - 1, axis=0)
        ridx = lax.broadcasted_iota(jnp.int32, (nblk, dh), 0)
        b_next = jnp.where(ridx < nblk - 1, b_next, 0.0)
        pe_term = _dot(pe_ref[...], w1f_ref[...])[0:1, :]
        hdn = a + b_next + pe_term
        o_ref[0, 0] = _dot((hdn * _sigmoid(hdn)).astype(BF16), w2_ref[...])

    one(k_ref, pek_ref, w1k_ref, w1kf_ref, w2k_ref, ok_ref)
    one(v_ref, pev_ref, w1v_ref, w1vf_ref, w2v_ref, ov_ref)


def _compress(kv, pe_k, w1_k, w2_k, pe_v, w1_v, w2_v, batch, seq):
    dh, g = NSA_DH, NSA_KV_GROUPS
    nblk = seq // CMP_STRIDE
    half = CMP_LEN // 2

    def prep(pe, w1, w2):
        w1b = w1.astype(BF16)
        w1_pair = jnp.concatenate([w1b[:half], w1b[half:]], axis=-1)
        pe_flat = jnp.broadcast_to(pe.reshape(1, CMP_LEN * dh), (8, CMP_LEN * dh)).astype(BF16)
        return pe_flat, w1_pair, w1b.reshape(CMP_LEN * dh, dh), w2.astype(BF16)

    args_k = prep(pe_k, w1_k, w2_k)
    args_v = prep(pe_v, w1_v, w2_v)
    kcol = 0
    vcol = KV_W // dh
    const = lambda shape: pl.BlockSpec(shape, lambda b, gg: (0,) * len(shape))
    wspecs = [const((8, CMP_LEN * dh)), const((half, dh, 2 * dh)), const((CMP_LEN * dh, dh)),
              const((dh, dh))]
    out_spec = pl.BlockSpec((1, 1, nblk, dh), lambda b, gg: (b, gg, 0, 0))
    return pl.pallas_call(
        _compress_kernel,
        grid=(batch, g),
        in_specs=[pl.BlockSpec((seq, dh), lambda b, gg: (b, kcol + gg)),
                  pl.BlockSpec((seq, dh), lambda b, gg: (b, vcol + gg))] + wspecs + wspecs,
        out_specs=[out_spec, out_spec],
        out_shape=[jax.ShapeDtypeStruct((batch, g, nblk, dh), F32)] * 2,
        compiler_params=pltpu.CompilerParams(
            dimension_semantics=("parallel", "parallel"), vmem_limit_bytes=VMEM_LIMIT),
        name="compress",
    )(kv, kv, *args_k, *args_v)


def _cmpattn_kernel(q_ref, kc_ref, vc_ref, tb_ref, pp_ref, ovt_ref, gate_ref, o_ref, sel_ref, rank_ref):
    tq = q_ref.shape[0]
    ncp = kc_ref.shape[2]
    dh = NSA_DH
    s0 = pl.program_id(2) * tq
    hpg = NSA_HPG
    kc = kc_ref[0, 0].astype(BF16)
    vc = vc_ref[0, 0].astype(BF16)
    scale = dh ** -0.5
    q = jnp.concatenate([q_ref[:, hh * dh:(hh + 1) * dh] for hh in range(hpg)], axis=0).astype(BF16)
    bias = _dot(tb_ref[...].reshape(hpg * tq, tb_ref.shape[2]), pp_ref[0])
    lm = _dot_nt(q, kc) * scale + bias
    m = jnp.max(lm, axis=-1, keepdims=True)
    e = jnp.exp(lm - m)
    den = jnp.sum(e, axis=-1, keepdims=True)
    p = e * jnp.where(m > 0.5 * NEG, 1.0 / den, 0.0)
    o = _dot(p.astype(BF16), vc)
    gate = _sigmoid(gate_ref[...].astype(F32))
    gg = pl.program_id(1)
    psum = p[0:tq]
    for hh in range(hpg):
        g_col = gate[:, hh * N_BRANCH:hh * N_BRANCH + 1]
        for g in range(1, NSA_KV_GROUPS):
            lane = (g * hpg + hh) * N_BRANCH
            g_col = jnp.where(gg == g, gate[:, lane:lane + 1], g_col)
        o_ref[:, hh * dh:(hh + 1) * dh] = (o[hh * tq:(hh + 1) * tq] * g_col).astype(o_ref.dtype)
        if hh:
            psum = psum + p[hh * tq:(hh + 1) * tq]
    nsel = ovt_ref.shape[0]
    imp_t = _dot_nt(ovt_ref[...], psum.astype(BF16))
    jb = lax.broadcasted_iota(jnp.int32, (nsel, tq), 0)
    pos = s0 + lax.broadcasted_iota(jnp.int32, (nsel, tq), 1)
    cur = jnp.right_shift(pos, SEL_LEN.bit_length() - 1)
    causal = jb * SEL_LEN <= pos
    forced = (jb == 0) | (jb == cur) | (jb == cur - 1)
    score = jnp.where(forced, FORCE, jnp.where(causal, imp_t, NEG))
    per_tile = tq // SEL_LEN
    rank_ref[...] = jnp.zeros_like(rank_ref)
    for c in range(nsel // per_tile):
        @pl.when(c <= pl.program_id(2))
        def _():
            rank = rank_ref[...]
            for kk in range(c * per_tile, (c + 1) * per_tile):
                row = score[kk:kk + 1, :]
                beats = (row > score) | ((row == score) & (jb > kk))
                rank = rank + jnp.where(beats, 1, 0)
            rank_ref[...] = rank
    rank = rank_ref[...]
    selb = jnp.where((rank < SEL_TOPK) & causal, 0.0, NEG)
    selb = jnp.concatenate([selb, jnp.zeros((SEL_PAD - nsel, tq), F32)], axis=0)
    sel_ref[0, 0] = selb.astype(BF16)


def _cmp_attention(proj, k_cmp, v_cmp, rel_bias, batch, seq):
    dh, g, hpg = NSA_DH, NSA_KV_GROUPS, NSA_HPG
    tq = CMP_TQ
    nq = seq // tq
    ncp = k_cmp.shape[2]
    nsel = seq // SEL_LEN
    n_cmp = (seq - CMP_LEN) // CMP_STRIDE + 1
    cmp_idx = np.arange(n_cmp)[:, None] * CMP_STRIDE + np.arange(CMP_LEN)[None, :]
    overlap = ((cmp_idx // SEL_LEN)[:, :, None] == np.arange(nsel)[None, None, :]).sum(1) / CMP_LEN
    ovt = np.zeros((nsel, ncp), np.float32)
    ovt[:, :n_cmp] = overlap.T
    nb = tq // CMP_STRIDE
    assert 2 * nb + 2 <= CMP_BAND and CMP_STRIDE * (nb + 1) - (CMP_LEN - 1) >= REL_MAX_DIST
    i = np.arange(tq)[:, None]
    r = np.arange(CMP_BAND)[None, :]
    rel = np.where(r < 2 * nb, i - CMP_STRIDE * (r - nb) - (CMP_LEN - 1), REL_MAX_DIST)
    tb = jnp.where(jnp.asarray(r <= 2 * nb), _bias_lookup(rel_bias, jnp.asarray(rel, jnp.int32)), 0.0)
    tb = jnp.where(jnp.asarray((rel < 0) | (r == 2 * nb + 1)), NEG, tb)
    tb_hi = tb.astype(BF16)
    tb_lo = (tb - tb_hi.astype(F32)).astype(BF16)
    tb = jnp.concatenate([tb_hi, tb_lo], axis=-1)
    n = np.arange(ncp)[None, None, :]
    first = (np.arange(nq) * nb - nb)[:, None, None]
    rr = np.arange(CMP_BAND)[None, :, None]
    pp = np.where(rr < 2 * nb, n == first + rr,
                  ((rr == 2 * nb) & (n < first)) | ((rr == 2 * nb + 1) & (n >= first + 2 * nb)))
    pp = pp.astype(np.float32)
    pp = np.concatenate([pp, pp], axis=1)
    qcol = IN_OFF[4] // (hpg * dh)
    assert IN_OFF[11] % LANE == 0 and IN_SPLITS[11] <= LANE
    gcol = IN_OFF[11] // LANE
    return pl.pallas_call(
        _cmpattn_kernel,
        grid=(batch, g, nq),
        in_specs=[
            pl.BlockSpec((tq, hpg * dh), lambda b, gg, t: (b * nq + t, qcol + gg)),
            pl.BlockSpec((1, 1, ncp, dh), lambda b, gg, t: (b, gg, 0, 0)),
            pl.BlockSpec((1, 1, ncp, dh), lambda b, gg, t: (b, gg, 0, 0)),
            pl.BlockSpec((hpg, tq, 2 * CMP_BAND), lambda b, gg, t: (gg, 0, 0)),
            pl.BlockSpec((1, 2 * CMP_BAND, ncp), lambda b, gg, t: (t, 0, 0)),
            pl.BlockSpec((nsel, ncp), lambda b, gg, t: (0, 0)),
            pl.BlockSpec((tq, LANE), lambda b, gg, t: (b * nq + t, gcol)),
        ],
        out_specs=[
            pl.BlockSpec((tq, hpg * dh), lambda b, gg, t: (b * nq + t, gg)),
            pl.BlockSpec((1, 1, SEL_PAD, tq), lambda b, gg, t: (b, gg, 0, t)),
        ],
        out_shape=[jax.ShapeDtypeStruct((batch * seq, g * hpg * dh), BF16),
                   jax.ShapeDtypeStruct((batch, g, SEL_PAD, seq), BF16)],
        scratch_shapes=[pltpu.VMEM((nsel, tq), jnp.int32)],
        compiler_params=pltpu.CompilerParams(
            dimension_semantics=("parallel", "parallel", "parallel"), vmem_limit_bytes=VMEM_LIMIT),
        name="cmp_attn",
    )(proj, k_cmp, v_cmp, tb, jnp.asarray(pp, BF16), jnp.asarray(ovt, BF16), proj)


def _band_kernel(qi_ref, ki_ref, var_ref, first_ref, last_ref, q_ref, k_ref, v_ref, fd_ref, gate_ref,
                 *rest, use_sel):
    if use_sel:
        selb_ref, e_ref, o_ref, qt_ref, m_ref, acc_ref, bias_ref = rest
    else:
        o_ref, qt_ref, m_ref, acc_ref, bias_ref = rest
    t = pl.program_id(1)
    dh, hpg, ng = NSA_DH, NSA_HPG, NSA_KV_GROUPS
    tq, tk = q_ref.shape[0], k_ref.shape[0]

    @pl.when((pl.program_id(0) == 0) & (t == 0))
    def _():
        upper = (lax.broadcasted_iota(jnp.int32, (tk, tq), 1) >= lax.broadcasted_iota(jnp.int32, (tk, tq), 0))
        for gg in range(ng):
            for hh in range(hpg):
                cols = slice(hh * tq, (hh + 1) * tq)
                by_dist = jnp.broadcast_to(fd_ref[gg, hh, 0, 0:1, :], (tk, tq))
                cyc = pltpu.roll(by_dist, 0, 1, stride=1, stride_axis=0)
                bias_ref[gg, 0, :, cols] = jnp.where(upper, cyc, NEG)
                if use_sel:
                    far = jnp.broadcast_to(fd_ref[gg, hh, 1, 0:1, :], (tk, tq))
                    bias_ref[gg, 1, :, cols] = jnp.where(upper, far, cyc)
                else:
                    bias_ref[gg, 1, :, cols] = jnp.where(upper, NEG, cyc)

    @pl.when(first_ref[t] == 1)
    def _():
        m_ref[...] = jnp.full_like(m_ref, NEG)
        acc_ref[...] = jnp.zeros_like(acc_ref)
        scale = dh ** -0.5 * LOG2E
        for gg in range(ng):
            for hh in range(hpg):
                h = gg * hpg + hh
                q = q_ref[:, h * dh:(h + 1) * dh].astype(F32)
                qt_ref[gg, 0:dh, hh * tq:(hh + 1) * tq] = (q * scale).T.astype(BF16)
                if use_sel:
                    qt_ref[gg, dh:, hh * tq:(hh + 1) * tq] = selb_ref[0, gg]

    var = var_ref[t]

    def update(far):
        ones = jnp.ones((ACC_PAD, tk), F32)
        scores = []
        for gg in range(ng):
            k = k_ref[:, gg * dh:(gg + 1) * dh]
            if use_sel:
                k = jnp.concatenate([k, e_ref[...]], axis=1)
            s = _dot(k, qt_ref[gg])
            scores.append(s if far else s + bias_ref[gg, var])
        for gg in range(ng):
            vt = jnp.concatenate([v_ref[:, gg * dh:(gg + 1) * dh].astype(F32).T, ones], axis=0).astype(BF16)
            s = scores[gg]
            m_prev = m_ref[gg]
            col_max = jnp.max(s, axis=0, keepdims=True)
            if far:
                c = jnp.concatenate([fd_ref[gg, hh, 1, 0:1, :] for hh in range(hpg)], axis=1)
                m_new = jnp.maximum(m_prev, col_max + c)
                shift = m_new - c
            else:
                m_new = jnp.maximum(m_prev, col_max)
                shift = m_new
            alpha = jnp.exp2(m_prev - m_new)
            p = jnp.exp2(s - shift).astype(BF16)
            acc_ref[gg] = alpha * acc_ref[gg] + _dot(vt, p)
            m_ref[gg] = m_new

    if use_sel:
        pl.when(var == ATT_NVAR - 1)(functools.partial(update, True))
        pl.when(var != ATT_NVAR - 1)(functools.partial(update, False))
    else:
        update(False)

    @pl.when(last_ref[t] == 1)
    def _():
        for gg in range(ng):
            gate = _sigmoid(gate_ref[0, gg, 0])
            for hh in range(hpg):
                h = gg * hpg + hh
                cols = slice(hh * tq, (hh + 1) * tq)
                o = acc_ref[gg, 0:dh, cols] * (gate[hh:hh + 1, :] / acc_ref[gg, dh:dh + 1, cols])
                o_ref[:, h * dh:(h + 1) * dh] = o.T.astype(o_ref.dtype)


def _band_steps(seq, window):
    tq, tk = ATT_T, ATT_TK
    steps = []
    for qi in range(seq // tq):
        k_hi = qi * tq // tk
        k_lo = max(0, (qi * tq - window + 1) // tk) if window else 0
        steps += [(qi, ki, min((qi * tq - ki * tk) // tq, ATT_NVAR - 1)) for ki in range(k_lo, k_hi + 1)]
    return steps


def _band_attention(proj, bias_by_dist, steps, koff, voff, gates, branch, batch, seq, selb=None, e_mat=None):
    dh, g, hpg = NSA_DH, NSA_KV_GROUPS, NSA_HPG
    tq, tk = ATT_T, ATT_TK
    nq, nk = seq // tq, seq // tk
    qi = np.array([s[0] for s in steps], np.int32)
    ki = np.array([s[1] for s in steps], np.int32)
    var = np.array([s[2] for s in steps], np.int32)
    first = np.concatenate([[1], (qi[1:] != qi[:-1]).astype(np.int32)]).astype(np.int32)
    last = np.concatenate([(qi[1:] != qi[:-1]).astype(np.int32), [1]]).astype(np.int32)
    qcol, kcol, vcol = IN_OFF[4] // NSA_W, koff // KV_W, voff // KV_W
    use_sel = selb is not None
    in_specs = [
        pl.BlockSpec((tq, NSA_W), lambda b, t, qi_r, ki_r, v_r, f_r, l_r: (b * nq + qi_r[t], qcol)),
        pl.BlockSpec((tk, KV_W), lambda b, t, qi_r, ki_r, v_r, f_r, l_r: (b * nk + ki_r[t], kcol)),
        pl.BlockSpec((tk, KV_W), lambda b, t, qi_r, ki_r, v_r, f_r, l_r: (b * nk + ki_r[t], vcol)),
        pl.BlockSpec(bias_by_dist.shape, lambda b, t, qi_r, ki_r, v_r, f_r, l_r: (0, 0, 0, 0, 0)),
        pl.BlockSpec((1, g, 1, GATE_ROWS, tq),
                     lambda b, t, qi_r, ki_r, v_r, f_r, l_r: (b, 0, branch, 0, qi_r[t])),
    ]
    args = [proj, proj, proj, bias_by_dist, gates]
    if use_sel:
        in_specs += [
            pl.BlockSpec((1, g, SEL_PAD, tq), lambda b, t, qi_r, ki_r, v_r, f_r, l_r: (b, 0, 0, qi_r[t])),
            pl.BlockSpec((tk, SEL_PAD), lambda b, t, qi_r, ki_r, v_r, f_r, l_r: (ki_r[t], 0)),
        ]
        args += [selb, e_mat]
    kdim = dh + SEL_PAD if use_sel else dh
    grid_spec = pltpu.PrefetchScalarGridSpec(
        num_scalar_prefetch=5,
        grid=(batch, len(steps)),
        in_specs=in_specs,
        out_specs=pl.BlockSpec((tq, NSA_W), lambda b, t, qi_r, ki_r, v_r, f_r, l_r: (b * nq + qi_r[t], 0)),
        scratch_shapes=[pltpu.VMEM((g, kdim, hpg * tq), BF16), pltpu.VMEM((g, 1, hpg * tq), F32),
                        pltpu.VMEM((g, dh + ACC_PAD, hpg * tq), F32),
                        pltpu.VMEM((g, ATT_NVAR - 1, tk, hpg * tq), F32)],
    )
    return pl.pallas_call(
        functools.partial(_band_kernel, use_sel=use_sel),
        grid_spec=grid_spec,
        out_shape=jax.ShapeDtypeStruct((batch * seq, NSA_W), BF16),
        compiler_params=pltpu.CompilerParams(
            dimension_semantics=("arbitrary", "arbitrary"), vmem_limit_bytes=VMEM_LIMIT),
        name="sel_attn" if use_sel else "win_attn",
    )(jnp.asarray(qi), jnp.asarray(ki), jnp.asarray(var), jnp.asarray(first), jnp.asarray(last), *args)


def _out_kernel(x_ref, yr_ref, oc_ref, os_ref, ow_ref, w_ref, o_ref):
    y_nsa = oc_ref[...].astype(F32) + os_ref[...].astype(F32) + ow_ref[...].astype(F32)
    y = jnp.concatenate([yr_ref[...], y_nsa.astype(BF16)], axis=-1)
    o_ref[...] = x_ref[...] + _dot(y, w_ref[...])


def _out_proj(x2d, y_ret, o_cmp, o_sel, o_win, w_out):
    tok, d = x2d.shape
    tm = OUT_TM
    wmix = y_ret.shape[1]
    row = lambda shape: pl.BlockSpec(shape, lambda i: (i, 0))
    return pl.pallas_call(
        _out_kernel,
        grid=(tok // tm,),
        in_specs=[row((tm, d)), row((tm, wmix)), row((tm, wmix)), row((tm, wmix)), row((tm, wmix)),
                  pl.BlockSpec(w_out.shape, lambda i: (0, 0))],
        out_specs=row((tm, d)),
        out_shape=jax.ShapeDtypeStruct((tok, d), F32),
        compiler_params=pltpu.CompilerParams(
            dimension_semantics=("parallel",), vmem_limit_bytes=VMEM_LIMIT),
        name="out_proj",
    )(x2d, y_ret, o_cmp, o_sel, o_win, w_out)


def _t5_bucket_of(rel):
    n = jnp.maximum(rel, 0)
    max_exact = REL_BUCKETS // 2
    nf = jnp.maximum(n, 1).astype(F32)
    large = max_exact + (jnp.log(nf / max_exact) / math.log(REL_MAX_DIST / max_exact)
                         * (REL_BUCKETS - max_exact)).astype(jnp.int32)
    large = jnp.minimum(large, REL_BUCKETS - 1)
    return jnp.where(n < max_exact, n, large)


def _bias_lookup(rel_bias, rel):
    bucket = _t5_bucket_of(rel)[None]
    tab = rel_bias.astype(F32).reshape((rel_bias.shape[0], REL_BUCKETS) + (1,) * rel.ndim)
    out = jnp.zeros((rel_bias.shape[0],) + rel.shape, F32)
    for b in range(REL_BUCKETS):
        out = jnp.where(bucket == b, tab[:, b], out)
    return out


def _bias_by_distance(rel_bias):
    t, g, hpg = ATT_T, NSA_KV_GROUPS, NSA_HPG
    assert ATT_TK == t and t >= REL_MAX_DIST and WIN == t
    by_dist = _bias_lookup(rel_bias, jnp.arange(t, dtype=jnp.int32)) * LOG2E
    far = jnp.broadcast_to(by_dist[:, t - 1:t], by_dist.shape)
    tab = jnp.stack([by_dist, far], axis=1).reshape(g, hpg, 2, 1, t)
    return jnp.broadcast_to(tab, (g, hpg, 2, 8, t))


def _token_mix(x2d, mix_norm, w_in, ret_gn_gain, pe_k, w1_k, w2_k, pe_v, w1_v, w2_v, w_out, rel_bias,
               batch, seq):
    d = x2d.shape[1]
    dh = NSA_DH
    w_pad = _to_bf16_transposed(w_in.T, D_IN_PAD)
    proj, cmp_kv = _inproj(x2d, mix_norm, w_pad)
    y_ret = _retention(proj, ret_gn_gain, batch, seq)

    gates = proj[:, IN_OFF[11]:IN_OFF[11] + IN_SPLITS[11]].astype(F32)
    gates = gates.reshape(batch, seq, NSA_KV_GROUPS, NSA_HPG, N_BRANCH).transpose(0, 2, 4, 3, 1)
    gates = jnp.pad(gates, ((0, 0), (0, 0), (0, 0), (0, GATE_ROWS - NSA_HPG), (0, 0)))

    k_cmp, v_cmp = _compress(cmp_kv, pe_k, w1_k, w2_k, pe_v, w1_v, w2_v, batch, seq)
    o_cmp, selb = _cmp_attention(proj, k_cmp, v_cmp, rel_bias, batch, seq)

    bias_by_dist = _bias_by_distance(rel_bias)
    e_np = np.zeros((seq, SEL_PAD), np.float32)
    e_np[np.arange(seq), np.arange(seq) // SEL_LEN] = 1.0
    o_sel = _band_attention(proj, bias_by_dist, _band_steps(seq, 0),
                            IN_OFF[7], IN_OFF[8], gates, 1, batch, seq,
                            selb=selb, e_mat=jnp.asarray(e_np, BF16))
    o_win = _band_attention(proj, bias_by_dist, _band_steps(seq, WIN),
                            IN_OFF[9], IN_OFF[10], gates, 2, batch, seq)
    return _out_proj(x2d, y_ret, o_cmp, o_sel, o_win, _to_bf16(w_out))


def _cast_t_kernel(x_ref, o_ref, *, valid_rows):
    tr = x_ref.shape[0]
    row = pl.program_id(0) * tr + lax.broadcasted_iota(jnp.int32, x_ref.shape, 0)
    o_ref[...] = jnp.where(row < valid_rows, x_ref[...], 0.0).T.astype(o_ref.dtype)


def _to_bf16_transposed(w_t, cols):
    c, r = w_t.shape
    tr = CAST_T_ROWS
    return pl.pallas_call(
        functools.partial(_cast_t_kernel, valid_rows=c),
        grid=(cols // tr,),
        in_specs=[pl.BlockSpec((tr, r), lambda i: (i, 0))],
        out_specs=pl.BlockSpec((r, tr), lambda i: (0, i)),
        out_shape=jax.ShapeDtypeStruct((r, cols), BF16),
        compiler_params=pltpu.CompilerParams(
            dimension_semantics=("parallel",), vmem_limit_bytes=VMEM_LIMIT),
        name="to_bf16_t",
    )(w_t)


def _cast_kernel(x_ref, o_ref):
    o_ref[...] = x_ref[...].astype(o_ref.dtype)


def _to_bf16(w):
    r, c = w.shape
    tr = 1 << ((CAST_BLOCK_BYTES // (4 * c)).bit_length() - 1)
    while r % tr:
        tr //= 2
    return pl.pallas_call(
        _cast_kernel,
        grid=(r // tr,),
        in_specs=[pl.BlockSpec((tr, c), lambda i: (i, 0))],
        out_specs=pl.BlockSpec((tr, c), lambda i: (i, 0)),
        out_shape=jax.ShapeDtypeStruct((r, c), BF16),
        compiler_params=pltpu.CompilerParams(
            dimension_semantics=("parallel",), vmem_limit_bytes=VMEM_LIMIT),
        name="to_bf16",
    )(w)


def kernel(x, ffn1_norm, ffn1_w1, ffn1_w3, ffn1_w2, mix_norm, w_in, ret_gn_gain, cmp_pe_k, cmp_w1_k,
           cmp_w2_k, cmp_pe_v, cmp_w1_v, cmp_w2_v, w_out, ffn2_norm, ffn2_w1, ffn2_w3, ffn2_w2,
           rel_bias, final_norm):
    batch, seq, d = x.shape
    depth = ffn1_norm.shape[0]
    h = x.reshape(batch * seq, d)
    for l in range(depth):
        last = l == depth - 1
        h = _ffn(h, ffn1_norm[l], _to_bf16(ffn1_w1[l]), _to_bf16(ffn1_w3[l]), _to_bf16(ffn1_w2[l]),
                 final_norm, False)
        h = _token_mix(h, mix_norm[l], w_in[l], ret_gn_gain[l], cmp_pe_k[l], cmp_w1_k[l], cmp_w2_k[l],
                       cmp_pe_v[l], cmp_w1_v[l], cmp_w2_v[l], w_out[l], rel_bias, batch, seq)
        h = _ffn(h, ffn2_norm[l], _to_bf16(ffn2_w1[l]), _to_bf16(ffn2_w3[l]), _to_bf16(ffn2_w2[l]),
                 final_norm, last)
    return h.reshape(batch, seq, d)
```

```python
import functools
import math

import jax
import jax.numpy as jnp
import numpy as np
from jax import lax
from jax.experimental import pallas as pl
from jax.experimental.pallas import tpu as pltpu

F32 = jnp.float32
BF16 = jnp.bfloat16

RET_HEADS = 4
RET_DK = 256
RET_DV = 256
ROPE_BASE = 10000.0
NSA_HEADS = 8
NSA_KV_GROUPS = 2
NSA_HPG = NSA_HEADS // NSA_KV_GROUPS
NSA_DH = 128
CMP_LEN = 32
CMP_STRIDE = 16
SEL_LEN = 64
SEL_TOPK = 16
WIN = 512
N_BRANCH = 3
REL_BUCKETS = 32
REL_MAX_DIST = 128
EPS = 1e-6
NEG = -1e30
FORCE = 1e4

RET_W = RET_HEADS * RET_DV
NSA_W = NSA_HEADS * NSA_DH
KV_W = NSA_KV_GROUPS * NSA_DH
IN_SPLITS = [RET_HEADS * RET_DK, RET_HEADS * RET_DK, RET_W, RET_W, NSA_W,
             KV_W, KV_W, KV_W, KV_W, KV_W, KV_W, NSA_HEADS * N_BRANCH]
D_IN = sum(IN_SPLITS)
IN_OFF = [sum(IN_SPLITS[:i]) for i in range(len(IN_SPLITS))]

LANE = 128
IN_TN = 2304
D_IN_PAD = 3 * IN_TN
CMP_KV_TILE = IN_OFF[5] // IN_TN
CMP_KV_LOCAL = IN_OFF[5] % IN_TN
assert CMP_KV_TILE > 0 and CMP_KV_LOCAL + 2 * KV_W <= IN_TN and IN_OFF[6] == IN_OFF[5] + KV_W
GATE_ROWS = 8
FFN_TM = 1024
FFN_TF = 512
IN_TM = 512
RET_C = 256
CMP_TQ = 512
ATT_T = 512
ATT_TK = 512
ATT_NVAR = 3
OUT_TM = 512
SEL_PAD = 128
ACC_PAD = 8
CMP_BAND = 128
VMEM_LIMIT = 56 * 1024 * 1024
CAST_BLOCK_BYTES = 6 * 1024 * 1024
CAST_T_ROWS = 256
LOG2E = math.log2(math.e)


def _dot(a, b):
    return jnp.dot(a, b, preferred_element_type=F32)


def _dot_nt(a, b):
    return lax.dot_general(a, b, (((1,), (1,)), ((), ())), preferred_element_type=F32)


def _dot_tn(a, b):
    return lax.dot_general(a, b, (((0,), (0,)), ((), ())), preferred_element_type=F32)


def _sigmoid(x):
    return 1.0 / (1.0 + jnp.exp(-x))


def _rms(x, g):
    ms = jnp.mean(x * x, axis=-1, keepdims=True)
    return x * lax.rsqrt(ms + EPS) * g


def _ffn_kernel(x_ref, g_ref, w1_ref, w3_ref, w2_ref, fg_ref, o_ref, n_ref, *, final_norm):
    j = pl.program_id(1)

    last = pl.num_programs(1) - 1

    def step(first, norm_out):
        if first:
            n = _rms(x_ref[...], g_ref[...]).astype(BF16)
            n_ref[...] = n
        else:
            n = n_ref[...]
        a = _dot(n, w1_ref[...])
        b = _dot(n, w3_ref[...])
        h = (0.5 * a * _sigmoid(a) * b).astype(BF16)
        y = (x_ref[...] if first else o_ref[...]) + _dot(h, w2_ref[...])
        o_ref[...] = _rms(y, fg_ref[...]) if norm_out else y

    pl.when(j == 0)(functools.partial(step, True, False))
    if final_norm:
        pl.when((j > 0) & (j < last))(functools.partial(step, False, False))
        pl.when(j == last)(functools.partial(step, False, True))
    else:
        pl.when(j > 0)(functools.partial(step, False, False))


def _ffn(x2d, g, w1, w3, w2, fg, final_norm):
    tok, d = x2d.shape
    dff = w1.shape[1]
    tm, tf = FFN_TM, FFN_TF
    assert dff // tf >= 2
    return pl.pallas_call(
        functools.partial(_ffn_kernel, final_norm=final_norm),
        grid=(tok // tm, dff // tf),
        in_specs=[
            pl.BlockSpec((tm, d), lambda i, j: (i, 0)),
            pl.BlockSpec((1, d), lambda i, j: (0, 0)),
            pl.BlockSpec((d, tf), lambda i, j: (0, j)),
            pl.BlockSpec((d, tf), lambda i, j: (0, j)),
            pl.BlockSpec((tf, d), lambda i, j: (j, 0)),
            pl.BlockSpec((1, d), lambda i, j: (0, 0)),
        ],
        out_specs=pl.BlockSpec((tm, d), lambda i, j: (i, 0)),
        out_shape=jax.ShapeDtypeStruct((tok, d), F32),
        scratch_shapes=[pltpu.VMEM((tm, d), BF16)],
        compiler_params=pltpu.CompilerParams(
            dimension_semantics=("parallel", "arbitrary"), vmem_limit_bytes=VMEM_LIMIT),
        name="ffn",
    )(x2d, g.reshape(1, d), w1, w3, w2, fg.reshape(1, d))


def _inproj_kernel(x_ref, g_ref, w_ref, o_ref, kv_ref, n_ref):
    j = pl.program_id(1)
    tn = o_ref.shape[1]
    for jj in range(w_ref.shape[1] // tn):
        @pl.when(j == jj)
        def _():
            if jj == 0:
                n = _rms(x_ref[...], g_ref[...]).astype(BF16)
                n_ref[...] = n
            else:
                n = n_ref[...]
            res = _dot(n, w_ref[:, jj * tn:(jj + 1) * tn])
            o_ref[...] = res.astype(o_ref.dtype)
            if jj == CMP_KV_TILE:
                kv_ref[...] = res[:, CMP_KV_LOCAL:CMP_KV_LOCAL + 2 * KV_W]


def _inproj(x2d, g, w_pad):
    tok, d = x2d.shape
    n_out = w_pad.shape[1]
    tm, tn = IN_TM, IN_TN
    return pl.pallas_call(
        _inproj_kernel,
        grid=(tok // tm, n_out // tn),
        in_specs=[
            pl.BlockSpec((tm, d), lambda i, j: (i, 0)),
            pl.BlockSpec((1, d), lambda i, j: (0, 0)),
            pl.BlockSpec(w_pad.shape, lambda i, j: (0, 0), pipeline_mode=pl.Buffered(1)),
        ],
        out_specs=[pl.BlockSpec((tm, tn), lambda i, j: (i, j)),
                   pl.BlockSpec((tm, 2 * KV_W), lambda i, j: (i, 0))],
        out_shape=[jax.ShapeDtypeStruct((tok, n_out), BF16),
                   jax.ShapeDtypeStruct((tok, 2 * KV_W), F32)],
        scratch_shapes=[pltpu.VMEM((tm, d), BF16)],
        compiler_params=pltpu.CompilerParams(
            dimension_semantics=("parallel", "arbitrary"), vmem_limit_bytes=VMEM_LIMIT),
        name="in_proj",
    )(x2d, g.reshape(1, d), w_pad)


def _ret_kernel(cd_ref, q_ref, k_ref, v_ref, g_ref, cos_ref, sin_ref, dec_ref, xi_ref, zeta_ref,
                gain_ref, o_ref, state_ref):
    @pl.when(pl.program_id(0) == 0)
    def _():
        state_ref[...] = jnp.zeros_like(state_ref)

    cos = cos_ref[...]
    sin = sin_ref[...]
    half = RET_DK // 2

    def rot(t):
        t1, t2 = t[:, :half], t[:, half:]
        return jnp.concatenate([t1 * cos - t2 * sin, t1 * sin + t2 * cos], axis=-1)

    for bi, h in [(bi, h) for bi in range(q_ref.shape[0]) for h in range(RET_HEADS)]:
        cs = slice(h * RET_DK, (h + 1) * RET_DK)
        q = rot(q_ref[bi, :, cs].astype(F32))
        k = rot(k_ref[bi, :, cs].astype(F32))
        v = v_ref[bi, :, cs].astype(BF16)
        qb = q.astype(BF16)
        s = _dot_nt(qb, k.astype(BF16)) * dec_ref[h]
        state = state_ref[bi, h]
        y = _dot(s.astype(BF16), v) + _dot((q * xi_ref[h]).astype(BF16), state.astype(BF16))
        state_ref[bi, h] = cd_ref[h] * state + _dot_tn((k * zeta_ref[h]).astype(BF16), v)
        mu = jnp.mean(y, axis=-1, keepdims=True)
        yc = y - mu
        var = jnp.mean(yc * yc, axis=-1, keepdims=True)
        yn = yc * lax.rsqrt(var + EPS) * gain_ref[:, cs]
        gate = g_ref[bi, :, cs].astype(F32)
        o_ref[bi, :, cs] = (gate * _sigmoid(gate) * yn).astype(o_ref.dtype)


def _retention(proj, gain, batch, seq):
    tok = proj.shape[0]
    c = RET_C
    n_chunks = seq // c
    h, dk = RET_HEADS, RET_DK
    half = dk // 2
    inv = np.float32(ROPE_BASE) ** (-np.arange(half, dtype=np.float32) / np.float32(half))
    ang = np.arange(seq, dtype=np.float32)[:, None] * inv[None, :]
    cos, sin = np.cos(ang).astype(np.float32), np.sin(ang).astype(np.float32)
    log_g = np.log(1.0 - 2.0 ** (-5.0 - np.arange(h, dtype=np.float64)))
    idx = np.arange(c, dtype=np.float64)
    diff = idx[:, None] - idx[None, :]
    dec = np.where(diff >= 0.0, np.exp(np.maximum(diff, 0.0)[None] * log_g[:, None, None]), 0.0)
    k_scale = dk ** -0.5
    assert math.frexp(k_scale)[0] == 0.5
    dec = dec.astype(np.float32) * np.float32(k_scale)
    xi = np.broadcast_to(np.exp((idx + 1.0)[None] * log_g[:, None])[:, :, None], (h, c, dk)).astype(np.float32)
    zeta = np.broadcast_to(np.exp((c - 1.0 - idx)[None] * log_g[:, None])[:, :, None],
                           (h, c, dk)).astype(np.float32) * np.float32(k_scale)
    cd = np.exp(c * log_g).astype(np.float32)

    w = h * dk
    proj3 = proj.reshape(batch, seq, proj.shape[1])
    col = lambda j: pl.BlockSpec((batch, c, w), lambda n: (0, n, j))
    y = pl.pallas_call(
        _ret_kernel,
        grid=(n_chunks,),
        in_specs=[
            pl.BlockSpec(memory_space=pltpu.SMEM),
            col(0), col(1), col(2), col(3),
            pl.BlockSpec((c, half), lambda n: (n, 0)),
            pl.BlockSpec((c, half), lambda n: (n, 0)),
            pl.BlockSpec((h, c, c), lambda n: (0, 0, 0)),
            pl.BlockSpec((h, c, dk), lambda n: (0, 0, 0)),
            pl.BlockSpec((h, c, dk), lambda n: (0, 0, 0)),
            pl.BlockSpec((1, w), lambda n: (0, 0)),
        ],
        out_specs=pl.BlockSpec((batch, c, w), lambda n: (0, n, 0)),
        out_shape=jax.ShapeDtypeStruct((batch, seq, w), BF16),
        scratch_shapes=[pltpu.VMEM((batch, h, dk, RET_DV), F32)],
        compiler_params=pltpu.CompilerParams(
            dimension_semantics=("arbitrary",), vmem_limit_bytes=VMEM_LIMIT),
        name="retention",
    )(cd, proj3, proj3, proj3, proj3, cos, sin, dec, xi, zeta, gain.reshape(1, w))
    return y.reshape(tok, w)


def _compress_kernel(k_ref, v_ref, pek_ref, w1k_ref, w1kf_ref, w2k_ref, pev_ref, w1v_ref, w1vf_ref,
                     w2v_ref, ok_ref, ov_ref):
    dh = NSA_DH
    nblk = k_ref.shape[0] // CMP_STRIDE

    def one(x_ref, pe_ref, w1_ref, w1f_ref, w2_ref, o_ref):
        ab = jnp.zeros((nblk, 2 * dh), F32)
        for l in range(CMP_STRIDE):
            rows = x_ref[pl.ds(l, nblk, stride=CMP_STRIDE), :].astype(BF16)
            ab = ab + _dot(rows, w1_ref[l])
        a, b = ab[:, :dh], ab[:, dh:]
        b_next = pltpu.roll(b, nblk - 1, axis=0)
        ridx = lax.broadcasted_iota(jnp.int32, (nblk, dh), 0)
        b_next = jnp.where(ridx < nblk - 1, b_next, 0.0)
        pe_term = _dot(pe_ref[...], w1f_ref[...])[0:1, :]
        hdn = a + b_next + pe_term
        o_ref[0, 0] = _dot((hdn * _sigmoid(hdn)).astype(BF16), w2_ref[...])

    one(k_ref, pek_ref, w1k_ref, w1kf_ref, w2k_ref, ok_ref)
    one(v_ref, pev_ref, w1v_ref, w1vf_ref, w2v_ref, ov_ref)


def _compress(kv, pe_k, w1_k, w2_k, pe_v, w1_v, w2_v, batch, seq):
    dh, g = NSA_DH, NSA_KV_GROUPS
    nblk = seq // CMP_STRIDE
    half = CMP_LEN // 2

    def prep(pe, w1, w2):
        w1b = w1.astype(BF16)
        w1_pair = jnp.concatenate([w1b[:half], w1b[half:]], axis=-1)
        pe_flat = jnp.broadcast_to(pe.reshape(1, CMP_LEN * dh), (8, CMP_LEN * dh)).astype(BF16)
        return pe_flat, w1_pair, w1b.reshape(CMP_LEN * dh, dh), w2.astype(BF16)

    args_k = prep(pe_k, w1_k, w2_k)
    args_v = prep(pe_v, w1_v, w2_v)
    kcol = 0
    vcol = KV_W // dh
    const = lambda shape: pl.BlockSpec(shape, lambda b, gg: (0,) * len(shape))
    wspecs = [const((8, CMP_LEN * dh)), const((half, dh, 2 * dh)), const((CMP_LEN * dh, dh)),
              const((dh, dh))]
    out_spec = pl.BlockSpec((1, 1, nblk, dh), lambda b, gg: (b, gg, 0, 0))
    return pl.pallas_call(
        _compress_kernel,
        grid=(batch, g),
        in_specs=[pl.BlockSpec((seq, dh), lambda b, gg: (b, kcol + gg)),
                  pl.BlockSpec((seq, dh), lambda b, gg: (b, vcol + gg))] + wspecs + wspecs,
        out_specs=[out_spec, out_spec],
        out_shape=[jax.ShapeDtypeStruct((batch, g, nblk, dh), F32)] * 2,
        compiler_params=pltpu.CompilerParams(
            dimension_semantics=("parallel", "parallel"), vmem_limit_bytes=VMEM_LIMIT),
        name="compress",
    )(kv, kv, *args_k, *args_v)


def _cmpattn_kernel(q_ref, kc_ref, vc_ref, tb_ref, pp_ref, ovt_ref, gate_ref, o_ref, sel_ref, rank_ref):
    tq = q_ref.shape[0]
    ncp = kc_ref.shape[2]
    dh = NSA_DH
    s0 = pl.program_id(2) * tq
    hpg = NSA_HPG
    kc = kc_ref[0, 0].astype(BF16)
    vc = vc_ref[0, 0].astype(BF16)
    scale = dh ** -0.5
    q = jnp.concatenate([q_ref[:, hh * dh:(hh + 1) * dh] for hh in range(hpg)], axis=0).astype(BF16)
    bias = _dot(tb_ref[...].reshape(hpg * tq, tb_ref.shape[2]), pp_ref[0])
    lm = _dot_nt(q, kc) * scale + bias
    m = jnp.max(lm, axis=-1, keepdims=True)
    e = jnp.exp(lm - m)
    den = jnp.sum(e, axis=-1, keepdims=True)
    p = e * jnp.where(m > 0.5 * NEG, 1.0 / den, 0.0)
    o = _dot(p.astype(BF16), vc)
    gate = _sigmoid(gate_ref[...].astype(F32))
    gg = pl.program_id(1)
    psum = p[0:tq]
    for hh in range(hpg):
        g_col = gate[:, hh * N_BRANCH:hh * N_BRANCH + 1]
        for g in range(1, NSA_KV_GROUPS):
            lane = (g * hpg + hh) * N_BRANCH
            g_col = jnp.where(gg == g, gate[:, lane:lane + 1], g_col)
        o_ref[:, hh * dh:(hh + 1) * dh] = (o[hh * tq:(hh + 1) * tq] * g_col).astype(o_ref.dtype)
        if hh:
            psum = psum + p[hh * tq:(hh + 1) * tq]
    nsel = ovt_ref.shape[0]
    imp_t = _dot_nt(ovt_ref[...], psum.astype(BF16))
    jb = lax.broadcasted_iota(jnp.int32, (nsel, tq), 0)
    pos = s0 + lax.broadcasted_iota(jnp.int32, (nsel, tq), 1)
    cur = jnp.right_shift(pos, SEL_LEN.bit_length() - 1)
    causal = jb * SEL_LEN <= pos
    forced = (jb == 0) | (jb == cur) | (jb == cur - 1)
    score = jnp.where(forced, FORCE, jnp.where(causal, imp_t, NEG))
    per_tile = tq // SEL_LEN
    rank_ref[...] = jnp.zeros_like(rank_ref)
    for c in range(nsel // per_tile):
        @pl.when(c <= pl.program_id(2))
        def _():
            rank = rank_ref[...]
            for kk in range(c * per_tile, (c + 1) * per_tile):
                row = score[kk:kk + 1, :]
                beats = (row > score) | ((row == score) & (jb > kk))
                rank = rank + jnp.where(beats, 1, 0)
            rank_ref[...] = rank
    rank = rank_ref[...]
    selb = jnp.where((rank < SEL_TOPK) & causal, 0.0, NEG)
    selb = jnp.concatenate([selb, jnp.zeros((SEL_PAD - nsel, tq), F32)], axis=0)
    sel_ref[0, 0] = selb.astype(BF16)


def _cmp_attention(proj, k_cmp, v_cmp, rel_bias, batch, seq):
    dh, g, hpg = NSA_DH, NSA_KV_GROUPS, NSA_HPG
    tq = CMP_TQ
    nq = seq // tq
    ncp = k_cmp.shape[2]
    nsel = seq // SEL_LEN
    n_cmp = (seq - CMP_LEN) // CMP_STRIDE + 1
    cmp_idx = np.arange(n_cmp)[:, None] * CMP_STRIDE + np.arange(CMP_LEN)[None, :]
    overlap = ((cmp_idx // SEL_LEN)[:, :, None] == np.arange(nsel)[None, None, :]).sum(1) / CMP_LEN
    ovt = np.zeros((nsel, ncp), np.float32)
    ovt[:, :n_cmp] = overlap.T
    nb = tq // CMP_STRIDE
    assert 2 * nb + 2 <= CMP_BAND and CMP_STRIDE * (nb + 1) - (CMP_LEN - 1) >= REL_MAX_DIST
    i = np.arange(tq)[:, None]
    r = np.arange(CMP_BAND)[None, :]
    rel = np.where(r < 2 * nb, i - CMP_STRIDE * (r - nb) - (CMP_LEN - 1), REL_MAX_DIST)
    tb = jnp.where(jnp.asarray(r <= 2 * nb), _bias_lookup(rel_bias, jnp.asarray(rel, jnp.int32)), 0.0)
    tb = jnp.where(jnp.asarray((rel < 0) | (r == 2 * nb + 1)), NEG, tb)
    tb_hi = tb.astype(BF16)
    tb_lo = (tb - tb_hi.astype(F32)).astype(BF16)
    tb = jnp.concatenate([tb_hi, tb_lo], axis=-1)
    n = np.arange(ncp)[None, None, :]
    first = (np.arange(nq) * nb - nb)[:, None, None]
    rr = np.arange(CMP_BAND)[None, :, None]
    pp = np.where(rr < 2 * nb, n == first + rr,
                  ((rr == 2 * nb) & (n < first)) | ((rr == 2 * nb + 1) & (n >= first + 2 * nb)))
    pp = pp.astype(np.float32)
    pp = np.concatenate([pp, pp], axis=1)
    qcol = IN_OFF[4] // (hpg * dh)
    assert IN_OFF[11] % LANE == 0 and IN_SPLITS[11] <= LANE
    gcol = IN_OFF[11] // LANE
    return pl.pallas_call(
        _cmpattn_kernel,
        grid=(batch, g, nq),
        in_specs=[
            pl.BlockSpec((tq, hpg * dh), lambda b, gg, t: (b * nq + t, qcol + gg)),
            pl.BlockSpec((1, 1, ncp, dh), lambda b, gg, t: (b, gg, 0, 0)),
            pl.BlockSpec((1, 1, ncp, dh), lambda b, gg, t: (b, gg, 0, 0)),
            pl.BlockSpec((hpg, tq, 2 * CMP_BAND), lambda b, gg, t: (gg, 0, 0)),
            pl.BlockSpec((1, 2 * CMP_BAND, ncp), lambda b, gg, t: (t, 0, 0)),
            pl.BlockSpec((nsel, ncp), lambda b, gg, t: (0, 0)),
            pl.BlockSpec((tq, LANE), lambda b, gg, t: (b * nq + t, gcol)),
        ],
        out_specs=[
            pl.BlockSpec((tq, hpg * dh), lambda b, gg, t: (b * nq + t, gg)),
            pl.BlockSpec((1, 1, SEL_PAD, tq), lambda b, gg, t: (b, gg, 0, t)),
        ],
        out_shape=[jax.ShapeDtypeStruct((batch * seq, g * hpg * dh), BF16),
                   jax.ShapeDtypeStruct((batch, g, SEL_PAD, seq), BF16)],
        scratch_shapes=[pltpu.VMEM((nsel, tq), jnp.int32)],
        compiler_params=pltpu.CompilerParams(
            dimension_semantics=("parallel", "parallel", "parallel"), vmem_limit_bytes=VMEM_LIMIT),
        name="cmp_attn",
    )(proj, k_cmp, v_cmp, tb, jnp.asarray(pp, BF16), jnp.asarray(ovt, BF16), proj)


def _band_kernel(qi_ref, ki_ref, var_ref, first_ref, last_ref, q_ref, k_ref, v_ref, fd_ref, gate_ref,
                 *rest, use_sel):
    if use_sel:
        selb_ref, e_ref, o_ref, qt_ref, m_ref, acc_ref, bias_ref = rest
    else:
        o_ref, qt_ref, m_ref, acc_ref, bias_ref = rest
    t = pl.program_id(1)
    dh, hpg, ng = NSA_DH, NSA_HPG, NSA_KV_GROUPS
    tq, tk = q_ref.shape[0], k_ref.shape[0]

    @pl.when((pl.program_id(0) == 0) & (t == 0))
    def _():
        upper = (lax.broadcasted_iota(jnp.int32, (tk, tq), 1) >= lax.broadcasted_iota(jnp.int32, (tk, tq), 0))
        for gg in range(ng):
            for hh in range(hpg):
                cols = slice(hh * tq, (hh + 1) * tq)
                by_dist = jnp.broadcast_to(fd_ref[gg, hh, 0, 0:1, :], (tk, tq))
                cyc = pltpu.roll(by_dist, 0, 1, stride=1, stride_axis=0)
                bias_ref[gg, 0, :, cols] = jnp.where(upper, cyc, NEG)
                if use_sel:
                    far = jnp.broadcast_to(fd_ref[gg, hh, 1, 0:1, :], (tk, tq))
                    bias_ref[gg, 1, :, cols] = jnp.where(upper, far, cyc)
                else:
                    bias_ref[gg, 1, :, cols] = jnp.where(upper, NEG, cyc)

    @pl.when(first_ref[t] == 1)
    def _():
        m_ref[...] = jnp.full_like(m_ref, NEG)
        acc_ref[...] = jnp.zeros_like(acc_ref)
        scale = dh ** -0.5 * LOG2E
        for gg in range(ng):
            for hh in range(hpg):
                h = gg * hpg + hh
                q = q_ref[:, h * dh:(h + 1) * dh].astype(F32)
                qt_ref[gg, 0:dh, hh * tq:(hh + 1) * tq] = (q * scale).T.astype(BF16)
                if use_sel:
                    qt_ref[gg, dh:, hh * tq:(hh + 1) * tq] = selb_ref[0, gg]

    var = var_ref[t]

    def update(far):
        ones = jnp.ones((ACC_PAD, tk), F32)
        scores = []
        for gg in range(ng):
            k = k_ref[:, gg * dh:(gg + 1) * dh]
            if use_sel:
                k = jnp.concatenate([k, e_ref[...]], axis=1)
            s = _dot(k, qt_ref[gg])
            scores.append(s if far else s + bias_ref[gg, var])
        for gg in range(ng):
            vt = jnp.concatenate([v_ref[:, gg * dh:(gg + 1) * dh].astype(F32).T, ones], axis=0).astype(BF16)
            s = scores[gg]
            m_prev = m_ref[gg]
            col_max = jnp.max(s, axis=0, keepdims=True)
            if far:
                c = jnp.concatenate([fd_ref[gg, hh, 1, 0:1, :] for hh in range(hpg)], axis=1)
                m_new = jnp.maximum(m_prev, col_max + c)
                shift = m_new - c
            else:
                m_new = jnp.maximum(m_prev, col_max)
                shift = m_new
            alpha = jnp.exp2(m_prev - m_new)
            p = jnp.exp2(s - shift).astype(BF16)
            acc_ref[gg] = alpha * acc_ref[gg] + _dot(vt, p)
            m_ref[gg] = m_new

    if use_sel:
        pl.when(var == ATT_NVAR - 1)(functools.partial(update, True))
        pl.when(var != ATT_NVAR - 1)(functools.partial(update, False))
    else:
        update(False)

    @pl.when(last_ref[t] == 1)
    def _():
        for gg in range(ng):
            gate = _sigmoid(gate_ref[0, gg, 0])
            for hh in range(hpg):
                h = gg * hpg + hh
                cols = slice(hh * tq, (hh + 1) * tq)
                o = acc_ref[gg, 0:dh, cols] * (gate[hh:hh + 1, :] / acc_ref[gg, dh:dh + 1, cols])
                o_ref[:, h * dh:(h + 1) * dh] = o.T.astype(o_ref.dtype)


def _band_steps(seq, window):
    tq, tk = ATT_T, ATT_TK
    steps = []
    for qi in range(seq // tq):
        k_hi = qi * tq // tk
        k_lo = max(0, (qi * tq - window + 1) // tk) if window else 0
        steps += [(qi, ki, min((qi * tq - ki * tk) // tq, ATT_NVAR - 1)) for ki in range(k_lo, k_hi + 1)]
    return steps


def _band_attention(proj, bias_by_dist, steps, koff, voff, gates, branch, batch, seq, selb=None, e_mat=None):
    dh, g, hpg = NSA_DH, NSA_KV_GROUPS, NSA_HPG
    tq, tk = ATT_T, ATT_TK
    nq, nk = seq // tq, seq // tk
    qi = np.array([s[0] for s in steps], np.int32)
    ki = np.array([s[1] for s in steps], np.int32)
    var = np.array([s[2] for s in steps], np.int32)
    first = np.concatenate([[1], (qi[1:] != qi[:-1]).astype(np.int32)]).astype(np.int32)
    last = np.concatenate([(qi[1:] != qi[:-1]).astype(np.int32), [1]]).astype(np.int32)
    qcol, kcol, vcol = IN_OFF[4] // NSA_W, koff // KV_W, voff // KV_W
    use_sel = selb is not None
    in_specs = [
        pl.BlockSpec((tq, NSA_W), lambda b, t, qi_r, ki_r, v_r, f_r, l_r: (b * nq + qi_r[t], qcol)),
        pl.BlockSpec((tk, KV_W), lambda b, t, qi_r, ki_r, v_r, f_r, l_r: (b * nk + ki_r[t], kcol)),
        pl.BlockSpec((tk, KV_W), lambda b, t, qi_r, ki_r, v_r, f_r, l_r: (b * nk + ki_r[t], vcol)),
        pl.BlockSpec(bias_by_dist.shape, lambda b, t, qi_r, ki_r, v_r, f_r, l_r: (0, 0, 0, 0, 0)),
        pl.BlockSpec((1, g, 1, GATE_ROWS, tq),
                     lambda b, t, qi_r, ki_r, v_r, f_r, l_r: (b, 0, branch, 0, qi_r[t])),
    ]
    args = [proj, proj, proj, bias_by_dist, gates]
    if use_sel:
        in_specs += [
            pl.BlockSpec((1, g, SEL_PAD, tq), lambda b, t, qi_r, ki_r, v_r, f_r, l_r: (b, 0, 0, qi_r[t])),
            pl.BlockSpec((tk, SEL_PAD), lambda b, t, qi_r, ki_r, v_r, f_r, l_r: (ki_r[t], 0)),
        ]
        args += [selb, e_mat]
    kdim = dh + SEL_PAD if use_sel else dh
    grid_spec = pltpu.PrefetchScalarGridSpec(
        num_scalar_prefetch=5,
        grid=(batch, len(steps)),
        in_specs=in_specs,
        out_specs=pl.BlockSpec((tq, NSA_W), lambda b, t, qi_r, ki_r, v_r, f_r, l_r: (b * nq + qi_r[t], 0)),
        scratch_shapes=[pltpu.VMEM((g, kdim, hpg * tq), BF16), pltpu.VMEM((g, 1, hpg * tq), F32),
                        pltpu.VMEM((g, dh + ACC_PAD, hpg * tq), F32),
                        pltpu.VMEM((g, ATT_NVAR - 1, tk, hpg * tq), F32)],
    )
    return pl.pallas_call(
        functools.partial(_band_kernel, use_sel=use_sel),
        grid_spec=grid_spec,
        out_shape=jax.ShapeDtypeStruct((batch * seq, NSA_W), BF16),
        compiler_params=pltpu.CompilerParams(
            dimension_semantics=("arbitrary", "arbitrary"), vmem_limit_bytes=VMEM_LIMIT),
        name="sel_attn" if use_sel else "win_attn",
    )(jnp.asarray(qi), jnp.asarray(ki), jnp.asarray(var), jnp.asarray(first), jnp.asarray(last), *args)


def _out_kernel(x_ref, yr_ref, oc_ref, os_ref, ow_ref, w_ref, o_ref, wb_ref):
    @pl.when(pl.program_id(0) == 0)
    def _():
        wb_ref[...] = w_ref[...].astype(BF16)

    y_nsa = oc_ref[...].astype(F32) + os_ref[...].astype(F32) + ow_ref[...].astype(F32)
    y = jnp.concatenate([yr_ref[...], y_nsa.astype(BF16)], axis=-1)
    o_ref[...] = x_ref[...] + _dot(y, wb_ref[...])


def _out_proj(x2d, y_ret, o_cmp, o_sel, o_win, w_out):
    tok, d = x2d.shape
    tm = OUT_TM
    wmix = y_ret.shape[1]
    row = lambda shape: pl.BlockSpec(shape, lambda i: (i, 0))
    return pl.pallas_call(
        _out_kernel,
        grid=(tok // tm,),
        in_specs=[row((tm, d)), row((tm, wmix)), row((tm, wmix)), row((tm, wmix)), row((tm, wmix)),
                  pl.BlockSpec(w_out.shape, lambda i: (0, 0), pipeline_mode=pl.Buffered(1))],
        out_specs=row((tm, d)),
        out_shape=jax.ShapeDtypeStruct((tok, d), F32),
        scratch_shapes=[pltpu.VMEM(w_out.shape, BF16)],
        compiler_params=pltpu.CompilerParams(
            dimension_semantics=("arbitrary",), vmem_limit_bytes=VMEM_LIMIT),
        name="out_proj",
    )(x2d, y_ret, o_cmp, o_sel, o_win, w_out)


def _t5_bucket_of(rel):
    n = jnp.maximum(rel, 0)
    max_exact = REL_BUCKETS // 2
    nf = jnp.maximum(n, 1).astype(F32)
    large = max_exact + (jnp.log(nf / max_exact) / math.log(REL_MAX_DIST / max_exact)
                         * (REL_BUCKETS - max_exact)).astype(jnp.int32)
    large = jnp.minimum(large, REL_BUCKETS - 1)
    return jnp.where(n < max_exact, n, large)


def _bias_lookup(rel_bias, rel):
    bucket = _t5_bucket_of(rel)[None]
    tab = rel_bias.astype(F32).reshape((rel_bias.shape[0], REL_BUCKETS) + (1,) * rel.ndim)
    out = jnp.zeros((rel_bias.shape[0],) + rel.shape, F32)
    for b in range(REL_BUCKETS):
        out = jnp.where(bucket == b, tab[:, b], out)
    return out


def _bias_by_distance(rel_bias):
    t, g, hpg = ATT_T, NSA_KV_GROUPS, NSA_HPG
    assert ATT_TK == t and t >= REL_MAX_DIST and WIN == t
    by_dist = _bias_lookup(rel_bias, jnp.arange(t, dtype=jnp.int32)) * LOG2E
    far = jnp.broadcast_to(by_dist[:, t - 1:t], by_dist.shape)
    tab = jnp.stack([by_dist, far], axis=1).reshape(g, hpg, 2, 1, t)
    return jnp.broadcast_to(tab, (g, hpg, 2, 8, t))


def _token_mix(x2d, mix_norm, w_in, ret_gn_gain, pe_k, w1_k, w2_k, pe_v, w1_v, w2_v, w_out, rel_bias,
               batch, seq):
    d = x2d.shape[1]
    dh = NSA_DH
    w_pad = _to_bf16_transposed(w_in.T, D_IN_PAD)
    proj, cmp_kv = _inproj(x2d, mix_norm, w_pad)
    y_ret = _retention(proj, ret_gn_gain, batch, seq)

    gates = proj[:, IN_OFF[11]:IN_OFF[11] + IN_SPLITS[11]].astype(F32)
    gates = gates.reshape(batch, seq, NSA_KV_GROUPS, NSA_HPG, N_BRANCH).transpose(0, 2, 4, 3, 1)
    gates = jnp.pad(gates, ((0, 0), (0, 0), (0, 0), (0, GATE_ROWS - NSA_HPG), (0, 0)))

    k_cmp, v_cmp = _compress(cmp_kv, pe_k, w1_k, w2_k, pe_v, w1_v, w2_v, batch, seq)
    o_cmp, selb = _cmp_attention(proj, k_cmp, v_cmp, rel_bias, batch, seq)

    bias_by_dist = _bias_by_distance(rel_bias)
    e_np = np.zeros((seq, SEL_PAD), np.float32)
    e_np[np.arange(seq), np.arange(seq) // SEL_LEN] = 1.0
    o_sel = _band_attention(proj, bias_by_dist, _band_steps(seq, 0),
                            IN_OFF[7], IN_OFF[8], gates, 1, batch, seq,
                            selb=selb, e_mat=jnp.asarray(e_np, BF16))
    o_win = _band_attention(proj, bias_by_dist, _band_steps(seq, WIN),
                            IN_OFF[9], IN_OFF[10], gates, 2, batch, seq)
    return _out_proj(x2d, y_ret, o_cmp, o_sel, o_win, w_out)


def _cast_t_kernel(x_ref, o_ref, *, valid_rows):
    tr = x_ref.shape[0]
    row = pl.program_id(0) * tr + lax.broadcasted_iota(jnp.int32, x_ref.shape, 0)
    o_ref[...] = jnp.where(row < valid_rows, x_ref[...], 0.0).T.astype(o_ref.dtype)


def _to_bf16_transposed(w_t, cols):
    c, r = w_t.shape
    tr = CAST_T_ROWS
    return pl.pallas_call(
        functools.partial(_cast_t_kernel, valid_rows=c),
        grid=(cols // tr,),
        in_specs=[pl.BlockSpec((tr, r), lambda i: (i, 0))],
        out_specs=pl.BlockSpec((r, tr), lambda i: (0, i)),
        out_shape=jax.ShapeDtypeStruct((r, cols), BF16),
        compiler_params=pltpu.CompilerParams(
            dimension_semantics=("parallel",), vmem_limit_bytes=VMEM_LIMIT),
        name="to_bf16_t",
    )(w_t)


def _cast_kernel(x_ref, o_ref):
    o_ref[...] = x_ref[...].astype(o_ref.dtype)


def _to_bf16(w):
    r, c = w.shape
    tr = 1 << ((CAST_BLOCK_BYTES // (4 * c)).bit_length() - 1)
    while r % tr:
        tr //= 2
    return pl.pallas_call(
        _cast_kernel,
        grid=(r // tr,),
        in_specs=[pl.BlockSpec((tr, c), lambda i: (i, 0))],
        out_specs=pl.BlockSpec((tr, c), lambda i: (i, 0)),
        out_shape=jax.ShapeDtypeStruct((r, c), BF16),
        compiler_params=pltpu.CompilerParams(
            dimension_semantics=("parallel",), vmem_limit_bytes=VMEM_LIMIT),
        name="to_bf16",
    )(w)


def kernel(x, ffn1_norm, ffn1_w1, ffn1_w3, ffn1_w2, mix_norm, w_in, ret_gn_gain, cmp_pe_k, cmp_w1_k,
           cmp_w2_k, cmp_pe_v, cmp_w1_v, cmp_w2_v, w_out, ffn2_norm, ffn2_w1, ffn2_w3, ffn2_w2,
           rel_bias, final_norm):
    batch, seq, d = x.shape
    depth = ffn1_norm.shape[0]
    h = x.reshape(batch * seq, d)
    for l in range(depth):
        last = l == depth - 1
        h = _ffn(h, ffn1_norm[l], _to_bf16(ffn1_w1[l]), _to_bf16(ffn1_w3[l]), _to_bf16(ffn1_w2[l]),
                 final_norm, False)
        h = _token_mix(h, mix_norm[l], w_in[l], ret_gn_gain[l], cmp_pe_k[l], cmp_w1_k[l], cmp_w2_k[l],
                       cmp_pe_v[l], cmp_w1_v[l], cmp_w2_v[l], w_out[l], rel_bias, batch, seq)
        h = _ffn(h, ffn2_norm[l], _to_bf16(ffn2_w1[l]), _to_bf16(ffn2_w3[l]), _to_bf16(ffn2_w2[l]),
                 final_norm, last)
    return h.reshape(batch, seq, d)
```

```python
import functools
import math

import jax
import jax.numpy as jnp
import numpy as np
from jax import lax
from jax.experimental import pallas as pl
from jax.experimental.pallas import tpu as pltpu

F32 = jnp.float32
BF16 = jnp.bfloat16

RET_HEADS = 4
RET_DK = 256
RET_DV = 256
ROPE_BASE = 10000.0
NSA_HEADS = 8
NSA_KV_GROUPS = 2
NSA_HPG = NSA_HEADS // NSA_KV_GROUPS
NSA_DH = 128
CMP_LEN = 32
CMP_STRIDE = 16
SEL_LEN = 64
SEL_TOPK = 16
WIN = 512
N_BRANCH = 3
REL_BUCKETS = 32
REL_MAX_DIST = 128
EPS = 1e-6
NEG = -1e30
FORCE = 1e4

RET_W = RET_HEADS * RET_DV
NSA_W = NSA_HEADS * NSA_DH
KV_W = NSA_KV_GROUPS * NSA_DH
IN_SPLITS = [RET_HEADS * RET_DK, RET_HEADS * RET_DK, RET_W, RET_W, NSA_W,
             KV_W, KV_W, KV_W, KV_W, KV_W, KV_W, NSA_HEADS * N_BRANCH]
D_IN = sum(IN_SPLITS)
IN_OFF = [sum(IN_SPLITS[:i]) for i in range(len(IN_SPLITS))]

LANE = 128
IN_TN = 2304
D_IN_PAD = 3 * IN_TN
CMP_KV_TILE = IN_OFF[5] // IN_TN
CMP_KV_LOCAL = IN_OFF[5] % IN_TN
assert CMP_KV_TILE > 0 and CMP_KV_LOCAL + 2 * KV_W <= IN_TN and IN_OFF[6] == IN_OFF[5] + KV_W
GATE_ROWS = 8
FFN_TM = 1024
FFN_TF = 512
IN_TM = 512
RET_C = 256
CMP_TQ = 512
ATT_T = 512
ATT_TK = 512
ATT_NVAR = 3
OUT_TM = 512
SEL_PAD = 128
ACC_PAD = 8
CMP_BAND = 128
VMEM_LIMIT = 56 * 1024 * 1024
CAST_BLOCK_BYTES = 6 * 1024 * 1024
CAST_T_ROWS = 256
LOG2E = math.log2(math.e)


def _dot(a, b):
    return jnp.dot(a, b, preferred_element_type=F32)


def _dot_nt(a, b):
    return lax.dot_general(a, b, (((1,), (1,)), ((), ())), preferred_element_type=F32)


def _dot_tn(a, b):
    return lax.dot_general(a, b, (((0,), (0,)), ((), ())), preferred_element_type=F32)


def _sigmoid(x):
    return 1.0 / (1.0 + jnp.exp(-x))


def _rms(x, g):
    ms = jnp.mean(x * x, axis=-1, keepdims=True)
    return x * lax.rsqrt(ms + EPS) * g


def _ffn_kernel(x_ref, g_ref, w1_ref, w3_ref, w2_ref, fg_ref, o_ref, n_ref, *, final_norm):
    j = pl.program_id(1)

    last = pl.num_programs(1) - 1

    def step(first, norm_out):
        if first:
            n = _rms(x_ref[...], g_ref[...]).astype(BF16)
            n_ref[...] = n
        else:
            n = n_ref[...]
        a = _dot(n, w1_ref[...])
        b = _dot(n, w3_ref[...])
        h = (0.5 * a * _sigmoid(a) * b).astype(BF16)
        y = (x_ref[...] if first else o_ref[...]) + _dot(h, w2_ref[...])
        o_ref[...] = _rms(y, fg_ref[...]) if norm_out else y

    pl.when(j == 0)(functools.partial(step, True, False))
    if final_norm:
        pl.when((j > 0) & (j < last))(functools.partial(step, False, False))
        pl.when(j == last)(functools.partial(step, False, True))
    else:
        pl.when(j > 0)(functools.partial(step, False, False))


def _ffn(x2d, g, w1, w3, w2, fg, final_norm):
    tok, d = x2d.shape
    dff = w1.shape[1]
    tm, tf = FFN_TM, FFN_TF
    assert dff // tf >= 2
    return pl.pallas_call(
        functools.partial(_ffn_kernel, final_norm=final_norm),
        grid=(tok // tm, dff // tf),
        in_specs=[
            pl.BlockSpec((tm, d), lambda i, j: (i, 0)),
            pl.BlockSpec((1, d), lambda i, j: (0, 0)),
            pl.BlockSpec((d, tf), lambda i, j: (0, j)),
            pl.BlockSpec((d, tf), lambda i, j: (0, j)),
            pl.BlockSpec((tf, d), lambda i, j: (j, 0)),
            pl.BlockSpec((1, d), lambda i, j: (0, 0)),
        ],
        out_specs=pl.BlockSpec((tm, d), lambda i, j: (i, 0)),
        out_shape=jax.ShapeDtypeStruct((tok, d), F32),
        scratch_shapes=[pltpu.VMEM((tm, d), BF16)],
        compiler_params=pltpu.CompilerParams(
            dimension_semantics=("parallel", "arbitrary"), vmem_limit_bytes=VMEM_LIMIT),
        name="ffn",
    )(x2d, g.reshape(1, d), w1, w3, w2, fg.reshape(1, d))


def _inproj_kernel(x_ref, g_ref, w_ref, o_ref, kv_ref, n_ref):
    j = pl.program_id(1)
    tn = o_ref.shape[1]
    for jj in range(w_ref.shape[1] // tn):
        @pl.when(j == jj)
        def _():
            if jj == 0:
                n = _rms(x_ref[...], g_ref[...]).astype(BF16)
                n_ref[...] = n
            else:
                n = n_ref[...]
            res = _dot(n, w_ref[:, jj * tn:(jj + 1) * tn])
            o_ref[...] = res.astype(o_ref.dtype)
            if jj == CMP_KV_TILE:
                kv_ref[...] = res[:, CMP_KV_LOCAL:CMP_KV_LOCAL + 2 * KV_W]


def _inproj(x2d, g, w_pad):
    tok, d = x2d.shape
    n_out = w_pad.shape[1]
    tm, tn = IN_TM, IN_TN
    return pl.pallas_call(
        _inproj_kernel,
        grid=(tok // tm, n_out // tn),
        in_specs=[
            pl.BlockSpec((tm, d), lambda i, j: (i, 0)),
            pl.BlockSpec((1, d), lambda i, j: (0, 0)),
            pl.BlockSpec(w_pad.shape, lambda i, j: (0, 0), pipeline_mode=pl.Buffered(1)),
        ],
        out_specs=[pl.BlockSpec((tm, tn), lambda i, j: (i, j)),
                   pl.BlockSpec((tm, 2 * KV_W), lambda i, j: (i, 0))],
        out_shape=[jax.ShapeDtypeStruct((tok, n_out), BF16),
                   jax.ShapeDtypeStruct((tok, 2 * KV_W), F32)],
        scratch_shapes=[pltpu.VMEM((tm, d), BF16)],
        compiler_params=pltpu.CompilerParams(
            dimension_semantics=("parallel", "arbitrary"), vmem_limit_bytes=VMEM_LIMIT),
        name="in_proj",
    )(x2d, g.reshape(1, d), w_pad)


def _ret_kernel(cd_ref, q_ref, k_ref, v_ref, g_ref, cos_ref, sin_ref, dec_ref, xi_ref, zeta_ref,
                gain_ref, o_ref, state_ref):
    @pl.when(pl.program_id(0) == 0)
    def _():
        state_ref[...] = jnp.zeros_like(state_ref)

    cos = cos_ref[...]
    sin = sin_ref[...]
    half = RET_DK // 2

    def rot(t):
        t1, t2 = t[:, :half], t[:, half:]
        return jnp.concatenate([t1 * cos - t2 * sin, t1 * sin + t2 * cos], axis=-1)

    for bi, h in [(bi, h) for bi in range(q_ref.shape[0]) for h in range(RET_HEADS)]:
        cs = slice(h * RET_DK, (h + 1) * RET_DK)
        q = rot(q_ref[bi, :, cs].astype(F32))
        k = rot(k_ref[bi, :, cs].astype(F32))
        v = v_ref[bi, :, cs].astype(BF16)
        qb = q.astype(BF16)
        s = _dot_nt(qb, k.astype(BF16)) * dec_ref[h]
        state = state_ref[bi, h]
        y = _dot(s.astype(BF16), v) + _dot((q * xi_ref[h]).astype(BF16), state.astype(BF16))
        state_ref[bi, h] = cd_ref[h] * state + _dot_tn((k * zeta_ref[h]).astype(BF16), v)
        mu = jnp.mean(y, axis=-1, keepdims=True)
        yc = y - mu
        var = jnp.mean(yc * yc, axis=-1, keepdims=True)
        yn = yc * lax.rsqrt(var + EPS) * gain_ref[:, cs]
        gate = g_ref[bi, :, cs].astype(F32)
        o_ref[bi, :, cs] = (gate * _sigmoid(gate) * yn).astype(o_ref.dtype)


def _retention(proj, gain, batch, seq):
    tok = proj.shape[0]
    c = RET_C
    n_chunks = seq // c
    h, dk = RET_HEADS, RET_DK
    half = dk // 2
    inv = np.float32(ROPE_BASE) ** (-np.arange(half, dtype=np.float32) / np.float32(half))
    ang = np.arange(seq, dtype=np.float32)[:, None] * inv[None, :]
    cos, sin = np.cos(ang).astype(np.float32), np.sin(ang).astype(np.float32)
    log_g = np.log(1.0 - 2.0 ** (-5.0 - np.arange(h, dtype=np.float64)))
    idx = np.arange(c, dtype=np.float64)
    diff = idx[:, None] - idx[None, :]
    dec = np.where(diff >= 0.0, np.exp(np.maximum(diff, 0.0)[None] * log_g[:, None, None]), 0.0)
    k_scale = dk ** -0.5
    assert math.frexp(k_scale)[0] == 0.5
    dec = dec.astype(np.float32) * np.float32(k_scale)
    xi = np.broadcast_to(np.exp((idx + 1.0)[None] * log_g[:, None])[:, :, None], (h, c, dk)).astype(np.float32)
    zeta = np.broadcast_to(np.exp((c - 1.0 - idx)[None] * log_g[:, None])[:, :, None],
                           (h, c, dk)).astype(np.float32) * np.float32(k_scale)
    cd = np.exp(c * log_g).astype(np.float32)

    w = h * dk
    proj3 = proj.reshape(batch, seq, proj.shape[1])
    col = lambda j: pl.BlockSpec((batch, c, w), lambda n: (0, n, j))
    y = pl.pallas_call(
        _ret_kernel,
        grid=(n_chunks,),
        in_specs=[
            pl.BlockSpec(memory_space=pltpu.SMEM),
            col(0), col(1), col(2), col(3),
            pl.BlockSpec((c, half), lambda n: (n, 0)),
            pl.BlockSpec((c, half), lambda n: (n, 0)),
            pl.BlockSpec((h, c, c), lambda n: (0, 0, 0)),
            pl.BlockSpec((h, c, dk), lambda n: (0, 0, 0)),
            pl.BlockSpec((h, c, dk), lambda n: (0, 0, 0)),
            pl.BlockSpec((1, w), lambda n: (0, 0)),
        ],
        out_specs=pl.BlockSpec((batch, c, w), lambda n: (0, n, 0)),
        out_shape=jax.ShapeDtypeStruct((batch, seq, w), BF16),
        scratch_shapes=[pltpu.VMEM((batch, h, dk, RET_DV), F32)],
        compiler_params=pltpu.CompilerParams(
            dimension_semantics=("arbitrary",), vmem_limit_bytes=VMEM_LIMIT),
        name="retention",
    )(cd, proj3, proj3, proj3, proj3, cos, sin, dec, xi, zeta, gain.reshape(1, w))
    return y.reshape(tok, w)


def _compress_kernel(k_ref, v_ref, pek_ref, w1k_ref, w1kf_ref, w2k_ref, pev_ref, w1v_ref, w1vf_ref,
                     w2v_ref, ok_ref, ov_ref):
    dh = NSA_DH
    nblk = k_ref.shape[0] // CMP_STRIDE

    def one(x_ref, pe_ref, w1_ref, w1f_ref, w2_ref, o_ref):
        ab = jnp.zeros((nblk, 2 * dh), F32)
        for l in range(CMP_STRIDE):
            rows = x_ref[pl.ds(l, nblk, stride=CMP_STRIDE), :].astype(BF16)
            ab = ab + _dot(rows, w1_ref[l])
        a, b = ab[:, :dh], ab[:, dh:]
        b_next = pltpu.roll(b, nblk - 1, axis=0)
        ridx = lax.broadcasted_iota(jnp.int32, (nblk, dh), 0)
        b_next = jnp.where(ridx < nblk - 1, b_next, 0.0)
        pe_term = _dot(pe_ref[...], w1f_ref[...])[0:1, :]
        hdn = a + b_next + pe_term
        o_ref[0, 0] = _dot((hdn * _sigmoid(hdn)).astype(BF16), w2_ref[...])

    one(k_ref, pek_ref, w1k_ref, w1kf_ref, w2k_ref, ok_ref)
    one(v_ref, pev_ref, w1v_ref, w1vf_ref, w2v_ref, ov_ref)


def _compress(kv, pe_k, w1_k, w2_k, pe_v, w1_v, w2_v, batch, seq):
    dh, g = NSA_DH, NSA_KV_GROUPS
    nblk = seq // CMP_STRIDE
    half = CMP_LEN // 2

    def prep(pe, w1, w2):
        w1b = w1.astype(BF16)
        w1_pair = jnp.concatenate([w1b[:half], w1b[half:]], axis=-1)
        pe_flat = jnp.broadcast_to(pe.reshape(1, CMP_LEN * dh), (8, CMP_LEN * dh)).astype(BF16)
        return pe_flat, w1_pair, w1b.reshape(CMP_LEN * dh, dh), w2.astype(BF16)

    args_k = prep(pe_k, w1_k, w2_k)
    args_v = prep(pe_v, w1_v, w2_v)
    kcol = 0
    vcol = KV_W // dh
    const = lambda shape: pl.BlockSpec(shape, lambda b, gg: (0,) * len(shape))
    wspecs = [const((8, CMP_LEN * dh)), const((half, dh, 2 * dh)), const((CMP_LEN * dh, dh)),
              const((dh, dh))]
    out_spec = pl.BlockSpec((1, 1, nblk, dh), lambda b, gg: (b, gg, 0, 0))
    return pl.pallas_call(
        _compress_kernel,
        grid=(batch, g),
        in_specs=[pl.BlockSpec((seq, dh), lambda b, gg: (b, kcol + gg)),
                  pl.BlockSpec((seq, dh), lambda b, gg: (b, vcol + gg))] + wspecs + wspecs,
        out_specs=[out_spec, out_spec],
        out_shape=[jax.ShapeDtypeStruct((batch, g, nblk, dh), F32)] * 2,
        compiler_params=pltpu.CompilerParams(
            dimension_semantics=("parallel", "parallel"), vmem_limit_bytes=VMEM_LIMIT),
        name="compress",
    )(kv, kv, *args_k, *args_v)


def _cmpattn_kernel(q_ref, kc_ref, vc_ref, tb_ref, pp_ref, ovt_ref, gate_ref, o_ref, sel_ref, rank_ref):
    tq = q_ref.shape[0]
    ncp = kc_ref.shape[2]
    dh = NSA_DH
    s0 = pl.program_id(2) * tq
    hpg = NSA_HPG
    kc = kc_ref[0, 0].astype(BF16)
    vc = vc_ref[0, 0].astype(BF16)
    scale = dh ** -0.5
    q = jnp.concatenate([q_ref[:, hh * dh:(hh + 1) * dh] for hh in range(hpg)], axis=0).astype(BF16)
    bias = _dot(tb_ref[...].reshape(hpg * tq, tb_ref.shape[2]), pp_ref[0])
    lm = _dot_nt(q, kc) * scale + bias
    m = jnp.max(lm, axis=-1, keepdims=True)
    e = jnp.exp(lm - m)
    den = jnp.sum(e, axis=-1, keepdims=True)
    p = e * jnp.where(m > 0.5 * NEG, 1.0 / den, 0.0)
    o = _dot(p.astype(BF16), vc)
    gate = _sigmoid(gate_ref[...].astype(F32))
    gg = pl.program_id(1)
    psum = p[0:tq]
    for hh in range(hpg):
        g_col = gate[:, hh * N_BRANCH:hh * N_BRANCH + 1]
        for g in range(1, NSA_KV_GROUPS):
            lane = (g * hpg + hh) * N_BRANCH
            g_col = jnp.where(gg == g, gate[:, lane:lane + 1], g_col)
        o_ref[:, hh * dh:(hh + 1) * dh] = (o[hh * tq:(hh + 1) * tq] * g_col).astype(o_ref.dtype)
        if hh:
            psum = psum + p[hh * tq:(hh + 1) * tq]
    nsel = ovt_ref.shape[0]
    imp_t = _dot_nt(ovt_ref[...], psum.astype(BF16))
    jb = lax.broadcasted_iota(jnp.int32, (nsel, tq), 0)
    pos = s0 + lax.broadcasted_iota(jnp.int32, (nsel, tq), 1)
    cur = jnp.right_shift(pos, SEL_LEN.bit_length() - 1)
    causal = jb * SEL_LEN <= pos
    forced = (jb == 0) | (jb == cur) | (jb == cur - 1)
    score = jnp.where(forced, FORCE, jnp.where(causal, imp_t, NEG))
    per_tile = tq // SEL_LEN
    rank_ref[...] = jnp.zeros_like(rank_ref)
    for c in range(nsel // per_tile):
        @pl.when(c <= pl.program_id(2))
        def _():
            rank = rank_ref[...]
            for kk in range(c * per_tile, (c + 1) * per_tile):
                row = score[kk:kk + 1, :]
                beats = (row > score) | ((row == score) & (jb > kk))
                rank = rank + jnp.where(beats, 1, 0)
            rank_ref[...] = rank
    rank = rank_ref[...]
    selb = jnp.where((rank < SEL_TOPK) & causal, 0.0, NEG)
    selb = jnp.concatenate([selb, jnp.zeros((SEL_PAD - nsel, tq), F32)], axis=0)
    sel_ref[0, 0] = selb.astype(BF16)


def _cmp_attention(proj, k_cmp, v_cmp, rel_bias, batch, seq):
    dh, g, hpg = NSA_DH, NSA_KV_GROUPS, NSA_HPG
    tq = CMP_TQ
    nq = seq // tq
    ncp = k_cmp.shape[2]
    nsel = seq // SEL_LEN
    n_cmp = (seq - CMP_LEN) // CMP_STRIDE + 1
    cmp_idx = np.arange(n_cmp)[:, None] * CMP_STRIDE + np.arange(CMP_LEN)[None, :]
    overlap = ((cmp_idx // SEL_LEN)[:, :, None] == np.arange(nsel)[None, None, :]).sum(1) / CMP_LEN
    ovt = np.zeros((nsel, ncp), np.float32)
    ovt[:, :n_cmp] = overlap.T
    nb = tq // CMP_STRIDE
    assert 2 * nb + 2 <= CMP_BAND and CMP_STRIDE * (nb + 1) - (CMP_LEN - 1) >= REL_MAX_DIST
    i = np.arange(tq)[:, None]
    r = np.arange(CMP_BAND)[None, :]
    rel = np.where(r < 2 * nb, i - CMP_STRIDE * (r - nb) - (CMP_LEN - 1), REL_MAX_DIST)
    tb = jnp.where(jnp.asarray(r <= 2 * nb), _bias_lookup(rel_bias, jnp.asarray(rel, jnp.int32)), 0.0)
    tb = jnp.where(jnp.asarray((rel < 0) | (r == 2 * nb + 1)), NEG, tb)
    tb_hi = tb.astype(BF16)
    tb_lo = (tb - tb_hi.astype(F32)).astype(BF16)
    tb = jnp.concatenate([tb_hi, tb_lo], axis=-1)
    n = np.arange(ncp)[None, None, :]
    first = (np.arange(nq) * nb - nb)[:, None, None]
    rr = np.arange(CMP_BAND)[None, :, None]
    pp = np.where(rr < 2 * nb, n == first + rr,
                  ((rr == 2 * nb) & (n < first)) | ((rr == 2 * nb + 1) & (n >= first + 2 * nb)))
    pp = pp.astype(np.float32)
    pp = np.concatenate([pp, pp], axis=1)
    qcol = IN_OFF[4] // (hpg * dh)
    assert IN_OFF[11] % LANE == 0 and IN_SPLITS[11] <= LANE
    gcol = IN_OFF[11] // LANE
    return pl.pallas_call(
        _cmpattn_kernel,
        grid=(batch, g, nq),
        in_specs=[
            pl.BlockSpec((tq, hpg * dh), lambda b, gg, t: (b * nq + t, qcol + gg)),
            pl.BlockSpec((1, 1, ncp, dh), lambda b, gg, t: (b, gg, 0, 0)),
            pl.BlockSpec((1, 1, ncp, dh), lambda b, gg, t: (b, gg, 0, 0)),
            pl.BlockSpec((hpg, tq, 2 * CMP_BAND), lambda b, gg, t: (gg, 0, 0)),
            pl.BlockSpec((1, 2 * CMP_BAND, ncp), lambda b, gg, t: (t, 0, 0)),
            pl.BlockSpec((nsel, ncp), lambda b, gg, t: (0, 0)),
            pl.BlockSpec((tq, LANE), lambda b, gg, t: (b * nq + t, gcol)),
        ],
        out_specs=[
            pl.BlockSpec((tq, hpg * dh), lambda b, gg, t: (b * nq + t, gg)),
            pl.BlockSpec((1, 1, SEL_PAD, tq), lambda b, gg, t: (b, gg, 0, t)),
        ],
        out_shape=[jax.ShapeDtypeStruct((batch * seq, g * hpg * dh), BF16),
                   jax.ShapeDtypeStruct((batch, g, SEL_PAD, seq), BF16)],
        scratch_shapes=[pltpu.VMEM((nsel, tq), jnp.int32)],
        compiler_params=pltpu.CompilerParams(
            dimension_semantics=("parallel", "parallel", "parallel"), vmem_limit_bytes=VMEM_LIMIT),
        name="cmp_attn",
    )(proj, k_cmp, v_cmp, tb, jnp.asarray(pp, BF16), jnp.asarray(ovt, BF16), proj)


def _band_kernel(qi_ref, ki_ref, var_ref, first_ref, last_ref, q_ref, kv_ref, fd_ref, gate_ref,
                 *rest, use_sel):
    if use_sel:
        selb_ref, e_ref, o_ref, qt_ref, m_ref, acc_ref, bias_ref = rest
    else:
        o_ref, qt_ref, m_ref, acc_ref, bias_ref = rest
    t = pl.program_id(1)
    dh, hpg, ng = NSA_DH, NSA_HPG, NSA_KV_GROUPS
    tq, tk = q_ref.shape[0], kv_ref.shape[0]

    @pl.when((pl.program_id(0) == 0) & (t == 0))
    def _():
        upper = (lax.broadcasted_iota(jnp.int32, (tk, tq), 1) >= lax.broadcasted_iota(jnp.int32, (tk, tq), 0))
        for gg in range(ng):
            for hh in range(hpg):
                cols = slice(hh * tq, (hh + 1) * tq)
                by_dist = jnp.broadcast_to(fd_ref[gg, hh, 0, 0:1, :], (tk, tq))
                cyc = pltpu.roll(by_dist, 0, 1, stride=1, stride_axis=0)
                bias_ref[gg, 0, :, cols] = jnp.where(upper, cyc, NEG)
                if use_sel:
                    far = jnp.broadcast_to(fd_ref[gg, hh, 1, 0:1, :], (tk, tq))
                    bias_ref[gg, 1, :, cols] = jnp.where(upper, far, cyc)
                else:
                    bias_ref[gg, 1, :, cols] = jnp.where(upper, NEG, cyc)

    @pl.when(first_ref[t] == 1)
    def _():
        m_ref[...] = jnp.full_like(m_ref, NEG)
        acc_ref[...] = jnp.zeros_like(acc_ref)
        scale = dh ** -0.5 * LOG2E
        for gg in range(ng):
            for hh in range(hpg):
                h = gg * hpg + hh
                q = q_ref[:, h * dh:(h + 1) * dh].astype(F32)
                qt_ref[gg, 0:dh, hh * tq:(hh + 1) * tq] = (q * scale).T.astype(BF16)
                if use_sel:
                    qt_ref[gg, dh:, hh * tq:(hh + 1) * tq] = selb_ref[0, gg]

    var = var_ref[t]

    def update(far):
        ones = jnp.ones((ACC_PAD, tk), F32)
        scores = []
        for gg in range(ng):
            k = kv_ref[:, gg * dh:(gg + 1) * dh]
            if use_sel:
                e_rows = e_ref[pl.ds(pl.multiple_of(ki_ref[t] * tk, tk), tk), :]
                k = jnp.concatenate([k, e_rows], axis=1)
            s = _dot(k, qt_ref[gg])
            scores.append(s if far else s + bias_ref[gg, var])
        for gg in range(ng):
            v = kv_ref[:, KV_W + gg * dh:KV_W + (gg + 1) * dh]
            vt = jnp.concatenate([v.astype(F32).T, ones], axis=0).astype(BF16)
            s = scores[gg]
            m_prev = m_ref[gg]
            col_max = jnp.max(s, axis=0, keepdims=True)
            if far:
                c = jnp.concatenate([fd_ref[gg, hh, 1, 0:1, :] for hh in range(hpg)], axis=1)
                m_new = jnp.maximum(m_prev, col_max + c)
                shift = m_new - c
            else:
                m_new = jnp.maximum(m_prev, col_max)
                shift = m_new
            alpha = jnp.exp2(m_prev - m_new)
            p = jnp.exp2(s - shift).astype(BF16)
            acc_ref[gg] = alpha * acc_ref[gg] + _dot(vt, p)
            m_ref[gg] = m_new

    if use_sel:
        pl.when(var == ATT_NVAR - 1)(functools.partial(update, True))
        pl.when(var != ATT_NVAR - 1)(functools.partial(update, False))
    else:
        update(False)

    @pl.when(last_ref[t] == 1)
    def _():
        for gg in range(ng):
            gate = _sigmoid(gate_ref[0, gg, 0])
            for hh in range(hpg):
                h = gg * hpg + hh
                cols = slice(hh * tq, (hh + 1) * tq)
                o = acc_ref[gg, 0:dh, cols] * (gate[hh:hh + 1, :] / acc_ref[gg, dh:dh + 1, cols])
                o_ref[:, h * dh:(h + 1) * dh] = o.T.astype(o_ref.dtype)


def _band_steps(seq, window):
    tq, tk = ATT_T, ATT_TK
    steps = []
    for qi in range(seq // tq):
        k_hi = qi * tq // tk
        k_lo = max(0, (qi * tq - window + 1) // tk) if window else 0
        steps += [(qi, ki, min((qi * tq - ki * tk) // tq, ATT_NVAR - 1)) for ki in range(k_lo, k_hi + 1)]
    return steps


def _band_attention(proj, bias_by_dist, steps, koff, voff, gates, branch, batch, seq, selb=None, e_mat=None):
    dh, g, hpg = NSA_DH, NSA_KV_GROUPS, NSA_HPG
    tq, tk = ATT_T, ATT_TK
    nq, nk = seq // tq, seq // tk
    qi = np.array([s[0] for s in steps], np.int32)
    ki = np.array([s[1] for s in steps], np.int32)
    var = np.array([s[2] for s in steps], np.int32)
    first = np.concatenate([[1], (qi[1:] != qi[:-1]).astype(np.int32)]).astype(np.int32)
    last = np.concatenate([(qi[1:] != qi[:-1]).astype(np.int32), [1]]).astype(np.int32)
    assert voff == koff + KV_W and koff % (2 * KV_W) == 0
    qcol, kvcol = IN_OFF[4] // NSA_W, koff // (2 * KV_W)
    use_sel = selb is not None
    in_specs = [
        pl.BlockSpec((tq, NSA_W), lambda b, t, qi_r, ki_r, v_r, f_r, l_r: (b * nq + qi_r[t], qcol)),
        pl.BlockSpec((tk, 2 * KV_W), lambda b, t, qi_r, ki_r, v_r, f_r, l_r: (b * nk + ki_r[t], kvcol)),
        pl.BlockSpec(bias_by_dist.shape, lambda b, t, qi_r, ki_r, v_r, f_r, l_r: (0, 0, 0, 0, 0)),
        pl.BlockSpec((1, g, 1, GATE_ROWS, tq),
                     lambda b, t, qi_r, ki_r, v_r, f_r, l_r: (b, 0, branch, 0, qi_r[t])),
    ]
    args = [proj, proj, bias_by_dist, gates]
    if use_sel:
        in_specs += [
            pl.BlockSpec((1, g, SEL_PAD, tq), lambda b, t, qi_r, ki_r, v_r, f_r, l_r: (b, 0, 0, qi_r[t])),
            pl.BlockSpec(e_mat.shape, lambda b, t, qi_r, ki_r, v_r, f_r, l_r: (0, 0)),
        ]
        args += [selb, e_mat]
    kdim = dh + SEL_PAD if use_sel else dh
    grid_spec = pltpu.PrefetchScalarGridSpec(
        num_scalar_prefetch=5,
        grid=(batch, len(steps)),
        in_specs=in_specs,
        out_specs=pl.BlockSpec((tq, NSA_W), lambda b, t, qi_r, ki_r, v_r, f_r, l_r: (b * nq + qi_r[t], 0)),
        scratch_shapes=[pltpu.VMEM((g, kdim, hpg * tq), BF16), pltpu.VMEM((g, 1, hpg * tq), F32),
                        pltpu.VMEM((g, dh + ACC_PAD, hpg * tq), F32),
                        pltpu.VMEM((g, ATT_NVAR - 1, tk, hpg * tq), F32)],
    )
    return pl.pallas_call(
        functools.partial(_band_kernel, use_sel=use_sel),
        grid_spec=grid_spec,
        out_shape=jax.ShapeDtypeStruct((batch * seq, NSA_W), BF16),
        compiler_params=pltpu.CompilerParams(
            dimension_semantics=("arbitrary", "arbitrary"), vmem_limit_bytes=VMEM_LIMIT),
        name="sel_attn" if use_sel else "win_attn",
    )(jnp.asarray(qi), jnp.asarray(ki), jnp.asarray(var), jnp.asarray(first), jnp.asarray(last), *args)


def _out_kernel(x_ref, yr_ref, oc_ref, os_ref, ow_ref, w_ref, o_ref, wb_ref):
    @pl.when(pl.program_id(0) == 0)
    def _():
        wb_ref[...] = w_ref[...].astype(BF16)

    y_nsa = oc_ref[...].astype(F32) + os_ref[...].astype(F32) + ow_ref[...].astype(F32)
    y = jnp.concatenate([yr_ref[...], y_nsa.astype(BF16)], axis=-1)
    o_ref[...] = x_ref[...] + _dot(y, wb_ref[...])


def _out_proj(x2d, y_ret, o_cmp, o_sel, o_win, w_out):
    tok, d = x2d.shape
    tm = OUT_TM
    wmix = y_ret.shape[1]
    row = lambda shape: pl.BlockSpec(shape, lambda i: (i, 0))
    return pl.pallas_call(
        _out_kernel,
        grid=(tok // tm,),
        in_specs=[row((tm, d)), row((tm, wmix)), row((tm, wmix)), row((tm, wmix)), row((tm, wmix)),
                  pl.BlockSpec(w_out.shape, lambda i: (0, 0), pipeline_mode=pl.Buffered(1))],
        out_specs=row((tm, d)),
        out_shape=jax.ShapeDtypeStruct((tok, d), F32),
        scratch_shapes=[pltpu.VMEM(w_out.shape, BF16)],
        compiler_params=pltpu.CompilerParams(
            dimension_semantics=("arbitrary",), vmem_limit_bytes=VMEM_LIMIT),
        name="out_proj",
    )(x2d, y_ret, o_cmp, o_sel, o_win, w_out)


def _t5_bucket_of(rel):
    n = jnp.maximum(rel, 0)
    max_exact = REL_BUCKETS // 2
    nf = jnp.maximum(n, 1).astype(F32)
    large = max_exact + (jnp.log(nf / max_exact) / math.log(REL_MAX_DIST / max_exact)
                         * (REL_BUCKETS - max_exact)).astype(jnp.int32)
    large = jnp.minimum(large, REL_BUCKETS - 1)
    return jnp.where(n < max_exact, n, large)


def _bias_lookup(rel_bias, rel):
    bucket = _t5_bucket_of(rel)[None]
    tab = rel_bias.astype(F32).reshape((rel_bias.shape[0], REL_BUCKETS) + (1,) * rel.ndim)
    out = jnp.zeros((rel_bias.shape[0],) + rel.shape, F32)
    for b in range(REL_BUCKETS):
        out = jnp.where(bucket == b, tab[:, b], out)
    return out


def _bias_by_distance(rel_bias):
    t, g, hpg = ATT_T, NSA_KV_GROUPS, NSA_HPG
    assert ATT_TK == t and t >= REL_MAX_DIST and WIN == t
    by_dist = _bias_lookup(rel_bias, jnp.arange(t, dtype=jnp.int32)) * LOG2E
    far = jnp.broadcast_to(by_dist[:, t - 1:t], by_dist.shape)
    tab = jnp.stack([by_dist, far], axis=1).reshape(g, hpg, 2, 1, t)
    return jnp.broadcast_to(tab, (g, hpg, 2, 8, t))


def _token_mix(x2d, mix_norm, w_in, ret_gn_gain, pe_k, w1_k, w2_k, pe_v, w1_v, w2_v, w_out, rel_bias,
               batch, seq):
    d = x2d.shape[1]
    dh = NSA_DH
    w_pad = _to_bf16_transposed(w_in.T, D_IN_PAD)
    proj, cmp_kv = _inproj(x2d, mix_norm, w_pad)
    y_ret = _retention(proj, ret_gn_gain, batch, seq)

    gates = proj[:, IN_OFF[11]:IN_OFF[11] + IN_SPLITS[11]].astype(F32)
    gates = gates.reshape(batch, seq, NSA_KV_GROUPS, NSA_HPG, N_BRANCH).transpose(0, 2, 4, 3, 1)
    gates = jnp.pad(gates, ((0, 0), (0, 0), (0, 0), (0, GATE_ROWS - NSA_HPG), (0, 0)))

    k_cmp, v_cmp = _compress(cmp_kv, pe_k, w1_k, w2_k, pe_v, w1_v, w2_v, batch, seq)
    o_cmp, selb = _cmp_attention(proj, k_cmp, v_cmp, rel_bias, batch, seq)

    bias_by_dist = _bias_by_distance(rel_bias)
    e_np = np.zeros((seq, SEL_PAD), np.float32)
    e_np[np.arange(seq), np.arange(seq) // SEL_LEN] = 1.0
    o_sel = _band_attention(proj, bias_by_dist, _band_steps(seq, 0),
                            IN_OFF[7], IN_OFF[8], gates, 1, batch, seq,
                            selb=selb, e_mat=jnp.asarray(e_np, BF16))
    o_win = _band_attention(proj, bias_by_dist, _band_steps(seq, WIN),
                            IN_OFF[9], IN_OFF[10], gates, 2, batch, seq)
    return _out_proj(x2d, y_ret, o_cmp, o_sel, o_win, w_out)


def _cast_t_kernel(x_ref, o_ref, *, valid_rows):
    tr = x_ref.shape[0]
    row = pl.program_id(0) * tr + lax.broadcasted_iota(jnp.int32, x_ref.shape, 0)
    o_ref[...] = jnp.where(row < valid_rows, x_ref[...], 0.0).T.astype(o_ref.dtype)


def _to_bf16_transposed(w_t, cols):
    c, r = w_t.shape
    tr = CAST_T_ROWS
    return pl.pallas_call(
        functools.partial(_cast_t_kernel, valid_rows=c),
        grid=(cols // tr,),
        in_specs=[pl.BlockSpec((tr, r), lambda i: (i, 0))],
        out_specs=pl.BlockSpec((r, tr), lambda i: (0, i)),
        out_shape=jax.ShapeDtypeStruct((r, cols), BF16),
        compiler_params=pltpu.CompilerParams(
            dimension_semantics=("parallel",), vmem_limit_bytes=VMEM_LIMIT),
        name="to_bf16_t",
    )(w_t)


def _cast_kernel(x_ref, o_ref):
    o_ref[...] = x_ref[...].astype(o_ref.dtype)


def _to_bf16(w):
    r, c = w.shape
    tr = 1 << ((CAST_BLOCK_BYTES // (4 * c)).bit_length() - 1)
    while r % tr:
        tr //= 2
    return pl.pallas_call(
        _cast_kernel,
        grid=(r // tr,),
        in_specs=[pl.BlockSpec((tr, c), lambda i: (i, 0))],
        out_specs=pl.BlockSpec((tr, c), lambda i: (i, 0)),
        out_shape=jax.ShapeDtypeStruct((r, c), BF16),
        compiler_params=pltpu.CompilerParams(
            dimension_semantics=("parallel",), vmem_limit_bytes=VMEM_LIMIT),
        name="to_bf16",
    )(w)


def kernel(x, ffn1_norm, ffn1_w1, ffn1_w3, ffn1_w2, mix_norm, w_in, ret_gn_gain, cmp_pe_k, cmp_w1_k,
           cmp_w2_k, cmp_pe_v, cmp_w1_v, cmp_w2_v, w_out, ffn2_norm, ffn2_w1, ffn2_w3, ffn2_w2,
           rel_bias, final_norm):
    batch, seq, d = x.shape
    depth = ffn1_norm.shape[0]
    h = x.reshape(batch * seq, d)
    for l in range(depth):
        last = l == depth - 1
        h = _ffn(h, ffn1_norm[l], _to_bf16(ffn1_w1[l]), _to_bf16(ffn1_w3[l]), _to_bf16(ffn1_w2[l]),
                 final_norm, False)
        h = _token_mix(h, mix_norm[l], w_in[l], ret_gn_gain[l], cmp_pe_k[l], cmp_w1_k[l], cmp_w2_k[l],
                       cmp_pe_v[l], cmp_w1_v[l], cmp_w2_v[l], w_out[l], rel_bias, batch, seq)
        h = _ffn(h, ffn2_norm[l], _to_bf16(ffn2_w1[l]), _to_bf16(ffn2_w3[l]), _to_bf16(ffn2_w2[l]),
                 final_norm, last)
    return h.reshape(batch, seq, d)
```
